```python
import jax, jax.numpy as jnp
from jax import lax
import numpy as np

D_MODEL = 2048
BATCH = 8
SEQ = 2048
DEPTH = 1
DEC_BATCH = 16
DEC_SEQ = 32
PAST_LEN = 4096

CHUNK = 64
N_HEADS = 8
N_KV_HEADS = 2
HEAD_DIM = 128
GQA_GROUP = N_HEADS // N_KV_HEADS
ATTN_WIDTH = N_HEADS * HEAD_DIM
KV_WIDTH = N_KV_HEADS * HEAD_DIM
IDX_HEADS = 16
IDX_DIM = 64
TOPK_MAX = 256
Q_BLOCK = 128
POOL_WINDOWS = (2, 4, 8, 16)
POOL_GROUPS = 4
POOL_WIDTH = D_MODEL // 2
POOL_GROUP_DIM = POOL_WIDTH // POOL_GROUPS
POOL_STATE = 15
MOE_GROUPS = 4
EXPERTS_PER_GROUP = 8
N_EXPERTS = MOE_GROUPS * EXPERTS_PER_GROUP
D_EXPERT = 512
TOP_K_FINE = 2
ROPE_THETA = 10000.0
EPS = 1e-6
SPLIT_SIZES = (ATTN_WIDTH, KV_WIDTH, KV_WIDTH, IDX_HEADS * IDX_DIM, IDX_DIM, IDX_HEADS, POOL_WIDTH)
IN_WIDTH = ATTN_WIDTH + 2 * KV_WIDTH + IDX_HEADS * IDX_DIM + IDX_DIM + IDX_HEADS + POOL_WIDTH

kernel_name = 'hybrid_dsa_pool_hmoe_stream_step'


def rmsnorm(x, g):
    xf = x.astype(jnp.float32)
    r = lax.rsqrt(jnp.mean(xf * xf, axis=-1, keepdims=True) + EPS)
    return (xf * r).astype(x.dtype) * g


def rope(x, pos):
    half = x.shape[-1] // 2
    inv = ROPE_THETA ** (-jnp.arange(half, dtype=jnp.float32) / half)
    ang = pos.astype(jnp.float32)[:, None] * inv[None, :]
    cos = jnp.cos(ang)[None, :, None, :].astype(x.dtype)
    sin = jnp.sin(ang)[None, :, None, :].astype(x.dtype)
    x1, x2 = x[..., :half], x[..., half:]
    return jnp.concatenate([x1 * cos - x2 * sin, x1 * sin + x2 * cos], axis=-1)


def mixer_inputs(h, pos, w_in):
    B, T, _ = h.shape
    points = np.cumsum(SPLIT_SIZES)[:-1].tolist()
    q, k, v, qi, ki, wi, p = jnp.split(h @ w_in, points, axis=-1)
    q = rope(q.reshape(B, T, N_HEADS, HEAD_DIM), pos)
    k = rope(k.reshape(B, T, N_KV_HEADS, HEAD_DIM), pos)
    v = v.reshape(B, T, N_KV_HEADS, HEAD_DIM)
    qi = rope(qi.reshape(B, T, IDX_HEADS, IDX_DIM), pos) * (IDX_DIM ** -0.5)
    ki = rope(ki[:, :, None, :], pos)[:, :, 0, :]
    wi = wi * (IDX_HEADS ** -0.5)
    return q, k, v, qi, ki, wi, p


def dsa_block(q, qi, wi, q_pos, k_all, v_all, ki_all, k_pos, topk):
    B, Tq = q.shape[:2]
    s = jnp.einsum('bthd,bsd->bths', qi, ki_all)
    score = jnp.einsum('bths,bth->bts', jax.nn.relu(s).astype(jnp.float32), wi.astype(jnp.float32))
    admissible = (k_pos[None, :] // CHUNK) <= (q_pos[:, None] // CHUNK)
    score = jnp.where(admissible[None], score, -jnp.inf)
    _, sel = lax.top_k(score, topk)
    valid = (k_pos[sel] // CHUNK) <= (q_pos[None, :, None] // CHUNK)
    gather = jax.vmap(lambda a, i: a[i])
    k_sel = gather(k_all, sel)
    v_sel = gather(v_all, sel)
    qg = q.reshape(B, Tq, N_KV_HEADS, GQA_GROUP, HEAD_DIM)
    logits = jnp.einsum('btkgd,btskd->btkgs', qg, k_sel).astype(jnp.float32) * (HEAD_DIM ** -0.5)
    logits = jnp.where(valid[:, :, None, None, :], logits, -jnp.inf)
    p = jax.nn.softmax(logits, axis=-1).astype(v_all.dtype)
    o = jnp.einsum('btkgs,btskd->btkgd', p, v_sel)
    return o.reshape(B, Tq, ATTN_WIDTH)


def dsa_prompt(q, qi, wi, k, v, ki):
    B, T = q.shape[:2]
    nb = T // Q_BLOCK
    pos = jnp.arange(T)
    topk = min(TOPK_MAX, T // 4)

    def to_blocks(a):
        return jnp.moveaxis(a.reshape((B, nb, Q_BLOCK) + a.shape[2:]), 1, 0)

    def one_block(args):
        qb, qib, wib, pb = args
        return dsa_block(qb, qib, wib, pb, k, v, ki, pos, topk)

    out = lax.map(one_block, (to_blocks(q), to_blocks(qi), to_blocks(wi), pos.reshape(nb, Q_BLOCK)))
    return jnp.moveaxis(out, 0, 1).reshape(B, T, ATTN_WIDTH)


def pool_mix(p_ext, n_out, start_pos, w_grp, scale):
    B = p_ext.shape[0]
    n_hist = p_ext.shape[1] - n_out
    pf = p_ext.astype(jnp.float32)
    cs = jnp.pad(jnp.cumsum(pf, axis=1), ((0, 0), (1, 0), (0, 0)))
    rows = n_hist + jnp.arange(n_out)
    pos = start_pos + jnp.arange(n_out)
    hi = cs[:, n_hist + 1:]
    means = []
    for g, w in enumerate(POOL_WINDOWS):
        sl = slice(g * POOL_GROUP_DIM, (g + 1) * POOL_GROUP_DIM)
        lo = jnp.maximum(rows + 1 - w, 0)
        cnt = jnp.minimum(pos + 1, w).astype(jnp.float32)[None, :, None]
        means.append((hi[..., sl] - cs[:, lo][..., sl]) / cnt)
    d = (jnp.concatenate(means, axis=-1) - pf[:, n_hist:]).astype(p_ext.dtype)
    d = d.reshape(B, n_out, POOL_GROUPS, POOL_GROUP_DIM)
    y = jnp.einsum('btgc,gcd->btgd', d, w_grp).reshape(B, n_out, POOL_WIDTH)
    return y * scale


def merge_project(h, o_attn, o_pool, w_gate, b_gate, w_o_attn, w_o_pool, w_out):
    g_attn, g_pool = jnp.split(jax.nn.sigmoid(h @ w_gate + b_gate), 2, axis=-1)
    return (g_attn * (o_attn @ w_o_attn) + g_pool * (o_pool @ w_o_pool)) @ w_out


def hier_moe(h, w_rg, b_rg, w_re, b_re, w_g, w_u, w_d):
    B, T, D = h.shape
    hf = h.reshape(B * T, D)
    cl = (hf @ w_rg + b_rg).astype(jnp.float32)
    g_sel = jnp.argmax(cl, axis=-1)
    g_val = jnp.take_along_axis(jax.nn.softmax(cl, axis=-1), g_sel[:, None], axis=-1)
    fl = (hf @ w_re + b_re).astype(jnp.float32).reshape(-1, MOE_GROUPS, EXPERTS_PER_GROUP)
    fl = jnp.take_along_axis(fl, g_sel[:, None, None], axis=1)[:, 0]
    tv, ti = lax.top_k(fl, TOP_K_FINE)
    tw = jax.nn.softmax(tv, axis=-1) * g_val
    eid = g_sel[:, None] * EXPERTS_PER_GROUP + ti
    combine = jnp.einsum('nke,nk->ne', jax.nn.one_hot(eid, N_EXPERTS, dtype=jnp.float32), tw).astype(h.dtype)
    out = jnp.zeros_like(hf)
    for g in range(MOE_GROUPS):
        es = slice(g * EXPERTS_PER_GROUP, (g + 1) * EXPERTS_PER_GROUP)
        a = jnp.einsum('nd,edf->nef', hf, w_g[es])
        u = jnp.einsum('nd,edf->nef', hf, w_u[es])
        act = jax.nn.silu(a) * u * combine[:, es, None]
        out = out + jnp.einsum('nef,efd->nd', act, w_d[es])
    return out.reshape(B, T, D)


def setup_inputs(seed: int = 0) -> dict:
    key = jax.random.key(seed)
    ks = jax.random.split(key, 32)
    f32 = jnp.float32
    nrm = lambda k, shape, s: jax.random.normal(k, shape, f32) * s
    L = DEPTH
    return {
        'x_prompt': nrm(ks[0], (BATCH, SEQ, D_MODEL), 1.0),
        'x_sample': nrm(ks[1], (DEC_BATCH, DEC_SEQ, D_MODEL), 1.0),
        'cache_k': nrm(ks[2], (L, DEC_BATCH, PAST_LEN, N_KV_HEADS, HEAD_DIM), 1.0),
        'cache_v': nrm(ks[3], (L, DEC_BATCH, PAST_LEN, N_KV_HEADS, HEAD_DIM), 1.0),
        'cache_kidx': nrm(ks[4], (L, DEC_BATCH, PAST_LEN, IDX_DIM), 1.0),
        'state_pool': nrm(ks[5], (L, DEC_BATCH, POOL_STATE, POOL_WIDTH), 1.0),
        'norm1_g': 1.0 + nrm(ks[6], (L, D_MODEL), 0.02),
        'w_in': nrm(ks[7], (L, D_MODEL, IN_WIDTH), D_MODEL ** -0.5),
        'w_gate': nrm(ks[8], (L, D_MODEL, 2 * D_MODEL), D_MODEL ** -0.5),
        'b_gate': nrm(ks[9], (L, 2 * D_MODEL), 0.02),
        'w_pool_grp': nrm(ks[10], (L, POOL_GROUPS, POOL_GROUP_DIM, POOL_GROUP_DIM), POOL_GROUP_DIM ** -0.5),
        'pool_scale': 1.0 + nrm(ks[11], (L, POOL_WIDTH), 0.02),
        'w_o_attn': nrm(ks[12], (L, ATTN_WIDTH, D_MODEL), ATTN_WIDTH ** -0.5),
        'w_o_pool': nrm(ks[13], (L, POOL_WIDTH, D_MODEL), POOL_WIDTH ** -0.5),
        'w_out': nrm(ks[14], (L, D_MODEL, D_MODEL), D_MODEL ** -0.5),
        'norm2_g': 1.0 + nrm(ks[15], (L, D_MODEL), 0.02),
        'w_router_group': nrm(ks[16], (L, D_MODEL, MOE_GROUPS), D_MODEL ** -0.5),
        'b_router_group': nrm(ks[17], (L, MOE_GROUPS), 0.01),
        'w_router_expert': nrm(ks[18], (L, D_MODEL, N_EXPERTS), D_MODEL ** -0.5),
        'b_router_expert': nrm(ks[19], (L, N_EXPERTS), 0.01),
        'w_exp_gate': nrm(ks[20], (L, N_EXPERTS, D_MODEL, D_EXPERT), D_MODEL ** -0.5),
        'w_exp_up': nrm(ks[21], (L, N_EXPERTS, D_MODEL, D_EXPERT), D_MODEL ** -0.5),
        'w_exp_down': nrm(ks[22], (L, N_EXPERTS, D_EXPERT, D_MODEL), D_EXPERT ** -0.5),
        'norm_f_g': 1.0 + nrm(ks[23], (D_MODEL,), 0.02),
    }


def reference(x_prompt, x_sample, cache_k, cache_v, cache_kidx, state_pool,
              norm1_g, w_in, w_gate, b_gate, w_pool_grp, pool_scale, w_o_attn, w_o_pool, w_out,
              norm2_g, w_router_group, b_router_group, w_router_expert, b_router_expert,
              w_exp_gate, w_exp_up, w_exp_down, norm_f_g):
    T = x_prompt.shape[1]
    Ts = x_sample.shape[1]
    past = cache_k.shape[2]
    pos_p = jnp.arange(T)
    pos_s = past + jnp.arange(Ts)
    L_s = past + Ts
    topk_s = min(TOPK_MAX, L_s // 4)
    xp, xs = x_prompt, x_sample
    kp_l, vp_l, kip_l, pp_l, ks_l, vs_l, kis_l, ps_l = [], [], [], [], [], [], [], []
    for l in range(DEPTH):
        hp = rmsnorm(xp, norm1_g[l])
        q, k, v, qi, ki, wi, pin = mixer_inputs(hp, pos_p, w_in[l])
        o_attn = dsa_prompt(q, qi, wi, k, v, ki)
        o_pool = pool_mix(pin, T, 0, w_pool_grp[l], pool_scale[l])
        xp = xp + merge_project(hp, o_attn, o_pool, w_gate[l], b_gate[l], w_o_attn[l], w_o_pool[l], w_out[l])
        xp = xp + hier_moe(rmsnorm(xp, norm2_g[l]), w_router_group[l], b_router_group[l],
                           w_router_expert[l], b_router_expert[l], w_exp_gate[l], w_exp_up[l], w_exp_down[l])
        kp_l.append(k); vp_l.append(v); kip_l.append(ki); pp_l.append(pin[:, -POOL_STATE:])
        hs = rmsnorm(xs, norm1_g[l])
        q, k, v, qi, ki, wi, pin = mixer_inputs(hs, pos_s, w_in[l])
        k_all = jnp.concatenate([cache_k[l], k], axis=1)
        v_all = jnp.concatenate([cache_v[l], v], axis=1)
        ki_all = jnp.concatenate([cache_kidx[l], ki], axis=1)
        o_attn = dsa_block(q, qi, wi, pos_s, k_all, v_all, ki_all, jnp.arange(L_s), topk_s)
        p_ext = jnp.concatenate([state_pool[l], pin], axis=1)
        o_pool = pool_mix(p_ext, Ts, past, w_pool_grp[l], pool_scale[l])
        xs = xs + merge_project(hs, o_attn, o_pool, w_gate[l], b_gate[l], w_o_attn[l], w_o_pool[l], w_out[l])
        xs = xs + hier_moe(rmsnorm(xs, norm2_g[l]), w_router_group[l], b_router_group[l],
                           w_router_expert[l], b_router_expert[l], w_exp_gate[l], w_exp_up[l], w_exp_down[l])
        ks_l.append(k); vs_l.append(v); kis_l.append(ki); ps_l.append(p_ext[:, -POOL_STATE:])
    y_prompt = rmsnorm(xp, norm_f_g)
    y_sample = rmsnorm(xs, norm_f_g)
    return (y_prompt, y_sample,
            jnp.stack(kp_l), jnp.stack(vp_l), jnp.stack(kip_l), jnp.stack(pp_l),
            jnp.stack(ks_l), jnp.stack(vs_l), jnp.stack(kis_l), jnp.stack(ps_l))
```

```python
import functools

import numpy as np
import jax
import jax.numpy as jnp
from jax import lax
from jax.experimental import pallas as pl
from jax.experimental.pallas import tpu as pltpu

F32 = jnp.float32
BF16 = jnp.bfloat16
I32 = jnp.int32

D_MODEL = 2048
CHUNK = 64
N_HEADS = 8
N_KV_HEADS = 2
HEAD_DIM = 128
GQA_GROUP = N_HEADS // N_KV_HEADS
ATTN_WIDTH = N_HEADS * HEAD_DIM
KV_WIDTH = N_KV_HEADS * HEAD_DIM
IDX_HEADS = 16
IDX_DIM = 64
IDX_PAIRS = IDX_HEADS // 2
TOPK_MAX = 256
POOL_WINDOWS = (2, 4, 8, 16)
POOL_GROUPS = 4
POOL_WIDTH = D_MODEL // 2
POOL_GROUP_DIM = POOL_WIDTH // POOL_GROUPS
POOL_STATE = 15
POOL_HALO = 16
MOE_GROUPS = 4
EXPERTS_PER_GROUP = 8
N_EXPERTS = MOE_GROUPS * EXPERTS_PER_GROUP
D_EXPERT = 512
ROPE_THETA = 10000.0
CHUNK_SHIFT = CHUNK.bit_length() - 1
GROUP_SHIFT = EXPERTS_PER_GROUP.bit_length() - 1
EPS = 1e-6

LANES = 128
INT_MIN = -(2 ** 31)
NEG_BIG = -1e30
VMEM_LIMIT = 56 * 1024 * 1024

KEY_BLOCK = 256
TOK_TILE = 256
EXP_TILE = 256


def _cparams(n_axes):
    return pltpu.CompilerParams(dimension_semantics=("arbitrary",) * n_axes,
                                vmem_limit_bytes=VMEM_LIMIT)


def _dot(a, b):
    return jnp.dot(a, b, preferred_element_type=F32)


def _dot_nt(a, b):
    return lax.dot_general(a, b, (((1,), (1,)), ((), ())), preferred_element_type=F32)


_C_Q = 0
_C_K = _C_Q + ATTN_WIDTH
_C_V = _C_K + KV_WIDTH
_C_QI = _C_V + KV_WIDTH
_C_KW = _C_QI + IDX_HEADS * IDX_DIM
_C_P = _C_KW + LANES
_C_END = _C_P + POOL_WIDTH


def _rope128(y, cos, sin_signed):
    return y * cos + pltpu.roll(y, HEAD_DIM // 2, 1) * sin_signed


def _rope64(y, cos, sin_a, sin_b):
    half = IDX_DIM // 2
    return y * cos + pltpu.roll(y, LANES - half, 1) * sin_a + pltpu.roll(y, half, 1) * sin_b


def _inproj_kernel(x_ref, g_ref, w_ref, ck_ref, sk_ref, cq_ref, sq_ref, ci_ref, sia_ref, sib_ref,
                   cw_ref, swa_ref, swb_ref,
                   h_ref, q_ref, k_ref, v_ref, qi_ref, kw_ref, p_ref):
    x = x_ref[...]
    r = lax.rsqrt(jnp.mean(x * x, axis=-1, keepdims=True) + EPS)
    h = ((x * r) * g_ref[...]).astype(BF16)
    h_ref[...] = h
    cq, sq = cq_ref[...], sq_ref[...]
    for c in range(ATTN_WIDTH // 256):
        y = _dot(h, w_ref[:, _C_Q + c * 256:_C_Q + (c + 1) * 256])
        for s in range(2):
            q_ref[:, c * 256 + s * LANES:c * 256 + (s + 1) * LANES] = _rope128(
                y[:, s * LANES:(s + 1) * LANES], cq, sq).astype(BF16)
    ck, sk = ck_ref[...], sk_ref[...]
    y = _dot(h, w_ref[:, _C_K:_C_K + KV_WIDTH])
    for s in range(N_KV_HEADS):
        k_ref[:, s * LANES:(s + 1) * LANES] = _rope128(y[:, s * LANES:(s + 1) * LANES], ck, sk)
    v_ref[...] = _dot(h, w_ref[:, _C_V:_C_V + KV_WIDTH])
    ci, sia, sib = ci_ref[...], sia_ref[...], sib_ref[...]
    for c in range(IDX_PAIRS // 2):
        y = _dot(h, w_ref[:, _C_QI + c * 256:_C_QI + (c + 1) * 256])
        for s in range(2):
            qi_ref[2 * c + s] = _rope64(y[:, s * LANES:(s + 1) * LANES], ci, sia, sib).astype(BF16)
    y = _dot(h, w_ref[:, _C_KW:_C_KW + LANES])
    kw_ref[...] = _rope64(y, cw_ref[...], swa_ref[...], swb_ref[...])
    for c in range(POOL_WIDTH // 256):
        p_ref[:, c * 256:(c + 1) * 256] = _dot(h, w_ref[:, _C_P + c * 256:_C_P + (c + 1) * 256])


def _rope_tables(pos):
    pos = pos.astype(F32)[:, None]

    def cs(dim):
        half = dim // 2
        inv = ROPE_THETA ** (-jnp.arange(half, dtype=F32) / half)
        ang = pos * inv[None, :]
        return jnp.cos(ang), jnp.sin(ang)

    c, s = cs(HEAD_DIM)
    ck = jnp.concatenate([c, c], axis=1)
    sk = jnp.concatenate([-s, s], axis=1)
    qs = HEAD_DIM ** -0.5
    c, s = cs(IDX_DIM)
    z = jnp.zeros_like(s)
    ci = jnp.concatenate([c, c, c, c], axis=1)
    sia = jnp.concatenate([-s, z, -s, z], axis=1)
    sib = jnp.concatenate([z, s, z, s], axis=1)
    iscale = IDX_DIM ** -0.5
    n = pos.shape[0]
    wscale = jnp.full((n, IDX_HEADS), IDX_HEADS ** -0.5, F32)
    pad = jnp.zeros((n, LANES - IDX_DIM - IDX_HEADS), F32)
    zz = jnp.zeros((n, LANES - IDX_DIM), F32)
    cw = jnp.concatenate([c, c, wscale, pad], axis=1)
    swa = jnp.concatenate([-s, z, zz], axis=1)
    swb = jnp.concatenate([z, s, zz], axis=1)
    return (ck, sk, ck * qs, sk * qs, ci * iscale, sia * iscale, sib * iscale, cw, swa, swb)


def _in_proj(xc, g1, w_pack, tables, n_prompt_tiles, tiles_per_seq):
    n = xc.shape[0]
    tm = TOK_TILE
    n_tiles = n // tm

    def tab_idx(i):
        return (jnp.where(i < n_prompt_tiles, i % tiles_per_seq, tiles_per_seq + i - n_prompt_tiles), 0)

    tok = lambda w: pl.BlockSpec((tm, w), lambda i: (i, 0))
    tab = pl.BlockSpec((tm, LANES), tab_idx)
    out_shape = (
        jax.ShapeDtypeStruct((n, D_MODEL), BF16),
        jax.ShapeDtypeStruct((n, ATTN_WIDTH), BF16),
        jax.ShapeDtypeStruct((n, KV_WIDTH), F32),
        jax.ShapeDtypeStruct((n, KV_WIDTH), F32),
        jax.ShapeDtypeStruct((IDX_PAIRS, n, LANES), BF16),
        jax.ShapeDtypeStruct((n, LANES), F32),
        jax.ShapeDtypeStruct((n, POOL_WIDTH), F32),
    )
    return pl.pallas_call(
        _inproj_kernel,
        grid=(n_tiles,),
        in_specs=[tok(D_MODEL),
                  pl.BlockSpec((1, D_MODEL), lambda i: (0, 0)),
                  pl.BlockSpec((D_MODEL, _C_END), lambda i: (0, 0), pipeline_mode=pl.Buffered(1))]
                 + [tab] * 10,
        out_specs=(tok(D_MODEL), tok(ATTN_WIDTH), tok(KV_WIDTH), tok(KV_WIDTH),
                   pl.BlockSpec((IDX_PAIRS, tm, LANES), lambda i: (0, i, 0)),
                   tok(LANES), tok(POOL_WIDTH)),
        out_shape=out_shape,
        compiler_params=_cparams(1),
        name="in_proj",
    )(xc, g1, w_pack, *tables)


def _sortable_key(score):
    bits = lax.bitcast_convert_type(score, I32)
    return bits ^ ((bits >> 31) & 0x7FFFFFFF)


def _dsa_kernel(*refs, tq, n_key_rows, n_cache, n_new, topk, prompt):
    if prompt:
        (q_ref, qi_ref, wq_ref, kn_ref, vn_ref, kwn_ref, o_ref,
         k16, v16, kia, kib, wbe, wbo, sc, qst, m_s, l_s, acc_s) = refs
    else:
        (q_ref, qi_ref, wq_ref, kn_ref, vn_ref, kwn_ref, kc_ref, vc_ref, kic_ref, o_ref,
         k16, v16, kia, kib, wbe, wbo, sc, qst, m_s, l_s, acc_s) = refs
    qb = pl.program_id(1)
    n_blocks_all = n_key_rows // KEY_BLOCK

    @pl.when(qb == 0)
    def _():
        lane = lax.broadcasted_iota(I32, (n_new, LANES), 1)
        kin = jnp.where(lane < IDX_DIM, kwn_ref[...], 0.0).astype(BF16)
        if prompt:
            k16[...] = kn_ref[...].astype(BF16)
            v16[...] = vn_ref[...].astype(BF16)
            kia[...] = kin
            kib[...] = pltpu.roll(kin.astype(F32), IDX_DIM, 1).astype(BF16)
        else:
            n_tail = n_key_rows - n_cache
            k16[0:n_cache, :] = kc_ref[...].astype(BF16)
            v16[0:n_cache, :] = vc_ref[...].astype(BF16)
            kic = jnp.concatenate(
                [kic_ref[...], jnp.zeros((n_cache, LANES - IDX_DIM), F32)], axis=1)
            kia[0:n_cache, :] = kic.astype(BF16)
            kib[0:n_cache, :] = pltpu.roll(kic, IDX_DIM, 1).astype(BF16)
            k16[n_cache:n_key_rows, :] = jnp.zeros((n_tail, KV_WIDTH), BF16)
            v16[n_cache:n_key_rows, :] = jnp.zeros((n_tail, KV_WIDTH), BF16)
            kia[n_cache:n_key_rows, :] = jnp.zeros((n_tail, LANES), BF16)
            kib[n_cache:n_key_rows, :] = jnp.zeros((n_tail, LANES), BF16)
            k16[n_cache:n_cache + n_new, :] = kn_ref[...].astype(BF16)
            v16[n_cache:n_cache + n_new, :] = vn_ref[...].astype(BF16)
            kia[n_cache:n_cache + n_new, :] = kin
            kib[n_cache:n_cache + n_new, :] = pltpu.roll(kin.astype(F32), IDX_DIM, 1).astype(BF16)

    if prompt:
        n_blocks = qb // (KEY_BLOCK // tq) + 1
    else:
        n_blocks = n_blocks_all

    wq = wq_ref[...]
    for j in range(IDX_HEADS):
        col = jnp.broadcast_to(wq[:, IDX_DIM + j:IDX_DIM + j + 1], (tq, LANES))
        if j % 2 == 0:
            wbe[j // 2] = col
        else:
            wbo[j // 2] = col
    for g in range(N_KV_HEADS):
        for j in range(GQA_GROUP):
            hd = g * GQA_GROUP + j
            qst[g, j * tq:(j + 1) * tq, :] = q_ref[:, hd * HEAD_DIM:(hd + 1) * HEAD_DIM]

    row = lax.broadcasted_iota(I32, (tq, KEY_BLOCK), 0)
    lane = lax.broadcasted_iota(I32, (tq, KEY_BLOCK), 1)

    qi2 = qi_ref[...].reshape(IDX_PAIRS * tq, LANES)

    def score_block(kb, carry):
        r0 = pl.multiple_of(kb * KEY_BLOCK, KEY_BLOCK)
        se = _dot_nt(qi2, kia[pl.ds(r0, KEY_BLOCK), :]).reshape(IDX_PAIRS, tq, KEY_BLOCK)
        so = _dot_nt(qi2, kib[pl.ds(r0, KEY_BLOCK), :]).reshape(IDX_PAIRS, tq, KEY_BLOCK)
        score = jnp.zeros((tq, KEY_BLOCK), F32)
        for p in range(IDX_PAIRS):
            we = wbe[p]
            wo = wbo[p]
            we2 = jnp.concatenate([we, we], axis=1)
            wo2 = jnp.concatenate([wo, wo], axis=1)
            score = score + jnp.maximum(se[p], 0.0) * we2 + jnp.maximum(so[p], 0.0) * wo2
        s_pos = kb * KEY_BLOCK + lane
        if prompt:
            adm = (s_pos >> CHUNK_SHIFT) <= ((qb * tq + row) >> CHUNK_SHIFT)
        else:
            adm = s_pos < (n_cache + n_new)
        sc[kb] = jnp.where(adm, _sortable_key(score), INT_MIN)
        return carry

    lax.fori_loop(0, n_blocks, score_block, 0)

    def count_where(pred):
        def body(kb, acc):
            m = pred(sc[kb], kb).astype(I32)
            return acc + m[:, :LANES] + m[:, LANES:]
        acc = lax.fori_loop(0, n_blocks, body, jnp.zeros((tq, LANES), I32))
        return jnp.sum(acc, axis=1, keepdims=True)

    def count_ge(cand):
        cand_b = jnp.broadcast_to(cand, (tq, KEY_BLOCK))
        return count_where(lambda keys, kb: keys >= cand_b)

    c0 = count_ge(jnp.zeros((tq, 1), I32))
    thr0 = jnp.where(c0 >= topk, 0, INT_MIN).astype(I32)
    cnt0 = jnp.where(c0 >= topk, c0, n_key_rows).astype(I32)

    def bit_step(i, carry):
        thr, cnt = carry
        cand = thr + (jnp.int32(1) << (30 - i))
        c = count_ge(cand)
        ok = c >= topk
        return jnp.where(ok, cand, thr), jnp.where(ok, c, cnt)

    thr, cnt = lax.fori_loop(0, 31, bit_step, (thr0, cnt0))
    has_k = thr > INT_MIN
    thr = jnp.where(has_k, thr, INT_MIN + 1)
    need_tie = has_k & (cnt > topk)

    @pl.when(jnp.max(need_tie.astype(I32)) > 0)
    def _():
        thr_b = jnp.broadcast_to(thr, (tq, KEY_BLOCK))
        n_gt = count_where(lambda keys, kb: keys > thr_b)
        n_keep = topk - n_gt

        def idx_step(i, bound):
            cand = bound + (jnp.int32(1) << (14 - i))
            cand_b = jnp.broadcast_to(cand, (tq, KEY_BLOCK))
            c = count_where(lambda keys, kb: (keys == thr_b) & ((kb * KEY_BLOCK + lane) < cand_b))
            return jnp.where(c <= n_keep, cand, bound)

        bound = lax.fori_loop(0, 15, idx_step, jnp.zeros((tq, 1), I32))
        bound_b = jnp.broadcast_to(jnp.where(need_tie, bound, jnp.int32(2 ** 30)), (tq, KEY_BLOCK))

        def drop_block(kb, carry):
            keys = sc[kb]
            drop = (keys == thr_b) & ((kb * KEY_BLOCK + lane) >= bound_b)
            sc[kb] = jnp.where(drop, INT_MIN, keys)
            return carry

        lax.fori_loop(0, n_blocks, drop_block, 0)

    rows = GQA_GROUP * tq
    m_s[...] = jnp.full(m_s.shape, NEG_BIG, F32)
    l_s[...] = jnp.zeros(l_s.shape, F32)
    acc_s[...] = jnp.zeros(acc_s.shape, F32)
    thr_b = jnp.broadcast_to(thr, (tq, KEY_BLOCK))

    def attn_block(kb, carry):
        r0 = pl.multiple_of(kb * KEY_BLOCK, KEY_BLOCK)
        keep = sc[kb] >= thr_b
        for g in range(N_KV_HEADS):
            kblk = k16[pl.ds(r0, KEY_BLOCK), g * HEAD_DIM:(g + 1) * HEAD_DIM]
            vblk = v16[pl.ds(r0, KEY_BLOCK), g * HEAD_DIM:(g + 1) * HEAD_DIM]
            logits = _dot_nt(qst[g], kblk).reshape(GQA_GROUP, tq, KEY_BLOCK)
            logits = jnp.where(keep[None], logits, NEG_BIG).reshape(rows, KEY_BLOCK)
            m_prev = m_s[g]
            m_new = jnp.maximum(m_prev, jnp.max(logits, axis=1, keepdims=True))
            alpha = jnp.exp(m_prev - m_new)
            p = jnp.exp(logits - m_new)
            l_s[g] = alpha * l_s[g] + jnp.sum(p, axis=1, keepdims=True)
            acc_s[g] = alpha * acc_s[g] + _dot(p.astype(BF16), vblk)
            m_s[g] = m_new
        return carry

    lax.fori_loop(0, n_blocks, attn_block, 0)

    for g in range(N_KV_HEADS):
        o = acc_s[g] / l_s[g]
        for j in range(GQA_GROUP):
            hd = g * GQA_GROUP + j
            o_ref[:, hd * HEAD_DIM:(hd + 1) * HEAD_DIM] = o[j * tq:(j + 1) * tq, :].astype(BF16)


def _dsa_scratch(tq, n_key_rows):
    rows = GQA_GROUP * tq
    return [
        pltpu.VMEM((n_key_rows, KV_WIDTH), BF16),
        pltpu.VMEM((n_key_rows, KV_WIDTH), BF16),
        pltpu.VMEM((n_key_rows, LANES), BF16),
        pltpu.VMEM((n_key_rows, LANES), BF16),
        pltpu.VMEM((IDX_PAIRS, tq, LANES), F32),
        pltpu.VMEM((IDX_PAIRS, tq, LANES), F32),
        pltpu.VMEM((n_key_rows // KEY_BLOCK, tq, KEY_BLOCK), I32),
        pltpu.VMEM((N_KV_HEADS, rows, HEAD_DIM), BF16),
        pltpu.VMEM((N_KV_HEADS, rows, 1), F32),
        pltpu.VMEM((N_KV_HEADS, rows, 1), F32),
        pltpu.VMEM((N_KV_HEADS, rows, HEAD_DIM), F32),
    ]


def _dsa_prompt(q, qi, kw, k, v, row0_tiles, batch, seq):
    tq = 128
    nqb = seq // tq
    topk = min(TOPK_MAX, seq // 4)
    kern = functools.partial(_dsa_kernel, tq=tq, n_key_rows=seq, n_cache=0, n_new=seq,
                             topk=topk, prompt=True)
    qmap = lambda b, i: (b * nqb + i, 0)
    smap = lambda b, i: (b, 0)
    return pl.pallas_call(
        kern,
        grid=(batch, nqb),
        in_specs=[pl.BlockSpec((tq, ATTN_WIDTH), qmap),
                  pl.BlockSpec((IDX_PAIRS, tq, LANES), lambda b, i: (0, b * nqb + i, 0)),
                  pl.BlockSpec((tq, LANES), qmap),
                  pl.BlockSpec((seq, KV_WIDTH), smap),
                  pl.BlockSpec((seq, KV_WIDTH), smap),
                  pl.BlockSpec((seq, LANES), smap)],
        out_specs=pl.BlockSpec((tq, ATTN_WIDTH), qmap),
        out_shape=jax.ShapeDtypeStruct((batch * seq, ATTN_WIDTH), BF16),
        scratch_shapes=_dsa_scratch(tq, seq),
        compiler_params=_cparams(2),
        name="dsa_prompt",
    )(q, qi, kw, k, v, kw)


def _dsa_sample(q, qi, kw, k, v, cache_k, cache_v, cache_ki, row0, batch, seq, past):
    tq = seq
    n_key_rows = past + KEY_BLOCK
    topk = min(TOPK_MAX, (past + seq) // 4)
    kern = functools.partial(_dsa_kernel, tq=tq, n_key_rows=n_key_rows, n_cache=past, n_new=seq,
                             topk=topk, prompt=False)
    t0 = row0 // seq
    qmap = lambda b, i: (t0 + b, 0)
    cmap = lambda b, i: (b, 0)
    return pl.pallas_call(
        kern,
        grid=(batch, 1),
        in_specs=[pl.BlockSpec((tq, ATTN_WIDTH), qmap),
                  pl.BlockSpec((IDX_PAIRS, tq, LANES), lambda b, i: (0, t0 + b, 0)),
                  pl.BlockSpec((tq, LANES), qmap),
                  pl.BlockSpec((tq, KV_WIDTH), qmap),
                  pl.BlockSpec((tq, KV_WIDTH), qmap),
                  pl.BlockSpec((tq, LANES), qmap),
                  pl.BlockSpec((past, KV_WIDTH), cmap),
                  pl.BlockSpec((past, KV_WIDTH), cmap),
                  pl.BlockSpec((past, IDX_DIM), cmap)],
        out_specs=pl.BlockSpec((tq, ATTN_WIDTH), lambda b, i: (b, 0)),
        out_shape=jax.ShapeDtypeStruct((batch * seq, ATTN_WIDTH), BF16),
        scratch_shapes=_dsa_scratch(tq, n_key_rows),
        compiler_params=_cparams(2),
        name="dsa_sample",
    )(q, qi, kw, k, v, kw, cache_k, cache_v, cache_ki)


def _pool_kernel(*refs, tp, pos0, carried):
    if carried:
        p_ref, w_ref, s_ref, o_ref, ext, save = refs
    else:
        p_ref, halo_ref, w_ref, s_ref, o_ref, ext = refs
    i = pl.program_id(1)
    if carried:
        @pl.when(i == 0)
        def _():
            ext[0:POOL_HALO, :] = jnp.zeros((POOL_HALO, POOL_WIDTH), F32)

        @pl.when(i > 0)
        def _():
            ext[0:POOL_HALO, :] = save[...]
    else:
        ext[0:POOL_HALO, :] = halo_ref[0]
    x = p_ref[...]
    ext[POOL_HALO:POOL_HALO + tp, :] = x
    if carried:
        save[...] = x[tp - POOL_HALO:tp, :]
    pos = pos0 + i * tp + lax.broadcasted_iota(I32, (tp, 1), 0)
    for g, w in enumerate(POOL_WINDOWS):
        c0, c1 = g * POOL_GROUP_DIM, (g + 1) * POOL_GROUP_DIM
        xg = x[:, c0:c1]
        tot = xg
        for s in range(1, w):
            tot = tot + ext[POOL_HALO - s:POOL_HALO - s + tp, c0:c1]
        inv = 1.0 / jnp.minimum(pos + 1, w).astype(F32)
        d = (tot * inv - xg).astype(BF16)
        o_ref[:, c0:c1] = (_dot(d, w_ref[g]) * s_ref[:, c0:c1]).astype(BF16)


def _pool(pin, w_grp, scale, tile0, batch, seq, pos0, halo=None):
    carried = halo is None
    tp = min(seq, 256)
    nt = seq // tp
    kern = functools.partial(_pool_kernel, tp=tp, pos0=pos0, carried=carried)
    in_specs = [pl.BlockSpec((tp, POOL_WIDTH), lambda b, i: (tile0 + b * nt + i, 0))]
    args = [pin]
    if not carried:
        in_specs.append(pl.BlockSpec((1, POOL_HALO, POOL_WIDTH), lambda b, i: (b, 0, 0)))
        args.append(halo)
    in_specs += [pl.BlockSpec((POOL_GROUPS, POOL_GROUP_DIM, POOL_GROUP_DIM), lambda b, i: (0, 0, 0)),
                 pl.BlockSpec((1, POOL_WIDTH), lambda b, i: (0, 0))]
    scratch = [pltpu.VMEM((POOL_HALO + tp, POOL_WIDTH), F32)]
    if carried:
        scratch.append(pltpu.VMEM((POOL_HALO, POOL_WIDTH), F32))
    return pl.pallas_call(
        kern,
        grid=(batch, nt),
        in_specs=in_specs,
        out_specs=pl.BlockSpec((tp, POOL_WIDTH), lambda b, i: (b * nt + i, 0)),
        out_shape=jax.ShapeDtypeStruct((batch * seq, POOL_WIDTH), BF16),
        scratch_shapes=scratch,
        compiler_params=_cparams(2),
        name="pool_prompt" if carried else "pool_sample",
    )(*args, w_grp, scale)


def _merge_kernel(h_ref, oa_ref, op_ref, wga_ref, wgp_ref, ba_ref, bp_ref, woa_ref, wop_ref, m_ref):
    h = h_ref[...]
    ga = jax.nn.sigmoid(_dot(h, wga_ref[...]) + ba_ref[...])
    gp = jax.nn.sigmoid(_dot(h, wgp_ref[...]) + bp_ref[...])
    a = _dot(oa_ref[...], woa_ref[...])
    p = _dot(op_ref[...], wop_ref[...])
    m_ref[...] = (ga * a + gp * p).astype(BF16)


def _merge(h, o_attn, o_pool, w_gate, b_gate, w_oa, w_op):
    n = h.shape[0]
    tm = 2 * TOK_TILE
    tn = D_MODEL // 2
    nj = D_MODEL // tn
    return pl.pallas_call(
        _merge_kernel,
        grid=(nj, n // tm),
        in_specs=[pl.BlockSpec((tm, D_MODEL), lambda j, i: (i, 0)),
                  pl.BlockSpec((tm, ATTN_WIDTH), lambda j, i: (i, 0)),
                  pl.BlockSpec((tm, POOL_WIDTH), lambda j, i: (i, 0)),
                  pl.BlockSpec((D_MODEL, tn), lambda j, i: (0, j)),
                  pl.BlockSpec((D_MODEL, tn), lambda j, i: (0, nj + j)),
                  pl.BlockSpec((1, tn), lambda j, i: (0, j)),
                  pl.BlockSpec((1, tn), lambda j, i: (0, nj + j)),
                  pl.BlockSpec((ATTN_WIDTH, tn), lambda j, i: (0, j)),
                  pl.BlockSpec((POOL_WIDTH, tn), lambda j, i: (0, j))],
        out_specs=pl.BlockSpec((tm, tn), lambda j, i: (i, j)),
        out_shape=jax.ShapeDtypeStruct((n, D_MODEL), BF16),
        compiler_params=_cparams(2),
        name="merge",
    )(h, o_attn, o_pool, w_gate, w_gate, b_gate, b_gate, w_oa, w_op)


_R_E1, _R_E2, _R_W1, _R_W2, _R_RANK1, _R_RANK2 = range(6)
_R_LOGIT0 = MOE_GROUPS


def _outproj_kernel(m_ref, x_ref, wo_ref, g2_ref, wrh_ref, wrl_ref, br_ref,
                    x1_ref, h2_ref, route_ref, counts_ref, run):
    i = pl.program_id(0)
    tm = m_ref.shape[0]

    @pl.when(i == 0)
    def _():
        run[...] = jnp.zeros(run.shape, F32)

    x1 = x_ref[...] + _dot(m_ref[...], wo_ref[...])
    x1_ref[...] = x1
    r = lax.rsqrt(jnp.mean(x1 * x1, axis=-1, keepdims=True) + EPS)
    hf = (x1 * r) * g2_ref[...]
    h2_ref[...] = hf

    hi = hf.astype(BF16)
    lo = (hf - hi.astype(F32)).astype(BF16)
    lg = _dot(hi, wrh_ref[...]) + (_dot(hi, wrl_ref[...]) + _dot(lo, wrh_ref[...])) + br_ref[...]

    lane = lax.broadcasted_iota(I32, (tm, LANES), 1)
    neg_inf = jnp.float32(-jnp.inf)
    is_g = lane < MOE_GROUPS
    cl = jnp.where(is_g, lg, neg_inf)
    gmax = jnp.max(cl, axis=1, keepdims=True)
    g_sel = jnp.min(jnp.where(cl == gmax, lane, LANES), axis=1, keepdims=True)
    den = jnp.sum(jnp.where(is_g, jnp.exp(cl - gmax), 0.0), axis=1, keepdims=True)
    g_val = 1.0 / den
    e_lane = lane - _R_LOGIT0
    in_grp = (e_lane >= 0) & (e_lane < N_EXPERTS) & ((e_lane >> GROUP_SHIFT) == g_sel)
    f1 = jnp.where(in_grp, lg, neg_inf)
    v1 = jnp.max(f1, axis=1, keepdims=True)
    i1 = jnp.min(jnp.where(f1 == v1, lane, LANES), axis=1, keepdims=True)
    f2 = jnp.where(lane == i1, neg_inf, f1)
    v2 = jnp.max(f2, axis=1, keepdims=True)
    i2 = jnp.min(jnp.where(f2 == v2, lane, LANES), axis=1, keepdims=True)
    e2 = jnp.exp(v2 - v1)
    w1 = g_val / (1.0 + e2)
    w2 = g_val * e2 / (1.0 + e2)

    oh1 = lane == i1
    oh2 = lane == i2
    c = (oh1 | oh2).astype(BF16)
    rr = lax.broadcasted_iota(I32, (tm, tm), 0)
    cc = lax.broadcasted_iota(I32, (tm, tm), 1)
    before = (cc < rr).astype(BF16)
    prior = _dot(before, c) + run[...]
    rank1 = jnp.sum(jnp.where(oh1, prior, 0.0), axis=1, keepdims=True)
    rank2 = jnp.sum(jnp.where(oh2, prior, 0.0), axis=1, keepdims=True)
    run[...] = run[...] + jnp.sum(c.astype(F32), axis=0, keepdims=True)
    counts_ref[...] = jnp.broadcast_to(run[...], counts_ref.shape)

    rec = jnp.zeros((tm, LANES), F32)
    for ln, val in ((_R_E1, (i1 - _R_LOGIT0).astype(F32)), (_R_E2, (i2 - _R_LOGIT0).astype(F32)),
                    (_R_W1, w1), (_R_W2, w2), (_R_RANK1, rank1), (_R_RANK2, rank2)):
        rec = jnp.where(lane == ln, val, rec)
    route_ref[...] = rec


def _out_proj(m, xc, w_out, g2, wr_hi, wr_lo, b_r):
    n = xc.shape[0]
    tm = TOK_TILE
    tok = lambda w: pl.BlockSpec((tm, w), lambda i: (i, 0))
    const = lambda s: pl.BlockSpec(s, lambda i: (0,) * len(s))
    return pl.pallas_call(
        _outproj_kernel,
        grid=(n // tm,),
        in_specs=[tok(D_MODEL), tok(D_MODEL),
                  pl.BlockSpec((D_MODEL, D_MODEL), lambda i: (0, 0), pipeline_mode=pl.Buffered(1)),
                  const((1, D_MODEL)), const((D_MODEL, LANES)), const((D_MODEL, LANES)),
                  const((1, LANES))],
        out_specs=(tok(D_MODEL), tok(D_MODEL), tok(LANES), const((8, LANES))),
        out_shape=(jax.ShapeDtypeStruct((n, D_MODEL), F32),
                   jax.ShapeDtypeStruct((n, D_MODEL), F32),
                   jax.ShapeDtypeStruct((n, LANES), F32),
                   jax.ShapeDtypeStruct((8, LANES), F32)),
        scratch_shapes=[pltpu.VMEM((1, LANES), F32)],
        compiler_params=_cparams(1),
        name="out_proj",
    )(m, xc, w_out, g2, wr_hi, wr_lo, b_r)


DISPATCH_TOKENS = 256


def _dispatch_kernel(pos_ref, h2_ref, xs_in_ref, xs_ref, sem):
    del xs_in_ref
    i = pl.program_id(0)
    nsteps = pl.num_programs(0)
    tb = DISPATCH_TOKENS

    def row_copy(t, dst):
        return pltpu.make_async_copy(h2_ref.at[pl.ds(t, 1)], xs_ref.at[pl.ds(dst, 1)], sem)

    def issue(r, carry):
        t = i * tb + r
        row_copy(t, pos_ref[2 * r]).start()
        row_copy(t, pos_ref[2 * r + 1]).start()
        return carry

    lax.fori_loop(0, tb, issue, 0)

    def drain(r, carry):
        row_copy(0, 0).wait()
        return carry

    @pl.when(i > 0)
    def _():
        lax.fori_loop(0, 2 * tb, drain, 0)

    @pl.when(i == nsteps - 1)
    def _():
        lax.fori_loop(0, 2 * tb, drain, 0)


def _dispatch(pos_flat, h2, n_rows):
    n = h2.shape[0]
    tb = DISPATCH_TOKENS
    xs0 = jnp.zeros((n_rows, D_MODEL), F32)
    return pl.pallas_call(
        _dispatch_kernel,
        grid=(n // tb,),
        in_specs=[pl.BlockSpec((2 * tb,), lambda i: (i,), memory_space=pltpu.SMEM),
                  pl.BlockSpec(memory_space=pl.ANY),
                  pl.BlockSpec(memory_space=pl.ANY)],
        out_specs=pl.BlockSpec(memory_space=pl.ANY),
        out_shape=jax.ShapeDtypeStruct((n_rows, D_MODEL), F32),
        scratch_shapes=[pltpu.SemaphoreType.DMA(())],
        input_output_aliases={2: 0},
        compiler_params=_cparams(1),
        name="dispatch",
    )(pos_flat, h2, xs0)


def _experts_kernel(te_ref, nu_ref, xs_ref, wg_ref, wu_ref, wd_ref, ys_ref, wg16, wu16, wd16):
    i = pl.program_id(0)
    prev = te_ref[jnp.maximum(i - 1, 0)]
    fresh = (i == 0) | (te_ref[i] != prev)
    used = i < nu_ref[0]

    @pl.when(fresh & used)
    def _():
        wg16[...] = wg_ref[0].astype(BF16)
        wu16[...] = wu_ref[0].astype(BF16)
        wd16[...] = wd_ref[0].astype(BF16)

    @pl.when(used)
    def _():
        x = xs_ref[...].astype(BF16)
        a = _dot(x, wg16[...])
        u = _dot(x, wu16[...])
        act = (a * jax.nn.sigmoid(a)) * u
        ys_ref[...] = _dot(act.astype(BF16), wd16[...])

    @pl.when(jnp.logical_not(used))
    def _():
        ys_ref[...] = jnp.zeros(ys_ref.shape, F32)


def _experts(tile_expert, n_used, xs, w_g, w_u, w_d):
    n_rows = xs.shape[0]
    tm = EXP_TILE
    grid_spec = pltpu.PrefetchScalarGridSpec(
        num_scalar_prefetch=2,
        grid=(n_rows // tm,),
        in_specs=[pl.BlockSpec((tm, D_MODEL), lambda i, te, nu: (i, 0)),
                  pl.BlockSpec((1, D_MODEL, D_EXPERT), lambda i, te, nu: (te[i], 0, 0)),
                  pl.BlockSpec((1, D_MODEL, D_EXPERT), lambda i, te, nu: (te[i], 0, 0)),
                  pl.BlockSpec((1, D_EXPERT, D_MODEL), lambda i, te, nu: (te[i], 0, 0))],
        out_specs=pl.BlockSpec((tm, D_MODEL), lambda i, te, nu: (i, 0)),
        scratch_shapes=[pltpu.VMEM((D_MODEL, D_EXPERT), BF16),
                        pltpu.VMEM((D_MODEL, D_EXPERT), BF16),
                        pltpu.VMEM((D_EXPERT, D_MODEL), BF16)],
    )
    return pl.pallas_call(
        _experts_kernel,
        grid_spec=grid_spec,
        out_shape=jax.ShapeDtypeStruct((n_rows, D_MODEL), F32),
        compiler_params=_cparams(1),
        name="experts",
    )(tile_expert, n_used, xs, w_g, w_u, w_d)


COMBINE_TOKENS = 256


def _combine_kernel(pos_ref, x1_ref, route_ref, gf_ref, ys_ref, y_ref, ybuf, sem):
    tc = COMBINE_TOKENS

    def row_copy(r, slot):
        return pltpu.make_async_copy(ys_ref.at[pl.ds(pos_ref[2 * r + slot], 1)],
                                     ybuf.at[slot, pl.ds(r, 1)], sem)

    def issue(r, carry):
        row_copy(r, 0).start()
        row_copy(r, 1).start()
        return carry

    lax.fori_loop(0, tc, issue, 0)

    def drain(r, carry):
        row_copy(r, 0).wait()
        row_copy(r, 1).wait()
        return carry

    lax.fori_loop(0, tc, drain, 0)

    route = route_ref[...]
    w1 = route[:, _R_W1:_R_W1 + 1]
    w2 = route[:, _R_W2:_R_W2 + 1]
    x2 = x1_ref[...] + (w1 * ybuf[0] + w2 * ybuf[1])
    r = lax.rsqrt(jnp.mean(x2 * x2, axis=-1, keepdims=True) + EPS)
    y_ref[...] = (x2 * r) * gf_ref[...]


def _combine(pos_flat, x1, route, gf, ys):
    n = x1.shape[0]
    tc = COMBINE_TOKENS
    return pl.pallas_call(
        _combine_kernel,
        grid=(n // tc,),
        in_specs=[pl.BlockSpec((2 * tc,), lambda i: (i,), memory_space=pltpu.SMEM),
                  pl.BlockSpec((tc, D_MODEL), lambda i: (i, 0)),
                  pl.BlockSpec((tc, LANES), lambda i: (i, 0)),
                  pl.BlockSpec((1, D_MODEL), lambda i: (0, 0)),
                  pl.BlockSpec(memory_space=pl.ANY)],
        out_specs=pl.BlockSpec((tc, D_MODEL), lambda i: (i, 0)),
        out_shape=jax.ShapeDtypeStruct((n, D_MODEL), F32),
        scratch_shapes=[pltpu.VMEM((2, tc, D_MODEL), F32), pltpu.SemaphoreType.DMA(())],
        compiler_params=_cparams(1),
        name="combine",
    )(pos_flat, x1, route, gf, ys)


def _pack_w_in(w_in):
    o_q = 0
    o_k = o_q + ATTN_WIDTH
    o_v = o_k + KV_WIDTH
    o_qi = o_v + KV_WIDTH
    o_ki = o_qi + IDX_HEADS * IDX_DIM
    o_wi = o_ki + IDX_DIM
    o_p = o_wi + IDX_HEADS
    pad = jnp.zeros((w_in.shape[0], LANES - IDX_DIM - IDX_HEADS), w_in.dtype)
    return jnp.concatenate([w_in[:, :o_p], pad, w_in[:, o_p:]], axis=1).astype(BF16)


def _forward(x_prompt, x_sample, cache_k, cache_v, cache_kidx, state_pool,
             norm1_g, w_in, w_gate, b_gate, w_pool_grp, pool_scale, w_o_attn, w_o_pool, w_out,
             norm2_g, w_rg, b_rg, w_re, b_re, w_eg, w_eu, w_ed, norm_f_g):
    batch, seq, d = x_prompt.shape
    dec_batch, dec_seq, _ = x_sample.shape
    past = cache_k.shape[1]
    n_p = batch * seq
    n_s = dec_batch * dec_seq
    n = n_p + n_s
    tm = TOK_TILE
    assert d == D_MODEL and seq % tm == 0 and n_s % tm == 0 and n % (2 * tm) == 0
    assert past % KEY_BLOCK == 0 and dec_seq >= POOL_STATE and dec_seq <= KEY_BLOCK

    xc = jnp.concatenate([x_prompt.reshape(n_p, d), x_sample.reshape(n_s, d)], axis=0)

    pos_tab = jnp.concatenate([jnp.arange(seq), past + jnp.arange(n_s) % dec_seq])
    tables = _rope_tables(pos_tab)
    h, q, k, v, qi, kw, pin = _in_proj(xc, norm1_g.reshape(1, d), _pack_w_in(w_in), tables,
                                       n_p // tm, seq // tm)

    oa_p = _dsa_prompt(q, qi, kw, k, v, 0, batch, seq)
    oa_s = _dsa_sample(q, qi, kw, k, v,
                       cache_k.reshape(dec_batch * past, KV_WIDTH),
                       cache_v.reshape(dec_batch * past, KV_WIDTH),
                       cache_kidx.reshape(dec_batch * past, IDX_DIM),
                       n_p, dec_batch, dec_seq, past)
    o_attn = jnp.concatenate([oa_p, oa_s], axis=0)

    w_grp16 = w_pool_grp.astype(BF16)
    scale = pool_scale.reshape(1, POOL_WIDTH)
    op_p = _pool(pin, w_grp16, scale, 0, batch, seq, 0)
    halo = jnp.concatenate([jnp.zeros((dec_batch, 1, POOL_WIDTH), F32), state_pool], axis=1)
    op_s = _pool(pin, w_grp16, scale, n_p // dec_seq, dec_batch, dec_seq, past, halo=halo)
    o_pool = jnp.concatenate([op_p, op_s], axis=0)

    m = _merge(h, o_attn, o_pool, w_gate.astype(BF16), b_gate.reshape(1, 2 * d),
               w_o_attn.astype(BF16), w_o_pool.astype(BF16))

    w_r = jnp.concatenate([w_rg, w_re, jnp.zeros((d, LANES - MOE_GROUPS - N_EXPERTS), F32)], axis=1)
    b_r = jnp.concatenate([b_rg, b_re, jnp.zeros((LANES - MOE_GROUPS - N_EXPERTS,), F32)]).reshape(1, LANES)
    wr_hi = w_r.astype(BF16)
    wr_lo = (w_r - wr_hi.astype(F32)).astype(BF16)
    x1, h2, route, counts = _out_proj(m, xc, w_out.astype(BF16), norm2_g.reshape(1, d), wr_hi, wr_lo, b_r)

    te = EXP_TILE
    n_tiles = (2 * n) // te + N_EXPERTS
    cnt = counts[0, _R_LOGIT0:_R_LOGIT0 + N_EXPERTS].astype(I32)
    tiles_per_e = (cnt + te - 1) // te
    tile_end = jnp.cumsum(tiles_per_e)
    row_start = (tile_end - tiles_per_e) * te
    n_used = tile_end[-1:].astype(I32)
    tile_expert = jnp.minimum(
        jnp.searchsorted(tile_end, jnp.arange(n_tiles, dtype=I32), side="right"), N_EXPERTS - 1
    ).astype(I32)
    last_e = tile_expert[jnp.maximum(n_used[0] - 1, 0)]
    tile_expert = jnp.where(jnp.arange(n_tiles) < n_used[0], tile_expert, last_e)
    eid = route[:, _R_E1:_R_E2 + 1].astype(I32)
    rank = route[:, _R_RANK1:_R_RANK2 + 1].astype(I32)
    pos_flat = (row_start[eid] + rank).reshape(2 * n)

    xs = _dispatch(pos_flat, h2, n_tiles * te)
    ys = _experts(tile_expert, n_used, xs, w_eg, w_eu, w_ed)
    y = _combine(pos_flat, x1, route, norm_f_g.reshape(1, d), ys)

    y_prompt = y[:n_p].reshape(batch, seq, d)
    y_sample = y[n_p:].reshape(dec_batch, dec_seq, d)
    k_p = k[:n_p].reshape(1, batch, seq, N_KV_HEADS, HEAD_DIM)
    v_p = v[:n_p].reshape(1, batch, seq, N_KV_HEADS, HEAD_DIM)
    ki_p = kw[:n_p, :IDX_DIM].reshape(1, batch, seq, IDX_DIM)
    pool_p = pin[:n_p].reshape(batch, seq, POOL_WIDTH)[None, :, seq - POOL_STATE:, :]
    k_s = k[n_p:].reshape(1, dec_batch, dec_seq, N_KV_HEADS, HEAD_DIM)
    v_s = v[n_p:].reshape(1, dec_batch, dec_seq, N_KV_HEADS, HEAD_DIM)
    ki_s = kw[n_p:, :IDX_DIM].reshape(1, dec_batch, dec_seq, IDX_DIM)
    pool_s = pin[n_p:].reshape(dec_batch, dec_seq, POOL_WIDTH)[None, :, dec_seq - POOL_STATE:, :]
    return (y_prompt, y_sample, k_p, v_p, ki_p, pool_p, k_s, v_s, ki_s, pool_s)


def kernel(x_prompt, x_sample, cache_k, cache_v, cache_kidx, state_pool, norm1_g, w_in, w_gate, b_gate,
           w_pool_grp, pool_scale, w_o_attn, w_o_pool, w_out, norm2_g, w_router_group, b_router_group,
           w_router_expert, b_router_expert, w_exp_gate, w_exp_up, w_exp_down, norm_f_g):
    assert cache_k.shape[0] == 1, "single-layer model"
    return _forward(x_prompt, x_sample, cache_k[0], cache_v[0], cache_kidx[0], state_pool[0],
                    norm1_g[0], w_in[0], w_gate[0], b_gate[0], w_pool_grp[0], pool_scale[0],
                    w_o_attn[0], w_o_pool[0], w_out[0], norm2_g[0], w_router_group[0],
                    b_router_group[0], w_router_expert[0], b_router_expert[0],
                    w_exp_gate[0], w_exp_up[0], w_exp_down[0], norm_f_g)
```

```python
import functools

import numpy as np
import jax
import jax.numpy as jnp
from jax import lax
from jax.experimental import pallas as pl
from jax.experimental.pallas import tpu as pltpu

F32 = jnp.float32
BF16 = jnp.bfloat16
I32 = jnp.int32

D_MODEL = 2048
CHUNK = 64
N_HEADS = 8
N_KV_HEADS = 2
HEAD_DIM = 128
GQA_GROUP = N_HEADS // N_KV_HEADS
ATTN_WIDTH = N_HEADS * HEAD_DIM
KV_WIDTH = N_KV_HEADS * HEAD_DIM
IDX_HEADS = 16
IDX_DIM = 64
IDX_PAIRS = IDX_HEADS // 2
TOPK_MAX = 256
POOL_WINDOWS = (2, 4, 8, 16)
POOL_GROUPS = 4
POOL_WIDTH = D_MODEL // 2
POOL_GROUP_DIM = POOL_WIDTH // POOL_GROUPS
POOL_STATE = 15
POOL_HALO = 16
MOE_GROUPS = 4
EXPERTS_PER_GROUP = 8
N_EXPERTS = MOE_GROUPS * EXPERTS_PER_GROUP
D_EXPERT = 512
ROPE_THETA = 10000.0
CHUNK_SHIFT = CHUNK.bit_length() - 1
GROUP_SHIFT = EXPERTS_PER_GROUP.bit_length() - 1
EPS = 1e-6

LANES = 128
INT_MIN = -(2 ** 31)
NEG_BIG = -1e30
VMEM_LIMIT = 56 * 1024 * 1024

KEY_BLOCK = 256
TOK_TILE = 256
EXP_TILE = 256


def _cparams(n_axes):
    return pltpu.CompilerParams(dimension_semantics=("arbitrary",) * n_axes,
                                vmem_limit_bytes=VMEM_LIMIT)


def _dot(a, b):
    return jnp.dot(a, b, preferred_element_type=F32)


def _dot_nt(a, b):
    return lax.dot_general(a, b, (((1,), (1,)), ((), ())), preferred_element_type=F32)


_C_Q = 0
_C_K = _C_Q + ATTN_WIDTH
_C_V = _C_K + KV_WIDTH
_C_QI = _C_V + KV_WIDTH
_C_KW = _C_QI + IDX_HEADS * IDX_DIM
_C_P = _C_KW + LANES
_C_END = _C_P + POOL_WIDTH


def _rope128(y, cos, sin_signed):
    return y * cos + pltpu.roll(y, HEAD_DIM // 2, 1) * sin_signed


def _rope64(y, cos, sin_a, sin_b):
    half = IDX_DIM // 2
    return y * cos + pltpu.roll(y, LANES - half, 1) * sin_a + pltpu.roll(y, half, 1) * sin_b


def _inproj_kernel(xp_ref, xs_ref, g_ref, w_ref, ck_ref, sk_ref, cq_ref, sq_ref, ci_ref, sia_ref, sib_ref,
                   cw_ref, swa_ref, swb_ref,
                   h_ref, q_ref, k_ref, v_ref, qi_ref, kw_ref, p_ref, *, n_prompt_tiles):
    x = jnp.where(pl.program_id(0) < n_prompt_tiles, xp_ref[...], xs_ref[...])
    r = lax.rsqrt(jnp.mean(x * x, axis=-1, keepdims=True) + EPS)
    h = ((x * r) * g_ref[...]).astype(BF16)
    h_ref[...] = h
    cq, sq = cq_ref[...], sq_ref[...]
    for c in range(ATTN_WIDTH // 256):
        y = _dot(h, w_ref[:, _C_Q + c * 256:_C_Q + (c + 1) * 256])
        for s in range(2):
            q_ref[:, c * 256 + s * LANES:c * 256 + (s + 1) * LANES] = _rope128(
                y[:, s * LANES:(s + 1) * LANES], cq, sq).astype(BF16)
    ck, sk = ck_ref[...], sk_ref[...]
    y = _dot(h, w_ref[:, _C_K:_C_K + KV_WIDTH])
    for s in range(N_KV_HEADS):
        k_ref[:, s * LANES:(s + 1) * LANES] = _rope128(y[:, s * LANES:(s + 1) * LANES], ck, sk)
    v_ref[...] = _dot(h, w_ref[:, _C_V:_C_V + KV_WIDTH])
    ci, sia, sib = ci_ref[...], sia_ref[...], sib_ref[...]
    for c in range(IDX_PAIRS // 2):
        y = _dot(h, w_ref[:, _C_QI + c * 256:_C_QI + (c + 1) * 256])
        for s in range(2):
            qi_ref[2 * c + s] = _rope64(y[:, s * LANES:(s + 1) * LANES], ci, sia, sib).astype(BF16)
    y = _dot(h, w_ref[:, _C_KW:_C_KW + LANES])
    kw_ref[...] = _rope64(y, cw_ref[...], swa_ref[...], swb_ref[...])
    for c in range(POOL_WIDTH // 256):
        p_ref[:, c * 256:(c + 1) * 256] = _dot(h, w_ref[:, _C_P + c * 256:_C_P + (c + 1) * 256])


def _rope_tables(pos):
    pos = pos.astype(F32)[:, None]

    def cs(dim):
        half = dim // 2
        inv = ROPE_THETA ** (-jnp.arange(half, dtype=F32) / half)
        ang = pos * inv[None, :]
        return jnp.cos(ang), jnp.sin(ang)

    c, s = cs(HEAD_DIM)
    ck = jnp.concatenate([c, c], axis=1)
    sk = jnp.concatenate([-s, s], axis=1)
    qs = HEAD_DIM ** -0.5
    c, s = cs(IDX_DIM)
    z = jnp.zeros_like(s)
    ci = jnp.concatenate([c, c, c, c], axis=1)
    sia = jnp.concatenate([-s, z, -s, z], axis=1)
    sib = jnp.concatenate([z, s, z, s], axis=1)
    iscale = IDX_DIM ** -0.5
    n = pos.shape[0]
    wscale = jnp.full((n, IDX_HEADS), IDX_HEADS ** -0.5, F32)
    pad = jnp.zeros((n, LANES - IDX_DIM - IDX_HEADS), F32)
    zz = jnp.zeros((n, LANES - IDX_DIM), F32)
    cw = jnp.concatenate([c, c, wscale, pad], axis=1)
    swa = jnp.concatenate([-s, z, zz], axis=1)
    swb = jnp.concatenate([z, s, zz], axis=1)
    return (ck, sk, ck * qs, sk * qs, ci * iscale, sia * iscale, sib * iscale, cw, swa, swb)


def _split_token_specs(tm, width, n_prompt_tiles):
    prompt = pl.BlockSpec((tm, width), lambda i: (jnp.minimum(i, n_prompt_tiles - 1), 0))
    sample = pl.BlockSpec((tm, width), lambda i: (jnp.maximum(i - n_prompt_tiles, 0), 0))
    return prompt, sample


def _in_proj(xp, xs, g1, w_pack, tables, tiles_per_seq):
    tm = TOK_TILE
    n_prompt_tiles = xp.shape[0] // tm
    n = xp.shape[0] + xs.shape[0]
    n_tiles = n // tm

    def tab_idx(i):
        return (jnp.where(i < n_prompt_tiles, i % tiles_per_seq, tiles_per_seq + i - n_prompt_tiles), 0)

    tok = lambda w: pl.BlockSpec((tm, w), lambda i: (i, 0))
    tab = pl.BlockSpec((tm, LANES), tab_idx)
    out_shape = (
        jax.ShapeDtypeStruct((n, D_MODEL), BF16),
        jax.ShapeDtypeStruct((n, ATTN_WIDTH), BF16),
        jax.ShapeDtypeStruct((n, KV_WIDTH), F32),
        jax.ShapeDtypeStruct((n, KV_WIDTH), F32),
        jax.ShapeDtypeStruct((IDX_PAIRS, n, LANES), BF16),
        jax.ShapeDtypeStruct((n, LANES), F32),
        jax.ShapeDtypeStruct((n, POOL_WIDTH), F32),
    )
    return pl.pallas_call(
        functools.partial(_inproj_kernel, n_prompt_tiles=n_prompt_tiles),
        grid=(n_tiles,),
        in_specs=[*_split_token_specs(tm, D_MODEL, n_prompt_tiles),
                  pl.BlockSpec((1, D_MODEL), lambda i: (0, 0)),
                  pl.BlockSpec((D_MODEL, _C_END), lambda i: (0, 0), pipeline_mode=pl.Buffered(1))]
                 + [tab] * 10,
        out_specs=(tok(D_MODEL), tok(ATTN_WIDTH), tok(KV_WIDTH), tok(KV_WIDTH),
                   pl.BlockSpec((IDX_PAIRS, tm, LANES), lambda i: (0, i, 0)),
                   tok(LANES), tok(POOL_WIDTH)),
        out_shape=out_shape,
        compiler_params=_cparams(1),
        name="in_proj",
    )(xp, xs, g1, w_pack, *tables)


def _sortable_key(score):
    bits = lax.bitcast_convert_type(score, I32)
    return bits ^ ((bits >> 31) & 0x7FFFFFFF)


def _kth_largest_key(topk, shape, n_key_rows, count_where):
    def count_ge(cand):
        return count_where(lambda keys, kb: keys >= jnp.broadcast_to(cand, keys.shape))

    c0 = count_ge(jnp.zeros(shape, I32))
    thr0 = jnp.where(c0 >= topk, 0, INT_MIN).astype(I32)
    cnt0 = jnp.where(c0 >= topk, c0, n_key_rows).astype(I32)

    def bit_step(i, carry):
        thr, cnt = carry
        cand = thr + (jnp.int32(1) << (30 - i))
        c = count_ge(cand)
        ok = c >= topk
        return jnp.where(ok, cand, thr), jnp.where(ok, c, cnt)

    thr, cnt = lax.fori_loop(0, 31, bit_step, (thr0, cnt0))
    has_k = thr > INT_MIN
    thr = jnp.where(has_k, thr, INT_MIN + 1)
    return thr, has_k & (cnt > topk)


def _drop_excess_ties(sc, n_blocks, topk, thr, need_tie, key_index, count_where):
    n_gt = count_where(lambda keys, kb: keys > jnp.broadcast_to(thr, keys.shape))
    n_keep = topk - n_gt

    def idx_step(i, bound):
        cand = bound + (jnp.int32(1) << (14 - i))
        c = count_where(lambda keys, kb: (keys == jnp.broadcast_to(thr, keys.shape))
                        & (key_index(kb) < jnp.broadcast_to(cand, keys.shape)))
        return jnp.where(c <= n_keep, cand, bound)

    bound = lax.fori_loop(0, 15, idx_step, jnp.zeros(thr.shape, I32))
    bound = jnp.where(need_tie, bound, jnp.int32(2 ** 30))

    def drop_block(kb, carry):
        keys = sc[kb]
        drop = (keys == jnp.broadcast_to(thr, keys.shape)) & (key_index(kb) >= jnp.broadcast_to(bound, keys.shape))
        sc[kb] = jnp.where(drop, INT_MIN, keys)
        return carry

    lax.fori_loop(0, n_blocks, drop_block, 0)


def _dsa_prompt_kernel(q_ref, qi_ref, wq_ref, kn_ref, vn_ref, kwn_ref, o_ref,
                       k16, vt16, kia, kib, sc, qt, m_s, l_s, acc_s, *, tq, seq, topk):
    qb = pl.program_id(1)
    n_blocks_all = seq // KEY_BLOCK

    @pl.when(qb == 0)
    def _():
        lane = lax.broadcasted_iota(I32, (seq, LANES), 1)
        kin = jnp.where(lane < IDX_DIM, kwn_ref[...], 0.0)
        k16[...] = kn_ref[...].astype(BF16)
        kia[...] = kin.astype(BF16)
        kib[...] = pltpu.roll(kin, IDX_DIM, 1).astype(BF16)
        for kb in range(n_blocks_all):
            vt16[kb] = vn_ref[kb * KEY_BLOCK:(kb + 1) * KEY_BLOCK, :].T.astype(BF16)

    n_blocks = qb // (KEY_BLOCK // tq) + 1

    qit = qi_ref[...].reshape(IDX_PAIRS * tq, LANES).astype(F32).T.astype(BF16)
    wt = wq_ref[...].T
    w_rows = [wt[IDX_DIM + j:IDX_DIM + j + 1, :] for j in range(IDX_HEADS)]
    for g in range(N_KV_HEADS):
        qg = jnp.concatenate([q_ref[:, (g * GQA_GROUP + j) * HEAD_DIM:(g * GQA_GROUP + j + 1) * HEAD_DIM]
                              for j in range(GQA_GROUP)], axis=0)
        qt[g] = qg.astype(F32).T.astype(BF16)

    key_row = lax.broadcasted_iota(I32, (KEY_BLOCK, tq), 0)
    q_pos = qb * tq + lax.broadcasted_iota(I32, (KEY_BLOCK, tq), 1)

    def score_block(kb, carry):
        r0 = pl.multiple_of(kb * KEY_BLOCK, KEY_BLOCK)
        se = _dot(kia[pl.ds(r0, KEY_BLOCK), :], qit)
        so = _dot(kib[pl.ds(r0, KEY_BLOCK), :], qit)
        score = jnp.zeros((KEY_BLOCK, tq), F32)
        for p in range(IDX_PAIRS):
            score = (score + jnp.maximum(se[:, p * tq:(p + 1) * tq], 0.0) * w_rows[2 * p]
                     + jnp.maximum(so[:, p * tq:(p + 1) * tq], 0.0) * w_rows[2 * p + 1])
        adm = ((kb * KEY_BLOCK + key_row) >> CHUNK_SHIFT) <= (q_pos >> CHUNK_SHIFT)
        sc[kb] = jnp.where(adm, _sortable_key(score), INT_MIN)
        return carry

    lax.fori_loop(0, n_blocks, score_block, 0)

    def count_where(pred):
        def body(kb, acc):
            m = pred(sc[kb], kb).astype(I32)
            return acc + jnp.sum(m.reshape(KEY_BLOCK // 8, 8, tq), axis=0)
        acc = lax.fori_loop(0, n_blocks, body, jnp.zeros((8, tq), I32))
        return jnp.sum(acc, axis=0, keepdims=True)

    thr, need_tie = _kth_largest_key(topk, (1, tq), seq, count_where)

    @pl.when(jnp.max(need_tie.astype(I32)) > 0)
    def _():
        _drop_excess_ties(sc, n_blocks, topk, thr, need_tie,
                          lambda kb: kb * KEY_BLOCK + key_row, count_where)

    m_s[...] = jnp.full(m_s.shape, NEG_BIG, F32)
    l_s[...] = jnp.zeros(l_s.shape, F32)
    acc_s[...] = jnp.zeros(acc_s.shape, F32)
    thr_b = jnp.broadcast_to(thr, (KEY_BLOCK, tq))

    def attn_block(kb, carry):
        r0 = pl.multiple_of(kb * KEY_BLOCK, KEY_BLOCK)
        bias = jnp.where(sc[kb] >= thr_b, 0.0, NEG_BIG)
        bias = jnp.concatenate([bias] * GQA_GROUP, axis=1)
        for g in range(N_KV_HEADS):
            kblk = k16[pl.ds(r0, KEY_BLOCK), g * HEAD_DIM:(g + 1) * HEAD_DIM]
            logits = _dot(kblk, qt[g]) + bias
            m_prev = m_s[g]
            m_new = jnp.maximum(m_prev, jnp.max(logits, axis=0, keepdims=True))
            alpha = jnp.exp(m_prev - m_new)
            p = jnp.exp(logits - m_new)
            l_s[g] = alpha * l_s[g] + jnp.sum(p, axis=0, keepdims=True)
            acc_s[g] = alpha * acc_s[g] + _dot(vt16[kb, g * HEAD_DIM:(g + 1) * HEAD_DIM, :], p.astype(BF16))
            m_s[g] = m_new
        return carry

    lax.fori_loop(0, n_blocks, attn_block, 0)

    for g in range(N_KV_HEADS):
        o = (acc_s[g] * (1.0 / l_s[g])).T
        for j in range(GQA_GROUP):
            hd = g * GQA_GROUP + j
            o_ref[:, hd * HEAD_DIM:(hd + 1) * HEAD_DIM] = o[j * tq:(j + 1) * tq, :].astype(BF16)


def _dsa_sample_kernel(q_ref, qi_ref, wq_ref, kn_ref, vn_ref, kwn_ref, kc_ref, vc_ref, kic_ref, o_ref,
                       k16, v16, kia, kib, wbe, wbo, sc, qst, m_s, l_s, acc_s, *,
                       tq, n_key_rows, n_cache, n_new, topk):
    n_blocks = n_key_rows // KEY_BLOCK

    lane = lax.broadcasted_iota(I32, (n_new, LANES), 1)
    kin = jnp.where(lane < IDX_DIM, kwn_ref[...], 0.0).astype(BF16)
    n_tail = n_key_rows - n_cache
    k16[0:n_cache, :] = kc_ref[...].astype(BF16)
    v16[0:n_cache, :] = vc_ref[...].astype(BF16)
    kic = jnp.concatenate([kic_ref[...], jnp.zeros((n_cache, LANES - IDX_DIM), F32)], axis=1)
    kia[0:n_cache, :] = kic.astype(BF16)
    kib[0:n_cache, :] = pltpu.roll(kic, IDX_DIM, 1).astype(BF16)
    k16[n_cache:n_key_rows, :] = jnp.zeros((n_tail, KV_WIDTH), BF16)
    v16[n_cache:n_key_rows, :] = jnp.zeros((n_tail, KV_WIDTH), BF16)
    kia[n_cache:n_key_rows, :] = jnp.zeros((n_tail, LANES), BF16)
    kib[n_cache:n_key_rows, :] = jnp.zeros((n_tail, LANES), BF16)
    k16[n_cache:n_cache + n_new, :] = kn_ref[...].astype(BF16)
    v16[n_cache:n_cache + n_new, :] = vn_ref[...].astype(BF16)
    kia[n_cache:n_cache + n_new, :] = kin
    kib[n_cache:n_cache + n_new, :] = pltpu.roll(kin.astype(F32), IDX_DIM, 1).astype(BF16)

    wq = wq_ref[...]
    for j in range(IDX_HEADS):
        col = jnp.broadcast_to(wq[:, IDX_DIM + j:IDX_DIM + j + 1], (tq, LANES))
        if j % 2 == 0:
            wbe[j // 2] = col
        else:
            wbo[j // 2] = col
    for g in range(N_KV_HEADS):
        for j in range(GQA_GROUP):
            hd = g * GQA_GROUP + j
            qst[g, j * tq:(j + 1) * tq, :] = q_ref[:, hd * HEAD_DIM:(hd + 1) * HEAD_DIM]

    lane = lax.broadcasted_iota(I32, (tq, KEY_BLOCK), 1)

    qi2 = qi_ref[...].reshape(IDX_PAIRS * tq, LANES)

    def score_block(kb, carry):
        r0 = pl.multiple_of(kb * KEY_BLOCK, KEY_BLOCK)
        se = _dot_nt(qi2, kia[pl.ds(r0, KEY_BLOCK), :]).reshape(IDX_PAIRS, tq, KEY_BLOCK)
        so = _dot_nt(qi2, kib[pl.ds(r0, KEY_BLOCK), :]).reshape(IDX_PAIRS, tq, KEY_BLOCK)
        score = jnp.zeros((tq, KEY_BLOCK), F32)
        for p in range(IDX_PAIRS):
            we = wbe[p]
            wo = wbo[p]
            we2 = jnp.concatenate([we, we], axis=1)
            wo2 = jnp.concatenate([wo, wo], axis=1)
            score = score + jnp.maximum(se[p], 0.0) * we2 + jnp.maximum(so[p], 0.0) * wo2
        adm = (kb * KEY_BLOCK + lane) < (n_cache + n_new)
        sc[kb] = jnp.where(adm, _sortable_key(score), INT_MIN)
        return carry

    lax.fori_loop(0, n_blocks, score_block, 0)

    def count_where(pred):
        def body(kb, acc):
            m = pred(sc[kb], kb).astype(I32)
            return acc + m[:, :LANES] + m[:, LANES:]
        acc = lax.fori_loop(0, n_blocks, body, jnp.zeros((tq, LANES), I32))
        return jnp.sum(acc, axis=1, keepdims=True)

    thr, need_tie = _kth_largest_key(topk, (tq, 1), n_key_rows, count_where)

    @pl.when(jnp.max(need_tie.astype(I32)) > 0)
    def _():
        _drop_excess_ties(sc, n_blocks, topk, thr, need_tie,
                          lambda kb: kb * KEY_BLOCK + lane, count_where)

    rows = GQA_GROUP * tq
    m_s[...] = jnp.full(m_s.shape, NEG_BIG, F32)
    l_s[...] = jnp.zeros(l_s.shape, F32)
    acc_s[...] = jnp.zeros(acc_s.shape, F32)
    thr_b = jnp.broadcast_to(thr, (tq, KEY_BLOCK))

    def attn_block(kb, carry):
        r0 = pl.multiple_of(kb * KEY_BLOCK, KEY_BLOCK)
        keep = sc[kb] >= thr_b
        for g in range(N_KV_HEADS):
            kblk = k16[pl.ds(r0, KEY_BLOCK), g * HEAD_DIM:(g + 1) * HEAD_DIM]
            vblk = v16[pl.ds(r0, KEY_BLOCK), g * HEAD_DIM:(g + 1) * HEAD_DIM]
            logits = _dot_nt(qst[g], kblk).reshape(GQA_GROUP, tq, KEY_BLOCK)
            logits = jnp.where(keep[None], logits, NEG_BIG).reshape(rows, KEY_BLOCK)
            m_prev = m_s[g]
            m_new = jnp.maximum(m_prev, jnp.max(logits, axis=1, keepdims=True))
            alpha = jnp.exp(m_prev - m_new)
            p = jnp.exp(logits - m_new)
            l_s[g] = alpha * l_s[g] + jnp.sum(p, axis=1, keepdims=True)
            acc_s[g] = alpha * acc_s[g] + _dot(p.astype(BF16), vblk)
            m_s[g] = m_new
        return carry

    lax.fori_loop(0, n_blocks, attn_block, 0)

    for g in range(N_KV_HEADS):
        o = acc_s[g] / l_s[g]
        for j in range(GQA_GROUP):
            hd = g * GQA_GROUP + j
            o_ref[:, hd * HEAD_DIM:(hd + 1) * HEAD_DIM] = o[j * tq:(j + 1) * tq, :].astype(BF16)


def _dsa_sample_scratch(tq, n_key_rows):
    rows = GQA_GROUP * tq
    return [
        pltpu.VMEM((n_key_rows, KV_WIDTH), BF16),
        pltpu.VMEM((n_key_rows, KV_WIDTH), BF16),
        pltpu.VMEM((n_key_rows, LANES), BF16),
        pltpu.VMEM((n_key_rows, LANES), BF16),
        pltpu.VMEM((IDX_PAIRS, tq, LANES), F32),
        pltpu.VMEM((IDX_PAIRS, tq, LANES), F32),
        pltpu.VMEM((n_key_rows // KEY_BLOCK, tq, KEY_BLOCK), I32),
        pltpu.VMEM((N_KV_HEADS, rows, HEAD_DIM), BF16),
        pltpu.VMEM((N_KV_HEADS, rows, 1), F32),
        pltpu.VMEM((N_KV_HEADS, rows, 1), F32),
        pltpu.VMEM((N_KV_HEADS, rows, HEAD_DIM), F32),
    ]


def _dsa_prompt(q, qi, kw, k, v, batch, seq):
    tq = 128
    nqb = seq // tq
    topk = min(TOPK_MAX, seq // 4)
    kern = functools.partial(_dsa_prompt_kernel, tq=tq, seq=seq, topk=topk)
    rows_q = GQA_GROUP * tq
    scratch = [
        pltpu.VMEM((seq, KV_WIDTH), BF16),
        pltpu.VMEM((seq // KEY_BLOCK, KV_WIDTH, KEY_BLOCK), BF16),
        pltpu.VMEM((seq, LANES), BF16),
        pltpu.VMEM((seq, LANES), BF16),
        pltpu.VMEM((seq // KEY_BLOCK, KEY_BLOCK, tq), I32),
        pltpu.VMEM((N_KV_HEADS, HEAD_DIM, rows_q), BF16),
        pltpu.VMEM((N_KV_HEADS, 1, rows_q), F32),
        pltpu.VMEM((N_KV_HEADS, 1, rows_q), F32),
        pltpu.VMEM((N_KV_HEADS, HEAD_DIM, rows_q), F32),
    ]
    qmap = lambda b, i: (b * nqb + i, 0)
    smap = lambda b, i: (b, 0)
    return pl.pallas_call(
        kern,
        grid=(batch, nqb),
        in_specs=[pl.BlockSpec((tq, ATTN_WIDTH), qmap),
                  pl.BlockSpec((IDX_PAIRS, tq, LANES), lambda b, i: (0, b * nqb + i, 0)),
                  pl.BlockSpec((tq, LANES), qmap),
                  pl.BlockSpec((seq, KV_WIDTH), smap),
                  pl.BlockSpec((seq, KV_WIDTH), smap),
                  pl.BlockSpec((seq, LANES), smap)],
        out_specs=pl.BlockSpec((tq, ATTN_WIDTH), qmap),
        out_shape=jax.ShapeDtypeStruct((batch * seq, ATTN_WIDTH), BF16),
        scratch_shapes=scratch,
        compiler_params=_cparams(2),
        name="dsa_prompt",
    )(q, qi, kw, k, v, kw)


def _dsa_sample(q, qi, kw, k, v, cache_k, cache_v, cache_ki, row0, batch, seq, past):
    tq = seq
    n_key_rows = past + KEY_BLOCK
    topk = min(TOPK_MAX, (past + seq) // 4)
    kern = functools.partial(_dsa_sample_kernel, tq=tq, n_key_rows=n_key_rows, n_cache=past, n_new=seq,
                             topk=topk)
    t0 = row0 // seq
    qmap = lambda b, i: (t0 + b, 0)
    cmap = lambda b, i: (b, 0)
    return pl.pallas_call(
        kern,
        grid=(batch, 1),
        in_specs=[pl.BlockSpec((tq, ATTN_WIDTH), qmap),
                  pl.BlockSpec((IDX_PAIRS, tq, LANES), lambda b, i: (0, t0 + b, 0)),
                  pl.BlockSpec((tq, LANES), qmap),
                  pl.BlockSpec((tq, KV_WIDTH), qmap),
                  pl.BlockSpec((tq, KV_WIDTH), qmap),
                  pl.BlockSpec((tq, LANES), qmap),
                  pl.BlockSpec((past, KV_WIDTH), cmap),
                  pl.BlockSpec((past, KV_WIDTH), cmap),
                  pl.BlockSpec((past, IDX_DIM), cmap)],
        out_specs=pl.BlockSpec((tq, ATTN_WIDTH), cmap),
        out_shape=jax.ShapeDtypeStruct((batch * seq, ATTN_WIDTH), BF16),
        scratch_shapes=_dsa_sample_scratch(tq, n_key_rows),
        compiler_params=_cparams(2),
        name="dsa_sample",
    )(q, qi, kw, k, v, kw, cache_k, cache_v, cache_ki)


def _pool_kernel(*refs, tp, pos0, carried):
    if carried:
        p_ref, w_ref, s_ref, o_ref, ext, save = refs
    else:
        p_ref, halo_ref, w_ref, s_ref, o_ref, ext = refs
    i = pl.program_id(1)
    if carried:
        @pl.when(i == 0)
        def _():
            ext[0:POOL_HALO, :] = jnp.zeros((POOL_HALO, POOL_WIDTH), F32)

        @pl.when(i > 0)
        def _():
            ext[0:POOL_HALO, :] = save[...]
    else:
        ext[0:POOL_HALO, :] = halo_ref[0]
    x = p_ref[...]
    ext[POOL_HALO:POOL_HALO + tp, :] = x
    if carried:
        save[...] = x[tp - POOL_HALO:tp, :]
    pos = pos0 + i * tp + lax.broadcasted_iota(I32, (tp, 1), 0)
    for g, w in enumerate(POOL_WINDOWS):
        c0, c1 = g * POOL_GROUP_DIM, (g + 1) * POOL_GROUP_DIM
        xg = x[:, c0:c1]
        tot = xg
        for s in range(1, w):
            tot = tot + ext[POOL_HALO - s:POOL_HALO - s + tp, c0:c1]
        inv = 1.0 / jnp.minimum(pos + 1, w).astype(F32)
        d = (tot * inv - xg).astype(BF16)
        o_ref[:, c0:c1] = (_dot(d, w_ref[g]) * s_ref[:, c0:c1]).astype(BF16)


def _pool(pin, w_grp, scale, tile0, batch, seq, pos0, halo=None):
    carried = halo is None
    tp = min(seq, 256)
    nt = seq // tp
    kern = functools.partial(_pool_kernel, tp=tp, pos0=pos0, carried=carried)
    in_specs = [pl.BlockSpec((tp, POOL_WIDTH), lambda b, i: (tile0 + b * nt + i, 0))]
    args = [pin]
    if not carried:
        in_specs.append(pl.BlockSpec((1, POOL_HALO, POOL_WIDTH), lambda b, i: (b, 0, 0)))
        args.append(halo)
    in_specs += [pl.BlockSpec((POOL_GROUPS, POOL_GROUP_DIM, POOL_GROUP_DIM), lambda b, i: (0, 0, 0)),
                 pl.BlockSpec((1, POOL_WIDTH), lambda b, i: (0, 0))]
    args += [w_grp, scale]
    scratch = [pltpu.VMEM((POOL_HALO + tp, POOL_WIDTH), F32)]
    if carried:
        scratch.append(pltpu.VMEM((POOL_HALO, POOL_WIDTH), F32))
    return pl.pallas_call(
        kern,
        grid=(batch, nt),
        in_specs=in_specs,
        out_specs=pl.BlockSpec((tp, POOL_WIDTH), lambda b, i: (b * nt + i, 0)),
        out_shape=jax.ShapeDtypeStruct((batch * seq, POOL_WIDTH), BF16),
        scratch_shapes=scratch,
        compiler_params=_cparams(2),
        name="pool_prompt" if carried else "pool_sample",
    )(*args)


def _merge_kernel(h_ref, oap_ref, oas_ref, opp_ref, ops_ref, wga_ref, wgp_ref, ba_ref, bp_ref, woa_ref,
                  wop_ref, m_ref, *, n_prompt_tiles):
    h = h_ref[...]
    is_prompt = pl.program_id(1) < n_prompt_tiles
    oa = jnp.where(is_prompt, oap_ref[...], oas_ref[...])
    op = jnp.where(is_prompt, opp_ref[...], ops_ref[...])
    ga = jax.nn.sigmoid(_dot(h, wga_ref[...]) + ba_ref[...])
    gp = jax.nn.sigmoid(_dot(h, wgp_ref[...]) + bp_ref[...])
    a = _dot(oa, woa_ref[...])
    p = _dot(op, wop_ref[...])
    m_ref[...] = (ga * a + gp * p).astype(BF16)


def _merge(h, oa_p, oa_s, op_p, op_s, w_gate, b_gate, w_oa, w_op):
    n = h.shape[0]
    tm = 2 * TOK_TILE
    tn = D_MODEL // 2
    nj = D_MODEL // tn
    npt = oa_p.shape[0] // tm
    prompt = lambda w: pl.BlockSpec((tm, w), lambda j, i: (jnp.minimum(i, npt - 1), 0))
    sample = lambda w: pl.BlockSpec((tm, w), lambda j, i: (jnp.maximum(i - npt, 0), 0))
    return pl.pallas_call(
        functools.partial(_merge_kernel, n_prompt_tiles=npt),
        grid=(nj, n // tm),
        in_specs=[pl.BlockSpec((tm, D_MODEL), lambda j, i: (i, 0)),
                  prompt(ATTN_WIDTH), sample(ATTN_WIDTH), prompt(POOL_WIDTH), sample(POOL_WIDTH),
                  pl.BlockSpec((D_MODEL, tn), lambda j, i: (0, j)),
                  pl.BlockSpec((D_MODEL, tn), lambda j, i: (0, nj + j)),
                  pl.BlockSpec((1, tn), lambda j, i: (0, j)),
                  pl.BlockSpec((1, tn), lambda j, i: (0, nj + j)),
                  pl.BlockSpec((ATTN_WIDTH, tn), lambda j, i: (0, j)),
                  pl.BlockSpec((POOL_WIDTH, tn), lambda j, i: (0, j))],
        out_specs=pl.BlockSpec((tm, tn), lambda j, i: (i, j)),
        out_shape=jax.ShapeDtypeStruct((n, D_MODEL), BF16),
        compiler_params=_cparams(2),
        name="merge",
    )(h, oa_p, oa_s, op_p, op_s, w_gate, w_gate, b_gate, b_gate, w_oa, w_op)


_R_E1, _R_E2, _R_W1, _R_W2, _R_RANK1, _R_RANK2 = range(6)
_R_LOGIT0 = MOE_GROUPS


def _outproj_kernel(m_ref, xp_ref, xs_ref, wo_ref, g2_ref, wrh_ref, wrl_ref, br_ref,
                    x1_ref, h2_ref, route_ref, counts_ref, run, *, n_prompt_tiles):
    i = pl.program_id(0)
    tm = m_ref.shape[0]

    @pl.when(i == 0)
    def _():
        run[...] = jnp.zeros(run.shape, F32)

    x = jnp.where(i < n_prompt_tiles, xp_ref[...], xs_ref[...])
    x1 = x + _dot(m_ref[...], wo_ref[...])
    x1_ref[...] = x1
    r = lax.rsqrt(jnp.mean(x1 * x1, axis=-1, keepdims=True) + EPS)
    hf = (x1 * r) * g2_ref[...]
    h2_ref[...] = hf

    hi = hf.astype(BF16)
    lo = (hf - hi.astype(F32)).astype(BF16)
    lg = _dot(hi, wrh_ref[...]) + (_dot(hi, wrl_ref[...]) + _dot(lo, wrh_ref[...])) + br_ref[...]

    lane = lax.broadcasted_iota(I32, (tm, LANES), 1)
    neg_inf = jnp.float32(-jnp.inf)
    is_g = lane < MOE_GROUPS
    cl = jnp.where(is_g, lg, neg_inf)
    gmax = jnp.max(cl, axis=1, keepdims=True)
    g_sel = jnp.min(jnp.where(cl == gmax, lane, LANES), axis=1, keepdims=True)
    den = jnp.sum(jnp.where(is_g, jnp.exp(cl - gmax), 0.0), axis=1, keepdims=True)
    g_val = 1.0 / den
    e_lane = lane - _R_LOGIT0
    in_grp = (e_lane >= 0) & (e_lane < N_EXPERTS) & ((e_lane >> GROUP_SHIFT) == g_sel)
    f1 = jnp.where(in_grp, lg, neg_inf)
    v1 = jnp.max(f1, axis=1, keepdims=True)
    i1 = jnp.min(jnp.where(f1 == v1, lane, LANES), axis=1, keepdims=True)
    f2 = jnp.where(lane == i1, neg_inf, f1)
    v2 = jnp.max(f2, axis=1, keepdims=True)
    i2 = jnp.min(jnp.where(f2 == v2, lane, LANES), axis=1, keepdims=True)
    e2 = jnp.exp(v2 - v1)
    w1 = g_val / (1.0 + e2)
    w2 = g_val * e2 / (1.0 + e2)

    oh1 = lane == i1
    oh2 = lane == i2
    c = (oh1 | oh2).astype(BF16)
    rr = lax.broadcasted_iota(I32, (tm, tm), 0)
    cc = lax.broadcasted_iota(I32, (tm, tm), 1)
    before = (cc < rr).astype(BF16)
    prior = _dot(before, c) + run[...]
    rank1 = jnp.sum(jnp.where(oh1, prior, 0.0), axis=1, keepdims=True)
    rank2 = jnp.sum(jnp.where(oh2, prior, 0.0), axis=1, keepdims=True)
    run[...] = run[...] + jnp.sum(c.astype(F32), axis=0, keepdims=True)
    counts_ref[...] = jnp.broadcast_to(run[...], counts_ref.shape)

    rec = jnp.zeros((tm, LANES), F32)
    for ln, val in ((_R_E1, (i1 - _R_LOGIT0).astype(F32)), (_R_E2, (i2 - _R_LOGIT0).astype(F32)),
                    (_R_W1, w1), (_R_W2, w2), (_R_RANK1, rank1), (_R_RANK2, rank2)):
        rec = jnp.where(lane == ln, val, rec)
    route_ref[...] = rec


def _out_proj(m, xp, xs, w_out, g2, wr_hi, wr_lo, b_r):
    n = m.shape[0]
    tm = TOK_TILE
    n_prompt_tiles = xp.shape[0] // tm
    tok = lambda w: pl.BlockSpec((tm, w), lambda i: (i, 0))
    const = lambda s: pl.BlockSpec(s, lambda i: (0,) * len(s))
    return pl.pallas_call(
        functools.partial(_outproj_kernel, n_prompt_tiles=n_prompt_tiles),
        grid=(n // tm,),
        in_specs=[tok(D_MODEL), *_split_token_specs(tm, D_MODEL, n_prompt_tiles),
                  pl.BlockSpec((D_MODEL, D_MODEL), lambda i: (0, 0), pipeline_mode=pl.Buffered(1)),
                  const((1, D_MODEL)), const((D_MODEL, LANES)), const((D_MODEL, LANES)),
                  const((1, LANES))],
        out_specs=(tok(D_MODEL), tok(D_MODEL), tok(LANES), const((8, LANES))),
        out_shape=(jax.ShapeDtypeStruct((n, D_MODEL), F32),
                   jax.ShapeDtypeStruct((n, D_MODEL), F32),
                   jax.ShapeDtypeStruct((n, LANES), F32),
                   jax.ShapeDtypeStruct((8, LANES), F32)),
        scratch_shapes=[pltpu.VMEM((1, LANES), F32)],
        compiler_params=_cparams(1),
        name="out_proj",
    )(m, xp, xs, w_out, g2, wr_hi, wr_lo, b_r)


DISPATCH_TOKENS = 256


def _dispatch_kernel(pos_ref, h2_ref, xs_in_ref, xs_ref, sem):
    del xs_in_ref
    tb = DISPATCH_TOKENS

    def row_copy(r, slot):
        return pltpu.make_async_copy(h2_ref.at[pl.ds(r, 1)],
                                     xs_ref.at[pl.ds(pos_ref[2 * r + slot], 1)], sem)

    def issue(r, carry):
        row_copy(r, 0).start()
        row_copy(r, 1).start()
        return carry

    lax.fori_loop(0, tb, issue, 0, unroll=8)

    def drain(r, carry):
        row_copy(r, 0).wait()
        row_copy(r, 1).wait()
        return carry

    lax.fori_loop(0, tb, drain, 0, unroll=8)


def _dispatch(pos_flat, h2, n_rows):
    n = h2.shape[0]
    tb = DISPATCH_TOKENS
    xs0 = jnp.zeros((n_rows, D_MODEL), F32)
    return pl.pallas_call(
        _dispatch_kernel,
        grid=(n // tb,),
        in_specs=[pl.BlockSpec((2 * tb,), lambda i: (i,), memory_space=pltpu.SMEM),
                  pl.BlockSpec((tb, D_MODEL), lambda i: (i, 0)),
                  pl.BlockSpec(memory_space=pl.ANY)],
        out_specs=pl.BlockSpec(memory_space=pl.ANY),
        out_shape=jax.ShapeDtypeStruct((n_rows, D_MODEL), F32),
        scratch_shapes=[pltpu.SemaphoreType.DMA(())],
        input_output_aliases={2: 0},
        compiler_params=_cparams(1),
        name="dispatch",
    )(pos_flat, h2, xs0)


def _experts_kernel(te_ref, nu_ref, xs_ref, wg_ref, wu_ref, wd_ref, ys_ref, wg16, wu16, wd16):
    i = pl.program_id(0)
    prev = te_ref[jnp.maximum(i - 1, 0)]
    fresh = (i == 0) | (te_ref[i] != prev)
    used = i < nu_ref[0]

    @pl.when(fresh & used)
    def _():
        wg16[...] = wg_ref[0].astype(BF16)
        wu16[...] = wu_ref[0].astype(BF16)
        wd16[...] = wd_ref[0].astype(BF16)

    @pl.when(used)
    def _():
        x = xs_ref[...].astype(BF16)
        a = _dot(x, wg16[...])
        u = _dot(x, wu16[...])
        act = (a * jax.nn.sigmoid(a)) * u
        ys_ref[...] = _dot(act.astype(BF16), wd16[...])

    @pl.when(jnp.logical_not(used))
    def _():
        ys_ref[...] = jnp.zeros(ys_ref.shape, F32)


def _experts(tile_expert, n_used, xs, w_g, w_u, w_d):
    n_rows = xs.shape[0]
    tm = EXP_TILE
    grid_spec = pltpu.PrefetchScalarGridSpec(
        num_scalar_prefetch=2,
        grid=(n_rows // tm,),
        in_specs=[pl.BlockSpec((tm, D_MODEL), lambda i, te, nu: (i, 0)),
                  pl.BlockSpec((1, D_MODEL, D_EXPERT), lambda i, te, nu: (te[i], 0, 0)),
                  pl.BlockSpec((1, D_MODEL, D_EXPERT), lambda i, te, nu: (te[i], 0, 0)),
                  pl.BlockSpec((1, D_EXPERT, D_MODEL), lambda i, te, nu: (te[i], 0, 0))],
        out_specs=pl.BlockSpec((tm, D_MODEL), lambda i, te, nu: (i, 0)),
        scratch_shapes=[pltpu.VMEM((D_MODEL, D_EXPERT), BF16),
                        pltpu.VMEM((D_MODEL, D_EXPERT), BF16),
                        pltpu.VMEM((D_EXPERT, D_MODEL), BF16)],
    )
    return pl.pallas_call(
        _experts_kernel,
        grid_spec=grid_spec,
        out_shape=jax.ShapeDtypeStruct((n_rows, D_MODEL), F32),
        compiler_params=_cparams(1),
        name="experts",
    )(tile_expert, n_used, xs, w_g, w_u, w_d)


COMBINE_TOKENS = 256


def _combine_kernel(pos_ref, x1_ref, route_ref, gf_ref, ys_ref, yp_ref, ysm_ref, ybuf, sem, *,
                    n_prompt_tiles):
    tc = COMBINE_TOKENS

    def row_copy(r, slot):
        return pltpu.make_async_copy(ys_ref.at[pl.ds(pos_ref[2 * r + slot], 1)],
                                     ybuf.at[slot, pl.ds(r, 1)], sem)

    def issue(r, carry):
        row_copy(r, 0).start()
        row_copy(r, 1).start()
        return carry

    lax.fori_loop(0, tc, issue, 0)

    def drain(r, carry):
        row_copy(r, 0).wait()
        row_copy(r, 1).wait()
        return carry

    lax.fori_loop(0, tc, drain, 0)

    route = route_ref[...]
    w1 = route[:, _R_W1:_R_W1 + 1]
    w2 = route[:, _R_W2:_R_W2 + 1]
    x2 = x1_ref[...] + (w1 * ybuf[0] + w2 * ybuf[1])
    r = lax.rsqrt(jnp.mean(x2 * x2, axis=-1, keepdims=True) + EPS)
    y = (x2 * r) * gf_ref[...]
    is_prompt = pl.program_id(0) < n_prompt_tiles

    @pl.when(is_prompt)
    def _():
        yp_ref[...] = y

    @pl.when(jnp.logical_not(is_prompt))
    def _():
        ysm_ref[...] = y


def _combine(pos_flat, x1, route, gf, ys, n_prompt):
    n = x1.shape[0]
    tc = COMBINE_TOKENS
    n_prompt_tiles = n_prompt // tc
    return pl.pallas_call(
        functools.partial(_combine_kernel, n_prompt_tiles=n_prompt_tiles),
        grid=(n // tc,),
        in_specs=[pl.BlockSpec((2 * tc,), lambda i: (i,), memory_space=pltpu.SMEM),
                  pl.BlockSpec((tc, D_MODEL), lambda i: (i, 0)),
                  pl.BlockSpec((tc, LANES), lambda i: (i, 0)),
                  pl.BlockSpec((1, D_MODEL), lambda i: (0, 0)),
                  pl.BlockSpec(memory_space=pl.ANY)],
        out_specs=_split_token_specs(tc, D_MODEL, n_prompt_tiles),
        out_shape=(jax.ShapeDtypeStruct((n_prompt, D_MODEL), F32),
                   jax.ShapeDtypeStruct((n - n_prompt, D_MODEL), F32)),
        scratch_shapes=[pltpu.VMEM((2, tc, D_MODEL), F32), pltpu.SemaphoreType.DMA(())],
        compiler_params=_cparams(1),
        name="combine",
    )(pos_flat, x1, route, gf, ys)


def _pack_w_in(w_in):
    o_q = 0
    o_k = o_q + ATTN_WIDTH
    o_v = o_k + KV_WIDTH
    o_qi = o_v + KV_WIDTH
    o_ki = o_qi + IDX_HEADS * IDX_DIM
    o_wi = o_ki + IDX_DIM
    o_p = o_wi + IDX_HEADS
    pad = jnp.zeros((w_in.shape[0], LANES - IDX_DIM - IDX_HEADS), w_in.dtype)
    return jnp.concatenate([w_in[:, :o_p], pad, w_in[:, o_p:]], axis=1).astype(BF16)


def _forward(x_prompt, x_sample, cache_k, cache_v, cache_kidx, state_pool,
             norm1_g, w_in, w_gate, b_gate, w_pool_grp, pool_scale, w_o_attn, w_o_pool, w_out,
             norm2_g, w_rg, b_rg, w_re, b_re, w_eg, w_eu, w_ed, norm_f_g):
    batch, seq, d = x_prompt.shape
    dec_batch, dec_seq, _ = x_sample.shape
    past = cache_k.shape[1]
    n_p = batch * seq
    n_s = dec_batch * dec_seq
    n = n_p + n_s
    tm = TOK_TILE
    assert d == D_MODEL and seq % tm == 0 and n_s % tm == 0 and n % (2 * tm) == 0
    assert past % KEY_BLOCK == 0 and dec_seq >= POOL_STATE and dec_seq <= KEY_BLOCK

    xp = x_prompt.reshape(n_p, d)
    xs = x_sample.reshape(n_s, d)
    pos_tab = jnp.concatenate([jnp.arange(seq), past + jnp.arange(n_s) % dec_seq])
    tables = _rope_tables(pos_tab)
    h, q, k, v, qi, kw, pin = _in_proj(xp, xs, norm1_g.reshape(1, d), _pack_w_in(w_in), tables, seq // tm)

    oa_p = _dsa_prompt(q, qi, kw, k, v, batch, seq)
    oa_s = _dsa_sample(q, qi, kw, k, v,
                       cache_k.reshape(dec_batch * past, KV_WIDTH),
                       cache_v.reshape(dec_batch * past, KV_WIDTH),
                       cache_kidx.reshape(dec_batch * past, IDX_DIM),
                       n_p, dec_batch, dec_seq, past)

    w_grp16 = w_pool_grp.astype(BF16)
    scale = pool_scale.reshape(1, POOL_WIDTH)
    op_p = _pool(pin, w_grp16, scale, 0, batch, seq, 0)
    halo = jnp.concatenate([jnp.zeros((dec_batch, 1, POOL_WIDTH), F32), state_pool], axis=1)
    op_s = _pool(pin, w_grp16, scale, n_p // dec_seq, dec_batch, dec_seq, past, halo=halo)

    m = _merge(h, oa_p, oa_s, op_p, op_s, w_gate.astype(BF16), b_gate.reshape(1, 2 * d),
               w_o_attn.astype(BF16), w_o_pool.astype(BF16))

    w_r = jnp.concatenate([w_rg, w_re, jnp.zeros((d, LANES - MOE_GROUPS - N_EXPERTS), F32)], axis=1)
    b_r = jnp.concatenate([b_rg, b_re, jnp.zeros((LANES - MOE_GROUPS - N_EXPERTS,), F32)]).reshape(1, LANES)
    wr_hi = w_r.astype(BF16)
    wr_lo = (w_r - wr_hi.astype(F32)).astype(BF16)
    x1, h2, route, counts = _out_proj(m, xp, xs, w_out.astype(BF16), norm2_g.reshape(1, d), wr_hi, wr_lo, b_r)

    te = EXP_TILE
    n_tiles = (2 * n) // te + N_EXPERTS
    cnt = counts[0, _R_LOGIT0:_R_LOGIT0 + N_EXPERTS].astype(I32)
    tiles_per_e = (cnt + te - 1) // te
    tile_end = jnp.cumsum(tiles_per_e)
    row_start = (tile_end - tiles_per_e) * te
    n_used = tile_end[-1:].astype(I32)
    tile_ids = jnp.arange(n_tiles, dtype=I32)
    tile_expert = jnp.minimum(
        jnp.sum((tile_end[None, :] <= tile_ids[:, None]).astype(I32), axis=1), N_EXPERTS - 1)
    last_e = tile_expert[jnp.maximum(n_used[0] - 1, 0)]
    tile_expert = jnp.where(tile_ids < n_used[0], tile_expert, last_e)
    eid = route[:, _R_E1:_R_E2 + 1].astype(I32)
    rank = route[:, _R_RANK1:_R_RANK2 + 1].astype(I32)
    pos_flat = (row_start[eid] + rank).reshape(2 * n)

    xs = _dispatch(pos_flat, h2, n_tiles * te)
    ys = _experts(tile_expert, n_used, xs, w_eg, w_eu, w_ed)
    y_p, y_s = _combine(pos_flat, x1, route, norm_f_g.reshape(1, d), ys, n_p)

    y_prompt = y_p.reshape(batch, seq, d)
    y_sample = y_s.reshape(dec_batch, dec_seq, d)
    k_p = k[:n_p].reshape(1, batch, seq, N_KV_HEADS, HEAD_DIM)
    v_p = v[:n_p].reshape(1, batch, seq, N_KV_HEADS, HEAD_DIM)
    ki_p = kw[:n_p, :IDX_DIM].reshape(1, batch, seq, IDX_DIM)
    pool_p = pin[:n_p].reshape(batch, seq, POOL_WIDTH)[None, :, seq - POOL_STATE:, :]
    k_s = k[n_p:].reshape(1, dec_batch, dec_seq, N_KV_HEADS, HEAD_DIM)
    v_s = v[n_p:].reshape(1, dec_batch, dec_seq, N_KV_HEADS, HEAD_DIM)
    ki_s = kw[n_p:, :IDX_DIM].reshape(1, dec_batch, dec_seq, IDX_DIM)
    pool_s = pin[n_p:].reshape(dec_batch, dec_seq, POOL_WIDTH)[None, :, dec_seq - POOL_STATE:, :]
    return (y_prompt, y_sample, k_p, v_p, ki_p, pool_p, k_s, v_s, ki_s, pool_s)


def kernel(x_prompt, x_sample, cache_k, cache_v, cache_kidx, state_pool, norm1_g, w_in, w_gate, b_gate,
           w_pool_grp, pool_scale, w_o_attn, w_o_pool, w_out, norm2_g, w_router_group, b_router_group,
           w_router_expert, b_router_expert, w_exp_gate, w_exp_up, w_exp_down, norm_f_g):
    assert cache_k.shape[0] == 1, "single-layer model"
    return _forward(x_prompt, x_sample, cache_k[0], cache_v[0], cache_kidx[0], state_pool[0],
                    norm1_g[0], w_in[0], w_gate[0], b_gate[0], w_pool_grp[0], pool_scale[0],
                    w_o_attn[0], w_o_pool[0], w_out[0], norm2_g[0], w_router_group[0],
                    b_router_group[0], w_router_expert[0], b_router_expert[0],
                    w_exp_gate[0], w_exp_up[0], w_exp_down[0], norm_f_g)
```

```python
import functools

import numpy as np
import jax
import jax.numpy as jnp
from jax import lax
from jax.experimental import pallas as pl
from jax.experimental.pallas import tpu as pltpu

F32 = jnp.float32
BF16 = jnp.bfloat16
I32 = jnp.int32

D_MODEL = 2048
CHUNK = 64
N_HEADS = 8
N_KV_HEADS = 2
HEAD_DIM = 128
GQA_GROUP = N_HEADS // N_KV_HEADS
ATTN_WIDTH = N_HEADS * HEAD_DIM
KV_WIDTH = N_KV_HEADS * HEAD_DIM
IDX_HEADS = 16
IDX_DIM = 64
IDX_PAIRS = IDX_HEADS // 2
TOPK_MAX = 256
POOL_WINDOWS = (2, 4, 8, 16)
POOL_GROUPS = 4
POOL_WIDTH = D_MODEL // 2
POOL_GROUP_DIM = POOL_WIDTH // POOL_GROUPS
POOL_STATE = 15
POOL_HALO = 16
MOE_GROUPS = 4
EXPERTS_PER_GROUP = 8
N_EXPERTS = MOE_GROUPS * EXPERTS_PER_GROUP
D_EXPERT = 512
ROPE_THETA = 10000.0
CHUNK_SHIFT = CHUNK.bit_length() - 1
GROUP_SHIFT = EXPERTS_PER_GROUP.bit_length() - 1
EPS = 1e-6

LANES = 128
INT_MIN = -(2 ** 31)
NEG_BIG = -1e30
LOG2_E = 1.4426950408889634
VMEM_LIMIT = 56 * 1024 * 1024

KEY_BLOCK = 256
TOK_TILE = 256
EXP_TILE = 256


def _cparams(n_axes):
    return pltpu.CompilerParams(dimension_semantics=("arbitrary",) * n_axes,
                                vmem_limit_bytes=VMEM_LIMIT)


def _dot(a, b):
    return jnp.dot(a, b, preferred_element_type=F32)


def _dot_nt(a, b):
    return lax.dot_general(a, b, (((1,), (1,)), ((), ())), preferred_element_type=F32)


_C_Q = 0
_C_K = _C_Q + ATTN_WIDTH
_C_V = _C_K + KV_WIDTH
_C_QI = _C_V + KV_WIDTH
_C_KW = _C_QI + IDX_HEADS * IDX_DIM
_C_P = _C_KW + LANES
_C_END = _C_P + POOL_WIDTH


def _rope128(y, cos, sin_signed):
    return y * cos + pltpu.roll(y, HEAD_DIM // 2, 1) * sin_signed


def _rope64(y, cos, sin_a, sin_b):
    half = IDX_DIM // 2
    return y * cos + pltpu.roll(y, LANES - half, 1) * sin_a + pltpu.roll(y, half, 1) * sin_b


def _inproj_kernel(xp_ref, xs_ref, g_ref, w_ref, ck_ref, sk_ref, cq_ref, sq_ref, ci_ref, sia_ref, sib_ref,
                   cw_ref, swa_ref, swb_ref,
                   h_ref, q_ref, k_ref, v_ref, qi_ref, kw_ref, p_ref, *, n_prompt_tiles):
    x = jnp.where(pl.program_id(0) < n_prompt_tiles, xp_ref[...], xs_ref[...])
    r = lax.rsqrt(jnp.mean(x * x, axis=-1, keepdims=True) + EPS)
    h = ((x * r) * g_ref[...]).astype(BF16)
    h_ref[...] = h
    cq, sq = cq_ref[...], sq_ref[...]
    for c in range(ATTN_WIDTH // 256):
        y = _dot(h, w_ref[:, _C_Q + c * 256:_C_Q + (c + 1) * 256])
        for s in range(2):
            q_ref[:, c * 256 + s * LANES:c * 256 + (s + 1) * LANES] = _rope128(
                y[:, s * LANES:(s + 1) * LANES], cq, sq).astype(BF16)
    ck, sk = ck_ref[...], sk_ref[...]
    y = _dot(h, w_ref[:, _C_K:_C_K + KV_WIDTH])
    for s in range(N_KV_HEADS):
        k_ref[:, s * LANES:(s + 1) * LANES] = _rope128(y[:, s * LANES:(s + 1) * LANES], ck, sk)
    v_ref[...] = _dot(h, w_ref[:, _C_V:_C_V + KV_WIDTH])
    ci, sia, sib = ci_ref[...], sia_ref[...], sib_ref[...]
    for c in range(IDX_PAIRS // 2):
        y = _dot(h, w_ref[:, _C_QI + c * 256:_C_QI + (c + 1) * 256])
        for s in range(2):
            qi_ref[2 * c + s] = _rope64(y[:, s * LANES:(s + 1) * LANES], ci, sia, sib).astype(BF16)
    y = _dot(h, w_ref[:, _C_KW:_C_KW + LANES])
    kw_ref[...] = _rope64(y, cw_ref[...], swa_ref[...], swb_ref[...])
    for c in range(POOL_WIDTH // 256):
        p_ref[:, c * 256:(c + 1) * 256] = _dot(h, w_ref[:, _C_P + c * 256:_C_P + (c + 1) * 256])


def _rope_tables(pos):
    pos = pos.astype(F32)[:, None]

    def cs(dim):
        half = dim // 2
        inv = ROPE_THETA ** (-jnp.arange(half, dtype=F32) / half)
        ang = pos * inv[None, :]
        return jnp.cos(ang), jnp.sin(ang)

    c, s = cs(HEAD_DIM)
    ck = jnp.concatenate([c, c], axis=1)
    sk = jnp.concatenate([-s, s], axis=1)
    qs = HEAD_DIM ** -0.5 * LOG2_E
    c, s = cs(IDX_DIM)
    z = jnp.zeros_like(s)
    ci = jnp.concatenate([c, c, c, c], axis=1)
    sia = jnp.concatenate([-s, z, -s, z], axis=1)
    sib = jnp.concatenate([z, s, z, s], axis=1)
    iscale = IDX_DIM ** -0.5
    n = pos.shape[0]
    wscale = jnp.full((n, IDX_HEADS), IDX_HEADS ** -0.5, F32)
    pad = jnp.zeros((n, LANES - IDX_DIM - IDX_HEADS), F32)
    zz = jnp.zeros((n, LANES - IDX_DIM), F32)
    cw = jnp.concatenate([c, c, wscale, pad], axis=1)
    swa = jnp.concatenate([-s, z, zz], axis=1)
    swb = jnp.concatenate([z, s, zz], axis=1)
    return (ck, sk, ck * qs, sk * qs, ci * iscale, sia * iscale, sib * iscale, cw, swa, swb)


def _split_token_specs(tm, width, n_prompt_tiles):
    prompt = pl.BlockSpec((tm, width), lambda i: (jnp.minimum(i, n_prompt_tiles - 1), 0))
    sample = pl.BlockSpec((tm, width), lambda i: (jnp.maximum(i - n_prompt_tiles, 0), 0))
    return prompt, sample


def _in_proj(xp, xs, g1, w_pack, tables, tiles_per_seq):
    tm = TOK_TILE
    n_prompt_tiles = xp.shape[0] // tm
    n = xp.shape[0] + xs.shape[0]
    n_tiles = n // tm

    def tab_idx(i):
        return (jnp.where(i < n_prompt_tiles, i % tiles_per_seq, tiles_per_seq + i - n_prompt_tiles), 0)

    tok = lambda w: pl.BlockSpec((tm, w), lambda i: (i, 0))
    tab = pl.BlockSpec((tm, LANES), tab_idx)
    out_shape = (
        jax.ShapeDtypeStruct((n, D_MODEL), BF16),
        jax.ShapeDtypeStruct((n, ATTN_WIDTH), BF16),
        jax.ShapeDtypeStruct((n, KV_WIDTH), F32),
        jax.ShapeDtypeStruct((n, KV_WIDTH), F32),
        jax.ShapeDtypeStruct((IDX_PAIRS, n, LANES), BF16),
        jax.ShapeDtypeStruct((n, LANES), F32),
        jax.ShapeDtypeStruct((n, POOL_WIDTH), F32),
    )
    return pl.pallas_call(
        functools.partial(_inproj_kernel, n_prompt_tiles=n_prompt_tiles),
        grid=(n_tiles,),
        in_specs=[*_split_token_specs(tm, D_MODEL, n_prompt_tiles),
                  pl.BlockSpec((1, D_MODEL), lambda i: (0, 0)),
                  pl.BlockSpec((D_MODEL, _C_END), lambda i: (0, 0), pipeline_mode=pl.Buffered(1))]
                 + [tab] * 10,
        out_specs=(tok(D_MODEL), tok(ATTN_WIDTH), tok(KV_WIDTH), tok(KV_WIDTH),
                   pl.BlockSpec((IDX_PAIRS, tm, LANES), lambda i: (0, i, 0)),
                   tok(LANES), tok(POOL_WIDTH)),
        out_shape=out_shape,
        compiler_params=_cparams(1),
        name="in_proj",
    )(xp, xs, g1, w_pack, *tables)


def _ordinal_to_f32(k):
    return lax.bitcast_convert_type(k ^ ((k >> 31) & 0x7FFFFFFF), F32)


def _kth_largest_score(topk, shape, n_key_rows, count_where):
    def count_ge(cand):
        cand_f = _ordinal_to_f32(cand)
        return count_where(lambda keys, kb: keys >= jnp.broadcast_to(cand_f, keys.shape))

    c0 = count_ge(jnp.zeros(shape, I32))
    thr0 = jnp.where(c0 >= topk, 0, INT_MIN).astype(I32)
    cnt0 = jnp.where(c0 >= topk, c0, n_key_rows).astype(I32)

    def bit_step(i, carry):
        thr, cnt = carry
        cand = thr + (jnp.int32(1) << (30 - i))
        c = count_ge(cand)
        ok = c >= topk
        return jnp.where(ok, cand, thr), jnp.where(ok, c, cnt)

    thr, cnt = lax.fori_loop(0, 31, bit_step, (thr0, cnt0))
    has_k = thr > INT_MIN
    thr_f = jnp.where(has_k, _ordinal_to_f32(thr), jnp.finfo(F32).min)
    return thr_f, has_k & (cnt > topk)


def _drop_excess_ties(sc, n_blocks, topk, thr, need_tie, key_index, count_where):
    n_gt = count_where(lambda keys, kb: keys > jnp.broadcast_to(thr, keys.shape))
    n_keep = topk - n_gt

    def idx_step(i, bound):
        cand = bound + (jnp.int32(1) << (14 - i))
        c = count_where(lambda keys, kb: (keys == jnp.broadcast_to(thr, keys.shape))
                        & (key_index(kb) < jnp.broadcast_to(cand, keys.shape)))
        return jnp.where(c <= n_keep, cand, bound)

    bound = lax.fori_loop(0, 15, idx_step, jnp.zeros(thr.shape, I32))
    bound = jnp.where(need_tie, bound, jnp.int32(2 ** 30))

    def drop_block(kb, carry):
        keys = sc[kb]
        drop = (keys == jnp.broadcast_to(thr, keys.shape)) & (key_index(kb) >= jnp.broadcast_to(bound, keys.shape))
        sc[kb] = jnp.where(drop, -jnp.inf, keys)
        return carry

    lax.fori_loop(0, n_blocks, drop_block, 0)


def _stack_queries_t(q_ref, qt):
    for g in range(N_KV_HEADS):
        qg = jnp.concatenate([q_ref[:, (g * GQA_GROUP + j) * HEAD_DIM:(g * GQA_GROUP + j + 1) * HEAD_DIM]
                              for j in range(GQA_GROUP)], axis=0)
        qt[g] = qg.astype(F32).T.astype(BF16)


def _masked_attention(o_ref, n_blocks, bias_t, k16, vt16, qt, m_s, l_s, acc_s, tq):
    m_s[...] = jnp.full(m_s.shape, NEG_BIG, F32)
    l_s[...] = jnp.zeros(l_s.shape, F32)
    acc_s[...] = jnp.zeros(acc_s.shape, F32)

    def attn_block(kb, carry):
        r0 = pl.multiple_of(kb * KEY_BLOCK, KEY_BLOCK)
        bias = bias_t(kb)
        for g in range(N_KV_HEADS):
            kblk = k16[pl.ds(r0, KEY_BLOCK), g * HEAD_DIM:(g + 1) * HEAD_DIM]
            logits = _dot(kblk, qt[g]) + bias
            m_prev = m_s[g]
            m_new = jnp.maximum(m_prev, jnp.max(logits, axis=0, keepdims=True))
            alpha = jnp.exp2(m_prev - m_new)
            p = jnp.exp2(logits - m_new)
            l_s[g] = alpha * l_s[g] + jnp.sum(p, axis=0, keepdims=True)
            acc_s[g] = alpha * acc_s[g] + _dot(vt16[kb, g * HEAD_DIM:(g + 1) * HEAD_DIM, :], p.astype(BF16))
            m_s[g] = m_new
        return carry

    lax.fori_loop(0, n_blocks, attn_block, 0)

    for g in range(N_KV_HEADS):
        o = (acc_s[g] * (1.0 / l_s[g])).T
        for j in range(GQA_GROUP):
            hd = g * GQA_GROUP + j
            o_ref[:, hd * HEAD_DIM:(hd + 1) * HEAD_DIM] = o[j * tq:(j + 1) * tq, :].astype(BF16)


def _dsa_prompt_kernel(q_ref, qi_ref, wq_ref, kn_ref, vn_ref, kwn_ref, o_ref,
                       k16, vt16, kia, kib, sc, qt, m_s, l_s, acc_s, *, tq, seq, topk):
    qb = pl.program_id(1)
    n_blocks_all = seq // KEY_BLOCK

    @pl.when(qb == 0)
    def _():
        lane = lax.broadcasted_iota(I32, (seq, LANES), 1)
        kin = jnp.where(lane < IDX_DIM, kwn_ref[...], 0.0)
        k16[...] = kn_ref[...].astype(BF16)
        kia[...] = kin.astype(BF16)
        kib[...] = pltpu.roll(kin, IDX_DIM, 1).astype(BF16)
        for kb in range(n_blocks_all):
            vt16[kb] = vn_ref[kb * KEY_BLOCK:(kb + 1) * KEY_BLOCK, :].T.astype(BF16)

    n_blocks = qb // (KEY_BLOCK // tq) + 1

    qit = qi_ref[...].reshape(IDX_PAIRS * tq, LANES).astype(F32).T.astype(BF16)
    wt = wq_ref[...].T
    w_rows = [wt[IDX_DIM + j:IDX_DIM + j + 1, :] for j in range(IDX_HEADS)]
    _stack_queries_t(q_ref, qt)

    key_row = lax.broadcasted_iota(I32, (KEY_BLOCK, tq), 0)
    q_pos = qb * tq + lax.broadcasted_iota(I32, (KEY_BLOCK, tq), 1)

    def score_block(kb, carry):
        r0 = pl.multiple_of(kb * KEY_BLOCK, KEY_BLOCK)
        se = _dot(kia[pl.ds(r0, KEY_BLOCK), :], qit)
        so = _dot(kib[pl.ds(r0, KEY_BLOCK), :], qit)
        score = jnp.zeros((KEY_BLOCK, tq), F32)
        for p in range(IDX_PAIRS):
            score = (score + jnp.maximum(se[:, p * tq:(p + 1) * tq], 0.0) * w_rows[2 * p]
                     + jnp.maximum(so[:, p * tq:(p + 1) * tq], 0.0) * w_rows[2 * p + 1])
        adm = ((kb * KEY_BLOCK + key_row) >> CHUNK_SHIFT) <= (q_pos >> CHUNK_SHIFT)
        sc[kb] = jnp.where(adm, score, -jnp.inf)
        return carry

    lax.fori_loop(0, n_blocks, score_block, 0)

    def count_where(pred):
        def body(kb, acc):
            m = pred(sc[kb], kb).astype(I32)
            return acc + jnp.sum(m.reshape(KEY_BLOCK // 8, 8, tq), axis=0)
        acc = lax.fori_loop(0, n_blocks, body, jnp.zeros((8, tq), I32))
        return jnp.sum(acc, axis=0, keepdims=True)

    thr, need_tie = _kth_largest_score(topk, (1, tq), seq, count_where)

    @pl.when(jnp.max(need_tie.astype(I32)) > 0)
    def _():
        _drop_excess_ties(sc, n_blocks, topk, thr, need_tie,
                          lambda kb: kb * KEY_BLOCK + key_row, count_where)

    thr_b = jnp.broadcast_to(thr, (KEY_BLOCK, tq))

    def bias_t(kb):
        bias = jnp.where(sc[kb] >= thr_b, 0.0, NEG_BIG)
        return jnp.concatenate([bias] * GQA_GROUP, axis=1)

    _masked_attention(o_ref, n_blocks, bias_t, k16, vt16, qt, m_s, l_s, acc_s, tq)


def _dsa_sample_kernel(q_ref, qi_ref, wq_ref, kn_ref, vn_ref, kwn_ref, kc_ref, vc_ref, kic_ref, o_ref,
                       k16, vt16, kia, kib, wbe, wbo, sc, qt, m_s, l_s, acc_s, *,
                       tq, n_key_rows, n_cache, n_new, topk):
    n_blocks = n_key_rows // KEY_BLOCK
    n_cache_blocks = n_cache // KEY_BLOCK

    lane = lax.broadcasted_iota(I32, (n_new, LANES), 1)
    kin = jnp.where(lane < IDX_DIM, kwn_ref[...], 0.0).astype(BF16)
    n_tail = n_key_rows - n_cache
    k16[0:n_cache, :] = kc_ref[...].astype(BF16)
    for kb in range(n_cache_blocks):
        vt16[kb] = vc_ref[kb * KEY_BLOCK:(kb + 1) * KEY_BLOCK, :].T.astype(BF16)
    v_tail = jnp.concatenate([vn_ref[...], jnp.zeros((n_tail - n_new, KV_WIDTH), F32)], axis=0)
    vt16[n_cache_blocks] = v_tail.T.astype(BF16)
    kic = jnp.concatenate([kic_ref[...], jnp.zeros((n_cache, LANES - IDX_DIM), F32)], axis=1)
    kia[0:n_cache, :] = kic.astype(BF16)
    kib[0:n_cache, :] = pltpu.roll(kic, IDX_DIM, 1).astype(BF16)
    k16[n_cache:n_key_rows, :] = jnp.zeros((n_tail, KV_WIDTH), BF16)
    kia[n_cache:n_key_rows, :] = jnp.zeros((n_tail, LANES), BF16)
    kib[n_cache:n_key_rows, :] = jnp.zeros((n_tail, LANES), BF16)
    k16[n_cache:n_cache + n_new, :] = kn_ref[...].astype(BF16)
    kia[n_cache:n_cache + n_new, :] = kin
    kib[n_cache:n_cache + n_new, :] = pltpu.roll(kin.astype(F32), IDX_DIM, 1).astype(BF16)

    wq = wq_ref[...]
    for j in range(IDX_HEADS):
        col = jnp.broadcast_to(wq[:, IDX_DIM + j:IDX_DIM + j + 1], (tq, LANES))
        if j % 2 == 0:
            wbe[j // 2] = col
        else:
            wbo[j // 2] = col
    _stack_queries_t(q_ref, qt)

    lane = lax.broadcasted_iota(I32, (tq, KEY_BLOCK), 1)

    qi2 = qi_ref[...].reshape(IDX_PAIRS * tq, LANES)

    def score_block(kb, carry):
        r0 = pl.multiple_of(kb * KEY_BLOCK, KEY_BLOCK)
        se = _dot_nt(qi2, kia[pl.ds(r0, KEY_BLOCK), :]).reshape(IDX_PAIRS, tq, KEY_BLOCK)
        so = _dot_nt(qi2, kib[pl.ds(r0, KEY_BLOCK), :]).reshape(IDX_PAIRS, tq, KEY_BLOCK)
        score = jnp.zeros((tq, KEY_BLOCK), F32)
        for p in range(IDX_PAIRS):
            we = wbe[p]
            wo = wbo[p]
            we2 = jnp.concatenate([we, we], axis=1)
            wo2 = jnp.concatenate([wo, wo], axis=1)
            score = score + jnp.maximum(se[p], 0.0) * we2 + jnp.maximum(so[p], 0.0) * wo2
        adm = (kb * KEY_BLOCK + lane) < (n_cache + n_new)
        sc[kb] = jnp.where(adm, score, -jnp.inf)
        return carry

    lax.fori_loop(0, n_blocks, score_block, 0)

    def count_where(pred):
        def body(kb, acc):
            m = pred(sc[kb], kb).astype(I32)
            return acc + m[:, :LANES] + m[:, LANES:]
        acc = lax.fori_loop(0, n_blocks, body, jnp.zeros((tq, LANES), I32))
        return jnp.sum(acc, axis=1, keepdims=True)

    thr, need_tie = _kth_largest_score(topk, (tq, 1), n_key_rows, count_where)

    @pl.when(jnp.max(need_tie.astype(I32)) > 0)
    def _():
        _drop_excess_ties(sc, n_blocks, topk, thr, need_tie,
                          lambda kb: kb * KEY_BLOCK + lane, count_where)

    thr_b = jnp.broadcast_to(thr, (tq, KEY_BLOCK))

    def bias_t(kb):
        bias = jnp.where(sc[kb] >= thr_b, 0.0, NEG_BIG)
        return jnp.concatenate([bias] * GQA_GROUP, axis=0).T

    _masked_attention(o_ref, n_blocks, bias_t, k16, vt16, qt, m_s, l_s, acc_s, tq)


def _dsa_sample_scratch(tq, n_key_rows):
    rows = GQA_GROUP * tq
    return [
        pltpu.VMEM((n_key_rows, KV_WIDTH), BF16),
        pltpu.VMEM((n_key_rows // KEY_BLOCK, KV_WIDTH, KEY_BLOCK), BF16),
        pltpu.VMEM((n_key_rows, LANES), BF16),
        pltpu.VMEM((n_key_rows, LANES), BF16),
        pltpu.VMEM((IDX_PAIRS, tq, LANES), F32),
        pltpu.VMEM((IDX_PAIRS, tq, LANES), F32),
        pltpu.VMEM((n_key_rows // KEY_BLOCK, tq, KEY_BLOCK), F32),
        pltpu.VMEM((N_KV_HEADS, HEAD_DIM, rows), BF16),
        pltpu.VMEM((N_KV_HEADS, 1, rows), F32),
        pltpu.VMEM((N_KV_HEADS, 1, rows), F32),
        pltpu.VMEM((N_KV_HEADS, HEAD_DIM, rows), F32),
    ]


def _dsa_prompt(q, qi, kw, k, v, batch, seq):
    tq = 128
    nqb = seq // tq
    topk = min(TOPK_MAX, seq // 4)
    kern = functools.partial(_dsa_prompt_kernel, tq=tq, seq=seq, topk=topk)
    rows_q = GQA_GROUP * tq
    scratch = [
        pltpu.VMEM((seq, KV_WIDTH), BF16),
        pltpu.VMEM((seq // KEY_BLOCK, KV_WIDTH, KEY_BLOCK), BF16),
        pltpu.VMEM((seq, LANES), BF16),
        pltpu.VMEM((seq, LANES), BF16),
        pltpu.VMEM((seq // KEY_BLOCK, KEY_BLOCK, tq), F32),
        pltpu.VMEM((N_KV_HEADS, HEAD_DIM, rows_q), BF16),
        pltpu.VMEM((N_KV_HEADS, 1, rows_q), F32),
        pltpu.VMEM((N_KV_HEADS, 1, rows_q), F32),
        pltpu.VMEM((N_KV_HEADS, HEAD_DIM, rows_q), F32),
    ]
    qmap = lambda b, i: (b * nqb + i, 0)
    smap = lambda b, i: (b, 0)
    return pl.pallas_call(
        kern,
        grid=(batch, nqb),
        in_specs=[pl.BlockSpec((tq, ATTN_WIDTH), qmap),
                  pl.BlockSpec((IDX_PAIRS, tq, LANES), lambda b, i: (0, b * nqb + i, 0)),
                  pl.BlockSpec((tq, LANES), qmap),
                  pl.BlockSpec((seq, KV_WIDTH), smap),
                  pl.BlockSpec((seq, KV_WIDTH), smap),
                  pl.BlockSpec((seq, LANES), smap)],
        out_specs=pl.BlockSpec((tq, ATTN_WIDTH), qmap),
        out_shape=jax.ShapeDtypeStruct((batch * seq, ATTN_WIDTH), BF16),
        scratch_shapes=scratch,
        compiler_params=_cparams(2),
        name="dsa_prompt",
    )(q, qi, kw, k, v, kw)


def _dsa_sample(q, qi, kw, k, v, cache_k, cache_v, cache_ki, row0, batch, seq, past):
    tq = seq
    n_key_rows = past + KEY_BLOCK
    topk = min(TOPK_MAX, (past + seq) // 4)
    kern = functools.partial(_dsa_sample_kernel, tq=tq, n_key_rows=n_key_rows, n_cache=past, n_new=seq,
                             topk=topk)
    t0 = row0 // seq
    qmap = lambda b, i: (t0 + b, 0)
    cmap = lambda b, i: (b, 0)
    return pl.pallas_call(
        kern,
        grid=(batch, 1),
        in_specs=[pl.BlockSpec((tq, ATTN_WIDTH), qmap),
                  pl.BlockSpec((IDX_PAIRS, tq, LANES), lambda b, i: (0, t0 + b, 0)),
                  pl.BlockSpec((tq, LANES), qmap),
                  pl.BlockSpec((tq, KV_WIDTH), qmap),
                  pl.BlockSpec((tq, KV_WIDTH), qmap),
                  pl.BlockSpec((tq, LANES), qmap),
                  pl.BlockSpec((past, KV_WIDTH), cmap),
                  pl.BlockSpec((past, KV_WIDTH), cmap),
                  pl.BlockSpec((past, IDX_DIM), cmap)],
        out_specs=pl.BlockSpec((tq, ATTN_WIDTH), cmap),
        out_shape=jax.ShapeDtypeStruct((batch * seq, ATTN_WIDTH), BF16),
        scratch_shapes=_dsa_sample_scratch(tq, n_key_rows),
        compiler_params=_cparams(2),
        name="dsa_sample",
    )(q, qi, kw, k, v, kw, cache_k, cache_v, cache_ki)


def _pool_kernel(*refs, tp, pos0, carried):
    if carried:
        p_ref, w_ref, s_ref, o_ref, ext, save = refs
    else:
        p_ref, halo_ref, w_ref, s_ref, o_ref, ext = refs
    i = pl.program_id(1)
    if carried:
        @pl.when(i == 0)
        def _():
            ext[0:POOL_HALO, :] = jnp.zeros((POOL_HALO, POOL_WIDTH), F32)

        @pl.when(i > 0)
        def _():
            ext[0:POOL_HALO, :] = save[...]
    else:
        ext[0:POOL_HALO, :] = halo_ref[0]
    x = p_ref[...]
    ext[POOL_HALO:POOL_HALO + tp, :] = x
    if carried:
        save[...] = x[tp - POOL_HALO:tp, :]
    pos = pos0 + i * tp + lax.broadcasted_iota(I32, (tp, 1), 0)
    for g, w in enumerate(POOL_WINDOWS):
        c0, c1 = g * POOL_GROUP_DIM, (g + 1) * POOL_GROUP_DIM
        xg = x[:, c0:c1]
        tot = xg
        for s in range(1, w):
            tot = tot + ext[POOL_HALO - s:POOL_HALO - s + tp, c0:c1]
        inv = 1.0 / jnp.minimum(pos + 1, w).astype(F32)
        d = (tot * inv - xg).astype(BF16)
        o_ref[:, c0:c1] = (_dot(d, w_ref[g]) * s_ref[:, c0:c1]).astype(BF16)


def _pool(pin, w_grp, scale, tile0, batch, seq, pos0, halo=None):
    carried = halo is None
    tp = min(seq, 256)
    nt = seq // tp
    kern = functools.partial(_pool_kernel, tp=tp, pos0=pos0, carried=carried)
    in_specs = [pl.BlockSpec((tp, POOL_WIDTH), lambda b, i: (tile0 + b * nt + i, 0))]
    args = [pin]
    if not carried:
        in_specs.append(pl.BlockSpec((1, POOL_HALO, POOL_WIDTH), lambda b, i: (b, 0, 0)))
        args.append(halo)
    in_specs += [pl.BlockSpec((POOL_GROUPS, POOL_GROUP_DIM, POOL_GROUP_DIM), lambda b, i: (0, 0, 0)),
                 pl.BlockSpec((1, POOL_WIDTH), lambda b, i: (0, 0))]
    args += [w_grp, scale]
    scratch = [pltpu.VMEM((POOL_HALO + tp, POOL_WIDTH), F32)]
    if carried:
        scratch.append(pltpu.VMEM((POOL_HALO, POOL_WIDTH), F32))
    return pl.pallas_call(
        kern,
        grid=(batch, nt),
        in_specs=in_specs,
        out_specs=pl.BlockSpec((tp, POOL_WIDTH), lambda b, i: (b * nt + i, 0)),
        out_shape=jax.ShapeDtypeStruct((batch * seq, POOL_WIDTH), BF16),
        scratch_shapes=scratch,
        compiler_params=_cparams(2),
        name="pool_prompt" if carried else "pool_sample",
    )(*args)


def _merge_kernel(h_ref, oap_ref, oas_ref, opp_ref, ops_ref, wga_ref, wgp_ref, ba_ref, bp_ref, woa_ref,
                  wop_ref, m_ref, *, n_prompt_tiles):
    h = h_ref[...]
    is_prompt = pl.program_id(1) < n_prompt_tiles
    oa = jnp.where(is_prompt, oap_ref[...], oas_ref[...])
    op = jnp.where(is_prompt, opp_ref[...], ops_ref[...])
    ga = jax.nn.sigmoid(_dot(h, wga_ref[...]) + ba_ref[...])
    gp = jax.nn.sigmoid(_dot(h, wgp_ref[...]) + bp_ref[...])
    a = _dot(oa, woa_ref[...])
    p = _dot(op, wop_ref[...])
    m_ref[...] = (ga * a + gp * p).astype(BF16)


def _merge(h, oa_p, oa_s, op_p, op_s, w_gate, b_gate, w_oa, w_op):
    n = h.shape[0]
    tm = 2 * TOK_TILE
    tn = D_MODEL // 2
    nj = D_MODEL // tn
    npt = oa_p.shape[0] // tm
    prompt = lambda w: pl.BlockSpec((tm, w), lambda j, i: (jnp.minimum(i, npt - 1), 0))
    sample = lambda w: pl.BlockSpec((tm, w), lambda j, i: (jnp.maximum(i - npt, 0), 0))
    return pl.pallas_call(
        functools.partial(_merge_kernel, n_prompt_tiles=npt),
        grid=(nj, n // tm),
        in_specs=[pl.BlockSpec((tm, D_MODEL), lambda j, i: (i, 0)),
                  prompt(ATTN_WIDTH), sample(ATTN_WIDTH), prompt(POOL_WIDTH), sample(POOL_WIDTH),
                  pl.BlockSpec((D_MODEL, tn), lambda j, i: (0, j)),
                  pl.BlockSpec((D_MODEL, tn), lambda j, i: (0, nj + j)),
                  pl.BlockSpec((1, tn), lambda j, i: (0, j)),
                  pl.BlockSpec((1, tn), lambda j, i: (0, nj + j)),
                  pl.BlockSpec((ATTN_WIDTH, tn), lambda j, i: (0, j)),
                  pl.BlockSpec((POOL_WIDTH, tn), lambda j, i: (0, j))],
        out_specs=pl.BlockSpec((tm, tn), lambda j, i: (i, j)),
        out_shape=jax.ShapeDtypeStruct((n, D_MODEL), BF16),
        compiler_params=_cparams(2),
        name="merge",
    )(h, oa_p, oa_s, op_p, op_s, w_gate, w_gate, b_gate, b_gate, w_oa, w_op)


_R_E1, _R_E2, _R_W1, _R_W2, _R_RANK1, _R_RANK2 = range(6)
_R_LOGIT0 = MOE_GROUPS


def _outproj_kernel(m_ref, xp_ref, xs_ref, wo_ref, g2_ref, wr_ref, br_ref,
                    x1_ref, h2_ref, route_ref, counts_ref, run, *, n_prompt_tiles):
    i = pl.program_id(0)
    tm = m_ref.shape[0]

    @pl.when(i == 0)
    def _():
        run[...] = jnp.zeros(run.shape, F32)

    x = jnp.where(i < n_prompt_tiles, xp_ref[...], xs_ref[...])
    x1 = x + _dot(m_ref[...], wo_ref[...])
    x1_ref[...] = x1
    r = lax.rsqrt(jnp.mean(x1 * x1, axis=-1, keepdims=True) + EPS)
    hf = (x1 * r) * g2_ref[...]
    h2_ref[...] = hf

    hi = hf.astype(BF16)
    lo = (hf - hi.astype(F32)).astype(BF16)
    hw = _dot(hi, wr_ref[...])
    lg = hw[:, :LANES] + (hw[:, LANES:] + _dot(lo, wr_ref[:, :LANES])) + br_ref[...]

    lane = lax.broadcasted_iota(I32, (tm, LANES), 1)
    neg_inf = jnp.float32(-jnp.inf)
    is_g = lane < MOE_GROUPS
    cl = jnp.where(is_g, lg, neg_inf)
    gmax = jnp.max(cl, axis=1, keepdims=True)
    g_sel = jnp.min(jnp.where(cl == gmax, lane, LANES), axis=1, keepdims=True)
    den = jnp.sum(jnp.where(is_g, jnp.exp(cl - gmax), 0.0), axis=1, keepdims=True)
    g_val = 1.0 / den
    e_lane = lane - _R_LOGIT0
    in_grp = (e_lane >= 0) & (e_lane < N_EXPERTS) & ((e_lane >> GROUP_SHIFT) == g_sel)
    f1 = jnp.where(in_grp, lg, neg_inf)
    v1 = jnp.max(f1, axis=1, keepdims=True)
    i1 = jnp.min(jnp.where(f1 == v1, lane, LANES), axis=1, keepdims=True)
    f2 = jnp.where(lane == i1, neg_inf, f1)
    v2 = jnp.max(f2, axis=1, keepdims=True)
    i2 = jnp.min(jnp.where(f2 == v2, lane, LANES), axis=1, keepdims=True)
    e2 = jnp.exp(v2 - v1)
    w1 = g_val / (1.0 + e2)
    w2 = g_val * e2 / (1.0 + e2)

    oh1 = lane == i1
    oh2 = lane == i2
    c = (oh1 | oh2).astype(BF16)
    rr = lax.broadcasted_iota(I32, (tm, tm), 0)
    cc = lax.broadcasted_iota(I32, (tm, tm), 1)
    before = (cc < rr).astype(BF16)
    prior = _dot(before, c) + run[...]
    rank1 = jnp.sum(jnp.where(oh1, prior, 0.0), axis=1, keepdims=True)
    rank2 = jnp.sum(jnp.where(oh2, prior, 0.0), axis=1, keepdims=True)
    run[...] = run[...] + jnp.sum(c.astype(F32), axis=0, keepdims=True)
    counts_ref[...] = jnp.broadcast_to(run[...], counts_ref.shape)

    rec = jnp.zeros((tm, LANES), F32)
    for ln, val in ((_R_E1, (i1 - _R_LOGIT0).astype(F32)), (_R_E2, (i2 - _R_LOGIT0).astype(F32)),
                    (_R_W1, w1), (_R_W2, w2), (_R_RANK1, rank1), (_R_RANK2, rank2)):
        rec = jnp.where(lane == ln, val, rec)
    route_ref[...] = rec


def _out_proj(m, xp, xs, w_out, g2, wr_split, b_r):
    n = m.shape[0]
    tm = TOK_TILE
    n_prompt_tiles = xp.shape[0] // tm
    tok = lambda w: pl.BlockSpec((tm, w), lambda i: (i, 0))
    const = lambda s: pl.BlockSpec(s, lambda i: (0,) * len(s))
    return pl.pallas_call(
        functools.partial(_outproj_kernel, n_prompt_tiles=n_prompt_tiles),
        grid=(n // tm,),
        in_specs=[tok(D_MODEL), *_split_token_specs(tm, D_MODEL, n_prompt_tiles),
                  pl.BlockSpec((D_MODEL, D_MODEL), lambda i: (0, 0), pipeline_mode=pl.Buffered(1)),
                  const((1, D_MODEL)), const((D_MODEL, 2 * LANES)), const((1, LANES))],
        out_specs=(tok(D_MODEL), tok(D_MODEL), tok(LANES), const((8, LANES))),
        out_shape=(jax.ShapeDtypeStruct((n, D_MODEL), F32),
                   jax.ShapeDtypeStruct((n, D_MODEL), F32),
                   jax.ShapeDtypeStruct((n, LANES), F32),
                   jax.ShapeDtypeStruct((8, LANES), F32)),
        scratch_shapes=[pltpu.VMEM((1, LANES), F32)],
        compiler_params=_cparams(1),
        name="out_proj",
    )(m, xp, xs, w_out, g2, wr_split, b_r)


DISPATCH_TOKENS = 256


def _dispatch_kernel(pos_ref, h2_ref, xs_in_ref, xs_ref, sem):
    del xs_in_ref
    tb = DISPATCH_TOKENS

    def row_copy(r, slot):
        return pltpu.make_async_copy(h2_ref.at[pl.ds(r, 1)],
                                     xs_ref.at[pl.ds(pos_ref[2 * r + slot], 1)], sem)

    def issue(r, carry):
        row_copy(r, 0).start()
        row_copy(r, 1).start()
        return carry

    lax.fori_loop(0, tb, issue, 0, unroll=8)

    def drain(r, carry):
        row_copy(r, 0).wait()
        row_copy(r, 1).wait()
        return carry

    lax.fori_loop(0, tb, drain, 0, unroll=8)


def _dispatch(pos_flat, h2, n_rows):
    n = h2.shape[0]
    tb = DISPATCH_TOKENS
    xs0 = jnp.zeros((n_rows, D_MODEL), F32)
    return pl.pallas_call(
        _dispatch_kernel,
        grid=(n // tb,),
        in_specs=[pl.BlockSpec((2 * tb,), lambda i: (i,), memory_space=pltpu.SMEM),
                  pl.BlockSpec((tb, D_MODEL), lambda i: (i, 0)),
                  pl.BlockSpec(memory_space=pl.ANY)],
        out_specs=pl.BlockSpec(memory_space=pl.ANY),
        out_shape=jax.ShapeDtypeStruct((n_rows, D_MODEL), F32),
        scratch_shapes=[pltpu.SemaphoreType.DMA(())],
        input_output_aliases={2: 0},
        compiler_params=_cparams(1),
        name="dispatch",
    )(pos_flat, h2, xs0)


def _experts_kernel(te_ref, nu_ref, xs_ref, wg_ref, wu_ref, wd_ref, ys_ref, wg16, wu16, wd16):
    i = pl.program_id(0)
    prev = te_ref[jnp.maximum(i - 1, 0)]
    fresh = (i == 0) | (te_ref[i] != prev)
    used = i < nu_ref[0]

    @pl.when(fresh & used)
    def _():
        wg16[...] = wg_ref[0].astype(BF16)
        wu16[...] = wu_ref[0].astype(BF16)
        wd16[...] = wd_ref[0].astype(BF16)

    @pl.when(used)
    def _():
        x = xs_ref[...].astype(BF16)
        a = _dot(x, wg16[...])
        u = _dot(x, wu16[...])
        act = (a * jax.nn.sigmoid(a)) * u
        ys_ref[...] = _dot(act.astype(BF16), wd16[...])

    @pl.when(jnp.logical_not(used))
    def _():
        ys_ref[...] = jnp.zeros(ys_ref.shape, F32)


def _experts(tile_expert, n_used, xs, w_g, w_u, w_d):
    n_rows = xs.shape[0]
    tm = EXP_TILE
    grid_spec = pltpu.PrefetchScalarGridSpec(
        num_scalar_prefetch=2,
        grid=(n_rows // tm,),
        in_specs=[pl.BlockSpec((tm, D_MODEL), lambda i, te, nu: (i, 0)),
                  pl.BlockSpec((1, D_MODEL, D_EXPERT), lambda i, te, nu: (te[i], 0, 0)),
                  pl.BlockSpec((1, D_MODEL, D_EXPERT), lambda i, te, nu: (te[i], 0, 0)),
                  pl.BlockSpec((1, D_EXPERT, D_MODEL), lambda i, te, nu: (te[i], 0, 0))],
        out_specs=pl.BlockSpec((tm, D_MODEL), lambda i, te, nu: (i, 0)),
        scratch_shapes=[pltpu.VMEM((D_MODEL, D_EXPERT), BF16),
                        pltpu.VMEM((D_MODEL, D_EXPERT), BF16),
                        pltpu.VMEM((D_EXPERT, D_MODEL), BF16)],
    )
    return pl.pallas_call(
        _experts_kernel,
        grid_spec=grid_spec,
        out_shape=jax.ShapeDtypeStruct((n_rows, D_MODEL), F32),
        compiler_params=_cparams(1),
        name="experts",
    )(tile_expert, n_used, xs, w_g, w_u, w_d)


COMBINE_TOKENS = 256


def _combine_kernel(pos_ref, pos_next_ref, x1_ref, route_ref, gf_ref, ys_ref, yp_ref, ysm_ref, ybuf, sem, *,
                    n_prompt_tiles):
    tc = COMBINE_TOKENS
    i = pl.program_id(0)
    buf = i % 2

    def row_copy(p_ref, b, r, slot):
        return pltpu.make_async_copy(ys_ref.at[pl.ds(p_ref[2 * r + slot], 1)],
                                     ybuf.at[b, slot, pl.ds(r, 1)], sem.at[b])

    def start_rows(p_ref, b):
        def issue(r, carry):
            row_copy(p_ref, b, r, 0).start()
            row_copy(p_ref, b, r, 1).start()
            return carry
        lax.fori_loop(0, tc, issue, 0, unroll=8)

    @pl.when(i == 0)
    def _():
        start_rows(pos_ref, 0)

    @pl.when(i + 1 < pl.num_programs(0))
    def _():
        start_rows(pos_next_ref, 1 - buf)

    def drain(r, carry):
        row_copy(pos_ref, buf, r, 0).wait()
        row_copy(pos_ref, buf, r, 1).wait()
        return carry

    lax.fori_loop(0, tc, drain, 0, unroll=8)

    route = route_ref[...]
    w1 = route[:, _R_W1:_R_W1 + 1]
    w2 = route[:, _R_W2:_R_W2 + 1]
    x2 = x1_ref[...] + (w1 * ybuf[buf, 0] + w2 * ybuf[buf, 1])
    r = lax.rsqrt(jnp.mean(x2 * x2, axis=-1, keepdims=True) + EPS)
    y = (x2 * r) * gf_ref[...]
    is_prompt = pl.program_id(0) < n_prompt_tiles

    @pl.when(is_prompt)
    def _():
        yp_ref[...] = y

    @pl.when(jnp.logical_not(is_prompt))
    def _():
        ysm_ref[...] = y


def _combine(pos_flat, x1, route, gf, ys, n_prompt):
    n = x1.shape[0]
    tc = COMBINE_TOKENS
    n_prompt_tiles = n_prompt // tc
    n_steps = n // tc
    return pl.pallas_call(
        functools.partial(_combine_kernel, n_prompt_tiles=n_prompt_tiles),
        grid=(n_steps,),
        in_specs=[pl.BlockSpec((2 * tc,), lambda i: (i,), memory_space=pltpu.SMEM),
                  pl.BlockSpec((2 * tc,), lambda i: (jnp.minimum(i + 1, n_steps - 1),),
                               memory_space=pltpu.SMEM),
                  pl.BlockSpec((tc, D_MODEL), lambda i: (i, 0)),
                  pl.BlockSpec((tc, LANES), lambda i: (i, 0)),
                  pl.BlockSpec((1, D_MODEL), lambda i: (0, 0)),
                  pl.BlockSpec(memory_space=pl.ANY)],
        out_specs=_split_token_specs(tc, D_MODEL, n_prompt_tiles),
        out_shape=(jax.ShapeDtypeStruct((n_prompt, D_MODEL), F32),
                   jax.ShapeDtypeStruct((n - n_prompt, D_MODEL), F32)),
        scratch_shapes=[pltpu.VMEM((2, 2, tc, D_MODEL), F32), pltpu.SemaphoreType.DMA((2,))],
        compiler_params=_cparams(1),
        name="combine",
    )(pos_flat, pos_flat, x1, route, gf, ys)


def _pack_w_in(w_in):
    o_q = 0
    o_k = o_q + ATTN_WIDTH
    o_v = o_k + KV_WIDTH
    o_qi = o_v + KV_WIDTH
    o_ki = o_qi + IDX_HEADS * IDX_DIM
    o_wi = o_ki + IDX_DIM
    o_p = o_wi + IDX_HEADS
    pad = jnp.zeros((w_in.shape[0], LANES - IDX_DIM - IDX_HEADS), w_in.dtype)
    return jnp.concatenate([w_in[:, :o_p], pad, w_in[:, o_p:]], axis=1).astype(BF16)


def _forward(x_prompt, x_sample, cache_k, cache_v, cache_kidx, state_pool,
             norm1_g, w_in, w_gate, b_gate, w_pool_grp, pool_scale, w_o_attn, w_o_pool, w_out,
             norm2_g, w_rg, b_rg, w_re, b_re, w_eg, w_eu, w_ed, norm_f_g):
    batch, seq, d = x_prompt.shape
    dec_batch, dec_seq, _ = x_sample.shape
    past = cache_k.shape[1]
    n_p = batch * seq
    n_s = dec_batch * dec_seq
    n = n_p + n_s
    tm = TOK_TILE
    assert d == D_MODEL and seq % tm == 0 and n_s % tm == 0 and n % (2 * tm) == 0
    assert past % KEY_BLOCK == 0 and dec_seq >= POOL_STATE and dec_seq <= KEY_BLOCK

    xp = x_prompt.reshape(n_p, d)
    xs = x_sample.reshape(n_s, d)
    pos_tab = jnp.concatenate([jnp.arange(seq), past + jnp.arange(n_s) % dec_seq])
    tables = _rope_tables(pos_tab)
    h, q, k, v, qi, kw, pin = _in_proj(xp, xs, norm1_g.reshape(1, d), _pack_w_in(w_in), tables, seq // tm)

    oa_p = _dsa_prompt(q, qi, kw, k, v, batch, seq)
    oa_s = _dsa_sample(q, qi, kw, k, v,
                       cache_k.reshape(dec_batch * past, KV_WIDTH),
                       cache_v.reshape(dec_batch * past, KV_WIDTH),
                       cache_kidx.reshape(dec_batch * past, IDX_DIM),
                       n_p, dec_batch, dec_seq, past)

    w_grp16 = w_pool_grp.astype(BF16)
    scale = pool_scale.reshape(1, POOL_WIDTH)
    op_p = _pool(pin, w_grp16, scale, 0, batch, seq, 0)
    halo = jnp.concatenate([jnp.zeros((dec_batch, 1, POOL_WIDTH), F32), state_pool], axis=1)
    op_s = _pool(pin, w_grp16, scale, n_p // dec_seq, dec_batch, dec_seq, past, halo=halo)

    m = _merge(h, oa_p, oa_s, op_p, op_s, w_gate.astype(BF16), b_gate.reshape(1, 2 * d),
               w_o_attn.astype(BF16), w_o_pool.astype(BF16))

    w_r = jnp.concatenate([w_rg, w_re, jnp.zeros((d, LANES - MOE_GROUPS - N_EXPERTS), F32)], axis=1)
    b_r = jnp.concatenate([b_rg, b_re, jnp.zeros((LANES - MOE_GROUPS - N_EXPERTS,), F32)]).reshape(1, LANES)
    wr_hi = w_r.astype(BF16)
    wr_lo = (w_r - wr_hi.astype(F32)).astype(BF16)
    wr_split = jnp.concatenate([wr_hi, wr_lo], axis=1)
    x1, h2, route, counts = _out_proj(m, xp, xs, w_out.astype(BF16), norm2_g.reshape(1, d), wr_split, b_r)

    te = EXP_TILE
    n_tiles = (2 * n) // te + N_EXPERTS
    cnt = counts[0, _R_LOGIT0:_R_LOGIT0 + N_EXPERTS].astype(I32)
    tiles_per_e = (cnt + te - 1) // te
    tile_end = jnp.cumsum(tiles_per_e)
    row_start = (tile_end - tiles_per_e) * te
    n_used = tile_end[-1:].astype(I32)
    tile_ids = jnp.arange(n_tiles, dtype=I32)
    tile_expert = jnp.minimum(
        jnp.sum((tile_end[None, :] <= tile_ids[:, None]).astype(I32), axis=1), N_EXPERTS - 1)
    last_e = tile_expert[jnp.maximum(n_used[0] - 1, 0)]
    tile_expert = jnp.where(tile_ids < n_used[0], tile_expert, last_e)
    eid = route[:, _R_E1:_R_E2 + 1].astype(I32)
    rank = route[:, _R_RANK1:_R_RANK2 + 1].astype(I32)
    pos_flat = (row_start[eid] + rank).reshape(2 * n)

    xs = _dispatch(pos_flat, h2, n_tiles * te)
    ys = _experts(tile_expert, n_used, xs, w_eg, w_eu, w_ed)
    y_p, y_s = _combine(pos_flat, x1, route, norm_f_g.reshape(1, d), ys, n_p)

    y_prompt = y_p.reshape(batch, seq, d)
    y_sample = y_s.reshape(dec_batch, dec_seq, d)
    k_p = k[:n_p].reshape(1, batch, seq, N_KV_HEADS, HEAD_DIM)
    v_p = v[:n_p].reshape(1, batch, seq, N_KV_HEADS, HEAD_DIM)
    ki_p = kw[:n_p, :IDX_DIM].reshape(1, batch, seq, IDX_DIM)
    pool_p = pin[:n_p].reshape(batch, seq, POOL_WIDTH)[None, :, seq - POOL_STATE:, :]
    k_s = k[n_p:].reshape(1, dec_batch, dec_seq, N_KV_HEADS, HEAD_DIM)
    v_s = v[n_p:].reshape(1, dec_batch, dec_seq, N_KV_HEADS, HEAD_DIM)
    ki_s = kw[n_p:, :IDX_DIM].reshape(1, dec_batch, dec_seq, IDX_DIM)
    pool_s = pin[n_p:].reshape(dec_batch, dec_seq, POOL_WIDTH)[None, :, dec_seq - POOL_STATE:, :]
    return (y_prompt, y_sample, k_p, v_p, ki_p, pool_p, k_s, v_s, ki_s, pool_s)


def kernel(x_prompt, x_sample, cache_k, cache_v, cache_kidx, state_pool, norm1_g, w_in, w_gate, b_gate,
           w_pool_grp, pool_scale, w_o_attn, w_o_pool, w_out, norm2_g, w_router_group, b_router_group,
           w_router_expert, b_router_expert, w_exp_gate, w_exp_up, w_exp_down, norm_f_g):
    assert cache_k.shape[0] == 1, "single-layer model"
    return _forward(x_prompt, x_sample, cache_k[0], cache_v[0], cache_kidx[0], state_pool[0],
                    norm1_g[0], w_in[0], w_gate[0], b_gate[0], w_pool_grp[0], pool_scale[0],
                    w_o_attn[0], w_o_pool[0], w_out[0], norm2_g[0], w_router_group[0],
                    b_router_group[0], w_router_expert[0], b_router_expert[0],
                    w_exp_gate[0], w_exp_up[0], w_exp_down[0], norm_f_g)
```

```python
import functools

import numpy as np
import jax
import jax.numpy as jnp
from jax import lax
from jax.experimental import pallas as pl
from jax.experimental.pallas import tpu as pltpu

F32 = jnp.float32
BF16 = jnp.bfloat16
I32 = jnp.int32

D_MODEL = 2048
CHUNK = 64
N_HEADS = 8
N_KV_HEADS = 2
HEAD_DIM = 128
GQA_GROUP = N_HEADS // N_KV_HEADS
ATTN_WIDTH = N_HEADS * HEAD_DIM
KV_WIDTH = N_KV_HEADS * HEAD_DIM
IDX_HEADS = 16
IDX_DIM = 64
IDX_PAIRS = IDX_HEADS // 2
TOPK_MAX = 256
POOL_WINDOWS = (2, 4, 8, 16)
POOL_GROUPS = 4
POOL_WIDTH = D_MODEL // 2
POOL_GROUP_DIM = POOL_WIDTH // POOL_GROUPS
POOL_STATE = 15
POOL_HALO = 16
MOE_GROUPS = 4
EXPERTS_PER_GROUP = 8
N_EXPERTS = MOE_GROUPS * EXPERTS_PER_GROUP
D_EXPERT = 512
ROPE_THETA = 10000.0
CHUNK_SHIFT = CHUNK.bit_length() - 1
GROUP_SHIFT = EXPERTS_PER_GROUP.bit_length() - 1
EPS = 1e-6

LANES = 128
INT_MIN = -(2 ** 31)
NEG_BIG = -1e30
LOG2_E = 1.4426950408889634
VMEM_LIMIT = 56 * 1024 * 1024

KEY_BLOCK = 256
TOK_TILE = 256
EXP_TILE = 256


def _cparams(n_axes):
    return pltpu.CompilerParams(dimension_semantics=("arbitrary",) * n_axes,
                                vmem_limit_bytes=VMEM_LIMIT)


def _dot(a, b):
    return jnp.dot(a, b, preferred_element_type=F32)


def _dot_nt(a, b):
    return lax.dot_general(a, b, (((1,), (1,)), ((), ())), preferred_element_type=F32)


_C_Q = 0
_C_K = _C_Q + ATTN_WIDTH
_C_V = _C_K + KV_WIDTH
_C_QI = _C_V + KV_WIDTH
_C_KW = _C_QI + IDX_HEADS * IDX_DIM
_C_P = _C_KW + LANES
_C_END = _C_P + POOL_WIDTH


def _rope128(y, cos, sin_signed):
    return y * cos + pltpu.roll(y, HEAD_DIM // 2, 1) * sin_signed


def _rope64(y, cos, sin_a, sin_b):
    half = IDX_DIM // 2
    return y * cos + pltpu.roll(y, LANES - half, 1) * sin_a + pltpu.roll(y, half, 1) * sin_b


def _inproj_kernel(xp_ref, xs_ref, g_ref, w_ref, ck_ref, sk_ref, cq_ref, sq_ref, ci_ref, sia_ref, sib_ref,
                   cw_ref, swa_ref, swb_ref,
                   h_ref, q_ref, k_ref, v_ref, qi_ref, kw_ref, p_ref, *, n_prompt_tiles):
    x = jnp.where(pl.program_id(0) < n_prompt_tiles, xp_ref[...], xs_ref[...])
    r = lax.rsqrt(jnp.mean(x * x, axis=-1, keepdims=True) + EPS)
    h = ((x * r) * g_ref[...]).astype(BF16)
    h_ref[...] = h
    cq, sq = cq_ref[...], sq_ref[...]
    for c in range(ATTN_WIDTH // 256):
        y = _dot(h, w_ref[:, _C_Q + c * 256:_C_Q + (c + 1) * 256])
        for s in range(2):
            q_ref[:, c * 256 + s * LANES:c * 256 + (s + 1) * LANES] = _rope128(
                y[:, s * LANES:(s + 1) * LANES], cq, sq).astype(BF16)
    ck, sk = ck_ref[...], sk_ref[...]
    y = _dot(h, w_ref[:, _C_K:_C_K + KV_WIDTH])
    for s in range(N_KV_HEADS):
        k_ref[:, s * LANES:(s + 1) * LANES] = _rope128(y[:, s * LANES:(s + 1) * LANES], ck, sk)
    v_ref[...] = _dot(h, w_ref[:, _C_V:_C_V + KV_WIDTH])
    ci, sia, sib = ci_ref[...], sia_ref[...], sib_ref[...]
    for c in range(IDX_PAIRS // 2):
        y = _dot(h, w_ref[:, _C_QI + c * 256:_C_QI + (c + 1) * 256])
        for s in range(2):
            qi_ref[2 * c + s] = _rope64(y[:, s * LANES:(s + 1) * LANES], ci, sia, sib).astype(BF16)
    y = _dot(h, w_ref[:, _C_KW:_C_KW + LANES])
    kw_ref[...] = _rope64(y, cw_ref[...], swa_ref[...], swb_ref[...])
    for c in range(POOL_WIDTH // 256):
        p_ref[:, c * 256:(c + 1) * 256] = _dot(h, w_ref[:, _C_P + c * 256:_C_P + (c + 1) * 256])


def _rope_tables(pos):
    pos = pos.astype(F32)[:, None]

    def cs(dim):
        half = dim // 2
        inv = ROPE_THETA ** (-jnp.arange(half, dtype=F32) / half)
        ang = pos * inv[None, :]
        return jnp.cos(ang), jnp.sin(ang)

    c, s = cs(HEAD_DIM)
    ck = jnp.concatenate([c, c], axis=1)
    sk = jnp.concatenate([-s, s], axis=1)
    qs = HEAD_DIM ** -0.5 * LOG2_E
    c, s = cs(IDX_DIM)
    z = jnp.zeros_like(s)
    ci = jnp.concatenate([c, c, c, c], axis=1)
    sia = jnp.concatenate([-s, z, -s, z], axis=1)
    sib = jnp.concatenate([z, s, z, s], axis=1)
    iscale = IDX_DIM ** -0.5
    n = pos.shape[0]
    wscale = jnp.full((n, IDX_HEADS), IDX_HEADS ** -0.5, F32)
    pad = jnp.zeros((n, LANES - IDX_DIM - IDX_HEADS), F32)
    zz = jnp.zeros((n, LANES - IDX_DIM), F32)
    cw = jnp.concatenate([c, c, wscale, pad], axis=1)
    swa = jnp.concatenate([-s, z, zz], axis=1)
    swb = jnp.concatenate([z, s, zz], axis=1)
    return (ck, sk, ck * qs, sk * qs, ci * iscale, sia * iscale, sib * iscale, cw, swa, swb)


def _split_token_specs(tm, width, n_prompt_tiles):
    prompt = pl.BlockSpec((tm, width), lambda i: (jnp.minimum(i, n_prompt_tiles - 1), 0))
    sample = pl.BlockSpec((tm, width), lambda i: (jnp.maximum(i - n_prompt_tiles, 0), 0))
    return prompt, sample


def _in_proj(xp, xs, g1, w_pack, tables, tiles_per_seq):
    tm = TOK_TILE
    n_prompt_tiles = xp.shape[0] // tm
    n = xp.shape[0] + xs.shape[0]
    n_tiles = n // tm

    def tab_idx(i):
        return (jnp.where(i < n_prompt_tiles, i % tiles_per_seq, tiles_per_seq + i - n_prompt_tiles), 0)

    tok = lambda w: pl.BlockSpec((tm, w), lambda i: (i, 0))
    tab = pl.BlockSpec((tm, LANES), tab_idx)
    out_shape = (
        jax.ShapeDtypeStruct((n, D_MODEL), BF16),
        jax.ShapeDtypeStruct((n, ATTN_WIDTH), BF16),
        jax.ShapeDtypeStruct((n, KV_WIDTH), F32),
        jax.ShapeDtypeStruct((n, KV_WIDTH), F32),
        jax.ShapeDtypeStruct((IDX_PAIRS, n, LANES), BF16),
        jax.ShapeDtypeStruct((n, LANES), F32),
        jax.ShapeDtypeStruct((n, POOL_WIDTH), F32),
    )
    return pl.pallas_call(
        functools.partial(_inproj_kernel, n_prompt_tiles=n_prompt_tiles),
        grid=(n_tiles,),
        in_specs=[*_split_token_specs(tm, D_MODEL, n_prompt_tiles),
                  pl.BlockSpec((1, D_MODEL), lambda i: (0, 0)),
                  pl.BlockSpec((D_MODEL, _C_END), lambda i: (0, 0), pipeline_mode=pl.Buffered(1))]
                 + [tab] * 10,
        out_specs=(tok(D_MODEL), tok(ATTN_WIDTH), tok(KV_WIDTH), tok(KV_WIDTH),
                   pl.BlockSpec((IDX_PAIRS, tm, LANES), lambda i: (0, i, 0)),
                   tok(LANES), tok(POOL_WIDTH)),
        out_shape=out_shape,
        compiler_params=_cparams(1),
        name="in_proj",
    )(xp, xs, g1, w_pack, *tables)


def _ordinal_to_f32(k):
    return lax.bitcast_convert_type(k ^ ((k >> 31) & 0x7FFFFFFF), F32)


def _kth_largest_score(topk, shape, n_key_rows, count_where):
    def count_ge(cand):
        cand_f = _ordinal_to_f32(cand)
        return count_where(lambda keys, kb: keys >= jnp.broadcast_to(cand_f, keys.shape))

    c0 = count_ge(jnp.zeros(shape, I32))
    thr0 = jnp.where(c0 >= topk, 0, INT_MIN).astype(I32)
    cnt0 = jnp.where(c0 >= topk, c0, n_key_rows).astype(I32)

    def bit_step(i, carry):
        thr, cnt = carry
        cand = thr + (jnp.int32(1) << (30 - i))
        c = count_ge(cand)
        ok = c >= topk
        return jnp.where(ok, cand, thr), jnp.where(ok, c, cnt)

    thr, cnt = lax.fori_loop(0, 31, bit_step, (thr0, cnt0))
    has_k = thr > INT_MIN
    thr_f = jnp.where(has_k, _ordinal_to_f32(thr), jnp.finfo(F32).min)
    return thr_f, has_k & (cnt > topk)


def _drop_excess_ties(sc, n_blocks, topk, thr, need_tie, key_index, count_where):
    n_gt = count_where(lambda keys, kb: keys > jnp.broadcast_to(thr, keys.shape))
    n_keep = topk - n_gt

    def idx_step(i, bound):
        cand = bound + (jnp.int32(1) << (14 - i))
        c = count_where(lambda keys, kb: (keys == jnp.broadcast_to(thr, keys.shape))
                        & (key_index(kb) < jnp.broadcast_to(cand, keys.shape)))
        return jnp.where(c <= n_keep, cand, bound)

    bound = lax.fori_loop(0, 15, idx_step, jnp.zeros(thr.shape, I32))
    bound = jnp.where(need_tie, bound, jnp.int32(2 ** 30))

    def drop_block(kb, carry):
        keys = sc[kb]
        drop = (keys == jnp.broadcast_to(thr, keys.shape)) & (key_index(kb) >= jnp.broadcast_to(bound, keys.shape))
        sc[kb] = jnp.where(drop, -jnp.inf, keys)
        return carry

    lax.fori_loop(0, n_blocks, drop_block, 0)


def _stack_queries_t(q_ref, qt):
    for g in range(N_KV_HEADS):
        qg = jnp.concatenate([q_ref[:, (g * GQA_GROUP + j) * HEAD_DIM:(g * GQA_GROUP + j + 1) * HEAD_DIM]
                              for j in range(GQA_GROUP)], axis=0)
        qt[g] = qg.astype(F32).T.astype(BF16)


def _attention_scratch(rows):
    return [
        pltpu.VMEM((N_KV_HEADS, 1, rows), F32),
        pltpu.VMEM((N_KV_HEADS, 1, rows), F32),
        pltpu.VMEM((N_KV_HEADS, HEAD_DIM, rows), F32),
        pltpu.VMEM((2, N_KV_HEADS, KEY_BLOCK, rows), F32),
        pltpu.VMEM((2, N_KV_HEADS, KEY_BLOCK, rows), BF16),
        pltpu.VMEM((2, N_KV_HEADS, 1, rows), F32),
    ]


def _masked_attention(o_ref, n_blocks, n_blocks_all, bias_t, k16, vt16, qt, att, tq):
    m_s, l_s, acc_s, lg_s, p_s, alpha_s = att
    m_s[...] = jnp.full(m_s.shape, NEG_BIG, F32)
    l_s[...] = jnp.zeros(l_s.shape, F32)
    acc_s[...] = jnp.zeros(acc_s.shape, F32)
    p_s[1] = jnp.zeros(p_s.shape[1:], BF16)
    alpha_s[1] = jnp.ones(alpha_s.shape[1:], F32)

    def logits_of(kb, slot):
        r0 = pl.multiple_of(kb * KEY_BLOCK, KEY_BLOCK)
        for g in range(N_KV_HEADS):
            lg_s[slot, g] = _dot(k16[pl.ds(r0, KEY_BLOCK), g * HEAD_DIM:(g + 1) * HEAD_DIM], qt[g])

    def value_product(kb, slot):
        for g in range(N_KV_HEADS):
            acc_s[g] = (alpha_s[slot, g] * acc_s[g]
                        + _dot(vt16[kb, g * HEAD_DIM:(g + 1) * HEAD_DIM, :], p_s[slot, g]))

    logits_of(0, 0)
    last = n_blocks_all - 1

    def step(kb, slot):
        logits_of(jnp.minimum(kb + 1, last), 1 - slot)
        bias = jnp.where(kb < n_blocks, bias_t(jnp.minimum(kb, last)), NEG_BIG)
        for g in range(N_KV_HEADS):
            logits = lg_s[slot, g] + bias
            m_prev = m_s[g]
            m_new = jnp.maximum(m_prev, jnp.max(logits, axis=0, keepdims=True))
            p = jnp.exp2(logits - m_new)
            alpha = jnp.exp2(m_prev - m_new)
            l_s[g] = alpha * l_s[g] + jnp.sum(p, axis=0, keepdims=True)
            p_s[slot, g] = p.astype(BF16)
            alpha_s[slot, g] = alpha
            m_s[g] = m_new
        value_product(jnp.clip(kb - 1, 0, last), 1 - slot)

    def two_steps(i, carry):
        step(2 * i, 0)
        step(2 * i + 1, 1)
        return carry

    n_pairs = (n_blocks + 1) // 2
    lax.fori_loop(0, n_pairs, two_steps, 0)
    value_product(jnp.minimum(2 * n_pairs - 1, last), 1)

    for g in range(N_KV_HEADS):
        o = (acc_s[g] * (1.0 / l_s[g])).T
        for j in range(GQA_GROUP):
            hd = g * GQA_GROUP + j
            o_ref[:, hd * HEAD_DIM:(hd + 1) * HEAD_DIM] = o[j * tq:(j + 1) * tq, :].astype(BF16)


def _dsa_prompt_kernel(q_ref, qi_ref, wq_ref, kn_ref, vn_ref, kwn_ref, o_ref,
                       k16, vt16, kia, kib, sc, raw, qt, *att, tq, seq, topk):
    qb = pl.program_id(1)
    n_blocks_all = seq // KEY_BLOCK

    @pl.when(qb == 0)
    def _():
        lane = lax.broadcasted_iota(I32, (seq, LANES), 1)
        kin = jnp.where(lane < IDX_DIM, kwn_ref[...], 0.0)
        k16[...] = kn_ref[...].astype(BF16)
        kia[...] = kin.astype(BF16)
        kib[...] = pltpu.roll(kin, IDX_DIM, 1).astype(BF16)
        for kb in range(n_blocks_all):
            vt16[kb] = vn_ref[kb * KEY_BLOCK:(kb + 1) * KEY_BLOCK, :].T.astype(BF16)

    n_blocks = qb // (KEY_BLOCK // tq) + 1

    qit = qi_ref[...].reshape(IDX_PAIRS * tq, LANES).astype(F32).T.astype(BF16)
    wt = wq_ref[...].T
    w_rows = [wt[IDX_DIM + j:IDX_DIM + j + 1, :] for j in range(IDX_HEADS)]
    _stack_queries_t(q_ref, qt)

    key_row = lax.broadcasted_iota(I32, (KEY_BLOCK, tq), 0)
    q_pos = qb * tq + lax.broadcasted_iota(I32, (KEY_BLOCK, tq), 1)

    last = n_blocks_all - 1
    n_pairs = (n_blocks + 1) // 2

    def raw_scores(kb, slot):
        r0 = pl.multiple_of(kb * KEY_BLOCK, KEY_BLOCK)
        raw[slot, 0] = _dot(kia[pl.ds(r0, KEY_BLOCK), :], qit)
        raw[slot, 1] = _dot(kib[pl.ds(r0, KEY_BLOCK), :], qit)

    def score_step(kb, slot):
        raw_scores(jnp.minimum(kb + 1, last), 1 - slot)
        se = raw[slot, 0]
        so = raw[slot, 1]
        score = jnp.zeros((KEY_BLOCK, tq), F32)
        for p in range(IDX_PAIRS):
            score = (score + jnp.maximum(se[:, p * tq:(p + 1) * tq], 0.0) * w_rows[2 * p]
                     + jnp.maximum(so[:, p * tq:(p + 1) * tq], 0.0) * w_rows[2 * p + 1])
        adm = ((kb * KEY_BLOCK + key_row) >> CHUNK_SHIFT) <= (q_pos >> CHUNK_SHIFT)
        sc[jnp.where(kb < n_blocks, kb, n_blocks_all)] = jnp.where(adm, score, -jnp.inf)

    def two_score_steps(i, carry):
        score_step(2 * i, 0)
        score_step(2 * i + 1, 1)
        return carry

    raw_scores(0, 0)
    lax.fori_loop(0, n_pairs, two_score_steps, 0)

    def count_where(pred):
        def body(i, acc):
            for s in range(2):
                kb = 2 * i + s
                kbc = jnp.minimum(kb, last)
                m = pred(sc[kbc], kbc).astype(I32)
                part = jnp.sum(m.reshape(KEY_BLOCK // 8, 8, tq), axis=0)
                acc = acc + jnp.where(kb < n_blocks, part, 0)
            return acc
        acc = lax.fori_loop(0, n_pairs, body, jnp.zeros((8, tq), I32))
        return jnp.sum(acc, axis=0, keepdims=True)

    thr, need_tie = _kth_largest_score(topk, (1, tq), seq, count_where)

    @pl.when(jnp.max(need_tie.astype(I32)) > 0)
    def _():
        _drop_excess_ties(sc, n_blocks, topk, thr, need_tie,
                          lambda kb: kb * KEY_BLOCK + key_row, count_where)

    thr_b = jnp.broadcast_to(thr, (KEY_BLOCK, tq))

    def bias_t(kb):
        bias = jnp.where(sc[kb] >= thr_b, 0.0, NEG_BIG)
        return jnp.concatenate([bias] * GQA_GROUP, axis=1)

    _masked_attention(o_ref, n_blocks, n_blocks_all, bias_t, k16, vt16, qt, att, tq)


def _dsa_sample_kernel(q_ref, qi_ref, wq_ref, kn_ref, vn_ref, kwn_ref, kc_ref, vc_ref, kic_ref, o_ref,
                       k16, vt16, kia, kib, wbe, wbo, sc, qt, *att,
                       tq, n_key_rows, n_cache, n_new, topk):
    n_blocks = n_key_rows // KEY_BLOCK
    n_cache_blocks = n_cache // KEY_BLOCK

    lane = lax.broadcasted_iota(I32, (n_new, LANES), 1)
    kin = jnp.where(lane < IDX_DIM, kwn_ref[...], 0.0).astype(BF16)
    n_tail = n_key_rows - n_cache
    k16[0:n_cache, :] = kc_ref[...].astype(BF16)
    for kb in range(n_cache_blocks):
        vt16[kb] = vc_ref[kb * KEY_BLOCK:(kb + 1) * KEY_BLOCK, :].T.astype(BF16)
    v_tail = jnp.concatenate([vn_ref[...], jnp.zeros((n_tail - n_new, KV_WIDTH), F32)], axis=0)
    vt16[n_cache_blocks] = v_tail.T.astype(BF16)
    kic = jnp.concatenate([kic_ref[...], jnp.zeros((n_cache, LANES - IDX_DIM), F32)], axis=1)
    kia[0:n_cache, :] = kic.astype(BF16)
    kib[0:n_cache, :] = pltpu.roll(kic, IDX_DIM, 1).astype(BF16)
    k16[n_cache:n_key_rows, :] = jnp.zeros((n_tail, KV_WIDTH), BF16)
    kia[n_cache:n_key_rows, :] = jnp.zeros((n_tail, LANES), BF16)
    kib[n_cache:n_key_rows, :] = jnp.zeros((n_tail, LANES), BF16)
    k16[n_cache:n_cache + n_new, :] = kn_ref[...].astype(BF16)
    kia[n_cache:n_cache + n_new, :] = kin
    kib[n_cache:n_cache + n_new, :] = pltpu.roll(kin.astype(F32), IDX_DIM, 1).astype(BF16)

    wq = wq_ref[...]
    for j in range(IDX_HEADS):
        col = jnp.broadcast_to(wq[:, IDX_DIM + j:IDX_DIM + j + 1], (tq, LANES))
        if j % 2 == 0:
            wbe[j // 2] = col
        else:
            wbo[j // 2] = col
    _stack_queries_t(q_ref, qt)

    lane = lax.broadcasted_iota(I32, (tq, KEY_BLOCK), 1)

    qi2 = qi_ref[...].reshape(IDX_PAIRS * tq, LANES)

    def score_block(kb, carry):
        r0 = pl.multiple_of(kb * KEY_BLOCK, KEY_BLOCK)
        se = _dot_nt(qi2, kia[pl.ds(r0, KEY_BLOCK), :]).reshape(IDX_PAIRS, tq, KEY_BLOCK)
        so = _dot_nt(qi2, kib[pl.ds(r0, KEY_BLOCK), :]).reshape(IDX_PAIRS, tq, KEY_BLOCK)
        score = jnp.zeros((tq, KEY_BLOCK), F32)
        for p in range(IDX_PAIRS):
            we = wbe[p]
            wo = wbo[p]
            we2 = jnp.concatenate([we, we], axis=1)
            wo2 = jnp.concatenate([wo, wo], axis=1)
            score = score + jnp.maximum(se[p], 0.0) * we2 + jnp.maximum(so[p], 0.0) * wo2
        adm = (kb * KEY_BLOCK + lane) < (n_cache + n_new)
        sc[kb] = jnp.where(adm, score, -jnp.inf)
        return carry

    lax.fori_loop(0, n_blocks, score_block, 0)

    def count_where(pred):
        def body(kb, acc):
            m = pred(sc[kb], kb).astype(I32)
            return acc + m[:, :LANES] + m[:, LANES:]
        acc = lax.fori_loop(0, n_blocks, body, jnp.zeros((tq, LANES), I32))
        return jnp.sum(acc, axis=1, keepdims=True)

    thr, need_tie = _kth_largest_score(topk, (tq, 1), n_key_rows, count_where)

    @pl.when(jnp.max(need_tie.astype(I32)) > 0)
    def _():
        _drop_excess_ties(sc, n_blocks, topk, thr, need_tie,
                          lambda kb: kb * KEY_BLOCK + lane, count_where)

    thr_b = jnp.broadcast_to(thr, (tq, KEY_BLOCK))

    def bias_t(kb):
        bias = jnp.where(sc[kb] >= thr_b, 0.0, NEG_BIG)
        return jnp.concatenate([bias] * GQA_GROUP, axis=0).T

    _masked_attention(o_ref, n_blocks, n_blocks, bias_t, k16, vt16, qt, att, tq)


def _dsa_sample_scratch(tq, n_key_rows):
    rows = GQA_GROUP * tq
    return [
        pltpu.VMEM((n_key_rows, KV_WIDTH), BF16),
        pltpu.VMEM((n_key_rows // KEY_BLOCK, KV_WIDTH, KEY_BLOCK), BF16),
        pltpu.VMEM((n_key_rows, LANES), BF16),
        pltpu.VMEM((n_key_rows, LANES), BF16),
        pltpu.VMEM((IDX_PAIRS, tq, LANES), F32),
        pltpu.VMEM((IDX_PAIRS, tq, LANES), F32),
        pltpu.VMEM((n_key_rows // KEY_BLOCK, tq, KEY_BLOCK), F32),
        pltpu.VMEM((N_KV_HEADS, HEAD_DIM, rows), BF16),
    ] + _attention_scratch(rows)


def _dsa_prompt(q, qi, kw, k, v, batch, seq):
    tq = 128
    nqb = seq // tq
    topk = min(TOPK_MAX, seq // 4)
    kern = functools.partial(_dsa_prompt_kernel, tq=tq, seq=seq, topk=topk)
    rows_q = GQA_GROUP * tq
    scratch = [
        pltpu.VMEM((seq, KV_WIDTH), BF16),
        pltpu.VMEM((seq // KEY_BLOCK, KV_WIDTH, KEY_BLOCK), BF16),
        pltpu.VMEM((seq, LANES), BF16),
        pltpu.VMEM((seq, LANES), BF16),
        pltpu.VMEM((seq // KEY_BLOCK + 1, KEY_BLOCK, tq), F32),
        pltpu.VMEM((2, 2, KEY_BLOCK, IDX_PAIRS * tq), F32),
        pltpu.VMEM((N_KV_HEADS, HEAD_DIM, rows_q), BF16),
    ] + _attention_scratch(rows_q)
    qmap = lambda b, i: (b * nqb + i, 0)
    smap = lambda b, i: (b, 0)
    return pl.pallas_call(
        kern,
        grid=(batch, nqb),
        in_specs=[pl.BlockSpec((tq, ATTN_WIDTH), qmap),
                  pl.BlockSpec((IDX_PAIRS, tq, LANES), lambda b, i: (0, b * nqb + i, 0)),
                  pl.BlockSpec((tq, LANES), qmap),
                  pl.BlockSpec((seq, KV_WIDTH), smap),
                  pl.BlockSpec((seq, KV_WIDTH), smap),
                  pl.BlockSpec((seq, LANES), smap)],
        out_specs=pl.BlockSpec((tq, ATTN_WIDTH), qmap),
        out_shape=jax.ShapeDtypeStruct((batch * seq, ATTN_WIDTH), BF16),
        scratch_shapes=scratch,
        compiler_params=_cparams(2),
        name="dsa_prompt",
    )(q, qi, kw, k, v, kw)


def _dsa_sample(q, qi, kw, k, v, cache_k, cache_v, cache_ki, row0, batch, seq, past):
    tq = seq
    n_key_rows = past + KEY_BLOCK
    topk = min(TOPK_MAX, (past + seq) // 4)
    kern = functools.partial(_dsa_sample_kernel, tq=tq, n_key_rows=n_key_rows, n_cache=past, n_new=seq,
                             topk=topk)
    t0 = row0 // seq
    qmap = lambda b, i: (t0 + b, 0)
    cmap = lambda b, i: (b, 0)
    return pl.pallas_call(
        kern,
        grid=(batch, 1),
        in_specs=[pl.BlockSpec((tq, ATTN_WIDTH), qmap),
                  pl.BlockSpec((IDX_PAIRS, tq, LANES), lambda b, i: (0, t0 + b, 0)),
                  pl.BlockSpec((tq, LANES), qmap),
                  pl.BlockSpec((tq, KV_WIDTH), qmap),
                  pl.BlockSpec((tq, KV_WIDTH), qmap),
                  pl.BlockSpec((tq, LANES), qmap),
                  pl.BlockSpec((past, KV_WIDTH), cmap),
                  pl.BlockSpec((past, KV_WIDTH), cmap),
                  pl.BlockSpec((past, IDX_DIM), cmap)],
        out_specs=pl.BlockSpec((tq, ATTN_WIDTH), cmap),
        out_shape=jax.ShapeDtypeStruct((batch * seq, ATTN_WIDTH), BF16),
        scratch_shapes=_dsa_sample_scratch(tq, n_key_rows),
        compiler_params=_cparams(2),
        name="dsa_sample",
    )(q, qi, kw, k, v, kw, cache_k, cache_v, cache_ki)


def _pool_kernel(*refs, tp, pos0, carried):
    if carried:
        p_ref, w_ref, s_ref, o_ref, ext, save = refs
    else:
        p_ref, halo_ref, w_ref, s_ref, o_ref, ext = refs
    i = pl.program_id(1)
    if carried:
        @pl.when(i == 0)
        def _():
            ext[0:POOL_HALO, :] = jnp.zeros((POOL_HALO, POOL_WIDTH), F32)

        @pl.when(i > 0)
        def _():
            ext[0:POOL_HALO, :] = save[...]
    else:
        ext[0:POOL_HALO, :] = halo_ref[0]
    x = p_ref[...]
    ext[POOL_HALO:POOL_HALO + tp, :] = x
    if carried:
        save[...] = x[tp - POOL_HALO:tp, :]
    pos = pos0 + i * tp + lax.broadcasted_iota(I32, (tp, 1), 0)
    for g, w in enumerate(POOL_WINDOWS):
        c0, c1 = g * POOL_GROUP_DIM, (g + 1) * POOL_GROUP_DIM
        xg = x[:, c0:c1]
        tot = xg
        for s in range(1, w):
            tot = tot + ext[POOL_HALO - s:POOL_HALO - s + tp, c0:c1]
        inv = 1.0 / jnp.minimum(pos + 1, w).astype(F32)
        d = (tot * inv - xg).astype(BF16)
        o_ref[:, c0:c1] = (_dot(d, w_ref[g]) * s_ref[:, c0:c1]).astype(BF16)


def _pool(pin, w_grp, scale, tile0, batch, seq, pos0, halo=None):
    carried = halo is None
    tp = min(seq, 256)
    nt = seq // tp
    kern = functools.partial(_pool_kernel, tp=tp, pos0=pos0, carried=carried)
    in_specs = [pl.BlockSpec((tp, POOL_WIDTH), lambda b, i: (tile0 + b * nt + i, 0))]
    args = [pin]
    if not carried:
        in_specs.append(pl.BlockSpec((1, POOL_HALO, POOL_WIDTH), lambda b, i: (b, 0, 0)))
        args.append(halo)
    in_specs += [pl.BlockSpec((POOL_GROUPS, POOL_GROUP_DIM, POOL_GROUP_DIM), lambda b, i: (0, 0, 0)),
                 pl.BlockSpec((1, POOL_WIDTH), lambda b, i: (0, 0))]
    args += [w_grp, scale]
    scratch = [pltpu.VMEM((POOL_HALO + tp, POOL_WIDTH), F32)]
    if carried:
        scratch.append(pltpu.VMEM((POOL_HALO, POOL_WIDTH), F32))
    return pl.pallas_call(
        kern,
        grid=(batch, nt),
        in_specs=in_specs,
        out_specs=pl.BlockSpec((tp, POOL_WIDTH), lambda b, i: (b * nt + i, 0)),
        out_shape=jax.ShapeDtypeStruct((batch * seq, POOL_WIDTH), BF16),
        scratch_shapes=scratch,
        compiler_params=_cparams(2),
        name="pool_prompt" if carried else "pool_sample",
    )(*args)


def _merge_kernel(h_ref, oap_ref, oas_ref, opp_ref, ops_ref, wga_ref, wgp_ref, ba_ref, bp_ref, woa_ref,
                  wop_ref, m_ref, *, n_prompt_tiles):
    h = h_ref[...]
    is_prompt = pl.program_id(1) < n_prompt_tiles
    oa = jnp.where(is_prompt, oap_ref[...], oas_ref[...])
    op = jnp.where(is_prompt, opp_ref[...], ops_ref[...])
    ga = jax.nn.sigmoid(_dot(h, wga_ref[...]) + ba_ref[...])
    gp = jax.nn.sigmoid(_dot(h, wgp_ref[...]) + bp_ref[...])
    a = _dot(oa, woa_ref[...])
    p = _dot(op, wop_ref[...])
    m_ref[...] = (ga * a + gp * p).astype(BF16)


def _merge(h, oa_p, oa_s, op_p, op_s, w_gate, b_gate, w_oa, w_op):
    n = h.shape[0]
    tm = 2 * TOK_TILE
    tn = D_MODEL // 2
    nj = D_MODEL // tn
    npt = oa_p.shape[0] // tm
    prompt = lambda w: pl.BlockSpec((tm, w), lambda j, i: (jnp.minimum(i, npt - 1), 0))
    sample = lambda w: pl.BlockSpec((tm, w), lambda j, i: (jnp.maximum(i - npt, 0), 0))
    return pl.pallas_call(
        functools.partial(_merge_kernel, n_prompt_tiles=npt),
        grid=(nj, n // tm),
        in_specs=[pl.BlockSpec((tm, D_MODEL), lambda j, i: (i, 0)),
                  prompt(ATTN_WIDTH), sample(ATTN_WIDTH), prompt(POOL_WIDTH), sample(POOL_WIDTH),
                  pl.BlockSpec((D_MODEL, tn), lambda j, i: (0, j)),
                  pl.BlockSpec((D_MODEL, tn), lambda j, i: (0, nj + j)),
                  pl.BlockSpec((1, tn), lambda j, i: (0, j)),
                  pl.BlockSpec((1, tn), lambda j, i: (0, nj + j)),
                  pl.BlockSpec((ATTN_WIDTH, tn), lambda j, i: (0, j)),
                  pl.BlockSpec((POOL_WIDTH, tn), lambda j, i: (0, j))],
        out_specs=pl.BlockSpec((tm, tn), lambda j, i: (i, j)),
        out_shape=jax.ShapeDtypeStruct((n, D_MODEL), BF16),
        compiler_params=_cparams(2),
        name="merge",
    )(h, oa_p, oa_s, op_p, op_s, w_gate, w_gate, b_gate, b_gate, w_oa, w_op)


_R_E1, _R_E2, _R_W1, _R_W2, _R_RANK1, _R_RANK2 = range(6)
_R_LOGIT0 = MOE_GROUPS


def _outproj_kernel(m_ref, xp_ref, xs_ref, wo_ref, g2_ref, wr_ref, br_ref,
                    x1_ref, h2_ref, route_ref, counts_ref, run, *, n_prompt_tiles):
    i = pl.program_id(0)
    tm = m_ref.shape[0]

    @pl.when(i == 0)
    def _():
        run[...] = jnp.zeros(run.shape, F32)

    x = jnp.where(i < n_prompt_tiles, xp_ref[...], xs_ref[...])
    x1 = x + _dot(m_ref[...], wo_ref[...])
    x1_ref[...] = x1
    r = lax.rsqrt(jnp.mean(x1 * x1, axis=-1, keepdims=True) + EPS)
    hf = (x1 * r) * g2_ref[...]
    h2_ref[...] = hf

    hi = hf.astype(BF16)
    lo = (hf - hi.astype(F32)).astype(BF16)
    hw = _dot(hi, wr_ref[...])
    lg = hw[:, :LANES] + (hw[:, LANES:] + _dot(lo, wr_ref[:, :LANES])) + br_ref[...]

    lane = lax.broadcasted_iota(I32, (tm, LANES), 1)
    neg_inf = jnp.float32(-jnp.inf)
    is_g = lane < MOE_GROUPS
    cl = jnp.where(is_g, lg, neg_inf)
    gmax = jnp.max(cl, axis=1, keepdims=True)
    g_sel = jnp.min(jnp.where(cl == gmax, lane, LANES), axis=1, keepdims=True)
    den = jnp.sum(jnp.where(is_g, jnp.exp(cl - gmax), 0.0), axis=1, keepdims=True)
    g_val = 1.0 / den
    e_lane = lane - _R_LOGIT0
    in_grp = (e_lane >= 0) & (e_lane < N_EXPERTS) & ((e_lane >> GROUP_SHIFT) == g_sel)
    f1 = jnp.where(in_grp, lg, neg_inf)
    v1 = jnp.max(f1, axis=1, keepdims=True)
    i1 = jnp.min(jnp.where(f1 == v1, lane, LANES), axis=1, keepdims=True)
    f2 = jnp.where(lane == i1, neg_inf, f1)
    v2 = jnp.max(f2, axis=1, keepdims=True)
    i2 = jnp.min(jnp.where(f2 == v2, lane, LANES), axis=1, keepdims=True)
    e2 = jnp.exp(v2 - v1)
    w1 = g_val / (1.0 + e2)
    w2 = g_val * e2 / (1.0 + e2)

    oh1 = lane == i1
    oh2 = lane == i2
    c = (oh1 | oh2).astype(BF16)
    rr = lax.broadcasted_iota(I32, (tm, tm), 0)
    cc = lax.broadcasted_iota(I32, (tm, tm), 1)
    before = (cc < rr).astype(BF16)
    prior = _dot(before, c) + run[...]
    rank1 = jnp.sum(jnp.where(oh1, prior, 0.0), axis=1, keepdims=True)
    rank2 = jnp.sum(jnp.where(oh2, prior, 0.0), axis=1, keepdims=True)
    run[...] = run[...] + jnp.sum(c.astype(F32), axis=0, keepdims=True)
    counts_ref[...] = jnp.broadcast_to(run[...], counts_ref.shape)

    rec = jnp.zeros((tm, LANES), F32)
    for ln, val in ((_R_E1, (i1 - _R_LOGIT0).astype(F32)), (_R_E2, (i2 - _R_LOGIT0).astype(F32)),
                    (_R_W1, w1), (_R_W2, w2), (_R_RANK1, rank1), (_R_RANK2, rank2)):
        rec = jnp.where(lane == ln, val, rec)
    route_ref[...] = rec


def _out_proj(m, xp, xs, w_out, g2, wr_split, b_r):
    n = m.shape[0]
    tm = TOK_TILE
    n_prompt_tiles = xp.shape[0] // tm
    tok = lambda w: pl.BlockSpec((tm, w), lambda i: (i, 0))
    const = lambda s: pl.BlockSpec(s, lambda i: (0,) * len(s))
    return pl.pallas_call(
        functools.partial(_outproj_kernel, n_prompt_tiles=n_prompt_tiles),
        grid=(n // tm,),
        in_specs=[tok(D_MODEL), *_split_token_specs(tm, D_MODEL, n_prompt_tiles),
                  pl.BlockSpec((D_MODEL, D_MODEL), lambda i: (0, 0), pipeline_mode=pl.Buffered(1)),
                  const((1, D_MODEL)), const((D_MODEL, 2 * LANES)), const((1, LANES))],
        out_specs=(tok(D_MODEL), tok(D_MODEL), tok(LANES), const((8, LANES))),
        out_shape=(jax.ShapeDtypeStruct((n, D_MODEL), F32),
                   jax.ShapeDtypeStruct((n, D_MODEL), F32),
                   jax.ShapeDtypeStruct((n, LANES), F32),
                   jax.ShapeDtypeStruct((8, LANES), F32)),
        scratch_shapes=[pltpu.VMEM((1, LANES), F32)],
        compiler_params=_cparams(1),
        name="out_proj",
    )(m, xp, xs, w_out, g2, wr_split, b_r)


DISPATCH_TOKENS = 256


def _dispatch_kernel(pos_ref, h2_ref, xs_in_ref, xs_ref, sem):
    del xs_in_ref
    tb = DISPATCH_TOKENS

    def row_copy(r, slot):
        return pltpu.make_async_copy(h2_ref.at[pl.ds(r, 1)],
                                     xs_ref.at[pl.ds(pos_ref[2 * r + slot], 1)], sem)

    def issue(r, carry):
        row_copy(r, 0).start()
        row_copy(r, 1).start()
        return carry

    lax.fori_loop(0, tb, issue, 0, unroll=8)

    def drain(r, carry):
        row_copy(r, 0).wait()
        row_copy(r, 1).wait()
        return carry

    lax.fori_loop(0, tb, drain, 0, unroll=8)


def _dispatch(pos_flat, h2, n_rows):
    n = h2.shape[0]
    tb = DISPATCH_TOKENS
    xs0 = jnp.zeros((n_rows, D_MODEL), F32)
    return pl.pallas_call(
        _dispatch_kernel,
        grid=(n // tb,),
        in_specs=[pl.BlockSpec((2 * tb,), lambda i: (i,), memory_space=pltpu.SMEM),
                  pl.BlockSpec((tb, D_MODEL), lambda i: (i, 0)),
                  pl.BlockSpec(memory_space=pl.ANY)],
        out_specs=pl.BlockSpec(memory_space=pl.ANY),
        out_shape=jax.ShapeDtypeStruct((n_rows, D_MODEL), F32),
        scratch_shapes=[pltpu.SemaphoreType.DMA(())],
        input_output_aliases={2: 0},
        compiler_params=_cparams(1),
        name="dispatch",
    )(pos_flat, h2, xs0)


def _experts_kernel(te_ref, nu_ref, xs_ref, wg_ref, wu_ref, wd_ref, ys_ref, wg16, wu16, wd16):
    i = pl.program_id(0)
    prev = te_ref[jnp.maximum(i - 1, 0)]
    fresh = (i == 0) | (te_ref[i] != prev)
    used = i < nu_ref[0]

    @pl.when(fresh & used)
    def _():
        wg16[...] = wg_ref[0].astype(BF16)
        wu16[...] = wu_ref[0].astype(BF16)
        wd16[...] = wd_ref[0].astype(BF16)

    @pl.when(used)
    def _():
        x = xs_ref[...].astype(BF16)
        a = _dot(x, wg16[...])
        u = _dot(x, wu16[...])
        act = (a * jax.nn.sigmoid(a)) * u
        ys_ref[...] = _dot(act.astype(BF16), wd16[...])

    @pl.when(jnp.logical_not(used))
    def _():
        ys_ref[...] = jnp.zeros(ys_ref.shape, F32)


def _experts(tile_expert, n_used, xs, w_g, w_u, w_d):
    n_rows = xs.shape[0]
    tm = EXP_TILE
    grid_spec = pltpu.PrefetchScalarGridSpec(
        num_scalar_prefetch=2,
        grid=(n_rows // tm,),
        in_specs=[pl.BlockSpec((tm, D_MODEL), lambda i, te, nu: (i, 0)),
                  pl.BlockSpec((1, D_MODEL, D_EXPERT), lambda i, te, nu: (te[i], 0, 0)),
                  pl.BlockSpec((1, D_MODEL, D_EXPERT), lambda i, te, nu: (te[i], 0, 0)),
                  pl.BlockSpec((1, D_EXPERT, D_MODEL), lambda i, te, nu: (te[i], 0, 0))],
        out_specs=pl.BlockSpec((tm, D_MODEL), lambda i, te, nu: (i, 0)),
        scratch_shapes=[pltpu.VMEM((D_MODEL, D_EXPERT), BF16),
                        pltpu.VMEM((D_MODEL, D_EXPERT), BF16),
                        pltpu.VMEM((D_EXPERT, D_MODEL), BF16)],
    )
    return pl.pallas_call(
        _experts_kernel,
        grid_spec=grid_spec,
        out_shape=jax.ShapeDtypeStruct((n_rows, D_MODEL), F32),
        compiler_params=_cparams(1),
        name="experts",
    )(tile_expert, n_used, xs, w_g, w_u, w_d)


COMBINE_TOKENS = 256


def _combine_kernel(pos_ref, pos_next_ref, x1_ref, route_ref, gf_ref, ys_ref, yp_ref, ysm_ref, ybuf, sem, *,
                    n_prompt_tiles):
    tc = COMBINE_TOKENS
    i = pl.program_id(0)
    buf = i % 2

    def row_copy(p_ref, b, r, slot):
        return pltpu.make_async_copy(ys_ref.at[pl.ds(p_ref[2 * r + slot], 1)],
                                     ybuf.at[b, slot, pl.ds(r, 1)], sem.at[b])

    def start_rows(p_ref, b):
        def issue(r, carry):
            row_copy(p_ref, b, r, 0).start()
            row_copy(p_ref, b, r, 1).start()
            return carry
        lax.fori_loop(0, tc, issue, 0, unroll=8)

    @pl.when(i == 0)
    def _():
        start_rows(pos_ref, 0)

    @pl.when(i + 1 < pl.num_programs(0))
    def _():
        start_rows(pos_next_ref, 1 - buf)

    def drain(r, carry):
        row_copy(pos_ref, buf, r, 0).wait()
        row_copy(pos_ref, buf, r, 1).wait()
        return carry

    lax.fori_loop(0, tc, drain, 0, unroll=8)

    route = route_ref[...]
    w1 = route[:, _R_W1:_R_W1 + 1]
    w2 = route[:, _R_W2:_R_W2 + 1]
    x2 = x1_ref[...] + (w1 * ybuf[buf, 0] + w2 * ybuf[buf, 1])
    r = lax.rsqrt(jnp.mean(x2 * x2, axis=-1, keepdims=True) + EPS)
    y = (x2 * r) * gf_ref[...]
    is_prompt = pl.program_id(0) < n_prompt_tiles

    @pl.when(is_prompt)
    def _():
        yp_ref[...] = y

    @pl.when(jnp.logical_not(is_prompt))
    def _():
        ysm_ref[...] = y


def _combine(pos_flat, x1, route, gf, ys, n_prompt):
    n = x1.shape[0]
    tc = COMBINE_TOKENS
    n_prompt_tiles = n_prompt // tc
    n_steps = n // tc
    return pl.pallas_call(
        functools.partial(_combine_kernel, n_prompt_tiles=n_prompt_tiles),
        grid=(n_steps,),
        in_specs=[pl.BlockSpec((2 * tc,), lambda i: (i,), memory_space=pltpu.SMEM),
                  pl.BlockSpec((2 * tc,), lambda i: (jnp.minimum(i + 1, n_steps - 1),),
                               memory_space=pltpu.SMEM),
                  pl.BlockSpec((tc, D_MODEL), lambda i: (i, 0)),
                  pl.BlockSpec((tc, LANES), lambda i: (i, 0)),
                  pl.BlockSpec((1, D_MODEL), lambda i: (0, 0)),
                  pl.BlockSpec(memory_space=pl.ANY)],
        out_specs=_split_token_specs(tc, D_MODEL, n_prompt_tiles),
        out_shape=(jax.ShapeDtypeStruct((n_prompt, D_MODEL), F32),
                   jax.ShapeDtypeStruct((n - n_prompt, D_MODEL), F32)),
        scratch_shapes=[pltpu.VMEM((2, 2, tc, D_MODEL), F32), pltpu.SemaphoreType.DMA((2,))],
        compiler_params=_cparams(1),
        name="combine",
    )(pos_flat, pos_flat, x1, route, gf, ys)


def _pack_w_in(w_in):
    o_q = 0
    o_k = o_q + ATTN_WIDTH
    o_v = o_k + KV_WIDTH
    o_qi = o_v + KV_WIDTH
    o_ki = o_qi + IDX_HEADS * IDX_DIM
    o_wi = o_ki + IDX_DIM
    o_p = o_wi + IDX_HEADS
    pad = jnp.zeros((w_in.shape[0], LANES - IDX_DIM - IDX_HEADS), w_in.dtype)
    return jnp.concatenate([w_in[:, :o_p], pad, w_in[:, o_p:]], axis=1).astype(BF16)


def _forward(x_prompt, x_sample, cache_k, cache_v, cache_kidx, state_pool,
             norm1_g, w_in, w_gate, b_gate, w_pool_grp, pool_scale, w_o_attn, w_o_pool, w_out,
             norm2_g, w_rg, b_rg, w_re, b_re, w_eg, w_eu, w_ed, norm_f_g):
    batch, seq, d = x_prompt.shape
    dec_batch, dec_seq, _ = x_sample.shape
    past = cache_k.shape[1]
    n_p = batch * seq
    n_s = dec_batch * dec_seq
    n = n_p + n_s
    tm = TOK_TILE
    assert d == D_MODEL and seq % tm == 0 and n_s % tm == 0 and n % (2 * tm) == 0
    assert past % KEY_BLOCK == 0 and dec_seq >= POOL_STATE and dec_seq <= KEY_BLOCK

    xp = x_prompt.reshape(n_p, d)
    xs = x_sample.reshape(n_s, d)
    pos_tab = jnp.concatenate([jnp.arange(seq), past + jnp.arange(n_s) % dec_seq])
    tables = _rope_tables(pos_tab)
    h, q, k, v, qi, kw, pin = _in_proj(xp, xs, norm1_g.reshape(1, d), _pack_w_in(w_in), tables, seq // tm)

    oa_p = _dsa_prompt(q, qi, kw, k, v, batch, seq)
    oa_s = _dsa_sample(q, qi, kw, k, v,
                       cache_k.reshape(dec_batch * past, KV_WIDTH),
                       cache_v.reshape(dec_batch * past, KV_WIDTH),
                       cache_kidx.reshape(dec_batch * past, IDX_DIM),
                       n_p, dec_batch, dec_seq, past)

    w_grp16 = w_pool_grp.astype(BF16)
    scale = pool_scale.reshape(1, POOL_WIDTH)
    op_p = _pool(pin, w_grp16, scale, 0, batch, seq, 0)
    halo = jnp.concatenate([jnp.zeros((dec_batch, 1, POOL_WIDTH), F32), state_pool], axis=1)
    op_s = _pool(pin, w_grp16, scale, n_p // dec_seq, dec_batch, dec_seq, past, halo=halo)

    m = _merge(h, oa_p, oa_s, op_p, op_s, w_gate.astype(BF16), b_gate.reshape(1, 2 * d),
               w_o_attn.astype(BF16), w_o_pool.astype(BF16))

    w_r = jnp.concatenate([w_rg, w_re, jnp.zeros((d, LANES - MOE_GROUPS - N_EXPERTS), F32)], axis=1)
    b_r = jnp.concatenate([b_rg, b_re, jnp.zeros((LANES - MOE_GROUPS - N_EXPERTS,), F32)]).reshape(1, LANES)
    wr_hi = w_r.astype(BF16)
    wr_lo = (w_r - wr_hi.astype(F32)).astype(BF16)
    wr_split = jnp.concatenate([wr_hi, wr_lo], axis=1)
    x1, h2, route, counts = _out_proj(m, xp, xs, w_out.astype(BF16), norm2_g.reshape(1, d), wr_split, b_r)

    te = EXP_TILE
    n_tiles = (2 * n) // te + N_EXPERTS
    cnt = counts[0, _R_LOGIT0:_R_LOGIT0 + N_EXPERTS].astype(I32)
    tiles_per_e = (cnt + te - 1) // te
    tile_end = jnp.cumsum(tiles_per_e)
    row_start = (tile_end - tiles_per_e) * te
    n_used = tile_end[-1:].astype(I32)
    tile_ids = jnp.arange(n_tiles, dtype=I32)
    tile_expert = jnp.minimum(
        jnp.sum((tile_end[None, :] <= tile_ids[:, None]).astype(I32), axis=1), N_EXPERTS - 1)
    last_e = tile_expert[jnp.maximum(n_used[0] - 1, 0)]
    tile_expert = jnp.where(tile_ids < n_used[0], tile_expert, last_e)
    eid = route[:, _R_E1:_R_E2 + 1].astype(I32)
    rank = route[:, _R_RANK1:_R_RANK2 + 1].astype(I32)
    pos_flat = (row_start[eid] + rank).reshape(2 * n)

    xs = _dispatch(pos_flat, h2, n_tiles * te)
    ys = _experts(tile_expert, n_used, xs, w_eg, w_eu, w_ed)
    y_p, y_s = _combine(pos_flat, x1, route, norm_f_g.reshape(1, d), ys, n_p)

    y_prompt = y_p.reshape(batch, seq, d)
    y_sample = y_s.reshape(dec_batch, dec_seq, d)
    k_p = k[:n_p].reshape(1, batch, seq, N_KV_HEADS, HEAD_DIM)
    v_p = v[:n_p].reshape(1, batch, seq, N_KV_HEADS, HEAD_DIM)
    ki_p = kw[:n_p, :IDX_DIM].reshape(1, batch, seq, IDX_DIM)
    pool_p = jnp.stack([pin[(b + 1) * seq - POOL_STATE:(b + 1) * seq] for b in range(batch)])[None]
    k_s = k[n_p:].reshape(1, dec_batch, dec_seq, N_KV_HEADS, HEAD_DIM)
    v_s = v[n_p:].reshape(1, dec_batch, dec_seq, N_KV_HEADS, HEAD_DIM)
    ki_s = kw[n_p:, :IDX_DIM].reshape(1, dec_batch, dec_seq, IDX_DIM)
    pool_s = pin[n_p:].reshape(dec_batch, dec_seq, POOL_WIDTH)[None, :, dec_seq - POOL_STATE:, :]
    return (y_prompt, y_sample, k_p, v_p, ki_p, pool_p, k_s, v_s, ki_s, pool_s)


def kernel(x_prompt, x_sample, cache_k, cache_v, cache_kidx, state_pool, norm1_g, w_in, w_gate, b_gate,
           w_pool_grp, pool_scale, w_o_attn, w_o_pool, w_out, norm2_g, w_router_group, b_router_group,
           w_router_expert, b_router_expert, w_exp_gate, w_exp_up, w_exp_down, norm_f_g):
    assert cache_k.shape[0] == 1, "single-layer model"
    return _forward(x_prompt, x_sample, cache_k[0], cache_v[0], cache_kidx[0], state_pool[0],
                    norm1_g[0], w_in[0], w_gate[0], b_gate[0], w_pool_grp[0], pool_scale[0],
                    w_o_attn[0], w_o_pool[0], w_out[0], norm2_g[0], w_router_group[0],
                    b_router_group[0], w_router_expert[0], b_router_expert[0],
                    w_exp_gate[0], w_exp_up[0], w_exp_down[0], norm_f_g)
```

```python
import functools

import numpy as np
import jax
import jax.numpy as jnp
from jax import lax
from jax.experimental import pallas as pl
from jax.experimental.pallas import tpu as pltpu

F32 = jnp.float32
BF16 = jnp.bfloat16
I32 = jnp.int32

D_MODEL = 2048
CHUNK = 64
N_HEADS = 8
N_KV_HEADS = 2
HEAD_DIM = 128
GQA_GROUP = N_HEADS // N_KV_HEADS
ATTN_WIDTH = N_HEADS * HEAD_DIM
KV_WIDTH = N_KV_HEADS * HEAD_DIM
IDX_HEADS = 16
IDX_DIM = 64
IDX_PAIRS = IDX_HEADS // 2
TOPK_MAX = 256
POOL_WINDOWS = (2, 4, 8, 16)
POOL_GROUPS = 4
POOL_WIDTH = D_MODEL // 2
POOL_GROUP_DIM = POOL_WIDTH // POOL_GROUPS
POOL_STATE = 15
POOL_HALO = 16
MOE_GROUPS = 4
EXPERTS_PER_GROUP = 8
N_EXPERTS = MOE_GROUPS * EXPERTS_PER_GROUP
D_EXPERT = 512
ROPE_THETA = 10000.0
CHUNK_SHIFT = CHUNK.bit_length() - 1
GROUP_SHIFT = EXPERTS_PER_GROUP.bit_length() - 1
EPS = 1e-6

LANES = 128
INT_MIN = -(2 ** 31)
NEG_BIG = -1e30
LOG2_E = 1.4426950408889634
VMEM_LIMIT = 56 * 1024 * 1024

KEY_BLOCK = 256
TOK_TILE = 256
EXP_TILE = 256


def _cparams(n_axes):
    return pltpu.CompilerParams(dimension_semantics=("arbitrary",) * n_axes,
                                vmem_limit_bytes=VMEM_LIMIT)


def _dot(a, b):
    return jnp.dot(a, b, preferred_element_type=F32)


def _dot_nt(a, b):
    return lax.dot_general(a, b, (((1,), (1,)), ((), ())), preferred_element_type=F32)


_C_Q = 0
_C_K = _C_Q + ATTN_WIDTH
_C_V = _C_K + KV_WIDTH
_C_QI = _C_V + KV_WIDTH
_C_KW = _C_QI + IDX_HEADS * IDX_DIM
_C_P = _C_KW + LANES
_C_END = _C_P + POOL_WIDTH


def _rope128(y, cos, sin_signed):
    return y * cos + pltpu.roll(y, HEAD_DIM // 2, 1) * sin_signed


def _rope64(y, cos, sin_a, sin_b):
    half = IDX_DIM // 2
    return y * cos + pltpu.roll(y, LANES - half, 1) * sin_a + pltpu.roll(y, half, 1) * sin_b


def _inproj_kernel(xp_ref, xs_ref, g_ref, w_ref, ck_ref, sk_ref, cq_ref, sq_ref, ci_ref, sia_ref, sib_ref,
                   cw_ref, swa_ref, swb_ref,
                   h_ref, q_ref, k_ref, v_ref, qi_ref, kw_ref, p_ref, *, n_prompt_tiles):
    x = jnp.where(pl.program_id(0) < n_prompt_tiles, xp_ref[...], xs_ref[...])
    r = lax.rsqrt(jnp.mean(x * x, axis=-1, keepdims=True) + EPS)
    h = ((x * r) * g_ref[...]).astype(BF16)
    h_ref[...] = h
    cq, sq = cq_ref[...], sq_ref[...]
    for c in range(ATTN_WIDTH // 256):
        y = _dot(h, w_ref[:, _C_Q + c * 256:_C_Q + (c + 1) * 256])
        for s in range(2):
            q_ref[:, c * 256 + s * LANES:c * 256 + (s + 1) * LANES] = _rope128(
                y[:, s * LANES:(s + 1) * LANES], cq, sq).astype(BF16)
    ck, sk = ck_ref[...], sk_ref[...]
    y = _dot(h, w_ref[:, _C_K:_C_K + KV_WIDTH])
    for s in range(N_KV_HEADS):
        k_ref[:, s * LANES:(s + 1) * LANES] = _rope128(y[:, s * LANES:(s + 1) * LANES], ck, sk)
    v_ref[...] = _dot(h, w_ref[:, _C_V:_C_V + KV_WIDTH])
    ci, sia, sib = ci_ref[...], sia_ref[...], sib_ref[...]
    for c in range(IDX_PAIRS // 2):
        y = _dot(h, w_ref[:, _C_QI + c * 256:_C_QI + (c + 1) * 256])
        for s in range(2):
            qi_ref[2 * c + s] = _rope64(y[:, s * LANES:(s + 1) * LANES], ci, sia, sib).astype(BF16)
    y = _dot(h, w_ref[:, _C_KW:_C_KW + LANES])
    kw_ref[...] = _rope64(y, cw_ref[...], swa_ref[...], swb_ref[...])
    for c in range(POOL_WIDTH // 256):
        p_ref[:, c * 256:(c + 1) * 256] = _dot(h, w_ref[:, _C_P + c * 256:_C_P + (c + 1) * 256])


def _rope_tables(pos):
    pos = pos.astype(F32)[:, None]

    def cs(dim):
        half = dim // 2
        inv = ROPE_THETA ** (-jnp.arange(half, dtype=F32) / half)
        ang = pos * inv[None, :]
        return jnp.cos(ang), jnp.sin(ang)

    c, s = cs(HEAD_DIM)
    ck = jnp.concatenate([c, c], axis=1)
    sk = jnp.concatenate([-s, s], axis=1)
    qs = HEAD_DIM ** -0.5 * LOG2_E
    c, s = cs(IDX_DIM)
    z = jnp.zeros_like(s)
    ci = jnp.concatenate([c, c, c, c], axis=1)
    sia = jnp.concatenate([-s, z, -s, z], axis=1)
    sib = jnp.concatenate([z, s, z, s], axis=1)
    iscale = IDX_DIM ** -0.5
    n = pos.shape[0]
    wscale = jnp.full((n, IDX_HEADS), IDX_HEADS ** -0.5, F32)
    pad = jnp.zeros((n, LANES - IDX_DIM - IDX_HEADS), F32)
    zz = jnp.zeros((n, LANES - IDX_DIM), F32)
    cw = jnp.concatenate([c, c, wscale, pad], axis=1)
    swa = jnp.concatenate([-s, z, zz], axis=1)
    swb = jnp.concatenate([z, s, zz], axis=1)
    return (ck, sk, ck * qs, sk * qs, ci * iscale, sia * iscale, sib * iscale, cw, swa, swb)


def _split_token_specs(tm, width, n_prompt_tiles):
    prompt = pl.BlockSpec((tm, width), lambda i: (jnp.minimum(i, n_prompt_tiles - 1), 0))
    sample = pl.BlockSpec((tm, width), lambda i: (jnp.maximum(i - n_prompt_tiles, 0), 0))
    return prompt, sample


def _in_proj(xp, xs, g1, w_pack, tables, tiles_per_seq):
    tm = TOK_TILE
    n_prompt_tiles = xp.shape[0] // tm
    n = xp.shape[0] + xs.shape[0]
    n_tiles = n // tm

    def tab_idx(i):
        return (jnp.where(i < n_prompt_tiles, i % tiles_per_seq, tiles_per_seq + i - n_prompt_tiles), 0)

    tok = lambda w: pl.BlockSpec((tm, w), lambda i: (i, 0))
    tab = pl.BlockSpec((tm, LANES), tab_idx)
    out_shape = (
        jax.ShapeDtypeStruct((n, D_MODEL), BF16),
        jax.ShapeDtypeStruct((n, ATTN_WIDTH), BF16),
        jax.ShapeDtypeStruct((n, KV_WIDTH), F32),
        jax.ShapeDtypeStruct((n, KV_WIDTH), F32),
        jax.ShapeDtypeStruct((IDX_PAIRS, n, LANES), BF16),
        jax.ShapeDtypeStruct((n, LANES), F32),
        jax.ShapeDtypeStruct((n, POOL_WIDTH), F32),
    )
    return pl.pallas_call(
        functools.partial(_inproj_kernel, n_prompt_tiles=n_prompt_tiles),
        grid=(n_tiles,),
        in_specs=[*_split_token_specs(tm, D_MODEL, n_prompt_tiles),
                  pl.BlockSpec((1, D_MODEL), lambda i: (0, 0)),
                  pl.BlockSpec((D_MODEL, _C_END), lambda i: (0, 0), pipeline_mode=pl.Buffered(1))]
                 + [tab] * 10,
        out_specs=(tok(D_MODEL), tok(ATTN_WIDTH), tok(KV_WIDTH), tok(KV_WIDTH),
                   pl.BlockSpec((IDX_PAIRS, tm, LANES), lambda i: (0, i, 0)),
                   tok(LANES), tok(POOL_WIDTH)),
        out_shape=out_shape,
        compiler_params=_cparams(1),
        name="in_proj",
    )(xp, xs, g1, w_pack, *tables)


def _ordinal_to_f32(k):
    return lax.bitcast_convert_type(k ^ ((k >> 31) & 0x7FFFFFFF), F32)


def _kth_largest_score(topk, shape, n_key_rows, count_where):
    def count_ge(cand):
        cand_f = _ordinal_to_f32(cand)
        return count_where(lambda keys, kb: keys >= jnp.broadcast_to(cand_f, keys.shape))

    c0 = count_ge(jnp.zeros(shape, I32))
    thr0 = jnp.where(c0 >= topk, 0, INT_MIN).astype(I32)
    cnt0 = jnp.where(c0 >= topk, c0, n_key_rows).astype(I32)

    def bit_step(i, carry):
        thr, cnt = carry
        cand = thr + (jnp.int32(1) << (30 - i))
        c = count_ge(cand)
        ok = c >= topk
        return jnp.where(ok, cand, thr), jnp.where(ok, c, cnt)

    thr, cnt = lax.fori_loop(0, 31, bit_step, (thr0, cnt0))
    has_k = thr > INT_MIN
    thr_f = jnp.where(has_k, _ordinal_to_f32(thr), jnp.finfo(F32).min)
    return thr_f, has_k & (cnt > topk)


def _drop_excess_ties(sc, n_blocks, topk, thr, need_tie, key_index, count_where):
    n_gt = count_where(lambda keys, kb: keys > jnp.broadcast_to(thr, keys.shape))
    n_keep = topk - n_gt

    def idx_step(i, bound):
        cand = bound + (jnp.int32(1) << (14 - i))
        c = count_where(lambda keys, kb: (keys == jnp.broadcast_to(thr, keys.shape))
                        & (key_index(kb) < jnp.broadcast_to(cand, keys.shape)))
        return jnp.where(c <= n_keep, cand, bound)

    bound = lax.fori_loop(0, 15, idx_step, jnp.zeros(thr.shape, I32))
    bound = jnp.where(need_tie, bound, jnp.int32(2 ** 30))

    def drop_block(kb, carry):
        keys = sc[kb]
        drop = (keys == jnp.broadcast_to(thr, keys.shape)) & (key_index(kb) >= jnp.broadcast_to(bound, keys.shape))
        sc[kb] = jnp.where(drop, -jnp.inf, keys)
        return carry

    lax.fori_loop(0, n_blocks, drop_block, 0)


def _stack_queries_t(q_ref, qt):
    for g in range(N_KV_HEADS):
        qg = jnp.concatenate([q_ref[:, (g * GQA_GROUP + j) * HEAD_DIM:(g * GQA_GROUP + j + 1) * HEAD_DIM]
                              for j in range(GQA_GROUP)], axis=0)
        qt[g] = qg.astype(F32).T.astype(BF16)


def _attention_scratch(rows):
    return [
        pltpu.VMEM((N_KV_HEADS, 1, rows), F32),
        pltpu.VMEM((N_KV_HEADS, 1, rows), F32),
        pltpu.VMEM((N_KV_HEADS, HEAD_DIM, rows), F32),
        pltpu.VMEM((2, N_KV_HEADS, KEY_BLOCK, rows), F32),
        pltpu.VMEM((2, N_KV_HEADS, KEY_BLOCK, rows), BF16),
        pltpu.VMEM((2, N_KV_HEADS, 1, rows), F32),
    ]


def _masked_attention(o_ref, n_blocks, n_blocks_all, bias_t, k16, vt16, qt, att, tq):
    m_s, l_s, acc_s, lg_s, p_s, alpha_s = att
    m_s[...] = jnp.full(m_s.shape, NEG_BIG, F32)
    l_s[...] = jnp.zeros(l_s.shape, F32)
    acc_s[...] = jnp.zeros(acc_s.shape, F32)
    p_s[1] = jnp.zeros(p_s.shape[1:], BF16)
    alpha_s[1] = jnp.ones(alpha_s.shape[1:], F32)

    def logits_of(kb, slot):
        r0 = pl.multiple_of(kb * KEY_BLOCK, KEY_BLOCK)
        for g in range(N_KV_HEADS):
            lg_s[slot, g] = _dot(k16[pl.ds(r0, KEY_BLOCK), g * HEAD_DIM:(g + 1) * HEAD_DIM], qt[g])

    def value_product(kb, slot):
        for g in range(N_KV_HEADS):
            acc_s[g] = (alpha_s[slot, g] * acc_s[g]
                        + _dot(vt16[kb, g * HEAD_DIM:(g + 1) * HEAD_DIM, :], p_s[slot, g]))

    logits_of(0, 0)
    last = n_blocks_all - 1

    def step(kb, slot):
        logits_of(jnp.minimum(kb + 1, last), 1 - slot)
        bias = jnp.where(kb < n_blocks, bias_t(jnp.minimum(kb, last)), NEG_BIG)
        for g in range(N_KV_HEADS):
            logits = lg_s[slot, g] + bias
            m_prev = m_s[g]
            m_new = jnp.maximum(m_prev, jnp.max(logits, axis=0, keepdims=True))
            p = jnp.exp2(logits - m_new)
            alpha = jnp.exp2(m_prev - m_new)
            l_s[g] = alpha * l_s[g] + jnp.sum(p, axis=0, keepdims=True)
            p_s[slot, g] = p.astype(BF16)
            alpha_s[slot, g] = alpha
            m_s[g] = m_new
        value_product(jnp.clip(kb - 1, 0, last), 1 - slot)

    def two_steps(i, carry):
        step(2 * i, 0)
        step(2 * i + 1, 1)
        return carry

    n_pairs = (n_blocks + 1) // 2
    lax.fori_loop(0, n_pairs, two_steps, 0)
    value_product(jnp.minimum(2 * n_pairs - 1, last), 1)

    for g in range(N_KV_HEADS):
        o = (acc_s[g] * (1.0 / l_s[g])).T
        for j in range(GQA_GROUP):
            hd = g * GQA_GROUP + j
            o_ref[:, hd * HEAD_DIM:(hd + 1) * HEAD_DIM] = o[j * tq:(j + 1) * tq, :].astype(BF16)


def _dsa_prompt_kernel(q_ref, qi_ref, wq_ref, kn_ref, vn_ref, kwn_ref, o_ref,
                       k16, vt16, kia, kib, sc, raw, qt, *att, tq, seq, topk):
    qb = pl.program_id(1)
    n_blocks_all = seq // KEY_BLOCK

    @pl.when(qb == 0)
    def _():
        lane = lax.broadcasted_iota(I32, (seq, LANES), 1)
        kin = jnp.where(lane < IDX_DIM, kwn_ref[...], 0.0)
        k16[...] = kn_ref[...].astype(BF16)
        kia[...] = kin.astype(BF16)
        kib[...] = pltpu.roll(kin, IDX_DIM, 1).astype(BF16)
        for kb in range(n_blocks_all):
            vt16[kb] = vn_ref[kb * KEY_BLOCK:(kb + 1) * KEY_BLOCK, :].T.astype(BF16)

    n_blocks = qb // (KEY_BLOCK // tq) + 1

    qit = qi_ref[...].reshape(IDX_PAIRS * tq, LANES).astype(F32).T.astype(BF16)
    wt = wq_ref[...].T
    w_rows = [wt[IDX_DIM + j:IDX_DIM + j + 1, :] for j in range(IDX_HEADS)]
    _stack_queries_t(q_ref, qt)

    key_row = lax.broadcasted_iota(I32, (KEY_BLOCK, tq), 0)
    q_pos = qb * tq + lax.broadcasted_iota(I32, (KEY_BLOCK, tq), 1)

    last = n_blocks_all - 1
    n_pairs = (n_blocks + 1) // 2

    def raw_scores(kb, slot):
        r0 = pl.multiple_of(kb * KEY_BLOCK, KEY_BLOCK)
        raw[slot, 0] = _dot(kia[pl.ds(r0, KEY_BLOCK), :], qit)
        raw[slot, 1] = _dot(kib[pl.ds(r0, KEY_BLOCK), :], qit)

    def score_step(kb, slot):
        raw_scores(jnp.minimum(kb + 1, last), 1 - slot)
        se = raw[slot, 0]
        so = raw[slot, 1]
        score = jnp.zeros((KEY_BLOCK, tq), F32)
        for p in range(IDX_PAIRS):
            score = (score + jnp.maximum(se[:, p * tq:(p + 1) * tq], 0.0) * w_rows[2 * p]
                     + jnp.maximum(so[:, p * tq:(p + 1) * tq], 0.0) * w_rows[2 * p + 1])
        adm = ((kb * KEY_BLOCK + key_row) >> CHUNK_SHIFT) <= (q_pos >> CHUNK_SHIFT)
        sc[jnp.where(kb < n_blocks, kb, n_blocks_all)] = jnp.where(adm, score, -jnp.inf)

    def two_score_steps(i, carry):
        score_step(2 * i, 0)
        score_step(2 * i + 1, 1)
        return carry

    raw_scores(0, 0)
    lax.fori_loop(0, n_pairs, two_score_steps, 0)

    def count_where(pred):
        def body(i, acc):
            for s in range(2):
                kb = 2 * i + s
                kbc = jnp.minimum(kb, last)
                m = pred(sc[kbc], kbc).astype(I32)
                part = jnp.sum(m.reshape(KEY_BLOCK // 8, 8, tq), axis=0)
                acc = acc + jnp.where(kb < n_blocks, part, 0)
            return acc
        acc = lax.fori_loop(0, n_pairs, body, jnp.zeros((8, tq), I32))
        return jnp.sum(acc, axis=0, keepdims=True)

    thr, need_tie = _kth_largest_score(topk, (1, tq), seq, count_where)

    @pl.when(jnp.max(need_tie.astype(I32)) > 0)
    def _():
        _drop_excess_ties(sc, n_blocks, topk, thr, need_tie,
                          lambda kb: kb * KEY_BLOCK + key_row, count_where)

    thr_b = jnp.broadcast_to(thr, (KEY_BLOCK, tq))

    def bias_t(kb):
        bias = jnp.where(sc[kb] >= thr_b, 0.0, NEG_BIG)
        return jnp.concatenate([bias] * GQA_GROUP, axis=1)

    _masked_attention(o_ref, n_blocks, n_blocks_all, bias_t, k16, vt16, qt, att, tq)


def _dsa_sample_kernel(q_ref, qi_ref, wq_ref, kn_ref, vn_ref, kwn_ref, kc_ref, vc_ref, kic_ref, o_ref,
                       k16, vt16, kia, kib, wbe, wbo, sc, qt, *att,
                       tq, n_key_rows, n_cache, n_new, topk):
    n_blocks = n_key_rows // KEY_BLOCK
    n_cache_blocks = n_cache // KEY_BLOCK

    lane = lax.broadcasted_iota(I32, (n_new, LANES), 1)
    kin = jnp.where(lane < IDX_DIM, kwn_ref[...], 0.0).astype(BF16)
    n_tail = n_key_rows - n_cache
    k16[0:n_cache, :] = kc_ref[...].astype(BF16)
    for kb in range(n_cache_blocks):
        vt16[kb] = vc_ref[kb * KEY_BLOCK:(kb + 1) * KEY_BLOCK, :].T.astype(BF16)
    v_tail = jnp.concatenate([vn_ref[...], jnp.zeros((n_tail - n_new, KV_WIDTH), F32)], axis=0)
    vt16[n_cache_blocks] = v_tail.T.astype(BF16)
    kic = jnp.concatenate([kic_ref[...], jnp.zeros((n_cache, LANES - IDX_DIM), F32)], axis=1)
    kia[0:n_cache, :] = kic.astype(BF16)
    kib[0:n_cache, :] = pltpu.roll(kic, IDX_DIM, 1).astype(BF16)
    k16[n_cache:n_key_rows, :] = jnp.zeros((n_tail, KV_WIDTH), BF16)
    kia[n_cache:n_key_rows, :] = jnp.zeros((n_tail, LANES), BF16)
    kib[n_cache:n_key_rows, :] = jnp.zeros((n_tail, LANES), BF16)
    k16[n_cache:n_cache + n_new, :] = kn_ref[...].astype(BF16)
    kia[n_cache:n_cache + n_new, :] = kin
    kib[n_cache:n_cache + n_new, :] = pltpu.roll(kin.astype(F32), IDX_DIM, 1).astype(BF16)

    wq = wq_ref[...]
    for j in range(IDX_HEADS):
        col = jnp.broadcast_to(wq[:, IDX_DIM + j:IDX_DIM + j + 1], (tq, LANES))
        if j % 2 == 0:
            wbe[j // 2] = col
        else:
            wbo[j // 2] = col
    _stack_queries_t(q_ref, qt)

    lane = lax.broadcasted_iota(I32, (tq, KEY_BLOCK), 1)

    qi2 = qi_ref[...].reshape(IDX_PAIRS * tq, LANES)

    def score_block(kb, carry):
        r0 = pl.multiple_of(kb * KEY_BLOCK, KEY_BLOCK)
        se = _dot_nt(qi2, kia[pl.ds(r0, KEY_BLOCK), :]).reshape(IDX_PAIRS, tq, KEY_BLOCK)
        so = _dot_nt(qi2, kib[pl.ds(r0, KEY_BLOCK), :]).reshape(IDX_PAIRS, tq, KEY_BLOCK)
        score = jnp.zeros((tq, KEY_BLOCK), F32)
        for p in range(IDX_PAIRS):
            we = wbe[p]
            wo = wbo[p]
            we2 = jnp.concatenate([we, we], axis=1)
            wo2 = jnp.concatenate([wo, wo], axis=1)
            score = score + jnp.maximum(se[p], 0.0) * we2 + jnp.maximum(so[p], 0.0) * wo2
        adm = (kb * KEY_BLOCK + lane) < (n_cache + n_new)
        sc[kb] = jnp.where(adm, score, -jnp.inf)
        return carry

    lax.fori_loop(0, n_blocks, score_block, 0)

    def count_where(pred):
        def body(kb, acc):
            m = pred(sc[kb], kb).astype(I32)
            return acc + m[:, :LANES] + m[:, LANES:]
        acc = lax.fori_loop(0, n_blocks, body, jnp.zeros((tq, LANES), I32))
        return jnp.sum(acc, axis=1, keepdims=True)

    thr, need_tie = _kth_largest_score(topk, (tq, 1), n_key_rows, count_where)

    @pl.when(jnp.max(need_tie.astype(I32)) > 0)
    def _():
        _drop_excess_ties(sc, n_blocks, topk, thr, need_tie,
                          lambda kb: kb * KEY_BLOCK + lane, count_where)

    thr_b = jnp.broadcast_to(thr, (tq, KEY_BLOCK))

    def bias_t(kb):
        bias = jnp.where(sc[kb] >= thr_b, 0.0, NEG_BIG)
        return jnp.concatenate([bias] * GQA_GROUP, axis=0).T

    _masked_attention(o_ref, n_blocks, n_blocks, bias_t, k16, vt16, qt, att, tq)


def _dsa_sample_scratch(tq, n_key_rows):
    rows = GQA_GROUP * tq
    return [
        pltpu.VMEM((n_key_rows, KV_WIDTH), BF16),
        pltpu.VMEM((n_key_rows // KEY_BLOCK, KV_WIDTH, KEY_BLOCK), BF16),
        pltpu.VMEM((n_key_rows, LANES), BF16),
        pltpu.VMEM((n_key_rows, LANES), BF16),
        pltpu.VMEM((IDX_PAIRS, tq, LANES), F32),
        pltpu.VMEM((IDX_PAIRS, tq, LANES), F32),
        pltpu.VMEM((n_key_rows // KEY_BLOCK, tq, KEY_BLOCK), F32),
        pltpu.VMEM((N_KV_HEADS, HEAD_DIM, rows), BF16),
    ] + _attention_scratch(rows)


def _dsa_prompt(q, qi, kw, k, v, batch, seq):
    tq = 128
    nqb = seq // tq
    topk = min(TOPK_MAX, seq // 4)
    kern = functools.partial(_dsa_prompt_kernel, tq=tq, seq=seq, topk=topk)
    rows_q = GQA_GROUP * tq
    scratch = [
        pltpu.VMEM((seq, KV_WIDTH), BF16),
        pltpu.VMEM((seq // KEY_BLOCK, KV_WIDTH, KEY_BLOCK), BF16),
        pltpu.VMEM((seq, LANES), BF16),
        pltpu.VMEM((seq, LANES), BF16),
        pltpu.VMEM((seq // KEY_BLOCK + 1, KEY_BLOCK, tq), F32),
        pltpu.VMEM((2, 2, KEY_BLOCK, IDX_PAIRS * tq), F32),
        pltpu.VMEM((N_KV_HEADS, HEAD_DIM, rows_q), BF16),
    ] + _attention_scratch(rows_q)
    qmap = lambda b, i: (b * nqb + i, 0)
    smap = lambda b, i: (b, 0)
    return pl.pallas_call(
        kern,
        grid=(batch, nqb),
        in_specs=[pl.BlockSpec((tq, ATTN_WIDTH), qmap),
                  pl.BlockSpec((IDX_PAIRS, tq, LANES), lambda b, i: (0, b * nqb + i, 0)),
                  pl.BlockSpec((tq, LANES), qmap),
                  pl.BlockSpec((seq, KV_WIDTH), smap),
                  pl.BlockSpec((seq, KV_WIDTH), smap),
                  pl.BlockSpec((seq, LANES), smap)],
        out_specs=pl.BlockSpec((tq, ATTN_WIDTH), qmap),
        out_shape=jax.ShapeDtypeStruct((batch * seq, ATTN_WIDTH), BF16),
        scratch_shapes=scratch,
        compiler_params=_cparams(2),
        name="dsa_prompt",
    )(q, qi, kw, k, v, kw)


def _dsa_sample(q, qi, kw, k, v, cache_k, cache_v, cache_ki, row0, batch, seq, past):
    tq = seq
    n_key_rows = past + KEY_BLOCK
    topk = min(TOPK_MAX, (past + seq) // 4)
    kern = functools.partial(_dsa_sample_kernel, tq=tq, n_key_rows=n_key_rows, n_cache=past, n_new=seq,
                             topk=topk)
    t0 = row0 // seq
    qmap = lambda b, i: (t0 + b, 0)
    cmap = lambda b, i: (b, 0)
    return pl.pallas_call(
        kern,
        grid=(batch, 1),
        in_specs=[pl.BlockSpec((tq, ATTN_WIDTH), qmap),
                  pl.BlockSpec((IDX_PAIRS, tq, LANES), lambda b, i: (0, t0 + b, 0)),
                  pl.BlockSpec((tq, LANES), qmap),
                  pl.BlockSpec((tq, KV_WIDTH), qmap),
                  pl.BlockSpec((tq, KV_WIDTH), qmap),
                  pl.BlockSpec((tq, LANES), qmap),
                  pl.BlockSpec((past, KV_WIDTH), cmap),
                  pl.BlockSpec((past, KV_WIDTH), cmap),
                  pl.BlockSpec((past, IDX_DIM), cmap)],
        out_specs=pl.BlockSpec((tq, ATTN_WIDTH), cmap),
        out_shape=jax.ShapeDtypeStruct((batch * seq, ATTN_WIDTH), BF16),
        scratch_shapes=_dsa_sample_scratch(tq, n_key_rows),
        compiler_params=_cparams(2),
        name="dsa_sample",
    )(q, qi, kw, k, v, kw, cache_k, cache_v, cache_ki)


def _pool_kernel(*refs, tp, pos0, carried):
    if carried:
        p_ref, w_ref, s_ref, o_ref, ext, save = refs
    else:
        p_ref, halo_ref, w_ref, s_ref, o_ref, ext = refs
    i = pl.program_id(1)
    if carried:
        @pl.when(i == 0)
        def _():
            ext[0:POOL_HALO, :] = jnp.zeros((POOL_HALO, POOL_WIDTH), F32)

        @pl.when(i > 0)
        def _():
            ext[0:POOL_HALO, :] = save[...]
    else:
        ext[0:POOL_HALO, :] = halo_ref[0]
    x = p_ref[...]
    ext[POOL_HALO:POOL_HALO + tp, :] = x
    if carried:
        save[...] = x[tp - POOL_HALO:tp, :]
    pos = pos0 + i * tp + lax.broadcasted_iota(I32, (tp, 1), 0)
    for g, w in enumerate(POOL_WINDOWS):
        c0, c1 = g * POOL_GROUP_DIM, (g + 1) * POOL_GROUP_DIM
        xg = x[:, c0:c1]
        tot = xg
        for s in range(1, w):
            tot = tot + ext[POOL_HALO - s:POOL_HALO - s + tp, c0:c1]
        inv = 1.0 / jnp.minimum(pos + 1, w).astype(F32)
        d = (tot * inv - xg).astype(BF16)
        o_ref[:, c0:c1] = (_dot(d, w_ref[g]) * s_ref[:, c0:c1]).astype(BF16)


def _pool(pin, w_grp, scale, tile0, batch, seq, pos0, halo=None):
    carried = halo is None
    tp = min(seq, 256)
    nt = seq // tp
    kern = functools.partial(_pool_kernel, tp=tp, pos0=pos0, carried=carried)
    in_specs = [pl.BlockSpec((tp, POOL_WIDTH), lambda b, i: (tile0 + b * nt + i, 0))]
    args = [pin]
    if not carried:
        in_specs.append(pl.BlockSpec((1, POOL_HALO, POOL_WIDTH), lambda b, i: (b, 0, 0)))
        args.append(halo)
    in_specs += [pl.BlockSpec((POOL_GROUPS, POOL_GROUP_DIM, POOL_GROUP_DIM), lambda b, i: (0, 0, 0)),
                 pl.BlockSpec((1, POOL_WIDTH), lambda b, i: (0, 0))]
    args += [w_grp, scale]
    scratch = [pltpu.VMEM((POOL_HALO + tp, POOL_WIDTH), F32)]
    if carried:
        scratch.append(pltpu.VMEM((POOL_HALO, POOL_WIDTH), F32))
    return pl.pallas_call(
        kern,
        grid=(batch, nt),
        in_specs=in_specs,
        out_specs=pl.BlockSpec((tp, POOL_WIDTH), lambda b, i: (b * nt + i, 0)),
        out_shape=jax.ShapeDtypeStruct((batch * seq, POOL_WIDTH), BF16),
        scratch_shapes=scratch,
        compiler_params=_cparams(2),
        name="pool_prompt" if carried else "pool_sample",
    )(*args)


def _merge_kernel(h_ref, oap_ref, oas_ref, opp_ref, ops_ref, wga_ref, wgp_ref, ba_ref, bp_ref, woa_ref,
                  wop_ref, m_ref, *, n_prompt_tiles):
    h = h_ref[...]
    is_prompt = pl.program_id(1) < n_prompt_tiles
    oa = jnp.where(is_prompt, oap_ref[...], oas_ref[...])
    op = jnp.where(is_prompt, opp_ref[...], ops_ref[...])
    ga = jax.nn.sigmoid(_dot(h, wga_ref[...]) + ba_ref[...])
    gp = jax.nn.sigmoid(_dot(h, wgp_ref[...]) + bp_ref[...])
    a = _dot(oa, woa_ref[...])
    p = _dot(op, wop_ref[...])
    m_ref[...] = (ga * a + gp * p).astype(BF16)


def _merge(h, oa_p, oa_s, op_p, op_s, w_gate, b_gate, w_oa, w_op):
    n = h.shape[0]
    tm = 2 * TOK_TILE
    tn = D_MODEL // 2
    nj = D_MODEL // tn
    npt = oa_p.shape[0] // tm
    prompt = lambda w: pl.BlockSpec((tm, w), lambda j, i: (jnp.minimum(i, npt - 1), 0))
    sample = lambda w: pl.BlockSpec((tm, w), lambda j, i: (jnp.maximum(i - npt, 0), 0))
    return pl.pallas_call(
        functools.partial(_merge_kernel, n_prompt_tiles=npt),
        grid=(nj, n // tm),
        in_specs=[pl.BlockSpec((tm, D_MODEL), lambda j, i: (i, 0)),
                  prompt(ATTN_WIDTH), sample(ATTN_WIDTH), prompt(POOL_WIDTH), sample(POOL_WIDTH),
                  pl.BlockSpec((D_MODEL, tn), lambda j, i: (0, j)),
                  pl.BlockSpec((D_MODEL, tn), lambda j, i: (0, nj + j)),
                  pl.BlockSpec((1, tn), lambda j, i: (0, j)),
                  pl.BlockSpec((1, tn), lambda j, i: (0, nj + j)),
                  pl.BlockSpec((ATTN_WIDTH, tn), lambda j, i: (0, j)),
                  pl.BlockSpec((POOL_WIDTH, tn), lambda j, i: (0, j))],
        out_specs=pl.BlockSpec((tm, tn), lambda j, i: (i, j)),
        out_shape=jax.ShapeDtypeStruct((n, D_MODEL), BF16),
        compiler_params=_cparams(2),
        name="merge",
    )(h, oa_p, oa_s, op_p, op_s, w_gate, w_gate, b_gate, b_gate, w_oa, w_op)


_R_E1, _R_E2, _R_W1, _R_W2, _R_RANK1, _R_RANK2 = range(6)
_R_LOGIT0 = MOE_GROUPS


def _outproj_kernel(m_ref, xp_ref, xs_ref, wo_ref, g2_ref, wr_ref, br_ref,
                    x1_ref, h2_ref, route_ref, counts_ref, run, *, n_prompt_tiles):
    i = pl.program_id(0)
    tm = m_ref.shape[0]

    @pl.when(i == 0)
    def _():
        run[...] = jnp.zeros(run.shape, F32)

    x = jnp.where(i < n_prompt_tiles, xp_ref[...], xs_ref[...])
    x1 = x + _dot(m_ref[...], wo_ref[...])
    x1_ref[...] = x1
    r = lax.rsqrt(jnp.mean(x1 * x1, axis=-1, keepdims=True) + EPS)
    hf = (x1 * r) * g2_ref[...]
    h2_ref[...] = hf

    hi = hf.astype(BF16)
    lo = (hf - hi.astype(F32)).astype(BF16)
    hw = _dot(hi, wr_ref[...])
    lg = hw[:, :LANES] + (hw[:, LANES:] + _dot(lo, wr_ref[:, :LANES])) + br_ref[...]

    lane = lax.broadcasted_iota(I32, (tm, LANES), 1)
    neg_inf = jnp.float32(-jnp.inf)
    is_g = lane < MOE_GROUPS
    cl = jnp.where(is_g, lg, neg_inf)
    gmax = jnp.max(cl, axis=1, keepdims=True)
    g_sel = jnp.min(jnp.where(cl == gmax, lane, LANES), axis=1, keepdims=True)
    den = jnp.sum(jnp.where(is_g, jnp.exp(cl - gmax), 0.0), axis=1, keepdims=True)
    g_val = 1.0 / den
    e_lane = lane - _R_LOGIT0
    in_grp = (e_lane >= 0) & (e_lane < N_EXPERTS) & ((e_lane >> GROUP_SHIFT) == g_sel)
    f1 = jnp.where(in_grp, lg, neg_inf)
    v1 = jnp.max(f1, axis=1, keepdims=True)
    i1 = jnp.min(jnp.where(f1 == v1, lane, LANES), axis=1, keepdims=True)
    f2 = jnp.where(lane == i1, neg_inf, f1)
    v2 = jnp.max(f2, axis=1, keepdims=True)
    i2 = jnp.min(jnp.where(f2 == v2, lane, LANES), axis=1, keepdims=True)
    e2 = jnp.exp(v2 - v1)
    w1 = g_val / (1.0 + e2)
    w2 = g_val * e2 / (1.0 + e2)

    oh1 = lane == i1
    oh2 = lane == i2
    c = (oh1 | oh2).astype(BF16)
    rr = lax.broadcasted_iota(I32, (tm, tm), 0)
    cc = lax.broadcasted_iota(I32, (tm, tm), 1)
    before = (cc < rr).astype(BF16)
    prior = _dot(before, c) + run[...]
    rank1 = jnp.sum(jnp.where(oh1, prior, 0.0), axis=1, keepdims=True)
    rank2 = jnp.sum(jnp.where(oh2, prior, 0.0), axis=1, keepdims=True)
    run[...] = run[...] + jnp.sum(c.astype(F32), axis=0, keepdims=True)
    counts_ref[...] = jnp.broadcast_to(run[...], counts_ref.shape)

    rec = jnp.zeros((tm, LANES), F32)
    for ln, val in ((_R_E1, (i1 - _R_LOGIT0).astype(F32)), (_R_E2, (i2 - _R_LOGIT0).astype(F32)),
                    (_R_W1, w1), (_R_W2, w2), (_R_RANK1, rank1), (_R_RANK2, rank2)):
        rec = jnp.where(lane == ln, val, rec)
    route_ref[...] = rec


def _out_proj(m, xp, xs, w_out, g2, wr_split, b_r):
    n = m.shape[0]
    tm = TOK_TILE
    n_prompt_tiles = xp.shape[0] // tm
    tok = lambda w: pl.BlockSpec((tm, w), lambda i: (i, 0))
    const = lambda s: pl.BlockSpec(s, lambda i: (0,) * len(s))
    return pl.pallas_call(
        functools.partial(_outproj_kernel, n_prompt_tiles=n_prompt_tiles),
        grid=(n // tm,),
        in_specs=[tok(D_MODEL), *_split_token_specs(tm, D_MODEL, n_prompt_tiles),
                  pl.BlockSpec((D_MODEL, D_MODEL), lambda i: (0, 0), pipeline_mode=pl.Buffered(1)),
                  const((1, D_MODEL)), const((D_MODEL, 2 * LANES)), const((1, LANES))],
        out_specs=(tok(D_MODEL), tok(D_MODEL), tok(LANES), const((8, LANES))),
        out_shape=(jax.ShapeDtypeStruct((n, D_MODEL), F32),
                   jax.ShapeDtypeStruct((n, D_MODEL), F32),
                   jax.ShapeDtypeStruct((n, LANES), F32),
                   jax.ShapeDtypeStruct((8, LANES), F32)),
        scratch_shapes=[pltpu.VMEM((1, LANES), F32)],
        compiler_params=_cparams(1),
        name="out_proj",
    )(m, xp, xs, w_out, g2, wr_split, b_r)


def _experts_kernel(te_ref, nu_ref, src_ref, src_next_ref, h2_ref, wg_ref, wu_ref, wd_ref, ys_ref,
                    xbuf, x16, sem, wg16, wu16, wd16):
    i = pl.program_id(0)
    tm = xbuf.shape[0]
    n_used = nu_ref[0]
    used = i < n_used

    def start_gather(rows_ref):
        for r in range(tm):
            pltpu.make_async_copy(h2_ref.at[pl.ds(rows_ref[r], 1)], xbuf.at[pl.ds(r, 1)], sem).start()

    def wait_gather():
        pltpu.make_async_copy(h2_ref.at[pl.ds(0, tm)], xbuf, sem).wait()

    @pl.when(i == 0)
    def _():
        start_gather(src_ref)

    @pl.when(i <= n_used)
    def _():
        wait_gather()

    prev = te_ref[jnp.maximum(i - 1, 0)]
    fresh = (i == 0) | (te_ref[i] != prev)

    @pl.when(fresh & used)
    def _():
        wg16[...] = wg_ref[0].astype(BF16)
        wu16[...] = wu_ref[0].astype(BF16)
        wd16[...] = wd_ref[0].astype(BF16)

    @pl.when(used)
    def _():
        x16[...] = xbuf[...].astype(BF16)
        start_gather(src_next_ref)
        x = x16[...]
        a = _dot(x, wg16[...])
        u = _dot(x, wu16[...])
        act = (a * jax.nn.sigmoid(a)) * u
        ys_ref[...] = _dot(act.astype(BF16), wd16[...])

    @pl.when(jnp.logical_not(used))
    def _():
        ys_ref[...] = jnp.zeros(ys_ref.shape, F32)

    @pl.when(used & (i == pl.num_programs(0) - 1))
    def _():
        wait_gather()


def _experts(tile_expert, n_used, src_rows, h2, w_g, w_u, w_d):
    n_rows = src_rows.shape[0]
    tm = EXP_TILE
    n_tiles = n_rows // tm
    grid_spec = pltpu.PrefetchScalarGridSpec(
        num_scalar_prefetch=2,
        grid=(n_tiles,),
        in_specs=[pl.BlockSpec((tm,), lambda i, te, nu: (i,), memory_space=pltpu.SMEM),
                  pl.BlockSpec((tm,), lambda i, te, nu: (jnp.minimum(i + 1, n_tiles - 1),),
                               memory_space=pltpu.SMEM),
                  pl.BlockSpec(memory_space=pl.ANY),
                  pl.BlockSpec((1, D_MODEL, D_EXPERT), lambda i, te, nu: (te[i], 0, 0)),
                  pl.BlockSpec((1, D_MODEL, D_EXPERT), lambda i, te, nu: (te[i], 0, 0)),
                  pl.BlockSpec((1, D_EXPERT, D_MODEL), lambda i, te, nu: (te[i], 0, 0))],
        out_specs=pl.BlockSpec((tm, D_MODEL), lambda i, te, nu: (i, 0)),
        scratch_shapes=[pltpu.VMEM((tm, D_MODEL), F32),
                        pltpu.VMEM((tm, D_MODEL), BF16),
                        pltpu.SemaphoreType.DMA(()),
                        pltpu.VMEM((D_MODEL, D_EXPERT), BF16),
                        pltpu.VMEM((D_MODEL, D_EXPERT), BF16),
                        pltpu.VMEM((D_EXPERT, D_MODEL), BF16)],
    )
    return pl.pallas_call(
        _experts_kernel,
        grid_spec=grid_spec,
        out_shape=jax.ShapeDtypeStruct((n_rows, D_MODEL), F32),
        compiler_params=_cparams(1),
        name="experts",
    )(tile_expert, n_used, src_rows, src_rows, h2, w_g, w_u, w_d)


COMBINE_TOKENS = 256


def _combine_kernel(pos_ref, pos_next_ref, x1_ref, route_ref, gf_ref, ys_ref, yp_ref, ysm_ref, ybuf, sem, *,
                    n_prompt_tiles):
    tc = COMBINE_TOKENS
    i = pl.program_id(0)
    buf = i % 2

    def row_copy(p_ref, b, r, slot):
        return pltpu.make_async_copy(ys_ref.at[pl.ds(p_ref[2 * r + slot], 1)],
                                     ybuf.at[b, slot, pl.ds(r, 1)], sem.at[b])

    def start_rows(p_ref, b):
        def issue(r, carry):
            row_copy(p_ref, b, r, 0).start()
            row_copy(p_ref, b, r, 1).start()
            return carry
        lax.fori_loop(0, tc, issue, 0, unroll=8)

    @pl.when(i == 0)
    def _():
        start_rows(pos_ref, 0)

    @pl.when(i + 1 < pl.num_programs(0))
    def _():
        start_rows(pos_next_ref, 1 - buf)

    def drain(r, carry):
        row_copy(pos_ref, buf, r, 0).wait()
        row_copy(pos_ref, buf, r, 1).wait()
        return carry

    lax.fori_loop(0, tc, drain, 0, unroll=8)

    route = route_ref[...]
    w1 = route[:, _R_W1:_R_W1 + 1]
    w2 = route[:, _R_W2:_R_W2 + 1]
    x2 = x1_ref[...] + (w1 * ybuf[buf, 0] + w2 * ybuf[buf, 1])
    r = lax.rsqrt(jnp.mean(x2 * x2, axis=-1, keepdims=True) + EPS)
    y = (x2 * r) * gf_ref[...]
    is_prompt = pl.program_id(0) < n_prompt_tiles

    @pl.when(is_prompt)
    def _():
        yp_ref[...] = y

    @pl.when(jnp.logical_not(is_prompt))
    def _():
        ysm_ref[...] = y


def _combine(pos_flat, x1, route, gf, ys, n_prompt):
    n = x1.shape[0]
    tc = COMBINE_TOKENS
    n_prompt_tiles = n_prompt // tc
    n_steps = n // tc
    return pl.pallas_call(
        functools.partial(_combine_kernel, n_prompt_tiles=n_prompt_tiles),
        grid=(n_steps,),
        in_specs=[pl.BlockSpec((2 * tc,), lambda i: (i,), memory_space=pltpu.SMEM),
                  pl.BlockSpec((2 * tc,), lambda i: (jnp.minimum(i + 1, n_steps - 1),),
                               memory_space=pltpu.SMEM),
                  pl.BlockSpec((tc, D_MODEL), lambda i: (i, 0)),
                  pl.BlockSpec((tc, LANES), lambda i: (i, 0)),
                  pl.BlockSpec((1, D_MODEL), lambda i: (0, 0)),
                  pl.BlockSpec(memory_space=pl.ANY)],
        out_specs=_split_token_specs(tc, D_MODEL, n_prompt_tiles),
        out_shape=(jax.ShapeDtypeStruct((n_prompt, D_MODEL), F32),
                   jax.ShapeDtypeStruct((n - n_prompt, D_MODEL), F32)),
        scratch_shapes=[pltpu.VMEM((2, 2, tc, D_MODEL), F32), pltpu.SemaphoreType.DMA((2,))],
        compiler_params=_cparams(1),
        name="combine",
    )(pos_flat, pos_flat, x1, route, gf, ys)


def _pack_w_in(w_in):
    o_q = 0
    o_k = o_q + ATTN_WIDTH
    o_v = o_k + KV_WIDTH
    o_qi = o_v + KV_WIDTH
    o_ki = o_qi + IDX_HEADS * IDX_DIM
    o_wi = o_ki + IDX_DIM
    o_p = o_wi + IDX_HEADS
    pad = jnp.zeros((w_in.shape[0], LANES - IDX_DIM - IDX_HEADS), w_in.dtype)
    return jnp.concatenate([w_in[:, :o_p], pad, w_in[:, o_p:]], axis=1).astype(BF16)


def _forward(x_prompt, x_sample, cache_k, cache_v, cache_kidx, state_pool,
             norm1_g, w_in, w_gate, b_gate, w_pool_grp, pool_scale, w_o_attn, w_o_pool, w_out,
             norm2_g, w_rg, b_rg, w_re, b_re, w_eg, w_eu, w_ed, norm_f_g):
    batch, seq, d = x_prompt.shape
    dec_batch, dec_seq, _ = x_sample.shape
    past = cache_k.shape[1]
    n_p = batch * seq
    n_s = dec_batch * dec_seq
    n = n_p + n_s
    tm = TOK_TILE
    assert d == D_MODEL and seq % tm == 0 and n_s % tm == 0 and n % (2 * tm) == 0
    assert past % KEY_BLOCK == 0 and dec_seq >= POOL_STATE and dec_seq <= KEY_BLOCK

    xp = x_prompt.reshape(n_p, d)
    xs = x_sample.reshape(n_s, d)
    pos_tab = jnp.concatenate([jnp.arange(seq), past + jnp.arange(n_s) % dec_seq])
    tables = _rope_tables(pos_tab)
    h, q, k, v, qi, kw, pin = _in_proj(xp, xs, norm1_g.reshape(1, d), _pack_w_in(w_in), tables, seq // tm)

    oa_p = _dsa_prompt(q, qi, kw, k, v, batch, seq)
    oa_s = _dsa_sample(q, qi, kw, k, v,
                       cache_k.reshape(dec_batch * past, KV_WIDTH),
                       cache_v.reshape(dec_batch * past, KV_WIDTH),
                       cache_kidx.reshape(dec_batch * past, IDX_DIM),
                       n_p, dec_batch, dec_seq, past)

    w_grp16 = w_pool_grp.astype(BF16)
    scale = pool_scale.reshape(1, POOL_WIDTH)
    op_p = _pool(pin, w_grp16, scale, 0, batch, seq, 0)
    halo = jnp.concatenate([jnp.zeros((dec_batch, 1, POOL_WIDTH), F32), state_pool], axis=1)
    op_s = _pool(pin, w_grp16, scale, n_p // dec_seq, dec_batch, dec_seq, past, halo=halo)

    m = _merge(h, oa_p, oa_s, op_p, op_s, w_gate.astype(BF16), b_gate.reshape(1, 2 * d),
               w_o_attn.astype(BF16), w_o_pool.astype(BF16))

    w_r = jnp.concatenate([w_rg, w_re, jnp.zeros((d, LANES - MOE_GROUPS - N_EXPERTS), F32)], axis=1)
    b_r = jnp.concatenate([b_rg, b_re, jnp.zeros((LANES - MOE_GROUPS - N_EXPERTS,), F32)]).reshape(1, LANES)
    wr_hi = w_r.astype(BF16)
    wr_lo = (w_r - wr_hi.astype(F32)).astype(BF16)
    wr_split = jnp.concatenate([wr_hi, wr_lo], axis=1)
    x1, h2, route, counts = _out_proj(m, xp, xs, w_out.astype(BF16), norm2_g.reshape(1, d), wr_split, b_r)

    te = EXP_TILE
    n_tiles = (2 * n) // te + N_EXPERTS
    cnt = counts[0, _R_LOGIT0:_R_LOGIT0 + N_EXPERTS].astype(I32)
    tiles_per_e = (cnt + te - 1) // te
    tile_end = jnp.cumsum(tiles_per_e)
    row_start = (tile_end - tiles_per_e) * te
    n_used = tile_end[-1:].astype(I32)
    tile_ids = jnp.arange(n_tiles, dtype=I32)
    tile_expert = jnp.minimum(
        jnp.sum((tile_end[None, :] <= tile_ids[:, None]).astype(I32), axis=1), N_EXPERTS - 1)
    last_e = tile_expert[jnp.maximum(n_used[0] - 1, 0)]
    tile_expert = jnp.where(tile_ids < n_used[0], tile_expert, last_e)
    eid = route[:, _R_E1:_R_E2 + 1].astype(I32)
    rank = route[:, _R_RANK1:_R_RANK2 + 1].astype(I32)
    pos_flat = (row_start[eid] + rank).reshape(2 * n)

    src_rows = jnp.zeros((n_tiles * te,), I32).at[pos_flat].set(
        jnp.arange(2 * n, dtype=I32) // 2, unique_indices=True)
    ys = _experts(tile_expert, n_used, src_rows, h2, w_eg, w_eu, w_ed)
    y_p, y_s = _combine(pos_flat, x1, route, norm_f_g.reshape(1, d), ys, n_p)

    y_prompt = y_p.reshape(batch, seq, d)
    y_sample = y_s.reshape(dec_batch, dec_seq, d)
    k_p = k[:n_p].reshape(1, batch, seq, N_KV_HEADS, HEAD_DIM)
    v_p = v[:n_p].reshape(1, batch, seq, N_KV_HEADS, HEAD_DIM)
    ki_p = kw[:n_p, :IDX_DIM].reshape(1, batch, seq, IDX_DIM)
    pool_p = jnp.stack([pin[(b + 1) * seq - POOL_STATE:(b + 1) * seq] for b in range(batch)])[None]
    k_s = k[n_p:].reshape(1, dec_batch, dec_seq, N_KV_HEADS, HEAD_DIM)
    v_s = v[n_p:].reshape(1, dec_batch, dec_seq, N_KV_HEADS, HEAD_DIM)
    ki_s = kw[n_p:, :IDX_DIM].reshape(1, dec_batch, dec_seq, IDX_DIM)
    pool_s = pin[n_p:].reshape(dec_batch, dec_seq, POOL_WIDTH)[None, :, dec_seq - POOL_STATE:, :]
    return (y_prompt, y_sample, k_p, v_p, ki_p, pool_p, k_s, v_s, ki_s, pool_s)


def kernel(x_prompt, x_sample, cache_k, cache_v, cache_kidx, state_pool, norm1_g, w_in, w_gate, b_gate,
           w_pool_grp, pool_scale, w_o_attn, w_o_pool, w_out, norm2_g, w_router_group, b_router_group,
           w_router_expert, b_router_expert, w_exp_gate, w_exp_up, w_exp_down, norm_f_g):
    assert cache_k.shape[0] == 1, "single-layer model"
    return _forward(x_prompt, x_sample, cache_k[0], cache_v[0], cache_kidx[0], state_pool[0],
                    norm1_g[0], w_in[0], w_gate[0], b_gate[0], w_pool_grp[0], pool_scale[0],
                    w_o_attn[0], w_o_pool[0], w_out[0], norm2_g[0], w_router_group[0],
                    b_router_group[0], w_router_expert[0], b_router_expert[0],
                    w_exp_gate[0], w_exp_up[0], w_exp_down[0], norm_f_g)
```

```python
import functools

import numpy as np
import jax
import jax.numpy as jnp
from jax import lax
from jax.experimental import pallas as pl
from jax.experimental.pallas import tpu as pltpu

F32 = jnp.float32
BF16 = jnp.bfloat16
I32 = jnp.int32

D_MODEL = 2048
CHUNK = 64
N_HEADS = 8
N_KV_HEADS = 2
HEAD_DIM = 128
GQA_GROUP = N_HEADS // N_KV_HEADS
ATTN_WIDTH = N_HEADS * HEAD_DIM
KV_WIDTH = N_KV_HEADS * HEAD_DIM
IDX_HEADS = 16
IDX_DIM = 64
IDX_PAIRS = IDX_HEADS // 2
TOPK_MAX = 256
POOL_WINDOWS = (2, 4, 8, 16)
POOL_GROUPS = 4
POOL_WIDTH = D_MODEL // 2
POOL_GROUP_DIM = POOL_WIDTH // POOL_GROUPS
POOL_STATE = 15
POOL_HALO = 16
MOE_GROUPS = 4
EXPERTS_PER_GROUP = 8
N_EXPERTS = MOE_GROUPS * EXPERTS_PER_GROUP
D_EXPERT = 512
ROPE_THETA = 10000.0
CHUNK_SHIFT = CHUNK.bit_length() - 1
GROUP_SHIFT = EXPERTS_PER_GROUP.bit_length() - 1
EPS = 1e-6

LANES = 128
INT_MIN = -(2 ** 31)
NEG_BIG = -1e30
LOG2_E = 1.4426950408889634
VMEM_LIMIT = 56 * 1024 * 1024

KEY_BLOCK = 256
TOK_TILE = 256
EXP_TILE = 256


def _cparams(n_axes):
    return pltpu.CompilerParams(dimension_semantics=("arbitrary",) * n_axes,
                                vmem_limit_bytes=VMEM_LIMIT)


def _dot(a, b):
    return jnp.dot(a, b, preferred_element_type=F32)


def _dot_nt(a, b):
    return lax.dot_general(a, b, (((1,), (1,)), ((), ())), preferred_element_type=F32)


_C_Q = 0
_C_K = _C_Q + ATTN_WIDTH
_C_V = _C_K + KV_WIDTH
_C_QI = _C_V + KV_WIDTH
_C_KW = _C_QI + IDX_HEADS * IDX_DIM
_C_P = _C_KW + LANES
_C_END = _C_P + POOL_WIDTH


def _rope128(y, cos, sin_signed):
    return y * cos + pltpu.roll(y, HEAD_DIM // 2, 1) * sin_signed


def _rope64(y, cos, sin_a, sin_b):
    half = IDX_DIM // 2
    return y * cos + pltpu.roll(y, LANES - half, 1) * sin_a + pltpu.roll(y, half, 1) * sin_b


def _inproj_kernel(xp_ref, xs_ref, g_ref, w_ref, ck_ref, sk_ref, cq_ref, sq_ref, ci_ref, sia_ref, sib_ref,
                   cw_ref, swa_ref, swb_ref,
                   h_ref, q_ref, k_ref, v_ref, qi_ref, kw_ref, p_ref, *, n_prompt_tiles):
    x = jnp.where(pl.program_id(0) < n_prompt_tiles, xp_ref[...], xs_ref[...])
    r = lax.rsqrt(jnp.mean(x * x, axis=-1, keepdims=True) + EPS)
    h = ((x * r) * g_ref[...]).astype(BF16)
    h_ref[...] = h
    cq, sq = cq_ref[...], sq_ref[...]
    for c in range(ATTN_WIDTH // 256):
        y = _dot(h, w_ref[:, _C_Q + c * 256:_C_Q + (c + 1) * 256])
        for s in range(2):
            q_ref[:, c * 256 + s * LANES:c * 256 + (s + 1) * LANES] = _rope128(
                y[:, s * LANES:(s + 1) * LANES], cq, sq).astype(BF16)
    ck, sk = ck_ref[...], sk_ref[...]
    y = _dot(h, w_ref[:, _C_K:_C_K + KV_WIDTH])
    for s in range(N_KV_HEADS):
        k_ref[:, s * LANES:(s + 1) * LANES] = _rope128(y[:, s * LANES:(s + 1) * LANES], ck, sk)
    v_ref[...] = _dot(h, w_ref[:, _C_V:_C_V + KV_WIDTH])
    ci, sia, sib = ci_ref[...], sia_ref[...], sib_ref[...]
    for c in range(IDX_PAIRS // 2):
        y = _dot(h, w_ref[:, _C_QI + c * 256:_C_QI + (c + 1) * 256])
        for s in range(2):
            qi_ref[2 * c + s] = _rope64(y[:, s * LANES:(s + 1) * LANES], ci, sia, sib).astype(BF16)
    y = _dot(h, w_ref[:, _C_KW:_C_KW + LANES])
    kw_ref[...] = _rope64(y, cw_ref[...], swa_ref[...], swb_ref[...])
    for c in range(POOL_WIDTH // 256):
        p_ref[:, c * 256:(c + 1) * 256] = _dot(h, w_ref[:, _C_P + c * 256:_C_P + (c + 1) * 256])


def _rope_tables(pos):
    pos = pos.astype(F32)[:, None]

    def cs(dim):
        half = dim // 2
        inv = ROPE_THETA ** (-jnp.arange(half, dtype=F32) / half)
        ang = pos * inv[None, :]
        return jnp.cos(ang), jnp.sin(ang)

    c, s = cs(HEAD_DIM)
    ck = jnp.concatenate([c, c], axis=1)
    sk = jnp.concatenate([-s, s], axis=1)
    qs = HEAD_DIM ** -0.5 * LOG2_E
    c, s = cs(IDX_DIM)
    z = jnp.zeros_like(s)
    ci = jnp.concatenate([c, c, c, c], axis=1)
    sia = jnp.concatenate([-s, z, -s, z], axis=1)
    sib = jnp.concatenate([z, s, z, s], axis=1)
    iscale = IDX_DIM ** -0.5
    n = pos.shape[0]
    wscale = jnp.full((n, IDX_HEADS), IDX_HEADS ** -0.5, F32)
    pad = jnp.zeros((n, LANES - IDX_DIM - IDX_HEADS), F32)
    zz = jnp.zeros((n, LANES - IDX_DIM), F32)
    cw = jnp.concatenate([c, c, wscale, pad], axis=1)
    swa = jnp.concatenate([-s, z, zz], axis=1)
    swb = jnp.concatenate([z, s, zz], axis=1)
    return (ck, sk, ck * qs, sk * qs, ci * iscale, sia * iscale, sib * iscale, cw, swa, swb)


def _split_token_specs(tm, width, n_prompt_tiles):
    prompt = pl.BlockSpec((tm, width), lambda i: (jnp.minimum(i, n_prompt_tiles - 1), 0))
    sample = pl.BlockSpec((tm, width), lambda i: (jnp.maximum(i - n_prompt_tiles, 0), 0))
    return prompt, sample


def _in_proj(xp, xs, g1, w_pack, tables, tiles_per_seq):
    tm = TOK_TILE
    n_prompt_tiles = xp.shape[0] // tm
    n = xp.shape[0] + xs.shape[0]
    n_tiles = n // tm

    def tab_idx(i):
        return (jnp.where(i < n_prompt_tiles, i % tiles_per_seq, tiles_per_seq + i - n_prompt_tiles), 0)

    tok = lambda w: pl.BlockSpec((tm, w), lambda i: (i, 0))
    tab = pl.BlockSpec((tm, LANES), tab_idx)
    out_shape = (
        jax.ShapeDtypeStruct((n, D_MODEL), BF16),
        jax.ShapeDtypeStruct((n, ATTN_WIDTH), BF16),
        jax.ShapeDtypeStruct((n, KV_WIDTH), F32),
        jax.ShapeDtypeStruct((n, KV_WIDTH), F32),
        jax.ShapeDtypeStruct((IDX_PAIRS, n, LANES), BF16),
        jax.ShapeDtypeStruct((n, LANES), F32),
        jax.ShapeDtypeStruct((n, POOL_WIDTH), F32),
    )
    return pl.pallas_call(
        functools.partial(_inproj_kernel, n_prompt_tiles=n_prompt_tiles),
        grid=(n_tiles,),
        in_specs=[*_split_token_specs(tm, D_MODEL, n_prompt_tiles),
                  pl.BlockSpec((1, D_MODEL), lambda i: (0, 0)),
                  pl.BlockSpec((D_MODEL, _C_END), lambda i: (0, 0), pipeline_mode=pl.Buffered(1))]
                 + [tab] * 10,
        out_specs=(tok(D_MODEL), tok(ATTN_WIDTH), tok(KV_WIDTH), tok(KV_WIDTH),
                   pl.BlockSpec((IDX_PAIRS, tm, LANES), lambda i: (0, i, 0)),
                   tok(LANES), tok(POOL_WIDTH)),
        out_shape=out_shape,
        compiler_params=_cparams(1),
        name="in_proj",
    )(xp, xs, g1, w_pack, *tables)


def _ordinal_to_f32(k):
    return lax.bitcast_convert_type(k ^ ((k >> 31) & 0x7FFFFFFF), F32)


def _kth_largest_score(topk, shape, n_key_rows, count_where):
    def count_ge(cand):
        cand_f = _ordinal_to_f32(cand)
        return count_where(lambda keys, kb: keys >= jnp.broadcast_to(cand_f, keys.shape))

    c0 = count_ge(jnp.zeros(shape, I32))
    thr0 = jnp.where(c0 >= topk, 0, INT_MIN).astype(I32)
    cnt0 = jnp.where(c0 >= topk, c0, n_key_rows).astype(I32)

    def bit_step(i, carry):
        thr, cnt = carry
        cand = thr + (jnp.int32(1) << (30 - i))
        c = count_ge(cand)
        ok = c >= topk
        return jnp.where(ok, cand, thr), jnp.where(ok, c, cnt)

    thr, cnt = lax.fori_loop(0, 31, bit_step, (thr0, cnt0))
    has_k = thr > INT_MIN
    thr_f = jnp.where(has_k, _ordinal_to_f32(thr), jnp.finfo(F32).min)
    return thr_f, has_k & (cnt > topk)


def _drop_excess_ties(sc, n_blocks, topk, thr, need_tie, key_index, count_where):
    n_gt = count_where(lambda keys, kb: keys > jnp.broadcast_to(thr, keys.shape))
    n_keep = topk - n_gt

    def idx_step(i, bound):
        cand = bound + (jnp.int32(1) << (14 - i))
        c = count_where(lambda keys, kb: (keys == jnp.broadcast_to(thr, keys.shape))
                        & (key_index(kb) < jnp.broadcast_to(cand, keys.shape)))
        return jnp.where(c <= n_keep, cand, bound)

    bound = lax.fori_loop(0, 15, idx_step, jnp.zeros(thr.shape, I32))
    bound = jnp.where(need_tie, bound, jnp.int32(2 ** 30))

    def drop_block(kb, carry):
        keys = sc[kb]
        drop = (keys == jnp.broadcast_to(thr, keys.shape)) & (key_index(kb) >= jnp.broadcast_to(bound, keys.shape))
        sc[kb] = jnp.where(drop, -jnp.inf, keys)
        return carry

    lax.fori_loop(0, n_blocks, drop_block, 0)


def _stack_queries_t(q_ref, qt):
    for g in range(N_KV_HEADS):
        qg = jnp.concatenate([q_ref[:, (g * GQA_GROUP + j) * HEAD_DIM:(g * GQA_GROUP + j + 1) * HEAD_DIM]
                              for j in range(GQA_GROUP)], axis=0)
        qt[g] = qg.astype(F32).T.astype(BF16)


def _attention_scratch(rows):
    return [
        pltpu.VMEM((N_KV_HEADS, 1, rows), F32),
        pltpu.VMEM((N_KV_HEADS, 1, rows), F32),
        pltpu.VMEM((N_KV_HEADS, HEAD_DIM, rows), F32),
        pltpu.VMEM((2, N_KV_HEADS, KEY_BLOCK, rows), F32),
        pltpu.VMEM((2, N_KV_HEADS, KEY_BLOCK, rows), BF16),
        pltpu.VMEM((2, N_KV_HEADS, 1, rows), F32),
    ]


def _masked_attention(o_ref, n_blocks, n_blocks_all, bias_t, k16, vt16, qt, att, tq):
    m_s, l_s, acc_s, lg_s, p_s, alpha_s = att
    m_s[...] = jnp.full(m_s.shape, NEG_BIG, F32)
    l_s[...] = jnp.zeros(l_s.shape, F32)
    acc_s[...] = jnp.zeros(acc_s.shape, F32)
    p_s[1] = jnp.zeros(p_s.shape[1:], BF16)
    alpha_s[1] = jnp.ones(alpha_s.shape[1:], F32)

    def logits_of(kb, slot):
        r0 = pl.multiple_of(kb * KEY_BLOCK, KEY_BLOCK)
        for g in range(N_KV_HEADS):
            lg_s[slot, g] = _dot(k16[pl.ds(r0, KEY_BLOCK), g * HEAD_DIM:(g + 1) * HEAD_DIM], qt[g])

    def value_product(kb, slot):
        for g in range(N_KV_HEADS):
            acc_s[g] = (alpha_s[slot, g] * acc_s[g]
                        + _dot(vt16[kb, g * HEAD_DIM:(g + 1) * HEAD_DIM, :], p_s[slot, g]))

    logits_of(0, 0)
    last = n_blocks_all - 1

    def step(kb, slot):
        logits_of(jnp.minimum(kb + 1, last), 1 - slot)
        bias = jnp.where(kb < n_blocks, bias_t(jnp.minimum(kb, last)), NEG_BIG)
        for g in range(N_KV_HEADS):
            logits = lg_s[slot, g] + bias
            m_prev = m_s[g]
            m_new = jnp.maximum(m_prev, jnp.max(logits, axis=0, keepdims=True))
            p = jnp.exp2(logits - m_new)
            alpha = jnp.exp2(m_prev - m_new)
            l_s[g] = alpha * l_s[g] + jnp.sum(p, axis=0, keepdims=True)
            p_s[slot, g] = p.astype(BF16)
            alpha_s[slot, g] = alpha
            m_s[g] = m_new
        value_product(jnp.clip(kb - 1, 0, last), 1 - slot)

    def two_steps(i, carry):
        step(2 * i, 0)
        step(2 * i + 1, 1)
        return carry

    n_pairs = (n_blocks + 1) // 2
    lax.fori_loop(0, n_pairs, two_steps, 0)
    value_product(jnp.minimum(2 * n_pairs - 1, last), 1)

    for g in range(N_KV_HEADS):
        o = (acc_s[g] * (1.0 / l_s[g])).T
        for j in range(GQA_GROUP):
            hd = g * GQA_GROUP + j
            o_ref[:, hd * HEAD_DIM:(hd + 1) * HEAD_DIM] = o[j * tq:(j + 1) * tq, :].astype(BF16)


def _dsa_prompt_kernel(q_ref, qi_ref, wq_ref, kn_ref, vn_ref, kwn_ref, o_ref,
                       k16, vt16, kia, kib, sc, raw, qt, *att, tq, seq, topk):
    qb = pl.program_id(1)
    n_blocks_all = seq // KEY_BLOCK

    @pl.when(qb == 0)
    def _():
        lane = lax.broadcasted_iota(I32, (seq, LANES), 1)
        kin = jnp.where(lane < IDX_DIM, kwn_ref[...], 0.0)
        k16[...] = kn_ref[...].astype(BF16)
        kia[...] = kin.astype(BF16)
        kib[...] = pltpu.roll(kin, IDX_DIM, 1).astype(BF16)
        for kb in range(n_blocks_all):
            vt16[kb] = vn_ref[kb * KEY_BLOCK:(kb + 1) * KEY_BLOCK, :].T.astype(BF16)

    n_blocks = qb // (KEY_BLOCK // tq) + 1

    qit = qi_ref[...].reshape(IDX_PAIRS * tq, LANES).astype(F32).T.astype(BF16)
    wt = wq_ref[...].T
    w_rows = [wt[IDX_DIM + j:IDX_DIM + j + 1, :] for j in range(IDX_HEADS)]
    _stack_queries_t(q_ref, qt)

    key_row = lax.broadcasted_iota(I32, (KEY_BLOCK, tq), 0)
    q_pos = qb * tq + lax.broadcasted_iota(I32, (KEY_BLOCK, tq), 1)

    last = n_blocks_all - 1
    n_pairs = (n_blocks + 1) // 2

    def raw_scores(kb, slot):
        r0 = pl.multiple_of(kb * KEY_BLOCK, KEY_BLOCK)
        raw[slot, 0] = _dot(kia[pl.ds(r0, KEY_BLOCK), :], qit)
        raw[slot, 1] = _dot(kib[pl.ds(r0, KEY_BLOCK), :], qit)

    def score_step(kb, slot):
        raw_scores(jnp.minimum(kb + 1, last), 1 - slot)
        se = raw[slot, 0]
        so = raw[slot, 1]
        score = jnp.zeros((KEY_BLOCK, tq), F32)
        for p in range(IDX_PAIRS):
            score = (score + jnp.maximum(se[:, p * tq:(p + 1) * tq], 0.0) * w_rows[2 * p]
                     + jnp.maximum(so[:, p * tq:(p + 1) * tq], 0.0) * w_rows[2 * p + 1])
        adm = ((kb * KEY_BLOCK + key_row) >> CHUNK_SHIFT) <= (q_pos >> CHUNK_SHIFT)
        sc[jnp.where(kb < n_blocks, kb, n_blocks_all)] = jnp.where(adm, score, -jnp.inf)

    def two_score_steps(i, carry):
        score_step(2 * i, 0)
        score_step(2 * i + 1, 1)
        return carry

    raw_scores(0, 0)
    lax.fori_loop(0, n_pairs, two_score_steps, 0)

    def count_where(pred):
        def body(i, acc):
            for s in range(2):
                kb = 2 * i + s
                kbc = jnp.minimum(kb, last)
                m = pred(sc[kbc], kbc).astype(I32)
                part = jnp.sum(m.reshape(KEY_BLOCK // 8, 8, tq), axis=0)
                acc = acc + jnp.where(kb < n_blocks, part, 0)
            return acc
        acc = lax.fori_loop(0, n_pairs, body, jnp.zeros((8, tq), I32))
        return jnp.sum(acc, axis=0, keepdims=True)

    thr, need_tie = _kth_largest_score(topk, (1, tq), seq, count_where)

    @pl.when(jnp.max(need_tie.astype(I32)) > 0)
    def _():
        _drop_excess_ties(sc, n_blocks, topk, thr, need_tie,
                          lambda kb: kb * KEY_BLOCK + key_row, count_where)

    thr_b = jnp.broadcast_to(thr, (KEY_BLOCK, tq))

    def bias_t(kb):
        bias = jnp.where(sc[kb] >= thr_b, 0.0, NEG_BIG)
        return jnp.concatenate([bias] * GQA_GROUP, axis=1)

    _masked_attention(o_ref, n_blocks, n_blocks_all, bias_t, k16, vt16, qt, att, tq)


def _dsa_sample_kernel(q_ref, qi_ref, wq_ref, kn_ref, vn_ref, kwn_ref, kc_ref, vc_ref, kic_ref, o_ref,
                       k16, vt16, kia, kib, wbe, wbo, sc, qt, *att,
                       tq, n_key_rows, n_cache, n_new, topk):
    n_blocks = n_key_rows // KEY_BLOCK
    n_cache_blocks = n_cache // KEY_BLOCK

    lane = lax.broadcasted_iota(I32, (n_new, LANES), 1)
    kin = jnp.where(lane < IDX_DIM, kwn_ref[...], 0.0).astype(BF16)
    n_tail = n_key_rows - n_cache
    k16[0:n_cache, :] = kc_ref[...].astype(BF16)
    for kb in range(n_cache_blocks):
        vt16[kb] = vc_ref[kb * KEY_BLOCK:(kb + 1) * KEY_BLOCK, :].T.astype(BF16)
    v_tail = jnp.concatenate([vn_ref[...], jnp.zeros((n_tail - n_new, KV_WIDTH), F32)], axis=0)
    vt16[n_cache_blocks] = v_tail.T.astype(BF16)
    kic = jnp.concatenate([kic_ref[...], jnp.zeros((n_cache, LANES - IDX_DIM), F32)], axis=1)
    kia[0:n_cache, :] = kic.astype(BF16)
    kib[0:n_cache, :] = pltpu.roll(kic, IDX_DIM, 1).astype(BF16)
    k16[n_cache:n_key_rows, :] = jnp.zeros((n_tail, KV_WIDTH), BF16)
    kia[n_cache:n_key_rows, :] = jnp.zeros((n_tail, LANES), BF16)
    kib[n_cache:n_key_rows, :] = jnp.zeros((n_tail, LANES), BF16)
    k16[n_cache:n_cache + n_new, :] = kn_ref[...].astype(BF16)
    kia[n_cache:n_cache + n_new, :] = kin
    kib[n_cache:n_cache + n_new, :] = pltpu.roll(kin.astype(F32), IDX_DIM, 1).astype(BF16)

    wq = wq_ref[...]
    for j in range(IDX_HEADS):
        col = jnp.broadcast_to(wq[:, IDX_DIM + j:IDX_DIM + j + 1], (tq, LANES))
        if j % 2 == 0:
            wbe[j // 2] = col
        else:
            wbo[j // 2] = col
    _stack_queries_t(q_ref, qt)

    lane = lax.broadcasted_iota(I32, (tq, KEY_BLOCK), 1)

    qi2 = qi_ref[...].reshape(IDX_PAIRS * tq, LANES)

    def score_block(kb, carry):
        r0 = pl.multiple_of(kb * KEY_BLOCK, KEY_BLOCK)
        se = _dot_nt(qi2, kia[pl.ds(r0, KEY_BLOCK), :]).reshape(IDX_PAIRS, tq, KEY_BLOCK)
        so = _dot_nt(qi2, kib[pl.ds(r0, KEY_BLOCK), :]).reshape(IDX_PAIRS, tq, KEY_BLOCK)
        score = jnp.zeros((tq, KEY_BLOCK), F32)
        for p in range(IDX_PAIRS):
            we = wbe[p]
            wo = wbo[p]
            we2 = jnp.concatenate([we, we], axis=1)
            wo2 = jnp.concatenate([wo, wo], axis=1)
            score = score + jnp.maximum(se[p], 0.0) * we2 + jnp.maximum(so[p], 0.0) * wo2
        adm = (kb * KEY_BLOCK + lane) < (n_cache + n_new)
        sc[kb] = jnp.where(adm, score, -jnp.inf)
        return carry

    lax.fori_loop(0, n_blocks, score_block, 0)

    def count_where(pred):
        def body(kb, acc):
            m = pred(sc[kb], kb).astype(I32)
            return acc + m[:, :LANES] + m[:, LANES:]
        acc = lax.fori_loop(0, n_blocks, body, jnp.zeros((tq, LANES), I32))
        return jnp.sum(acc, axis=1, keepdims=True)

    thr, need_tie = _kth_largest_score(topk, (tq, 1), n_key_rows, count_where)

    @pl.when(jnp.max(need_tie.astype(I32)) > 0)
    def _():
        _drop_excess_ties(sc, n_blocks, topk, thr, need_tie,
                          lambda kb: kb * KEY_BLOCK + lane, count_where)

    thr_b = jnp.broadcast_to(thr, (tq, KEY_BLOCK))

    def bias_t(kb):
        bias = jnp.where(sc[kb] >= thr_b, 0.0, NEG_BIG)
        return jnp.concatenate([bias] * GQA_GROUP, axis=0).T

    _masked_attention(o_ref, n_blocks, n_blocks, bias_t, k16, vt16, qt, att, tq)


def _dsa_sample_scratch(tq, n_key_rows):
    rows = GQA_GROUP * tq
    return [
        pltpu.VMEM((n_key_rows, KV_WIDTH), BF16),
        pltpu.VMEM((n_key_rows // KEY_BLOCK, KV_WIDTH, KEY_BLOCK), BF16),
        pltpu.VMEM((n_key_rows, LANES), BF16),
        pltpu.VMEM((n_key_rows, LANES), BF16),
        pltpu.VMEM((IDX_PAIRS, tq, LANES), F32),
        pltpu.VMEM((IDX_PAIRS, tq, LANES), F32),
        pltpu.VMEM((n_key_rows // KEY_BLOCK, tq, KEY_BLOCK), F32),
        pltpu.VMEM((N_KV_HEADS, HEAD_DIM, rows), BF16),
    ] + _attention_scratch(rows)


def _dsa_prompt(q, qi, kw, k, v, batch, seq):
    tq = 128
    nqb = seq // tq
    topk = min(TOPK_MAX, seq // 4)
    kern = functools.partial(_dsa_prompt_kernel, tq=tq, seq=seq, topk=topk)
    rows_q = GQA_GROUP * tq
    scratch = [
        pltpu.VMEM((seq, KV_WIDTH), BF16),
        pltpu.VMEM((seq // KEY_BLOCK, KV_WIDTH, KEY_BLOCK), BF16),
        pltpu.VMEM((seq, LANES), BF16),
        pltpu.VMEM((seq, LANES), BF16),
        pltpu.VMEM((seq // KEY_BLOCK + 1, KEY_BLOCK, tq), F32),
        pltpu.VMEM((2, 2, KEY_BLOCK, IDX_PAIRS * tq), F32),
        pltpu.VMEM((N_KV_HEADS, HEAD_DIM, rows_q), BF16),
    ] + _attention_scratch(rows_q)
    qmap = lambda b, i: (b * nqb + i, 0)
    smap = lambda b, i: (b, 0)
    return pl.pallas_call(
        kern,
        grid=(batch, nqb),
        in_specs=[pl.BlockSpec((tq, ATTN_WIDTH), qmap),
                  pl.BlockSpec((IDX_PAIRS, tq, LANES), lambda b, i: (0, b * nqb + i, 0)),
                  pl.BlockSpec((tq, LANES), qmap),
                  pl.BlockSpec((seq, KV_WIDTH), smap),
                  pl.BlockSpec((seq, KV_WIDTH), smap),
                  pl.BlockSpec((seq, LANES), smap)],
        out_specs=pl.BlockSpec((tq, ATTN_WIDTH), qmap),
        out_shape=jax.ShapeDtypeStruct((batch * seq, ATTN_WIDTH), BF16),
        scratch_shapes=scratch,
        compiler_params=_cparams(2),
        name="dsa_prompt",
    )(q, qi, kw, k, v, kw)


def _dsa_sample(q, qi, kw, k, v, cache_k, cache_v, cache_ki, row0, batch, seq, past):
    tq = seq
    n_key_rows = past + KEY_BLOCK
    topk = min(TOPK_MAX, (past + seq) // 4)
    kern = functools.partial(_dsa_sample_kernel, tq=tq, n_key_rows=n_key_rows, n_cache=past, n_new=seq,
                             topk=topk)
    t0 = row0 // seq
    qmap = lambda b, i: (t0 + b, 0)
    cmap = lambda b, i: (b, 0)
    return pl.pallas_call(
        kern,
        grid=(batch, 1),
        in_specs=[pl.BlockSpec((tq, ATTN_WIDTH), qmap),
                  pl.BlockSpec((IDX_PAIRS, tq, LANES), lambda b, i: (0, t0 + b, 0)),
                  pl.BlockSpec((tq, LANES), qmap),
                  pl.BlockSpec((tq, KV_WIDTH), qmap),
                  pl.BlockSpec((tq, KV_WIDTH), qmap),
                  pl.BlockSpec((tq, LANES), qmap),
                  pl.BlockSpec((past, KV_WIDTH), cmap),
                  pl.BlockSpec((past, KV_WIDTH), cmap),
                  pl.BlockSpec((past, IDX_DIM), cmap)],
        out_specs=pl.BlockSpec((tq, ATTN_WIDTH), cmap),
        out_shape=jax.ShapeDtypeStruct((batch * seq, ATTN_WIDTH), BF16),
        scratch_shapes=_dsa_sample_scratch(tq, n_key_rows),
        compiler_params=_cparams(2),
        name="dsa_sample",
    )(q, qi, kw, k, v, kw, cache_k, cache_v, cache_ki)


def _pool_kernel(*refs, tp, pos0, carried):
    if carried:
        p_ref, w_ref, s_ref, o_ref, ext, save = refs
    else:
        p_ref, halo_ref, w_ref, s_ref, o_ref, ext = refs
    i = pl.program_id(1)
    if carried:
        @pl.when(i == 0)
        def _():
            ext[0:POOL_HALO, :] = jnp.zeros((POOL_HALO, POOL_WIDTH), F32)

        @pl.when(i > 0)
        def _():
            ext[0:POOL_HALO, :] = save[...]
    else:
        ext[0:POOL_HALO, :] = halo_ref[0]
    x = p_ref[...]
    ext[POOL_HALO:POOL_HALO + tp, :] = x
    if carried:
        save[...] = x[tp - POOL_HALO:tp, :]
    pos = pos0 + i * tp + lax.broadcasted_iota(I32, (tp, 1), 0)
    for g, w in enumerate(POOL_WINDOWS):
        c0, c1 = g * POOL_GROUP_DIM, (g + 1) * POOL_GROUP_DIM
        xg = x[:, c0:c1]
        tot = xg
        for s in range(1, w):
            tot = tot + ext[POOL_HALO - s:POOL_HALO - s + tp, c0:c1]
        inv = 1.0 / jnp.minimum(pos + 1, w).astype(F32)
        d = (tot * inv - xg).astype(BF16)
        o_ref[:, c0:c1] = (_dot(d, w_ref[g]) * s_ref[:, c0:c1]).astype(BF16)


def _pool(pin, w_grp, scale, tile0, batch, seq, pos0, halo=None):
    carried = halo is None
    tp = min(seq, 256)
    nt = seq // tp
    kern = functools.partial(_pool_kernel, tp=tp, pos0=pos0, carried=carried)
    in_specs = [pl.BlockSpec((tp, POOL_WIDTH), lambda b, i: (tile0 + b * nt + i, 0))]
    args = [pin]
    if not carried:
        in_specs.append(pl.BlockSpec((1, POOL_HALO, POOL_WIDTH), lambda b, i: (b, 0, 0)))
        args.append(halo)
    in_specs += [pl.BlockSpec((POOL_GROUPS, POOL_GROUP_DIM, POOL_GROUP_DIM), lambda b, i: (0, 0, 0)),
                 pl.BlockSpec((1, POOL_WIDTH), lambda b, i: (0, 0))]
    args += [w_grp, scale]
    scratch = [pltpu.VMEM((POOL_HALO + tp, POOL_WIDTH), F32)]
    if carried:
        scratch.append(pltpu.VMEM((POOL_HALO, POOL_WIDTH), F32))
    return pl.pallas_call(
        kern,
        grid=(batch, nt),
        in_specs=in_specs,
        out_specs=pl.BlockSpec((tp, POOL_WIDTH), lambda b, i: (b * nt + i, 0)),
        out_shape=jax.ShapeDtypeStruct((batch * seq, POOL_WIDTH), BF16),
        scratch_shapes=scratch,
        compiler_params=_cparams(2),
        name="pool_prompt" if carried else "pool_sample",
    )(*args)


def _merge_kernel(h_ref, oap_ref, oas_ref, opp_ref, ops_ref, wga_ref, wgp_ref, ba_ref, bp_ref, woa_ref,
                  wop_ref, m_ref, *, n_prompt_tiles):
    h = h_ref[...]
    is_prompt = pl.program_id(1) < n_prompt_tiles
    oa = jnp.where(is_prompt, oap_ref[...], oas_ref[...])
    op = jnp.where(is_prompt, opp_ref[...], ops_ref[...])
    ga = jax.nn.sigmoid(_dot(h, wga_ref[...]) + ba_ref[...])
    gp = jax.nn.sigmoid(_dot(h, wgp_ref[...]) + bp_ref[...])
    a = _dot(oa, woa_ref[...])
    p = _dot(op, wop_ref[...])
    m_ref[...] = (ga * a + gp * p).astype(BF16)


def _merge(h, oa_p, oa_s, op_p, op_s, w_gate, b_gate, w_oa, w_op):
    n = h.shape[0]
    tm = 2 * TOK_TILE
    tn = D_MODEL // 2
    nj = D_MODEL // tn
    npt = oa_p.shape[0] // tm
    prompt = lambda w: pl.BlockSpec((tm, w), lambda j, i: (jnp.minimum(i, npt - 1), 0))
    sample = lambda w: pl.BlockSpec((tm, w), lambda j, i: (jnp.maximum(i - npt, 0), 0))
    return pl.pallas_call(
        functools.partial(_merge_kernel, n_prompt_tiles=npt),
        grid=(nj, n // tm),
        in_specs=[pl.BlockSpec((tm, D_MODEL), lambda j, i: (i, 0)),
                  prompt(ATTN_WIDTH), sample(ATTN_WIDTH), prompt(POOL_WIDTH), sample(POOL_WIDTH),
                  pl.BlockSpec((D_MODEL, tn), lambda j, i: (0, j)),
                  pl.BlockSpec((D_MODEL, tn), lambda j, i: (0, nj + j)),
                  pl.BlockSpec((1, tn), lambda j, i: (0, j)),
                  pl.BlockSpec((1, tn), lambda j, i: (0, nj + j)),
                  pl.BlockSpec((ATTN_WIDTH, tn), lambda j, i: (0, j)),
                  pl.BlockSpec((POOL_WIDTH, tn), lambda j, i: (0, j))],
        out_specs=pl.BlockSpec((tm, tn), lambda j, i: (i, j)),
        out_shape=jax.ShapeDtypeStruct((n, D_MODEL), BF16),
        compiler_params=_cparams(2),
        name="merge",
    )(h, oa_p, oa_s, op_p, op_s, w_gate, w_gate, b_gate, b_gate, w_oa, w_op)


_R_E1, _R_E2, _R_W1, _R_W2, _R_RANK1, _R_RANK2 = range(6)
_R_LOGIT0 = MOE_GROUPS


def _outproj_kernel(m_ref, xp_ref, xs_ref, wo_ref, g2_ref, wr_ref, br_ref,
                    x1_ref, h2_ref, route_ref, counts_ref, run, *, n_prompt_tiles):
    i = pl.program_id(0)
    tm = m_ref.shape[0]

    @pl.when(i == 0)
    def _():
        run[...] = jnp.zeros(run.shape, F32)

    x = jnp.where(i < n_prompt_tiles, xp_ref[...], xs_ref[...])
    x1 = x + _dot(m_ref[...], wo_ref[...])
    x1_ref[...] = x1
    r = lax.rsqrt(jnp.mean(x1 * x1, axis=-1, keepdims=True) + EPS)
    hf = (x1 * r) * g2_ref[...]
    h2_ref[...] = hf

    hi = hf.astype(BF16)
    lo = (hf - hi.astype(F32)).astype(BF16)
    hw = _dot(hi, wr_ref[...])
    lg = hw[:, :LANES] + (hw[:, LANES:] + _dot(lo, wr_ref[:, :LANES])) + br_ref[...]

    lane = lax.broadcasted_iota(I32, (tm, LANES), 1)
    neg_inf = jnp.float32(-jnp.inf)
    is_g = lane < MOE_GROUPS
    cl = jnp.where(is_g, lg, neg_inf)
    gmax = jnp.max(cl, axis=1, keepdims=True)
    g_sel = jnp.min(jnp.where(cl == gmax, lane, LANES), axis=1, keepdims=True)
    den = jnp.sum(jnp.where(is_g, jnp.exp(cl - gmax), 0.0), axis=1, keepdims=True)
    g_val = 1.0 / den
    e_lane = lane - _R_LOGIT0
    in_grp = (e_lane >= 0) & (e_lane < N_EXPERTS) & ((e_lane >> GROUP_SHIFT) == g_sel)
    f1 = jnp.where(in_grp, lg, neg_inf)
    v1 = jnp.max(f1, axis=1, keepdims=True)
    i1 = jnp.min(jnp.where(f1 == v1, lane, LANES), axis=1, keepdims=True)
    f2 = jnp.where(lane == i1, neg_inf, f1)
    v2 = jnp.max(f2, axis=1, keepdims=True)
    i2 = jnp.min(jnp.where(f2 == v2, lane, LANES), axis=1, keepdims=True)
    e2 = jnp.exp(v2 - v1)
    w1 = g_val / (1.0 + e2)
    w2 = g_val * e2 / (1.0 + e2)

    oh1 = lane == i1
    oh2 = lane == i2
    c = (oh1 | oh2).astype(BF16)
    rr = lax.broadcasted_iota(I32, (tm, tm), 0)
    cc = lax.broadcasted_iota(I32, (tm, tm), 1)
    before = (cc < rr).astype(BF16)
    prior = _dot(before, c) + run[...]
    rank1 = jnp.sum(jnp.where(oh1, prior, 0.0), axis=1, keepdims=True)
    rank2 = jnp.sum(jnp.where(oh2, prior, 0.0), axis=1, keepdims=True)
    run[...] = run[...] + jnp.sum(c.astype(F32), axis=0, keepdims=True)
    counts_ref[...] = jnp.broadcast_to(run[...], counts_ref.shape)

    rec = jnp.zeros((tm, LANES), F32)
    for ln, val in ((_R_E1, (i1 - _R_LOGIT0).astype(F32)), (_R_E2, (i2 - _R_LOGIT0).astype(F32)),
                    (_R_W1, w1), (_R_W2, w2), (_R_RANK1, rank1), (_R_RANK2, rank2)):
        rec = jnp.where(lane == ln, val, rec)
    route_ref[...] = rec


def _out_proj(m, xp, xs, w_out, g2, wr_split, b_r):
    n = m.shape[0]
    tm = TOK_TILE
    n_prompt_tiles = xp.shape[0] // tm
    tok = lambda w: pl.BlockSpec((tm, w), lambda i: (i, 0))
    const = lambda s: pl.BlockSpec(s, lambda i: (0,) * len(s))
    return pl.pallas_call(
        functools.partial(_outproj_kernel, n_prompt_tiles=n_prompt_tiles),
        grid=(n // tm,),
        in_specs=[tok(D_MODEL), *_split_token_specs(tm, D_MODEL, n_prompt_tiles),
                  pl.BlockSpec((D_MODEL, D_MODEL), lambda i: (0, 0), pipeline_mode=pl.Buffered(1)),
                  const((1, D_MODEL)), const((D_MODEL, 2 * LANES)), const((1, LANES))],
        out_specs=(tok(D_MODEL), tok(D_MODEL), tok(LANES), const((8, LANES))),
        out_shape=(jax.ShapeDtypeStruct((n, D_MODEL), F32),
                   jax.ShapeDtypeStruct((n, D_MODEL), F32),
                   jax.ShapeDtypeStruct((n, LANES), F32),
                   jax.ShapeDtypeStruct((8, LANES), F32)),
        scratch_shapes=[pltpu.VMEM((1, LANES), F32)],
        compiler_params=_cparams(1),
        name="out_proj",
    )(m, xp, xs, w_out, g2, wr_split, b_r)


def _experts_kernel(te_ref, nu_ref, src0_ref, src1_ref, src2_ref, h2_ref, wg_ref, wu_ref, wd_ref, ys_ref,
                    xbuf, x16, sem, wg16, wu16, wd16):
    i = pl.program_id(0)
    last = pl.num_programs(0) - 1
    tm = x16.shape[0]
    n_used = nu_ref[0]
    used = i < n_used
    slot = i % 2

    def start_gather(rows_ref, s):
        for r in range(tm):
            pltpu.make_async_copy(h2_ref.at[pl.ds(rows_ref[r], 1)], xbuf.at[s, pl.ds(r, 1)], sem.at[s]).start()

    def wait_gather(s):
        pltpu.make_async_copy(h2_ref.at[pl.ds(0, tm)], xbuf.at[s], sem.at[s]).wait()

    @pl.when(i == 0)
    def _():
        start_gather(src0_ref, 0)
        start_gather(src1_ref, 1)

    @pl.when(i < n_used + 2)
    def _():
        wait_gather(slot)

    prev = te_ref[jnp.maximum(i - 1, 0)]
    fresh = (i == 0) | (te_ref[i] != prev)

    @pl.when(fresh & used)
    def _():
        wg16[...] = wg_ref[0].astype(BF16)
        wu16[...] = wu_ref[0].astype(BF16)
        wd16[...] = wd_ref[0].astype(BF16)

    @pl.when(used)
    def _():
        x16[...] = xbuf[slot].astype(BF16)
        start_gather(src2_ref, slot)
        x = x16[...]
        a = _dot(x, wg16[...])
        u = _dot(x, wu16[...])
        act = (a * jax.nn.sigmoid(a)) * u
        ys_ref[...] = _dot(act.astype(BF16), wd16[...])

    @pl.when(jnp.logical_not(used))
    def _():
        ys_ref[...] = jnp.zeros(ys_ref.shape, F32)

    @pl.when((i == last) & (last - 1 < n_used))
    def _():
        wait_gather((last - 1) % 2)

    @pl.when((i == last) & used)
    def _():
        wait_gather(slot)


def _experts(tile_expert, n_used, src_rows, h2, w_g, w_u, w_d):
    n_rows = src_rows.shape[0]
    tm = EXP_TILE
    n_tiles = n_rows // tm
    src_spec = lambda ahead: pl.BlockSpec((tm,), lambda i, te, nu: (jnp.minimum(i + ahead, n_tiles - 1),),
                                          memory_space=pltpu.SMEM)
    grid_spec = pltpu.PrefetchScalarGridSpec(
        num_scalar_prefetch=2,
        grid=(n_tiles,),
        in_specs=[src_spec(0), src_spec(1), src_spec(2),
                  pl.BlockSpec(memory_space=pl.ANY),
                  pl.BlockSpec((1, D_MODEL, D_EXPERT), lambda i, te, nu: (te[i], 0, 0)),
                  pl.BlockSpec((1, D_MODEL, D_EXPERT), lambda i, te, nu: (te[i], 0, 0)),
                  pl.BlockSpec((1, D_EXPERT, D_MODEL), lambda i, te, nu: (te[i], 0, 0))],
        out_specs=pl.BlockSpec((tm, D_MODEL), lambda i, te, nu: (i, 0)),
        scratch_shapes=[pltpu.VMEM((2, tm, D_MODEL), F32),
                        pltpu.VMEM((tm, D_MODEL), BF16),
                        pltpu.SemaphoreType.DMA((2,)),
                        pltpu.VMEM((D_MODEL, D_EXPERT), BF16),
                        pltpu.VMEM((D_MODEL, D_EXPERT), BF16),
                        pltpu.VMEM((D_EXPERT, D_MODEL), BF16)],
    )
    return pl.pallas_call(
        _experts_kernel,
        grid_spec=grid_spec,
        out_shape=jax.ShapeDtypeStruct((n_rows, D_MODEL), F32),
        compiler_params=_cparams(1),
        name="experts",
    )(tile_expert, n_used, src_rows, src_rows, src_rows, h2, w_g, w_u, w_d)


COMBINE_TOKENS = 256


def _combine_kernel(pos_ref, pos_next_ref, x1_ref, route_ref, gf_ref, ys_ref, yp_ref, ysm_ref, ybuf, sem, *,
                    n_prompt_tiles):
    tc = COMBINE_TOKENS
    i = pl.program_id(0)
    buf = i % 2

    def row_copy(p_ref, b, r, slot):
        return pltpu.make_async_copy(ys_ref.at[pl.ds(p_ref[2 * r + slot], 1)],
                                     ybuf.at[b, slot, pl.ds(r, 1)], sem.at[b])

    def start_rows(p_ref, b):
        def issue(r, carry):
            row_copy(p_ref, b, r, 0).start()
            row_copy(p_ref, b, r, 1).start()
            return carry
        lax.fori_loop(0, tc, issue, 0, unroll=8)

    @pl.when(i == 0)
    def _():
        start_rows(pos_ref, 0)

    @pl.when(i + 1 < pl.num_programs(0))
    def _():
        start_rows(pos_next_ref, 1 - buf)

    def drain(r, carry):
        row_copy(pos_ref, buf, r, 0).wait()
        row_copy(pos_ref, buf, r, 1).wait()
        return carry

    lax.fori_loop(0, tc, drain, 0, unroll=8)

    route = route_ref[...]
    w1 = route[:, _R_W1:_R_W1 + 1]
    w2 = route[:, _R_W2:_R_W2 + 1]
    x2 = x1_ref[...] + (w1 * ybuf[buf, 0] + w2 * ybuf[buf, 1])
    r = lax.rsqrt(jnp.mean(x2 * x2, axis=-1, keepdims=True) + EPS)
    y = (x2 * r) * gf_ref[...]
    is_prompt = pl.program_id(0) < n_prompt_tiles

    @pl.when(is_prompt)
    def _():
        yp_ref[...] = y

    @pl.when(jnp.logical_not(is_prompt))
    def _():
        ysm_ref[...] = y


def _combine(pos_flat, x1, route, gf, ys, n_prompt):
    n = x1.shape[0]
    tc = COMBINE_TOKENS
    n_prompt_tiles = n_prompt // tc
    n_steps = n // tc
    return pl.pallas_call(
        functools.partial(_combine_kernel, n_prompt_tiles=n_prompt_tiles),
        grid=(n_steps,),
        in_specs=[pl.BlockSpec((2 * tc,), lambda i: (i,), memory_space=pltpu.SMEM),
                  pl.BlockSpec((2 * tc,), lambda i: (jnp.minimum(i + 1, n_steps - 1),),
                               memory_space=pltpu.SMEM),
                  pl.BlockSpec((tc, D_MODEL), lambda i: (i, 0)),
                  pl.BlockSpec((tc, LANES), lambda i: (i, 0)),
                  pl.BlockSpec((1, D_MODEL), lambda i: (0, 0)),
                  pl.BlockSpec(memory_space=pl.ANY)],
        out_specs=_split_token_specs(tc, D_MODEL, n_prompt_tiles),
        out_shape=(jax.ShapeDtypeStruct((n_prompt, D_MODEL), F32),
                   jax.ShapeDtypeStruct((n - n_prompt, D_MODEL), F32)),
        scratch_shapes=[pltpu.VMEM((2, 2, tc, D_MODEL), F32), pltpu.SemaphoreType.DMA((2,))],
        compiler_params=_cparams(1),
        name="combine",
    )(pos_flat, pos_flat, x1, route, gf, ys)


def _pack_w_in(w_in):
    o_q = 0
    o_k = o_q + ATTN_WIDTH
    o_v = o_k + KV_WIDTH
    o_qi = o_v + KV_WIDTH
    o_ki = o_qi + IDX_HEADS * IDX_DIM
    o_wi = o_ki + IDX_DIM
    o_p = o_wi + IDX_HEADS
    pad = jnp.zeros((w_in.shape[0], LANES - IDX_DIM - IDX_HEADS), w_in.dtype)
    return jnp.concatenate([w_in[:, :o_p], pad, w_in[:, o_p:]], axis=1).astype(BF16)


def _forward(x_prompt, x_sample, cache_k, cache_v, cache_kidx, state_pool,
             norm1_g, w_in, w_gate, b_gate, w_pool_grp, pool_scale, w_o_attn, w_o_pool, w_out,
             norm2_g, w_rg, b_rg, w_re, b_re, w_eg, w_eu, w_ed, norm_f_g):
    batch, seq, d = x_prompt.shape
    dec_batch, dec_seq, _ = x_sample.shape
    past = cache_k.shape[1]
    n_p = batch * seq
    n_s = dec_batch * dec_seq
    n = n_p + n_s
    tm = TOK_TILE
    assert d == D_MODEL and seq % tm == 0 and n_s % tm == 0 and n % (2 * tm) == 0
    assert past % KEY_BLOCK == 0 and dec_seq >= POOL_STATE and dec_seq <= KEY_BLOCK

    xp = x_prompt.reshape(n_p, d)
    xs = x_sample.reshape(n_s, d)
    pos_tab = jnp.concatenate([jnp.arange(seq), past + jnp.arange(n_s) % dec_seq])
    tables = _rope_tables(pos_tab)
    h, q, k, v, qi, kw, pin = _in_proj(xp, xs, norm1_g.reshape(1, d), _pack_w_in(w_in), tables, seq // tm)

    oa_p = _dsa_prompt(q, qi, kw, k, v, batch, seq)
    oa_s = _dsa_sample(q, qi, kw, k, v,
                       cache_k.reshape(dec_batch * past, KV_WIDTH),
                       cache_v.reshape(dec_batch * past, KV_WIDTH),
                       cache_kidx.reshape(dec_batch * past, IDX_DIM),
                       n_p, dec_batch, dec_seq, past)

    w_grp16 = w_pool_grp.astype(BF16)
    scale = pool_scale.reshape(1, POOL_WIDTH)
    op_p = _pool(pin, w_grp16, scale, 0, batch, seq, 0)
    halo = jnp.concatenate([jnp.zeros((dec_batch, 1, POOL_WIDTH), F32), state_pool], axis=1)
    op_s = _pool(pin, w_grp16, scale, n_p // dec_seq, dec_batch, dec_seq, past, halo=halo)

    m = _merge(h, oa_p, oa_s, op_p, op_s, w_gate.astype(BF16), b_gate.reshape(1, 2 * d),
               w_o_attn.astype(BF16), w_o_pool.astype(BF16))

    w_r = jnp.concatenate([w_rg, w_re, jnp.zeros((d, LANES - MOE_GROUPS - N_EXPERTS), F32)], axis=1)
    b_r = jnp.concatenate([b_rg, b_re, jnp.zeros((LANES - MOE_GROUPS - N_EXPERTS,), F32)]).reshape(1, LANES)
    wr_hi = w_r.astype(BF16)
    wr_lo = (w_r - wr_hi.astype(F32)).astype(BF16)
    wr_split = jnp.concatenate([wr_hi, wr_lo], axis=1)
    x1, h2, route, counts = _out_proj(m, xp, xs, w_out.astype(BF16), norm2_g.reshape(1, d), wr_split, b_r)

    te = EXP_TILE
    n_tiles = (2 * n) // te + N_EXPERTS
    cnt = counts[0, _R_LOGIT0:_R_LOGIT0 + N_EXPERTS].astype(I32)
    tiles_per_e = (cnt + te - 1) // te
    tile_end = jnp.cumsum(tiles_per_e)
    row_start = (tile_end - tiles_per_e) * te
    n_used = tile_end[-1:].astype(I32)
    tile_ids = jnp.arange(n_tiles, dtype=I32)
    tile_expert = jnp.minimum(
        jnp.sum((tile_end[None, :] <= tile_ids[:, None]).astype(I32), axis=1), N_EXPERTS - 1)
    last_e = tile_expert[jnp.maximum(n_used[0] - 1, 0)]
    tile_expert = jnp.where(tile_ids < n_used[0], tile_expert, last_e)
    eid = route[:, _R_E1:_R_E2 + 1].astype(I32)
    rank = route[:, _R_RANK1:_R_RANK2 + 1].astype(I32)
    pos_flat = (row_start[eid] + rank).reshape(2 * n)

    src_rows = jnp.zeros((n_tiles * te,), I32).at[pos_flat].set(
        jnp.arange(2 * n, dtype=I32) // 2, unique_indices=True)
    ys = _experts(tile_expert, n_used, src_rows, h2, w_eg, w_eu, w_ed)
    y_p, y_s = _combine(pos_flat, x1, route, norm_f_g.reshape(1, d), ys, n_p)

    y_prompt = y_p.reshape(batch, seq, d)
    y_sample = y_s.reshape(dec_batch, dec_seq, d)
    k_p = k[:n_p].reshape(1, batch, seq, N_KV_HEADS, HEAD_DIM)
    v_p = v[:n_p].reshape(1, batch, seq, N_KV_HEADS, HEAD_DIM)
    ki_p = kw[:n_p, :IDX_DIM].reshape(1, batch, seq, IDX_DIM)
    pool_p = jnp.stack([pin[(b + 1) * seq - POOL_STATE:(b + 1) * seq] for b in range(batch)])[None]
    k_s = k[n_p:].reshape(1, dec_batch, dec_seq, N_KV_HEADS, HEAD_DIM)
    v_s = v[n_p:].reshape(1, dec_batch, dec_seq, N_KV_HEADS, HEAD_DIM)
    ki_s = kw[n_p:, :IDX_DIM].reshape(1, dec_batch, dec_seq, IDX_DIM)
    pool_s = pin[n_p:].reshape(dec_batch, dec_seq, POOL_WIDTH)[None, :, dec_seq - POOL_STATE:, :]
    return (y_prompt, y_sample, k_p, v_p, ki_p, pool_p, k_s, v_s, ki_s, pool_s)


def kernel(x_prompt, x_sample, cache_k, cache_v, cache_kidx, state_pool, norm1_g, w_in, w_gate, b_gate,
           w_pool_grp, pool_scale, w_o_attn, w_o_pool, w_out, norm2_g, w_router_group, b_router_group,
           w_router_expert, b_router_expert, w_exp_gate, w_exp_up, w_exp_down, norm_f_g):
    assert cache_k.shape[0] == 1, "single-layer model"
    return _forward(x_prompt, x_sample, cache_k[0], cache_v[0], cache_kidx[0], state_pool[0],
                    norm1_g[0], w_in[0], w_gate[0], b_gate[0], w_pool_grp[0], pool_scale[0],
                    w_o_attn[0], w_o_pool[0], w_out[0], norm2_g[0], w_router_group[0],
                    b_router_group[0], w_router_expert[0], b_router_expert[0],
                    w_exp_gate[0], w_exp_up[0], w_exp_down[0], norm_f_g)
```

```python
import functools

import numpy as np
import jax
import jax.numpy as jnp
from jax import lax
from jax.experimental import pallas as pl
from jax.experimental.pallas import tpu as pltpu

F32 = jnp.float32
BF16 = jnp.bfloat16
I32 = jnp.int32

D_MODEL = 2048
CHUNK = 64
N_HEADS = 8
N_KV_HEADS = 2
HEAD_DIM = 128
GQA_GROUP = N_HEADS // N_KV_HEADS
ATTN_WIDTH = N_HEADS * HEAD_DIM
KV_WIDTH = N_KV_HEADS * HEAD_DIM
IDX_HEADS = 16
IDX_DIM = 64
IDX_PAIRS = IDX_HEADS // 2
TOPK_MAX = 256
POOL_WINDOWS = (2, 4, 8, 16)
POOL_GROUPS = 4
POOL_WIDTH = D_MODEL // 2
POOL_GROUP_DIM = POOL_WIDTH // POOL_GROUPS
POOL_STATE = 15
POOL_HALO = 16
MOE_GROUPS = 4
EXPERTS_PER_GROUP = 8
N_EXPERTS = MOE_GROUPS * EXPERTS_PER_GROUP
D_EXPERT = 512
ROPE_THETA = 10000.0
CHUNK_SHIFT = CHUNK.bit_length() - 1
GROUP_SHIFT = EXPERTS_PER_GROUP.bit_length() - 1
EPS = 1e-6

LANES = 128
INT_MIN = -(2 ** 31)
NEG_BIG = -1e30
LOG2_E = 1.4426950408889634
VMEM_LIMIT = 56 * 1024 * 1024

KEY_BLOCK = 256
TOK_TILE = 256
EXP_TILE = 256


def _cparams(n_axes):
    return pltpu.CompilerParams(dimension_semantics=("arbitrary",) * n_axes,
                                vmem_limit_bytes=VMEM_LIMIT)


def _dot(a, b):
    return jnp.dot(a, b, preferred_element_type=F32)


def _dot_nt(a, b):
    return lax.dot_general(a, b, (((1,), (1,)), ((), ())), preferred_element_type=F32)


_C_Q = 0
_C_K = _C_Q + ATTN_WIDTH
_C_V = _C_K + KV_WIDTH
_C_QI = _C_V + KV_WIDTH
_C_KW = _C_QI + IDX_HEADS * IDX_DIM
_C_P = _C_KW + IDX_DIM + IDX_HEADS


def _rope128(y, cos, sin_signed):
    return y * cos + pltpu.roll(y, HEAD_DIM // 2, 1) * sin_signed


def _rope64(y, cos, sin_a, sin_b):
    half = IDX_DIM // 2
    return y * cos + pltpu.roll(y, LANES - half, 1) * sin_a + pltpu.roll(y, half, 1) * sin_b


def _inproj_kernel(xp_ref, xs_ref, g_ref, w_ref, wkw_ref, wp_ref, ck_ref, sk_ref, cq_ref, sq_ref, ci_ref,
                   sia_ref, sib_ref, cw_ref, swa_ref, swb_ref,
                   h_ref, q_ref, kp_ref, ks_ref, vp_ref, vs_ref, qi_ref, kw_ref, p_ref, *, n_prompt_tiles):
    is_prompt = pl.program_id(0) < n_prompt_tiles
    x = jnp.where(is_prompt, xp_ref[...], xs_ref[...])
    r = lax.rsqrt(jnp.mean(x * x, axis=-1, keepdims=True) + EPS)
    h = ((x * r) * g_ref[...]).astype(BF16)
    h_ref[...] = h
    cq, sq = cq_ref[...], sq_ref[...]
    for c in range(ATTN_WIDTH // 256):
        y = _dot(h, w_ref[:, _C_Q + c * 256:_C_Q + (c + 1) * 256])
        for s in range(2):
            q_ref[:, c * 256 + s * LANES:c * 256 + (s + 1) * LANES] = _rope128(
                y[:, s * LANES:(s + 1) * LANES], cq, sq).astype(BF16)
    ck, sk = ck_ref[...], sk_ref[...]
    y = _dot(h, w_ref[:, _C_K:_C_K + KV_WIDTH])
    k = jnp.concatenate([_rope128(y[:, s * LANES:(s + 1) * LANES], ck, sk) for s in range(N_KV_HEADS)], axis=1)
    v = _dot(h, w_ref[:, _C_V:_C_V + KV_WIDTH])

    @pl.when(is_prompt)
    def _():
        kp_ref[...] = k
        vp_ref[...] = v

    @pl.when(jnp.logical_not(is_prompt))
    def _():
        ks_ref[...] = k
        vs_ref[...] = v

    ci, sia, sib = ci_ref[...], sia_ref[...], sib_ref[...]
    for c in range(IDX_PAIRS // 2):
        y = _dot(h, w_ref[:, _C_QI + c * 256:_C_QI + (c + 1) * 256])
        for s in range(2):
            qi_ref[2 * c + s] = _rope64(y[:, s * LANES:(s + 1) * LANES], ci, sia, sib).astype(BF16)
    y = _dot(h, wkw_ref[...])
    kw_ref[...] = _rope64(y, cw_ref[...], swa_ref[...], swb_ref[...])
    for c in range(POOL_WIDTH // 256):
        p_ref[:, c * 256:(c + 1) * 256] = _dot(h, wp_ref[:, c * 256:(c + 1) * 256])


def _rope_tables(pos):
    pos = pos.astype(F32)[:, None]

    def cs(dim):
        half = dim // 2
        inv = ROPE_THETA ** (-jnp.arange(half, dtype=F32) / half)
        ang = pos * inv[None, :]
        return jnp.cos(ang), jnp.sin(ang)

    c, s = cs(HEAD_DIM)
    ck = jnp.concatenate([c, c], axis=1)
    sk = jnp.concatenate([-s, s], axis=1)
    qs = HEAD_DIM ** -0.5 * LOG2_E
    c, s = cs(IDX_DIM)
    z = jnp.zeros_like(s)
    ci = jnp.concatenate([c, c, c, c], axis=1)
    sia = jnp.concatenate([-s, z, -s, z], axis=1)
    sib = jnp.concatenate([z, s, z, s], axis=1)
    iscale = IDX_DIM ** -0.5
    n = pos.shape[0]
    wscale = jnp.full((n, IDX_HEADS), IDX_HEADS ** -0.5, F32)
    pad = jnp.zeros((n, LANES - IDX_DIM - IDX_HEADS), F32)
    zz = jnp.zeros((n, LANES - IDX_DIM), F32)
    cw = jnp.concatenate([c, c, wscale, pad], axis=1)
    swa = jnp.concatenate([-s, z, zz], axis=1)
    swb = jnp.concatenate([z, s, zz], axis=1)
    return (ck, sk, ck * qs, sk * qs, ci * iscale, sia * iscale, sib * iscale, cw, swa, swb)


def _split_token_specs(tm, width, n_prompt_tiles):
    prompt = pl.BlockSpec((tm, width), lambda i: (jnp.minimum(i, n_prompt_tiles - 1), 0))
    sample = pl.BlockSpec((tm, width), lambda i: (jnp.maximum(i - n_prompt_tiles, 0), 0))
    return prompt, sample


def _in_proj(xp, xs, g1, w_main, w_kw, w_pool, tables, tiles_per_seq):
    tm = TOK_TILE
    n_prompt_tiles = xp.shape[0] // tm
    n = xp.shape[0] + xs.shape[0]
    n_tiles = n // tm

    def tab_idx(i):
        return (jnp.where(i < n_prompt_tiles, i % tiles_per_seq, tiles_per_seq + i - n_prompt_tiles), 0)

    tok = lambda w: pl.BlockSpec((tm, w), lambda i: (i, 0))
    tab = pl.BlockSpec((tm, LANES), tab_idx)
    resident = lambda a: pl.BlockSpec(a.shape, lambda i: (0, 0), pipeline_mode=pl.Buffered(1))
    n_s = xs.shape[0]
    kv_p, kv_s = _split_token_specs(tm, KV_WIDTH, n_prompt_tiles)
    out_shape = (
        jax.ShapeDtypeStruct((n, D_MODEL), BF16),
        jax.ShapeDtypeStruct((n, ATTN_WIDTH), BF16),
        jax.ShapeDtypeStruct((n - n_s, KV_WIDTH), F32),
        jax.ShapeDtypeStruct((n_s, KV_WIDTH), F32),
        jax.ShapeDtypeStruct((n - n_s, KV_WIDTH), F32),
        jax.ShapeDtypeStruct((n_s, KV_WIDTH), F32),
        jax.ShapeDtypeStruct((IDX_PAIRS, n, LANES), BF16),
        jax.ShapeDtypeStruct((n, LANES), F32),
        jax.ShapeDtypeStruct((n, POOL_WIDTH), F32),
    )
    return pl.pallas_call(
        functools.partial(_inproj_kernel, n_prompt_tiles=n_prompt_tiles),
        grid=(n_tiles,),
        in_specs=[*_split_token_specs(tm, D_MODEL, n_prompt_tiles),
                  pl.BlockSpec((1, D_MODEL), lambda i: (0, 0)),
                  resident(w_main), resident(w_kw), resident(w_pool)]
                 + [tab] * 10,
        out_specs=(tok(D_MODEL), tok(ATTN_WIDTH), kv_p, kv_s, kv_p, kv_s,
                   pl.BlockSpec((IDX_PAIRS, tm, LANES), lambda i: (0, i, 0)),
                   tok(LANES), tok(POOL_WIDTH)),
        out_shape=out_shape,
        compiler_params=_cparams(1),
        name="in_proj",
    )(xp, xs, g1, w_main, w_kw, w_pool, *tables)


def _ordinal_to_f32(k):
    return lax.bitcast_convert_type(k ^ ((k >> 31) & 0x7FFFFFFF), F32)


def _kth_largest_score(topk, shape, n_key_rows, count_where):
    def count_ge(cand):
        cand_f = _ordinal_to_f32(cand)
        return count_where(lambda keys, kb: keys >= jnp.broadcast_to(cand_f, keys.shape))

    c0 = count_ge(jnp.zeros(shape, I32))
    thr0 = jnp.where(c0 >= topk, 0, INT_MIN).astype(I32)
    cnt0 = jnp.where(c0 >= topk, c0, n_key_rows).astype(I32)

    def bit_step(i, carry):
        thr, cnt = carry
        cand = thr + (jnp.int32(1) << (30 - i))
        c = count_ge(cand)
        ok = c >= topk
        return jnp.where(ok, cand, thr), jnp.where(ok, c, cnt)

    thr, cnt = lax.fori_loop(0, 31, bit_step, (thr0, cnt0))
    has_k = thr > INT_MIN
    thr_f = jnp.where(has_k, _ordinal_to_f32(thr), jnp.finfo(F32).min)
    return thr_f, has_k & (cnt > topk)


def _drop_excess_ties(sc, n_blocks, topk, thr, need_tie, key_index, count_where):
    n_gt = count_where(lambda keys, kb: keys > jnp.broadcast_to(thr, keys.shape))
    n_keep = topk - n_gt

    def idx_step(i, bound):
        cand = bound + (jnp.int32(1) << (14 - i))
        c = count_where(lambda keys, kb: (keys == jnp.broadcast_to(thr, keys.shape))
                        & (key_index(kb) < jnp.broadcast_to(cand, keys.shape)))
        return jnp.where(c <= n_keep, cand, bound)

    bound = lax.fori_loop(0, 15, idx_step, jnp.zeros(thr.shape, I32))
    bound = jnp.where(need_tie, bound, jnp.int32(2 ** 30))

    def drop_block(kb, carry):
        keys = sc[kb]
        drop = (keys == jnp.broadcast_to(thr, keys.shape)) & (key_index(kb) >= jnp.broadcast_to(bound, keys.shape))
        sc[kb] = jnp.where(drop, -jnp.inf, keys)
        return carry

    lax.fori_loop(0, n_blocks, drop_block, 0)


def _stack_queries_t(q_ref, qt):
    for g in range(N_KV_HEADS):
        qg = jnp.concatenate([q_ref[:, (g * GQA_GROUP + j) * HEAD_DIM:(g * GQA_GROUP + j + 1) * HEAD_DIM]
                              for j in range(GQA_GROUP)], axis=0)
        qt[g] = qg.astype(F32).T.astype(BF16)


def _attention_scratch(rows):
    return [
        pltpu.VMEM((N_KV_HEADS, 1, rows), F32),
        pltpu.VMEM((N_KV_HEADS, 1, rows), F32),
        pltpu.VMEM((N_KV_HEADS, HEAD_DIM, rows), F32),
        pltpu.VMEM((2, N_KV_HEADS, KEY_BLOCK, rows), F32),
        pltpu.VMEM((2, N_KV_HEADS, KEY_BLOCK, rows), BF16),
        pltpu.VMEM((2, N_KV_HEADS, 1, rows), F32),
    ]


def _masked_attention(o_ref, n_blocks, n_blocks_all, bias_t, k16, vt16, qt, att, tq):
    m_s, l_s, acc_s, lg_s, p_s, alpha_s = att
    m_s[...] = jnp.full(m_s.shape, NEG_BIG, F32)
    l_s[...] = jnp.zeros(l_s.shape, F32)
    acc_s[...] = jnp.zeros(acc_s.shape, F32)
    p_s[1] = jnp.zeros(p_s.shape[1:], BF16)
    alpha_s[1] = jnp.ones(alpha_s.shape[1:], F32)

    def logits_of(kb, slot):
        r0 = pl.multiple_of(kb * KEY_BLOCK, KEY_BLOCK)
        for g in range(N_KV_HEADS):
            lg_s[slot, g] = _dot(k16[pl.ds(r0, KEY_BLOCK), g * HEAD_DIM:(g + 1) * HEAD_DIM], qt[g])

    def value_product(kb, slot):
        for g in range(N_KV_HEADS):
            acc_s[g] = (alpha_s[slot, g] * acc_s[g]
                        + _dot(vt16[kb, g * HEAD_DIM:(g + 1) * HEAD_DIM, :], p_s[slot, g]))

    logits_of(0, 0)
    last = n_blocks_all - 1

    def step(kb, slot):
        logits_of(jnp.minimum(kb + 1, last), 1 - slot)
        bias = jnp.where(kb < n_blocks, bias_t(jnp.minimum(kb, last)), NEG_BIG)
        for g in range(N_KV_HEADS):
            logits = lg_s[slot, g] + bias
            m_prev = m_s[g]
            m_new = jnp.maximum(m_prev, jnp.max(logits, axis=0, keepdims=True))
            p = jnp.exp2(logits - m_new)
            alpha = jnp.exp2(m_prev - m_new)
            l_s[g] = alpha * l_s[g] + jnp.sum(p, axis=0, keepdims=True)
            p_s[slot, g] = p.astype(BF16)
            alpha_s[slot, g] = alpha
            m_s[g] = m_new
        value_product(jnp.clip(kb - 1, 0, last), 1 - slot)

    def two_steps(i, carry):
        step(2 * i, 0)
        step(2 * i + 1, 1)
        return carry

    n_pairs = (n_blocks + 1) // 2
    lax.fori_loop(0, n_pairs, two_steps, 0)
    value_product(jnp.minimum(2 * n_pairs - 1, last), 1)

    for g in range(N_KV_HEADS):
        o = (acc_s[g] * (1.0 / l_s[g])).T
        for j in range(GQA_GROUP):
            hd = g * GQA_GROUP + j
            o_ref[:, hd * HEAD_DIM:(hd + 1) * HEAD_DIM] = o[j * tq:(j + 1) * tq, :].astype(BF16)


def _dsa_prompt_kernel(q_ref, qi_ref, wq_ref, kn_ref, vn_ref, kwn_ref, o_ref,
                       k16, vt16, kia, kib, sc, raw, qt, *att, tq, seq, topk):
    qb = pl.program_id(1)
    n_blocks_all = seq // KEY_BLOCK

    @pl.when(qb == 0)
    def _():
        lane = lax.broadcasted_iota(I32, (seq, LANES), 1)
        kin = jnp.where(lane < IDX_DIM, kwn_ref[...], 0.0)
        k16[...] = kn_ref[...].astype(BF16)
        kia[...] = kin.astype(BF16)
        kib[...] = pltpu.roll(kin, IDX_DIM, 1).astype(BF16)
        for kb in range(n_blocks_all):
            vt16[kb] = vn_ref[kb * KEY_BLOCK:(kb + 1) * KEY_BLOCK, :].T.astype(BF16)

    n_blocks = qb // (KEY_BLOCK // tq) + 1

    qit = qi_ref[...].reshape(IDX_PAIRS * tq, LANES).astype(F32).T.astype(BF16)
    wt = wq_ref[...].T
    w_rows = [wt[IDX_DIM + j:IDX_DIM + j + 1, :] for j in range(IDX_HEADS)]
    _stack_queries_t(q_ref, qt)

    key_row = lax.broadcasted_iota(I32, (KEY_BLOCK, tq), 0)
    q_pos = qb * tq + lax.broadcasted_iota(I32, (KEY_BLOCK, tq), 1)

    last = n_blocks_all - 1
    n_pairs = (n_blocks + 1) // 2

    def raw_scores(kb, slot):
        r0 = pl.multiple_of(kb * KEY_BLOCK, KEY_BLOCK)
        raw[slot, 0] = _dot(kia[pl.ds(r0, KEY_BLOCK), :], qit)
        raw[slot, 1] = _dot(kib[pl.ds(r0, KEY_BLOCK), :], qit)

    def score_step(kb, slot):
        raw_scores(jnp.minimum(kb + 1, last), 1 - slot)
        se = raw[slot, 0]
        so = raw[slot, 1]
        score = jnp.zeros((KEY_BLOCK, tq), F32)
        for p in range(IDX_PAIRS):
            score = (score + jnp.maximum(se[:, p * tq:(p + 1) * tq], 0.0) * w_rows[2 * p]
                     + jnp.maximum(so[:, p * tq:(p + 1) * tq], 0.0) * w_rows[2 * p + 1])
        adm = ((kb * KEY_BLOCK + key_row) >> CHUNK_SHIFT) <= (q_pos >> CHUNK_SHIFT)
        sc[jnp.where(kb < n_blocks, kb, n_blocks_all)] = jnp.where(adm, score, -jnp.inf)

    def two_score_steps(i, carry):
        score_step(2 * i, 0)
        score_step(2 * i + 1, 1)
        return carry

    raw_scores(0, 0)
    lax.fori_loop(0, n_pairs, two_score_steps, 0)

    def count_where(pred):
        def body(i, acc):
            for s in range(2):
                kb = 2 * i + s
                kbc = jnp.minimum(kb, last)
                m = pred(sc[kbc], kbc).astype(I32)
                part = jnp.sum(m.reshape(KEY_BLOCK // 8, 8, tq), axis=0)
                acc = acc + jnp.where(kb < n_blocks, part, 0)
            return acc
        acc = lax.fori_loop(0, n_pairs, body, jnp.zeros((8, tq), I32))
        return jnp.sum(acc, axis=0, keepdims=True)

    thr, need_tie = _kth_largest_score(topk, (1, tq), seq, count_where)

    @pl.when(jnp.max(need_tie.astype(I32)) > 0)
    def _():
        _drop_excess_ties(sc, n_blocks, topk, thr, need_tie,
                          lambda kb: kb * KEY_BLOCK + key_row, count_where)

    thr_b = jnp.broadcast_to(thr, (KEY_BLOCK, tq))

    def bias_t(kb):
        bias = jnp.where(sc[kb] >= thr_b, 0.0, NEG_BIG)
        return jnp.concatenate([bias] * GQA_GROUP, axis=1)

    _masked_attention(o_ref, n_blocks, n_blocks_all, bias_t, k16, vt16, qt, att, tq)


def _dsa_sample_kernel(q_ref, qi_ref, wq_ref, kn_ref, vn_ref, kwn_ref, kc_ref, vc_ref, kic_ref, o_ref,
                       k16, vt16, kia, kib, wbe, wbo, sc, qt, *att,
                       tq, n_key_rows, n_cache, n_new, topk):
    n_blocks = n_key_rows // KEY_BLOCK
    n_cache_blocks = n_cache // KEY_BLOCK

    lane = lax.broadcasted_iota(I32, (n_new, LANES), 1)
    kin = jnp.where(lane < IDX_DIM, kwn_ref[...], 0.0).astype(BF16)
    n_tail = n_key_rows - n_cache
    k16[0:n_cache, :] = kc_ref[...].astype(BF16)
    for kb in range(n_cache_blocks):
        vt16[kb] = vc_ref[kb * KEY_BLOCK:(kb + 1) * KEY_BLOCK, :].T.astype(BF16)
    v_tail = jnp.concatenate([vn_ref[...], jnp.zeros((n_tail - n_new, KV_WIDTH), F32)], axis=0)
    vt16[n_cache_blocks] = v_tail.T.astype(BF16)
    kic = jnp.concatenate([kic_ref[...], jnp.zeros((n_cache, LANES - IDX_DIM), F32)], axis=1)
    kia[0:n_cache, :] = kic.astype(BF16)
    kib[0:n_cache, :] = pltpu.roll(kic, IDX_DIM, 1).astype(BF16)
    k16[n_cache:n_key_rows, :] = jnp.zeros((n_tail, KV_WIDTH), BF16)
    kia[n_cache:n_key_rows, :] = jnp.zeros((n_tail, LANES), BF16)
    kib[n_cache:n_key_rows, :] = jnp.zeros((n_tail, LANES), BF16)
    k16[n_cache:n_cache + n_new, :] = kn_ref[...].astype(BF16)
    kia[n_cache:n_cache + n_new, :] = kin
    kib[n_cache:n_cache + n_new, :] = pltpu.roll(kin.astype(F32), IDX_DIM, 1).astype(BF16)

    wq = wq_ref[...]
    for j in range(IDX_HEADS):
        col = jnp.broadcast_to(wq[:, IDX_DIM + j:IDX_DIM + j + 1], (tq, LANES))
        if j % 2 == 0:
            wbe[j // 2] = col
        else:
            wbo[j // 2] = col
    _stack_queries_t(q_ref, qt)

    lane = lax.broadcasted_iota(I32, (tq, KEY_BLOCK), 1)

    qi2 = qi_ref[...].reshape(IDX_PAIRS * tq, LANES)

    def score_block(kb, carry):
        r0 = pl.multiple_of(kb * KEY_BLOCK, KEY_BLOCK)
        se = _dot_nt(qi2, kia[pl.ds(r0, KEY_BLOCK), :]).reshape(IDX_PAIRS, tq, KEY_BLOCK)
        so = _dot_nt(qi2, kib[pl.ds(r0, KEY_BLOCK), :]).reshape(IDX_PAIRS, tq, KEY_BLOCK)
        score = jnp.zeros((tq, KEY_BLOCK), F32)
        for p in range(IDX_PAIRS):
            we = wbe[p]
            wo = wbo[p]
            we2 = jnp.concatenate([we, we], axis=1)
            wo2 = jnp.concatenate([wo, wo], axis=1)
            score = score + jnp.maximum(se[p], 0.0) * we2 + jnp.maximum(so[p], 0.0) * wo2
        adm = (kb * KEY_BLOCK + lane) < (n_cache + n_new)
        sc[kb] = jnp.where(adm, score, -jnp.inf)
        return carry

    lax.fori_loop(0, n_blocks, score_block, 0)

    def count_where(pred):
        def body(kb, acc):
            m = pred(sc[kb], kb).astype(I32)
            return acc + m[:, :LANES] + m[:, LANES:]
        acc = lax.fori_loop(0, n_blocks, body, jnp.zeros((tq, LANES), I32))
        return jnp.sum(acc, axis=1, keepdims=True)

    thr, need_tie = _kth_largest_score(topk, (tq, 1), n_key_rows, count_where)

    @pl.when(jnp.max(need_tie.astype(I32)) > 0)
    def _():
        _drop_excess_ties(sc, n_blocks, topk, thr, need_tie,
                          lambda kb: kb * KEY_BLOCK + lane, count_where)

    thr_b = jnp.broadcast_to(thr, (tq, KEY_BLOCK))

    def bias_t(kb):
        bias = jnp.where(sc[kb] >= thr_b, 0.0, NEG_BIG)
        return jnp.concatenate([bias] * GQA_GROUP, axis=0).T

    _masked_attention(o_ref, n_blocks, n_blocks, bias_t, k16, vt16, qt, att, tq)


def _dsa_sample_scratch(tq, n_key_rows):
    rows = GQA_GROUP * tq
    return [
        pltpu.VMEM((n_key_rows, KV_WIDTH), BF16),
        pltpu.VMEM((n_key_rows // KEY_BLOCK, KV_WIDTH, KEY_BLOCK), BF16),
        pltpu.VMEM((n_key_rows, LANES), BF16),
        pltpu.VMEM((n_key_rows, LANES), BF16),
        pltpu.VMEM((IDX_PAIRS, tq, LANES), F32),
        pltpu.VMEM((IDX_PAIRS, tq, LANES), F32),
        pltpu.VMEM((n_key_rows // KEY_BLOCK, tq, KEY_BLOCK), F32),
        pltpu.VMEM((N_KV_HEADS, HEAD_DIM, rows), BF16),
    ] + _attention_scratch(rows)


def _dsa_prompt(q, qi, kw, k, v, batch, seq):
    tq = 128
    nqb = seq // tq
    topk = min(TOPK_MAX, seq // 4)
    kern = functools.partial(_dsa_prompt_kernel, tq=tq, seq=seq, topk=topk)
    rows_q = GQA_GROUP * tq
    scratch = [
        pltpu.VMEM((seq, KV_WIDTH), BF16),
        pltpu.VMEM((seq // KEY_BLOCK, KV_WIDTH, KEY_BLOCK), BF16),
        pltpu.VMEM((seq, LANES), BF16),
        pltpu.VMEM((seq, LANES), BF16),
        pltpu.VMEM((seq // KEY_BLOCK + 1, KEY_BLOCK, tq), F32),
        pltpu.VMEM((2, 2, KEY_BLOCK, IDX_PAIRS * tq), F32),
        pltpu.VMEM((N_KV_HEADS, HEAD_DIM, rows_q), BF16),
    ] + _attention_scratch(rows_q)
    qmap = lambda b, i: (b * nqb + i, 0)
    smap = lambda b, i: (b, 0)
    return pl.pallas_call(
        kern,
        grid=(batch, nqb),
        in_specs=[pl.BlockSpec((tq, ATTN_WIDTH), qmap),
                  pl.BlockSpec((IDX_PAIRS, tq, LANES), lambda b, i: (0, b * nqb + i, 0)),
                  pl.BlockSpec((tq, LANES), qmap),
                  pl.BlockSpec((seq, KV_WIDTH), smap),
                  pl.BlockSpec((seq, KV_WIDTH), smap),
                  pl.BlockSpec((seq, LANES), smap)],
        out_specs=pl.BlockSpec((tq, ATTN_WIDTH), qmap),
        out_shape=jax.ShapeDtypeStruct((batch * seq, ATTN_WIDTH), BF16),
        scratch_shapes=scratch,
        compiler_params=_cparams(2),
        name="dsa_prompt",
    )(q, qi, kw, k, v, kw)


def _dsa_sample(q, qi, kw, k, v, cache_k, cache_v, cache_ki, row0, batch, seq, past):
    tq = seq
    n_key_rows = past + KEY_BLOCK
    topk = min(TOPK_MAX, (past + seq) // 4)
    kern = functools.partial(_dsa_sample_kernel, tq=tq, n_key_rows=n_key_rows, n_cache=past, n_new=seq,
                             topk=topk)
    t0 = row0 // seq
    qmap = lambda b, i: (t0 + b, 0)
    cmap = lambda b, i: (b, 0)
    return pl.pallas_call(
        kern,
        grid=(batch, 1),
        in_specs=[pl.BlockSpec((tq, ATTN_WIDTH), qmap),
                  pl.BlockSpec((IDX_PAIRS, tq, LANES), lambda b, i: (0, t0 + b, 0)),
                  pl.BlockSpec((tq, LANES), qmap),
                  pl.BlockSpec((tq, KV_WIDTH), cmap),
                  pl.BlockSpec((tq, KV_WIDTH), cmap),
                  pl.BlockSpec((tq, LANES), qmap),
                  pl.BlockSpec((past, KV_WIDTH), cmap),
                  pl.BlockSpec((past, KV_WIDTH), cmap),
                  pl.BlockSpec((past, IDX_DIM), cmap)],
        out_specs=pl.BlockSpec((tq, ATTN_WIDTH), cmap),
        out_shape=jax.ShapeDtypeStruct((batch * seq, ATTN_WIDTH), BF16),
        scratch_shapes=_dsa_sample_scratch(tq, n_key_rows),
        compiler_params=_cparams(2),
        name="dsa_sample",
    )(q, qi, kw, k, v, kw, cache_k, cache_v, cache_ki)


def _pool_kernel(*refs, tp, pos0, carried):
    if carried:
        p_ref, w_ref, s_ref, o_ref, ext, save = refs
    else:
        p_ref, halo_ref, w_ref, s_ref, o_ref, ext = refs
    i = pl.program_id(1)
    if carried:
        @pl.when(i == 0)
        def _():
            ext[0:POOL_HALO, :] = jnp.zeros((POOL_HALO, POOL_WIDTH), F32)

        @pl.when(i > 0)
        def _():
            ext[0:POOL_HALO, :] = save[...]
    else:
        ext[0:POOL_HALO, :] = halo_ref[0]
    x = p_ref[...]
    ext[POOL_HALO:POOL_HALO + tp, :] = x
    if carried:
        save[...] = x[tp - POOL_HALO:tp, :]
    pos = pos0 + i * tp + lax.broadcasted_iota(I32, (tp, 1), 0)
    for g, w in enumerate(POOL_WINDOWS):
        c0, c1 = g * POOL_GROUP_DIM, (g + 1) * POOL_GROUP_DIM
        xg = x[:, c0:c1]
        tot = xg
        for s in range(1, w):
            tot = tot + ext[POOL_HALO - s:POOL_HALO - s + tp, c0:c1]
        inv = 1.0 / jnp.minimum(pos + 1, w).astype(F32)
        d = (tot * inv - xg).astype(BF16)
        o_ref[:, c0:c1] = (_dot(d, w_ref[g]) * s_ref[:, c0:c1]).astype(BF16)


def _pool(pin, w_grp, scale, tile0, batch, seq, pos0, halo=None):
    carried = halo is None
    tp = min(seq, 256)
    nt = seq // tp
    kern = functools.partial(_pool_kernel, tp=tp, pos0=pos0, carried=carried)
    in_specs = [pl.BlockSpec((tp, POOL_WIDTH), lambda b, i: (tile0 + b * nt + i, 0))]
    args = [pin]
    if not carried:
        in_specs.append(pl.BlockSpec((1, POOL_HALO, POOL_WIDTH), lambda b, i: (b, 0, 0)))
        args.append(halo)
    in_specs += [pl.BlockSpec((POOL_GROUPS, POOL_GROUP_DIM, POOL_GROUP_DIM), lambda b, i: (0, 0, 0)),
                 pl.BlockSpec((1, POOL_WIDTH), lambda b, i: (0, 0))]
    args += [w_grp, scale]
    scratch = [pltpu.VMEM((POOL_HALO + tp, POOL_WIDTH), F32)]
    if carried:
        scratch.append(pltpu.VMEM((POOL_HALO, POOL_WIDTH), F32))
    return pl.pallas_call(
        kern,
        grid=(batch, nt),
        in_specs=in_specs,
        out_specs=pl.BlockSpec((tp, POOL_WIDTH), lambda b, i: (b * nt + i, 0)),
        out_shape=jax.ShapeDtypeStruct((batch * seq, POOL_WIDTH), BF16),
        scratch_shapes=scratch,
        compiler_params=_cparams(2),
        name="pool_prompt" if carried else "pool_sample",
    )(*args)


def _merge_kernel(h_ref, oap_ref, oas_ref, opp_ref, ops_ref, wga_ref, wgp_ref, ba_ref, bp_ref, woa_ref,
                  wop_ref, m_ref, *, n_prompt_tiles):
    h = h_ref[...]
    is_prompt = pl.program_id(1) < n_prompt_tiles
    oa = jnp.where(is_prompt, oap_ref[...], oas_ref[...])
    op = jnp.where(is_prompt, opp_ref[...], ops_ref[...])
    ga = jax.nn.sigmoid(_dot(h, wga_ref[...]) + ba_ref[...])
    gp = jax.nn.sigmoid(_dot(h, wgp_ref[...]) + bp_ref[...])
    a = _dot(oa, woa_ref[...])
    p = _dot(op, wop_ref[...])
    m_ref[...] = (ga * a + gp * p).astype(BF16)


def _merge(h, oa_p, oa_s, op_p, op_s, w_gate, b_gate, w_oa, w_op):
    n = h.shape[0]
    tm = 2 * TOK_TILE
    tn = D_MODEL // 2
    nj = D_MODEL // tn
    npt = oa_p.shape[0] // tm
    prompt = lambda w: pl.BlockSpec((tm, w), lambda j, i: (jnp.minimum(i, npt - 1), 0))
    sample = lambda w: pl.BlockSpec((tm, w), lambda j, i: (jnp.maximum(i - npt, 0), 0))
    return pl.pallas_call(
        functools.partial(_merge_kernel, n_prompt_tiles=npt),
        grid=(nj, n // tm),
        in_specs=[pl.BlockSpec((tm, D_MODEL), lambda j, i: (i, 0)),
                  prompt(ATTN_WIDTH), sample(ATTN_WIDTH), prompt(POOL_WIDTH), sample(POOL_WIDTH),
                  pl.BlockSpec((D_MODEL, tn), lambda j, i: (0, j)),
                  pl.BlockSpec((D_MODEL, tn), lambda j, i: (0, nj + j)),
                  pl.BlockSpec((1, tn), lambda j, i: (0, j)),
                  pl.BlockSpec((1, tn), lambda j, i: (0, nj + j)),
                  pl.BlockSpec((ATTN_WIDTH, tn), lambda j, i: (0, j)),
                  pl.BlockSpec((POOL_WIDTH, tn), lambda j, i: (0, j))],
        out_specs=pl.BlockSpec((tm, tn), lambda j, i: (i, j)),
        out_shape=jax.ShapeDtypeStruct((n, D_MODEL), BF16),
        compiler_params=_cparams(2),
        name="merge",
    )(h, oa_p, oa_s, op_p, op_s, w_gate, w_gate, b_gate, b_gate, w_oa, w_op)


_R_E1, _R_E2, _R_W1, _R_W2, _R_RANK1, _R_RANK2 = range(6)
_R_LOGIT0 = MOE_GROUPS


def _outproj_kernel(m_ref, xp_ref, xs_ref, wo_ref, g2_ref, wr_ref, br_ref,
                    x1_ref, h2_ref, route_ref, route_t_ref, counts_ref, run, *, n_prompt_tiles):
    i = pl.program_id(0)
    tm = m_ref.shape[0]

    @pl.when(i == 0)
    def _():
        run[...] = jnp.zeros(run.shape, F32)

    x = jnp.where(i < n_prompt_tiles, xp_ref[...], xs_ref[...])
    x1 = x + _dot(m_ref[...], wo_ref[...])
    x1_ref[...] = x1
    r = lax.rsqrt(jnp.mean(x1 * x1, axis=-1, keepdims=True) + EPS)
    hf = (x1 * r) * g2_ref[...]
    h2_ref[...] = hf

    hi = hf.astype(BF16)
    lo = (hf - hi.astype(F32)).astype(BF16)
    hw = _dot(hi, wr_ref[...])
    lg = hw[:, :LANES] + (hw[:, LANES:] + _dot(lo, wr_ref[:, :LANES])) + br_ref[...]

    lane = lax.broadcasted_iota(I32, (tm, LANES), 1)
    neg_inf = jnp.float32(-jnp.inf)
    is_g = lane < MOE_GROUPS
    cl = jnp.where(is_g, lg, neg_inf)
    gmax = jnp.max(cl, axis=1, keepdims=True)
    g_sel = jnp.min(jnp.where(cl == gmax, lane, LANES), axis=1, keepdims=True)
    den = jnp.sum(jnp.where(is_g, jnp.exp(cl - gmax), 0.0), axis=1, keepdims=True)
    g_val = 1.0 / den
    e_lane = lane - _R_LOGIT0
    in_grp = (e_lane >= 0) & (e_lane < N_EXPERTS) & ((e_lane >> GROUP_SHIFT) == g_sel)
    f1 = jnp.where(in_grp, lg, neg_inf)
    v1 = jnp.max(f1, axis=1, keepdims=True)
    i1 = jnp.min(jnp.where(f1 == v1, lane, LANES), axis=1, keepdims=True)
    f2 = jnp.where(lane == i1, neg_inf, f1)
    v2 = jnp.max(f2, axis=1, keepdims=True)
    i2 = jnp.min(jnp.where(f2 == v2, lane, LANES), axis=1, keepdims=True)
    e2 = jnp.exp(v2 - v1)
    w1 = g_val / (1.0 + e2)
    w2 = g_val * e2 / (1.0 + e2)

    oh1 = lane == i1
    oh2 = lane == i2
    c = (oh1 | oh2).astype(BF16)
    rr = lax.broadcasted_iota(I32, (tm, tm), 0)
    cc = lax.broadcasted_iota(I32, (tm, tm), 1)
    before = (cc < rr).astype(BF16)
    prior = _dot(before, c) + run[...]
    rank1 = jnp.sum(jnp.where(oh1, prior, 0.0), axis=1, keepdims=True)
    rank2 = jnp.sum(jnp.where(oh2, prior, 0.0), axis=1, keepdims=True)
    run[...] = run[...] + jnp.sum(c.astype(F32), axis=0, keepdims=True)
    counts_ref[...] = jnp.broadcast_to(run[...], counts_ref.shape)

    rec = jnp.zeros((tm, LANES), F32)
    for ln, val in ((_R_E1, (i1 - _R_LOGIT0).astype(F32)), (_R_E2, (i2 - _R_LOGIT0).astype(F32)),
                    (_R_W1, w1), (_R_W2, w2), (_R_RANK1, rank1), (_R_RANK2, rank2)):
        rec = jnp.where(lane == ln, val, rec)
    route_ref[...] = rec
    route_t_ref[...] = rec.T[0:route_t_ref.shape[0], :]


def _out_proj(m, xp, xs, w_out, g2, wr_split, b_r):
    n = m.shape[0]
    tm = TOK_TILE
    n_prompt_tiles = xp.shape[0] // tm
    tok = lambda w: pl.BlockSpec((tm, w), lambda i: (i, 0))
    const = lambda s: pl.BlockSpec(s, lambda i: (0,) * len(s))
    return pl.pallas_call(
        functools.partial(_outproj_kernel, n_prompt_tiles=n_prompt_tiles),
        grid=(n // tm,),
        in_specs=[tok(D_MODEL), *_split_token_specs(tm, D_MODEL, n_prompt_tiles),
                  pl.BlockSpec((D_MODEL, D_MODEL), lambda i: (0, 0), pipeline_mode=pl.Buffered(1)),
                  const((1, D_MODEL)), const((D_MODEL, 2 * LANES)), const((1, LANES))],
        out_specs=(tok(D_MODEL), tok(D_MODEL), tok(LANES), pl.BlockSpec((8, tm), lambda i: (0, i)),
                   const((8, LANES))),
        out_shape=(jax.ShapeDtypeStruct((n, D_MODEL), F32),
                   jax.ShapeDtypeStruct((n, D_MODEL), F32),
                   jax.ShapeDtypeStruct((n, LANES), F32),
                   jax.ShapeDtypeStruct((8, n), F32),
                   jax.ShapeDtypeStruct((8, LANES), F32)),
        scratch_shapes=[pltpu.VMEM((1, LANES), F32)],
        compiler_params=_cparams(1),
        name="out_proj",
    )(m, xp, xs, w_out, g2, wr_split, b_r)


def _experts_kernel(te_ref, nu_ref, src0_ref, src1_ref, src2_ref, h2_ref, wg_ref, wu_ref, wd_ref, ys_ref,
                    xbuf, x16, sem, wg16, wu16, wd16):
    i = pl.program_id(0)
    last = pl.num_programs(0) - 1
    tm = x16.shape[0]
    n_used = nu_ref[0]
    used = i < n_used
    slot = i % 2

    def start_gather(rows_ref, s):
        for r in range(tm):
            pltpu.make_async_copy(h2_ref.at[pl.ds(rows_ref[r], 1)], xbuf.at[s, pl.ds(r, 1)], sem.at[s]).start()

    def wait_gather(s):
        pltpu.make_async_copy(h2_ref.at[pl.ds(0, tm)], xbuf.at[s], sem.at[s]).wait()

    @pl.when(i == 0)
    def _():
        start_gather(src0_ref, 0)
        start_gather(src1_ref, 1)

    @pl.when(i < n_used + 2)
    def _():
        wait_gather(slot)

    prev = te_ref[jnp.maximum(i - 1, 0)]
    fresh = (i == 0) | (te_ref[i] != prev)

    @pl.when(fresh & used)
    def _():
        wg16[...] = wg_ref[0].astype(BF16)
        wu16[...] = wu_ref[0].astype(BF16)
        wd16[...] = wd_ref[0].astype(BF16)

    @pl.when(used)
    def _():
        x16[...] = xbuf[slot].astype(BF16)
        start_gather(src2_ref, slot)
        x = x16[...]
        a = _dot(x, wg16[...])
        u = _dot(x, wu16[...])
        act = (a * jax.nn.sigmoid(a)) * u
        ys_ref[...] = _dot(act.astype(BF16), wd16[...])

    @pl.when(jnp.logical_not(used))
    def _():
        ys_ref[...] = jnp.zeros(ys_ref.shape, F32)

    @pl.when((i == last) & (last - 1 < n_used))
    def _():
        wait_gather((last - 1) % 2)

    @pl.when((i == last) & used)
    def _():
        wait_gather(slot)


def _experts(tile_expert, n_used, src_rows, h2, w_g, w_u, w_d):
    n_rows = src_rows.shape[0]
    tm = EXP_TILE
    n_tiles = n_rows // tm
    src_spec = lambda ahead: pl.BlockSpec((tm,), lambda i, te, nu: (jnp.minimum(i + ahead, n_tiles - 1),),
                                          memory_space=pltpu.SMEM)
    grid_spec = pltpu.PrefetchScalarGridSpec(
        num_scalar_prefetch=2,
        grid=(n_tiles,),
        in_specs=[src_spec(0), src_spec(1), src_spec(2),
                  pl.BlockSpec(memory_space=pl.ANY),
                  pl.BlockSpec((1, D_MODEL, D_EXPERT), lambda i, te, nu: (te[i], 0, 0)),
                  pl.BlockSpec((1, D_MODEL, D_EXPERT), lambda i, te, nu: (te[i], 0, 0)),
                  pl.BlockSpec((1, D_EXPERT, D_MODEL), lambda i, te, nu: (te[i], 0, 0))],
        out_specs=pl.BlockSpec((tm, D_MODEL), lambda i, te, nu: (i, 0)),
        scratch_shapes=[pltpu.VMEM((2, tm, D_MODEL), F32),
                        pltpu.VMEM((tm, D_MODEL), BF16),
                        pltpu.SemaphoreType.DMA((2,)),
                        pltpu.VMEM((D_MODEL, D_EXPERT), BF16),
                        pltpu.VMEM((D_MODEL, D_EXPERT), BF16),
                        pltpu.VMEM((D_EXPERT, D_MODEL), BF16)],
    )
    return pl.pallas_call(
        _experts_kernel,
        grid_spec=grid_spec,
        out_shape=jax.ShapeDtypeStruct((n_rows, D_MODEL), F32),
        compiler_params=_cparams(1),
        name="experts",
    )(tile_expert, n_used, src_rows, src_rows, src_rows, h2, w_g, w_u, w_d)


COMBINE_TOKENS = 256


def _combine_kernel(pos1_ref, pos2_ref, pos1_next_ref, pos2_next_ref, x1_ref, route_ref, gf_ref, ys_ref,
                    yp_ref, ysm_ref, ybuf, sem, *, n_prompt_tiles):
    tc = COMBINE_TOKENS
    i = pl.program_id(0)
    buf = i % 2

    def row_copy(p_refs, b, r, slot):
        return pltpu.make_async_copy(ys_ref.at[pl.ds(p_refs[slot][r], 1)],
                                     ybuf.at[b, slot, pl.ds(r, 1)], sem.at[b])

    def start_rows(p_refs, b):
        def issue(r, carry):
            row_copy(p_refs, b, r, 0).start()
            row_copy(p_refs, b, r, 1).start()
            return carry
        lax.fori_loop(0, tc, issue, 0, unroll=8)

    cur = (pos1_ref, pos2_ref)

    @pl.when(i == 0)
    def _():
        start_rows(cur, 0)

    @pl.when(i + 1 < pl.num_programs(0))
    def _():
        start_rows((pos1_next_ref, pos2_next_ref), 1 - buf)

    def drain(r, carry):
        row_copy(cur, buf, r, 0).wait()
        row_copy(cur, buf, r, 1).wait()
        return carry

    lax.fori_loop(0, tc, drain, 0, unroll=8)

    route = route_ref[...]
    w1 = route[:, _R_W1:_R_W1 + 1]
    w2 = route[:, _R_W2:_R_W2 + 1]
    x2 = x1_ref[...] + (w1 * ybuf[buf, 0] + w2 * ybuf[buf, 1])
    r = lax.rsqrt(jnp.mean(x2 * x2, axis=-1, keepdims=True) + EPS)
    y = (x2 * r) * gf_ref[...]
    is_prompt = pl.program_id(0) < n_prompt_tiles

    @pl.when(is_prompt)
    def _():
        yp_ref[...] = y

    @pl.when(jnp.logical_not(is_prompt))
    def _():
        ysm_ref[...] = y


def _combine(pos1, pos2, x1, route, gf, ys, n_prompt):
    n = x1.shape[0]
    tc = COMBINE_TOKENS
    n_prompt_tiles = n_prompt // tc
    n_steps = n // tc
    return pl.pallas_call(
        functools.partial(_combine_kernel, n_prompt_tiles=n_prompt_tiles),
        grid=(n_steps,),
        in_specs=[pl.BlockSpec((tc,), lambda i: (i,), memory_space=pltpu.SMEM),
                  pl.BlockSpec((tc,), lambda i: (i,), memory_space=pltpu.SMEM),
                  pl.BlockSpec((tc,), lambda i: (jnp.minimum(i + 1, n_steps - 1),), memory_space=pltpu.SMEM),
                  pl.BlockSpec((tc,), lambda i: (jnp.minimum(i + 1, n_steps - 1),), memory_space=pltpu.SMEM),
                  pl.BlockSpec((tc, D_MODEL), lambda i: (i, 0)),
                  pl.BlockSpec((tc, LANES), lambda i: (i, 0)),
                  pl.BlockSpec((1, D_MODEL), lambda i: (0, 0)),
                  pl.BlockSpec(memory_space=pl.ANY)],
        out_specs=_split_token_specs(tc, D_MODEL, n_prompt_tiles),
        out_shape=(jax.ShapeDtypeStruct((n_prompt, D_MODEL), F32),
                   jax.ShapeDtypeStruct((n - n_prompt, D_MODEL), F32)),
        scratch_shapes=[pltpu.VMEM((2, 2, tc, D_MODEL), F32), pltpu.SemaphoreType.DMA((2,))],
        compiler_params=_cparams(1),
        name="combine",
    )(pos1, pos2, pos1, pos2, x1, route, gf, ys)


def _split_w_in(w_in):
    w_main = w_in[:, :_C_KW].astype(BF16)
    w_kw = jnp.pad(w_in[:, _C_KW:_C_P].astype(BF16), ((0, 0), (0, LANES - (_C_P - _C_KW))))
    w_pool = w_in[:, _C_P:].astype(BF16)
    return w_main, w_kw, w_pool


def _forward(x_prompt, x_sample, cache_k, cache_v, cache_kidx, state_pool,
             norm1_g, w_in, w_gate, b_gate, w_pool_grp, pool_scale, w_o_attn, w_o_pool, w_out,
             norm2_g, w_rg, b_rg, w_re, b_re, w_eg, w_eu, w_ed, norm_f_g):
    batch, seq, d = x_prompt.shape
    dec_batch, dec_seq, _ = x_sample.shape
    past = cache_k.shape[1]
    n_p = batch * seq
    n_s = dec_batch * dec_seq
    n = n_p + n_s
    tm = TOK_TILE
    assert d == D_MODEL and seq % tm == 0 and n_s % tm == 0 and n % (2 * tm) == 0
    assert past % KEY_BLOCK == 0 and dec_seq >= POOL_STATE and dec_seq <= KEY_BLOCK

    xp = x_prompt.reshape(n_p, d)
    xs = x_sample.reshape(n_s, d)
    pos_tab = jnp.concatenate([jnp.arange(seq), past + jnp.arange(n_s) % dec_seq])
    tables = _rope_tables(pos_tab)
    h, q, k_p, k_s, v_p, v_s, qi, kw, pin = _in_proj(
        xp, xs, norm1_g.reshape(1, d), *_split_w_in(w_in), tables, seq // tm)

    oa_p = _dsa_prompt(q, qi, kw, k_p, v_p, batch, seq)
    oa_s = _dsa_sample(q, qi, kw, k_s, v_s,
                       cache_k.reshape(dec_batch * past, KV_WIDTH),
                       cache_v.reshape(dec_batch * past, KV_WIDTH),
                       cache_kidx.reshape(dec_batch * past, IDX_DIM),
                       n_p, dec_batch, dec_seq, past)

    w_grp16 = w_pool_grp.astype(BF16)
    scale = pool_scale.reshape(1, POOL_WIDTH)
    op_p = _pool(pin, w_grp16, scale, 0, batch, seq, 0)
    halo = jnp.concatenate([jnp.zeros((dec_batch, 1, POOL_WIDTH), F32), state_pool], axis=1)
    op_s = _pool(pin, w_grp16, scale, n_p // dec_seq, dec_batch, dec_seq, past, halo=halo)

    m = _merge(h, oa_p, oa_s, op_p, op_s, w_gate.astype(BF16), b_gate.reshape(1, 2 * d),
               w_o_attn.astype(BF16), w_o_pool.astype(BF16))

    w_r = jnp.concatenate([w_rg, w_re, jnp.zeros((d, LANES - MOE_GROUPS - N_EXPERTS), F32)], axis=1)
    b_r = jnp.concatenate([b_rg, b_re, jnp.zeros((LANES - MOE_GROUPS - N_EXPERTS,), F32)]).reshape(1, LANES)
    wr_hi = w_r.astype(BF16)
    wr_lo = (w_r - wr_hi.astype(F32)).astype(BF16)
    wr_split = jnp.concatenate([wr_hi, wr_lo], axis=1)
    x1, h2, route, route_t, counts = _out_proj(m, xp, xs, w_out.astype(BF16), norm2_g.reshape(1, d), wr_split, b_r)

    te = EXP_TILE
    n_tiles = (2 * n) // te + N_EXPERTS
    cnt = counts[0, _R_LOGIT0:_R_LOGIT0 + N_EXPERTS].astype(I32)
    tiles_per_e = (cnt + te - 1) // te
    tile_end = jnp.cumsum(tiles_per_e)
    row_start = (tile_end - tiles_per_e) * te
    n_used = tile_end[-1:].astype(I32)
    tile_ids = jnp.arange(n_tiles, dtype=I32)
    tile_expert = jnp.minimum(
        jnp.sum((tile_end[None, :] <= tile_ids[:, None]).astype(I32), axis=1), N_EXPERTS - 1)
    last_e = tile_expert[jnp.maximum(n_used[0] - 1, 0)]
    tile_expert = jnp.where(tile_ids < n_used[0], tile_expert, last_e)
    pos1 = row_start[route_t[_R_E1].astype(I32)] + route_t[_R_RANK1].astype(I32)
    pos2 = row_start[route_t[_R_E2].astype(I32)] + route_t[_R_RANK2].astype(I32)

    token = jnp.arange(n, dtype=I32)
    src_rows = jnp.zeros((n_tiles * te,), I32).at[jnp.concatenate([pos1, pos2])].set(
        jnp.concatenate([token, token]), unique_indices=True)
    ys = _experts(tile_expert, n_used, src_rows, h2, w_eg, w_eu, w_ed)
    y_p, y_s = _combine(pos1, pos2, x1, route, norm_f_g.reshape(1, d), ys, n_p)

    y_prompt = y_p.reshape(batch, seq, d)
    y_sample = y_s.reshape(dec_batch, dec_seq, d)
    k_p = k_p.reshape(1, batch, seq, N_KV_HEADS, HEAD_DIM)
    v_p = v_p.reshape(1, batch, seq, N_KV_HEADS, HEAD_DIM)
    ki_p = kw[:n_p, :IDX_DIM].reshape(1, batch, seq, IDX_DIM)
    pool_p = jnp.stack([pin[(b + 1) * seq - POOL_STATE:(b + 1) * seq] for b in range(batch)])[None]
    k_s = k_s.reshape(1, dec_batch, dec_seq, N_KV_HEADS, HEAD_DIM)
    v_s = v_s.reshape(1, dec_batch, dec_seq, N_KV_HEADS, HEAD_DIM)
    ki_s = kw[n_p:, :IDX_DIM].reshape(1, dec_batch, dec_seq, IDX_DIM)
    pool_s = pin[n_p:].reshape(dec_batch, dec_seq, POOL_WIDTH)[None, :, dec_seq - POOL_STATE:, :]
    return (y_prompt, y_sample, k_p, v_p, ki_p, pool_p, k_s, v_s, ki_s, pool_s)


def kernel(x_prompt, x_sample, cache_k, cache_v, cache_kidx, state_pool, norm1_g, w_in, w_gate, b_gate,
           w_pool_grp, pool_scale, w_o_attn, w_o_pool, w_out, norm2_g, w_router_group, b_router_group,
           w_router_expert, b_router_expert, w_exp_gate, w_exp_up, w_exp_down, norm_f_g):
    assert cache_k.shape[0] == 1, "single-layer model"
    return _forward(x_prompt, x_sample, cache_k[0], cache_v[0], cache_kidx[0], state_pool[0],
                    norm1_g[0], w_in[0], w_gate[0], b_gate[0], w_pool_grp[0], pool_scale[0],
                    w_o_attn[0], w_o_pool[0], w_out[0], norm2_g[0], w_router_group[0],
                    b_router_group[0], w_router_expert[0], b_router_expert[0],
                    w_exp_gate[0], w_exp_up[0], w_exp_down[0], norm_f_g)
```

```python
import functools

import numpy as np
import jax
import jax.numpy as jnp
from jax import lax
from jax.experimental import pallas as pl
from jax.experimental.pallas import tpu as pltpu

F32 = jnp.float32
BF16 = jnp.bfloat16
I32 = jnp.int32

D_MODEL = 2048
CHUNK = 64
N_HEADS = 8
N_KV_HEADS = 2
HEAD_DIM = 128
GQA_GROUP = N_HEADS // N_KV_HEADS
ATTN_WIDTH = N_HEADS * HEAD_DIM
KV_WIDTH = N_KV_HEADS * HEAD_DIM
IDX_HEADS = 16
IDX_DIM = 64
IDX_PAIRS = IDX_HEADS // 2
TOPK_MAX = 256
POOL_WINDOWS = (2, 4, 8, 16)
POOL_GROUPS = 4
POOL_WIDTH = D_MODEL // 2
POOL_GROUP_DIM = POOL_WIDTH // POOL_GROUPS
POOL_STATE = 15
POOL_HALO = 16
MOE_GROUPS = 4
EXPERTS_PER_GROUP = 8
N_EXPERTS = MOE_GROUPS * EXPERTS_PER_GROUP
D_EXPERT = 512
ROPE_THETA = 10000.0
CHUNK_SHIFT = CHUNK.bit_length() - 1
GROUP_SHIFT = EXPERTS_PER_GROUP.bit_length() - 1
EPS = 1e-6

LANES = 128
INT_MIN = -(2 ** 31)
NEG_BIG = -1e30
LOG2_E = 1.4426950408889634
VMEM_LIMIT = 56 * 1024 * 1024

KEY_BLOCK = 256
TOK_TILE = 256
EXP_TILE = 512


def _cparams(n_axes):
    return pltpu.CompilerParams(dimension_semantics=("arbitrary",) * n_axes,
                                vmem_limit_bytes=VMEM_LIMIT)


def _dot(a, b):
    return jnp.dot(a, b, preferred_element_type=F32)


def _dot_nt(a, b):
    return lax.dot_general(a, b, (((1,), (1,)), ((), ())), preferred_element_type=F32)


_C_Q = 0
_C_K = _C_Q + ATTN_WIDTH
_C_V = _C_K + KV_WIDTH
_C_QI = _C_V + KV_WIDTH
_C_KW = _C_QI + IDX_HEADS * IDX_DIM
_C_P = _C_KW + IDX_DIM + IDX_HEADS


def _rope128(y, cos, sin_signed):
    return y * cos + pltpu.roll(y, HEAD_DIM // 2, 1) * sin_signed


def _rope64(y, cos, sin_a, sin_b):
    half = IDX_DIM // 2
    return y * cos + pltpu.roll(y, LANES - half, 1) * sin_a + pltpu.roll(y, half, 1) * sin_b


def _inproj_kernel(xp_ref, xs_ref, g_ref, w_ref, wkw_ref, wp_ref, ck_ref, sk_ref, cq_ref, sq_ref, ci_ref,
                   sia_ref, sib_ref, cw_ref, swa_ref, swb_ref,
                   h_ref, q_ref, kp_ref, ks_ref, vp_ref, vs_ref, qi_ref, kw_ref, p_ref, *, n_prompt_tiles):
    is_prompt = pl.program_id(0) < n_prompt_tiles
    x = jnp.where(is_prompt, xp_ref[...], xs_ref[...])
    r = lax.rsqrt(jnp.mean(x * x, axis=-1, keepdims=True) + EPS)
    h = ((x * r) * g_ref[...]).astype(BF16)
    h_ref[...] = h
    cq, sq = cq_ref[...], sq_ref[...]
    for c in range(ATTN_WIDTH // 256):
        y = _dot(h, w_ref[:, _C_Q + c * 256:_C_Q + (c + 1) * 256])
        for s in range(2):
            q_ref[:, c * 256 + s * LANES:c * 256 + (s + 1) * LANES] = _rope128(
                y[:, s * LANES:(s + 1) * LANES], cq, sq).astype(BF16)
    ck, sk = ck_ref[...], sk_ref[...]
    y = _dot(h, w_ref[:, _C_K:_C_K + KV_WIDTH])
    k = jnp.concatenate([_rope128(y[:, s * LANES:(s + 1) * LANES], ck, sk) for s in range(N_KV_HEADS)], axis=1)
    v = _dot(h, w_ref[:, _C_V:_C_V + KV_WIDTH])

    @pl.when(is_prompt)
    def _():
        kp_ref[...] = k
        vp_ref[...] = v

    @pl.when(jnp.logical_not(is_prompt))
    def _():
        ks_ref[...] = k
        vs_ref[...] = v

    ci, sia, sib = ci_ref[...], sia_ref[...], sib_ref[...]
    for c in range(IDX_PAIRS // 2):
        y = _dot(h, w_ref[:, _C_QI + c * 256:_C_QI + (c + 1) * 256])
        for s in range(2):
            qi_ref[2 * c + s] = _rope64(y[:, s * LANES:(s + 1) * LANES], ci, sia, sib).astype(BF16)
    y = _dot(h, wkw_ref[...])
    kw_ref[...] = _rope64(y, cw_ref[...], swa_ref[...], swb_ref[...])
    for c in range(POOL_WIDTH // 256):
        p_ref[:, c * 256:(c + 1) * 256] = _dot(h, wp_ref[:, c * 256:(c + 1) * 256])


def _rope_tables(pos):
    pos = pos.astype(F32)[:, None]

    def cs(dim):
        half = dim // 2
        inv = ROPE_THETA ** (-jnp.arange(half, dtype=F32) / half)
        ang = pos * inv[None, :]
        return jnp.cos(ang), jnp.sin(ang)

    c, s = cs(HEAD_DIM)
    ck = jnp.concatenate([c, c], axis=1)
    sk = jnp.concatenate([-s, s], axis=1)
    qs = HEAD_DIM ** -0.5 * LOG2_E
    c, s = cs(IDX_DIM)
    z = jnp.zeros_like(s)
    ci = jnp.concatenate([c, c, c, c], axis=1)
    sia = jnp.concatenate([-s, z, -s, z], axis=1)
    sib = jnp.concatenate([z, s, z, s], axis=1)
    iscale = IDX_DIM ** -0.5
    n = pos.shape[0]
    wscale = jnp.full((n, IDX_HEADS), IDX_HEADS ** -0.5, F32)
    pad = jnp.zeros((n, LANES - IDX_DIM - IDX_HEADS), F32)
    zz = jnp.zeros((n, LANES - IDX_DIM), F32)
    cw = jnp.concatenate([c, c, wscale, pad], axis=1)
    swa = jnp.concatenate([-s, z, zz], axis=1)
    swb = jnp.concatenate([z, s, zz], axis=1)
    return (ck, sk, ck * qs, sk * qs, ci * iscale, sia * iscale, sib * iscale, cw, swa, swb)


def _split_token_specs(tm, width, n_prompt_tiles):
    prompt = pl.BlockSpec((tm, width), lambda i: (jnp.minimum(i, n_prompt_tiles - 1), 0))
    sample = pl.BlockSpec((tm, width), lambda i: (jnp.maximum(i - n_prompt_tiles, 0), 0))
    return prompt, sample


def _in_proj(xp, xs, g1, w_main, w_kw, w_pool, tables, tiles_per_seq):
    tm = TOK_TILE
    n_prompt_tiles = xp.shape[0] // tm
    n = xp.shape[0] + xs.shape[0]
    n_tiles = n // tm

    def tab_idx(i):
        return (jnp.where(i < n_prompt_tiles, i % tiles_per_seq, tiles_per_seq + i - n_prompt_tiles), 0)

    tok = lambda w: pl.BlockSpec((tm, w), lambda i: (i, 0))
    tab = pl.BlockSpec((tm, LANES), tab_idx)
    resident = lambda a: pl.BlockSpec(a.shape, lambda i: (0, 0), pipeline_mode=pl.Buffered(1))
    n_s = xs.shape[0]
    kv_p, kv_s = _split_token_specs(tm, KV_WIDTH, n_prompt_tiles)
    out_shape = (
        jax.ShapeDtypeStruct((n, D_MODEL), BF16),
        jax.ShapeDtypeStruct((n, ATTN_WIDTH), BF16),
        jax.ShapeDtypeStruct((n - n_s, KV_WIDTH), F32),
        jax.ShapeDtypeStruct((n_s, KV_WIDTH), F32),
        jax.ShapeDtypeStruct((n - n_s, KV_WIDTH), F32),
        jax.ShapeDtypeStruct((n_s, KV_WIDTH), F32),
        jax.ShapeDtypeStruct((IDX_PAIRS, n, LANES), BF16),
        jax.ShapeDtypeStruct((n, LANES), F32),
        jax.ShapeDtypeStruct((n, POOL_WIDTH), F32),
    )
    return pl.pallas_call(
        functools.partial(_inproj_kernel, n_prompt_tiles=n_prompt_tiles),
        grid=(n_tiles,),
        in_specs=[*_split_token_specs(tm, D_MODEL, n_prompt_tiles),
                  pl.BlockSpec((1, D_MODEL), lambda i: (0, 0)),
                  resident(w_main), resident(w_kw), resident(w_pool)]
                 + [tab] * 10,
        out_specs=(tok(D_MODEL), tok(ATTN_WIDTH), kv_p, kv_s, kv_p, kv_s,
                   pl.BlockSpec((IDX_PAIRS, tm, LANES), lambda i: (0, i, 0)),
                   tok(LANES), tok(POOL_WIDTH)),
        out_shape=out_shape,
        compiler_params=_cparams(1),
        name="in_proj",
    )(xp, xs, g1, w_main, w_kw, w_pool, *tables)


def _ordinal_to_f32(k):
    return lax.bitcast_convert_type(k ^ ((k >> 31) & 0x7FFFFFFF), F32)


def _kth_largest_score(topk, shape, n_key_rows, count_where):
    def count_ge(cand):
        cand_f = _ordinal_to_f32(cand)
        return count_where(lambda keys, kb: keys >= jnp.broadcast_to(cand_f, keys.shape))

    c0 = count_ge(jnp.zeros(shape, I32))
    thr0 = jnp.where(c0 >= topk, 0, INT_MIN).astype(I32)
    cnt0 = jnp.where(c0 >= topk, c0, n_key_rows).astype(I32)

    def bit_step(i, carry):
        thr, cnt = carry
        cand = thr + (jnp.int32(1) << (30 - i))
        c = count_ge(cand)
        ok = c >= topk
        return jnp.where(ok, cand, thr), jnp.where(ok, c, cnt)

    thr, cnt = lax.fori_loop(0, 31, bit_step, (thr0, cnt0))
    has_k = thr > INT_MIN
    thr_f = jnp.where(has_k, _ordinal_to_f32(thr), jnp.finfo(F32).min)
    return thr_f, has_k & (cnt > topk)


def _drop_excess_ties(sc, n_blocks, topk, thr, need_tie, key_index, count_where):
    n_gt = count_where(lambda keys, kb: keys > jnp.broadcast_to(thr, keys.shape))
    n_keep = topk - n_gt

    def idx_step(i, bound):
        cand = bound + (jnp.int32(1) << (14 - i))
        c = count_where(lambda keys, kb: (keys == jnp.broadcast_to(thr, keys.shape))
                        & (key_index(kb) < jnp.broadcast_to(cand, keys.shape)))
        return jnp.where(c <= n_keep, cand, bound)

    bound = lax.fori_loop(0, 15, idx_step, jnp.zeros(thr.shape, I32))
    bound = jnp.where(need_tie, bound, jnp.int32(2 ** 30))

    def drop_block(kb, carry):
        keys = sc[kb]
        drop = (keys == jnp.broadcast_to(thr, keys.shape)) & (key_index(kb) >= jnp.broadcast_to(bound, keys.shape))
        sc[kb] = jnp.where(drop, -jnp.inf, keys)
        return carry

    lax.fori_loop(0, n_blocks, drop_block, 0)


def _stack_queries_t(q_ref, qt):
    for g in range(N_KV_HEADS):
        qg = jnp.concatenate([q_ref[:, (g * GQA_GROUP + j) * HEAD_DIM:(g * GQA_GROUP + j + 1) * HEAD_DIM]
                              for j in range(GQA_GROUP)], axis=0)
        qt[g] = qg.astype(F32).T.astype(BF16)


def _attention_scratch(rows):
    return [
        pltpu.VMEM((N_KV_HEADS, 1, rows), F32),
        pltpu.VMEM((N_KV_HEADS, 1, rows), F32),
        pltpu.VMEM((N_KV_HEADS, HEAD_DIM, rows), F32),
        pltpu.VMEM((2, N_KV_HEADS, KEY_BLOCK, rows), F32),
        pltpu.VMEM((2, N_KV_HEADS, KEY_BLOCK, rows), BF16),
        pltpu.VMEM((2, N_KV_HEADS, 1, rows), F32),
    ]


def _masked_attention(o_ref, n_blocks, n_blocks_all, bias_t, k16, vt16, qt, att, tq):
    m_s, l_s, acc_s, lg_s, p_s, alpha_s = att
    m_s[...] = jnp.full(m_s.shape, NEG_BIG, F32)
    l_s[...] = jnp.zeros(l_s.shape, F32)
    acc_s[...] = jnp.zeros(acc_s.shape, F32)
    p_s[1] = jnp.zeros(p_s.shape[1:], BF16)
    alpha_s[1] = jnp.ones(alpha_s.shape[1:], F32)

    def logits_of(kb, slot):
        r0 = pl.multiple_of(kb * KEY_BLOCK, KEY_BLOCK)
        for g in range(N_KV_HEADS):
            lg_s[slot, g] = _dot(k16[pl.ds(r0, KEY_BLOCK), g * HEAD_DIM:(g + 1) * HEAD_DIM], qt[g])

    def value_product(kb, slot):
        for g in range(N_KV_HEADS):
            acc_s[g] = (alpha_s[slot, g] * acc_s[g]
                        + _dot(vt16[kb, g * HEAD_DIM:(g + 1) * HEAD_DIM, :], p_s[slot, g]))

    logits_of(0, 0)
    last = n_blocks_all - 1

    def step(kb, slot):
        logits_of(jnp.minimum(kb + 1, last), 1 - slot)
        bias = jnp.where(kb < n_blocks, bias_t(jnp.minimum(kb, last)), NEG_BIG)
        for g in range(N_KV_HEADS):
            logits = lg_s[slot, g] + bias
            m_prev = m_s[g]
            m_new = jnp.maximum(m_prev, jnp.max(logits, axis=0, keepdims=True))
            p = jnp.exp2(logits - m_new)
            alpha = jnp.exp2(m_prev - m_new)
            l_s[g] = alpha * l_s[g] + jnp.sum(p, axis=0, keepdims=True)
            p_s[slot, g] = p.astype(BF16)
            alpha_s[slot, g] = alpha
            m_s[g] = m_new
        value_product(jnp.clip(kb - 1, 0, last), 1 - slot)

    def two_steps(i, carry):
        step(2 * i, 0)
        step(2 * i + 1, 1)
        return carry

    n_pairs = (n_blocks + 1) // 2
    lax.fori_loop(0, n_pairs, two_steps, 0)
    value_product(jnp.minimum(2 * n_pairs - 1, last), 1)

    for g in range(N_KV_HEADS):
        o = (acc_s[g] * (1.0 / l_s[g])).T
        for j in range(GQA_GROUP):
            hd = g * GQA_GROUP + j
            o_ref[:, hd * HEAD_DIM:(hd + 1) * HEAD_DIM] = o[j * tq:(j + 1) * tq, :].astype(BF16)


def _dsa_prompt_kernel(q_ref, qi_ref, wq_ref, kn_ref, vn_ref, kwn_ref, o_ref,
                       k16, vt16, kia, kib, sc, raw, qt, *att, tq, seq, topk):
    qb = pl.program_id(1)
    n_blocks_all = seq // KEY_BLOCK

    @pl.when(qb == 0)
    def _():
        lane = lax.broadcasted_iota(I32, (seq, LANES), 1)
        kin = jnp.where(lane < IDX_DIM, kwn_ref[...], 0.0)
        k16[...] = kn_ref[...].astype(BF16)
        kia[...] = kin.astype(BF16)
        kib[...] = pltpu.roll(kin, IDX_DIM, 1).astype(BF16)
        for kb in range(n_blocks_all):
            vt16[kb] = vn_ref[kb * KEY_BLOCK:(kb + 1) * KEY_BLOCK, :].T.astype(BF16)

    n_blocks = qb // (KEY_BLOCK // tq) + 1

    qit = qi_ref[...].reshape(IDX_PAIRS * tq, LANES).astype(F32).T.astype(BF16)
    wt = wq_ref[...].T
    w_rows = [wt[IDX_DIM + j:IDX_DIM + j + 1, :] for j in range(IDX_HEADS)]
    _stack_queries_t(q_ref, qt)

    key_row = lax.broadcasted_iota(I32, (KEY_BLOCK, tq), 0)
    q_pos = qb * tq + lax.broadcasted_iota(I32, (KEY_BLOCK, tq), 1)

    last = n_blocks_all - 1
    n_pairs = (n_blocks + 1) // 2

    def raw_scores(kb, slot):
        r0 = pl.multiple_of(kb * KEY_BLOCK, KEY_BLOCK)
        raw[slot, 0] = _dot(kia[pl.ds(r0, KEY_BLOCK), :], qit)
        raw[slot, 1] = _dot(kib[pl.ds(r0, KEY_BLOCK), :], qit)

    def score_step(kb, slot):
        raw_scores(jnp.minimum(kb + 1, last), 1 - slot)
        se = raw[slot, 0]
        so = raw[slot, 1]
        score = jnp.zeros((KEY_BLOCK, tq), F32)
        for p in range(IDX_PAIRS):
            score = (score + jnp.maximum(se[:, p * tq:(p + 1) * tq], 0.0) * w_rows[2 * p]
                     + jnp.maximum(so[:, p * tq:(p + 1) * tq], 0.0) * w_rows[2 * p + 1])
        adm = ((kb * KEY_BLOCK + key_row) >> CHUNK_SHIFT) <= (q_pos >> CHUNK_SHIFT)
        sc[jnp.where(kb < n_blocks, kb, n_blocks_all)] = jnp.where(adm, score, -jnp.inf)

    def two_score_steps(i, carry):
        score_step(2 * i, 0)
        score_step(2 * i + 1, 1)
        return carry

    raw_scores(0, 0)
    lax.fori_loop(0, n_pairs, two_score_steps, 0)

    def count_where(pred):
        def body(i, acc):
            for s in range(2):
                kb = 2 * i + s
                kbc = jnp.minimum(kb, last)
                m = pred(sc[kbc], kbc).astype(I32)
                part = jnp.sum(m.reshape(KEY_BLOCK // 8, 8, tq), axis=0)
                acc = acc + jnp.where(kb < n_blocks, part, 0)
            return acc
        acc = lax.fori_loop(0, n_pairs, body, jnp.zeros((8, tq), I32))
        return jnp.sum(acc, axis=0, keepdims=True)

    thr, need_tie = _kth_largest_score(topk, (1, tq), seq, count_where)

    @pl.when(jnp.max(need_tie.astype(I32)) > 0)
    def _():
        _drop_excess_ties(sc, n_blocks, topk, thr, need_tie,
                          lambda kb: kb * KEY_BLOCK + key_row, count_where)

    thr_b = jnp.broadcast_to(thr, (KEY_BLOCK, tq))

    def bias_t(kb):
        bias = jnp.where(sc[kb] >= thr_b, 0.0, NEG_BIG)
        return jnp.concatenate([bias] * GQA_GROUP, axis=1)

    _masked_attention(o_ref, n_blocks, n_blocks_all, bias_t, k16, vt16, qt, att, tq)


def _dsa_sample_kernel(q_ref, qi_ref, wq_ref, kn_ref, vn_ref, kwn_ref, kc_ref, vc_ref, kic_ref, o_ref,
                       k16, vt16, kia, kib, wbe, wbo, sc, qt, *att,
                       tq, n_key_rows, n_cache, n_new, topk):
    n_blocks = n_key_rows // KEY_BLOCK
    n_cache_blocks = n_cache // KEY_BLOCK

    lane = lax.broadcasted_iota(I32, (n_new, LANES), 1)
    kin = jnp.where(lane < IDX_DIM, kwn_ref[...], 0.0).astype(BF16)
    n_tail = n_key_rows - n_cache
    k16[0:n_cache, :] = kc_ref[...].astype(BF16)
    for kb in range(n_cache_blocks):
        vt16[kb] = vc_ref[kb * KEY_BLOCK:(kb + 1) * KEY_BLOCK, :].T.astype(BF16)
    v_tail = jnp.concatenate([vn_ref[...], jnp.zeros((n_tail - n_new, KV_WIDTH), F32)], axis=0)
    vt16[n_cache_blocks] = v_tail.T.astype(BF16)
    kic = jnp.concatenate([kic_ref[...], jnp.zeros((n_cache, LANES - IDX_DIM), F32)], axis=1)
    kia[0:n_cache, :] = kic.astype(BF16)
    kib[0:n_cache, :] = pltpu.roll(kic, IDX_DIM, 1).astype(BF16)
    k16[n_cache:n_key_rows, :] = jnp.zeros((n_tail, KV_WIDTH), BF16)
    kia[n_cache:n_key_rows, :] = jnp.zeros((n_tail, LANES), BF16)
    kib[n_cache:n_key_rows, :] = jnp.zeros((n_tail, LANES), BF16)
    k16[n_cache:n_cache + n_new, :] = kn_ref[...].astype(BF16)
    kia[n_cache:n_cache + n_new, :] = kin
    kib[n_cache:n_cache + n_new, :] = pltpu.roll(kin.astype(F32), IDX_DIM, 1).astype(BF16)

    wq = wq_ref[...]
    for j in range(IDX_HEADS):
        col = jnp.broadcast_to(wq[:, IDX_DIM + j:IDX_DIM + j + 1], (tq, LANES))
        if j % 2 == 0:
            wbe[j // 2] = col
        else:
            wbo[j // 2] = col
    _stack_queries_t(q_ref, qt)

    lane = lax.broadcasted_iota(I32, (tq, KEY_BLOCK), 1)

    qi2 = qi_ref[...].reshape(IDX_PAIRS * tq, LANES)

    def score_block(kb, carry):
        r0 = pl.multiple_of(kb * KEY_BLOCK, KEY_BLOCK)
        se = _dot_nt(qi2, kia[pl.ds(r0, KEY_BLOCK), :]).reshape(IDX_PAIRS, tq, KEY_BLOCK)
        so = _dot_nt(qi2, kib[pl.ds(r0, KEY_BLOCK), :]).reshape(IDX_PAIRS, tq, KEY_BLOCK)
        score = jnp.zeros((tq, KEY_BLOCK), F32)
        for p in range(IDX_PAIRS):
            we = wbe[p]
            wo = wbo[p]
            we2 = jnp.concatenate([we, we], axis=1)
            wo2 = jnp.concatenate([wo, wo], axis=1)
            score = score + jnp.maximum(se[p], 0.0) * we2 + jnp.maximum(so[p], 0.0) * wo2
        adm = (kb * KEY_BLOCK + lane) < (n_cache + n_new)
        sc[kb] = jnp.where(adm, score, -jnp.inf)
        return carry

    lax.fori_loop(0, n_blocks, score_block, 0)

    def count_where(pred):
        def body(kb, acc):
            m = pred(sc[kb], kb).astype(I32)
            return acc + m[:, :LANES] + m[:, LANES:]
        acc = lax.fori_loop(0, n_blocks, body, jnp.zeros((tq, LANES), I32))
        return jnp.sum(acc, axis=1, keepdims=True)

    thr, need_tie = _kth_largest_score(topk, (tq, 1), n_key_rows, count_where)

    @pl.when(jnp.max(need_tie.astype(I32)) > 0)
    def _():
        _drop_excess_ties(sc, n_blocks, topk, thr, need_tie,
                          lambda kb: kb * KEY_BLOCK + lane, count_where)

    thr_b = jnp.broadcast_to(thr, (tq, KEY_BLOCK))

    def bias_t(kb):
        bias = jnp.where(sc[kb] >= thr_b, 0.0, NEG_BIG)
        return jnp.concatenate([bias] * GQA_GROUP, axis=0).T

    _masked_attention(o_ref, n_blocks, n_blocks, bias_t, k16, vt16, qt, att, tq)


def _dsa_sample_scratch(tq, n_key_rows):
    rows = GQA_GROUP * tq
    return [
        pltpu.VMEM((n_key_rows, KV_WIDTH), BF16),
        pltpu.VMEM((n_key_rows // KEY_BLOCK, KV_WIDTH, KEY_BLOCK), BF16),
        pltpu.VMEM((n_key_rows, LANES), BF16),
        pltpu.VMEM((n_key_rows, LANES), BF16),
        pltpu.VMEM((IDX_PAIRS, tq, LANES), F32),
        pltpu.VMEM((IDX_PAIRS, tq, LANES), F32),
        pltpu.VMEM((n_key_rows // KEY_BLOCK, tq, KEY_BLOCK), F32),
        pltpu.VMEM((N_KV_HEADS, HEAD_DIM, rows), BF16),
    ] + _attention_scratch(rows)


def _dsa_prompt(q, qi, kw, k, v, batch, seq):
    tq = 128
    nqb = seq // tq
    topk = min(TOPK_MAX, seq // 4)
    kern = functools.partial(_dsa_prompt_kernel, tq=tq, seq=seq, topk=topk)
    rows_q = GQA_GROUP * tq
    scratch = [
        pltpu.VMEM((seq, KV_WIDTH), BF16),
        pltpu.VMEM((seq // KEY_BLOCK, KV_WIDTH, KEY_BLOCK), BF16),
        pltpu.VMEM((seq, LANES), BF16),
        pltpu.VMEM((seq, LANES), BF16),
        pltpu.VMEM((seq // KEY_BLOCK + 1, KEY_BLOCK, tq), F32),
        pltpu.VMEM((2, 2, KEY_BLOCK, IDX_PAIRS * tq), F32),
        pltpu.VMEM((N_KV_HEADS, HEAD_DIM, rows_q), BF16),
    ] + _attention_scratch(rows_q)
    qmap = lambda b, i: (b * nqb + i, 0)
    smap = lambda b, i: (b, 0)
    return pl.pallas_call(
        kern,
        grid=(batch, nqb),
        in_specs=[pl.BlockSpec((tq, ATTN_WIDTH), qmap),
                  pl.BlockSpec((IDX_PAIRS, tq, LANES), lambda b, i: (0, b * nqb + i, 0)),
                  pl.BlockSpec((tq, LANES), qmap),
                  pl.BlockSpec((seq, KV_WIDTH), smap),
                  pl.BlockSpec((seq, KV_WIDTH), smap),
                  pl.BlockSpec((seq, LANES), smap)],
        out_specs=pl.BlockSpec((tq, ATTN_WIDTH), qmap),
        out_shape=jax.ShapeDtypeStruct((batch * seq, ATTN_WIDTH), BF16),
        scratch_shapes=scratch,
        compiler_params=_cparams(2),
        name="dsa_prompt",
    )(q, qi, kw, k, v, kw)


def _dsa_sample(q, qi, kw, k, v, cache_k, cache_v, cache_ki, row0, batch, seq, past):
    tq = seq
    n_key_rows = past + KEY_BLOCK
    topk = min(TOPK_MAX, (past + seq) // 4)
    kern = functools.partial(_dsa_sample_kernel, tq=tq, n_key_rows=n_key_rows, n_cache=past, n_new=seq,
                             topk=topk)
    t0 = row0 // seq
    qmap = lambda b, i: (t0 + b, 0)
    cmap = lambda b, i: (b, 0)
    return pl.pallas_call(
        kern,
        grid=(batch, 1),
        in_specs=[pl.BlockSpec((tq, ATTN_WIDTH), qmap),
                  pl.BlockSpec((IDX_PAIRS, tq, LANES), lambda b, i: (0, t0 + b, 0)),
                  pl.BlockSpec((tq, LANES), qmap),
                  pl.BlockSpec((tq, KV_WIDTH), cmap),
                  pl.BlockSpec((tq, KV_WIDTH), cmap),
                  pl.BlockSpec((tq, LANES), qmap),
                  pl.BlockSpec((past, KV_WIDTH), cmap),
                  pl.BlockSpec((past, KV_WIDTH), cmap),
                  pl.BlockSpec((past, IDX_DIM), cmap)],
        out_specs=pl.BlockSpec((tq, ATTN_WIDTH), cmap),
        out_shape=jax.ShapeDtypeStruct((batch * seq, ATTN_WIDTH), BF16),
        scratch_shapes=_dsa_sample_scratch(tq, n_key_rows),
        compiler_params=_cparams(2),
        name="dsa_sample",
    )(q, qi, kw, k, v, kw, cache_k, cache_v, cache_ki)


def _pool_kernel(*refs, tp, pos0, carried):
    if carried:
        p_ref, w_ref, s_ref, o_ref, ext, save = refs
    else:
        p_ref, halo_ref, w_ref, s_ref, o_ref, ext = refs
    i = pl.program_id(1)
    if carried:
        @pl.when(i == 0)
        def _():
            ext[0:POOL_HALO, :] = jnp.zeros((POOL_HALO, POOL_WIDTH), F32)

        @pl.when(i > 0)
        def _():
            ext[0:POOL_HALO, :] = save[...]
    else:
        ext[0:POOL_HALO, :] = halo_ref[0]
    x = p_ref[...]
    ext[POOL_HALO:POOL_HALO + tp, :] = x
    if carried:
        save[...] = x[tp - POOL_HALO:tp, :]
    pos = pos0 + i * tp + lax.broadcasted_iota(I32, (tp, 1), 0)
    for g, w in enumerate(POOL_WINDOWS):
        c0, c1 = g * POOL_GROUP_DIM, (g + 1) * POOL_GROUP_DIM
        xg = x[:, c0:c1]
        tot = xg
        for s in range(1, w):
            tot = tot + ext[POOL_HALO - s:POOL_HALO - s + tp, c0:c1]
        inv = 1.0 / jnp.minimum(pos + 1, w).astype(F32)
        d = (tot * inv - xg).astype(BF16)
        o_ref[:, c0:c1] = (_dot(d, w_ref[g]) * s_ref[:, c0:c1]).astype(BF16)


def _pool(pin, w_grp, scale, tile0, batch, seq, pos0, halo=None):
    carried = halo is None
    tp = min(seq, 256)
    nt = seq // tp
    kern = functools.partial(_pool_kernel, tp=tp, pos0=pos0, carried=carried)
    in_specs = [pl.BlockSpec((tp, POOL_WIDTH), lambda b, i: (tile0 + b * nt + i, 0))]
    args = [pin]
    if not carried:
        in_specs.append(pl.BlockSpec((1, POOL_HALO, POOL_WIDTH), lambda b, i: (b, 0, 0)))
        args.append(halo)
    in_specs += [pl.BlockSpec((POOL_GROUPS, POOL_GROUP_DIM, POOL_GROUP_DIM), lambda b, i: (0, 0, 0)),
                 pl.BlockSpec((1, POOL_WIDTH), lambda b, i: (0, 0))]
    args += [w_grp, scale]
    scratch = [pltpu.VMEM((POOL_HALO + tp, POOL_WIDTH), F32)]
    if carried:
        scratch.append(pltpu.VMEM((POOL_HALO, POOL_WIDTH), F32))
    return pl.pallas_call(
        kern,
        grid=(batch, nt),
        in_specs=in_specs,
        out_specs=pl.BlockSpec((tp, POOL_WIDTH), lambda b, i: (b * nt + i, 0)),
        out_shape=jax.ShapeDtypeStruct((batch * seq, POOL_WIDTH), BF16),
        scratch_shapes=scratch,
        compiler_params=_cparams(2),
        name="pool_prompt" if carried else "pool_sample",
    )(*args)


def _merge_kernel(h_ref, oap_ref, oas_ref, opp_ref, ops_ref, wga_ref, wgp_ref, ba_ref, bp_ref, woa_ref,
                  wop_ref, m_ref, *, n_prompt_tiles):
    h = h_ref[...]
    is_prompt = pl.program_id(1) < n_prompt_tiles
    oa = jnp.where(is_prompt, oap_ref[...], oas_ref[...])
    op = jnp.where(is_prompt, opp_ref[...], ops_ref[...])
    ga = jax.nn.sigmoid(_dot(h, wga_ref[...]) + ba_ref[...])
    gp = jax.nn.sigmoid(_dot(h, wgp_ref[...]) + bp_ref[...])
    a = _dot(oa, woa_ref[...])
    p = _dot(op, wop_ref[...])
    m_ref[...] = (ga * a + gp * p).astype(BF16)


def _merge(h, oa_p, oa_s, op_p, op_s, w_gate, b_gate, w_oa, w_op):
    n = h.shape[0]
    tm = 2 * TOK_TILE
    tn = D_MODEL // 2
    nj = D_MODEL // tn
    npt = oa_p.shape[0] // tm
    prompt = lambda w: pl.BlockSpec((tm, w), lambda j, i: (jnp.minimum(i, npt - 1), 0))
    sample = lambda w: pl.BlockSpec((tm, w), lambda j, i: (jnp.maximum(i - npt, 0), 0))
    return pl.pallas_call(
        functools.partial(_merge_kernel, n_prompt_tiles=npt),
        grid=(nj, n // tm),
        in_specs=[pl.BlockSpec((tm, D_MODEL), lambda j, i: (i, 0)),
                  prompt(ATTN_WIDTH), sample(ATTN_WIDTH), prompt(POOL_WIDTH), sample(POOL_WIDTH),
                  pl.BlockSpec((D_MODEL, tn), lambda j, i: (0, j)),
                  pl.BlockSpec((D_MODEL, tn), lambda j, i: (0, nj + j)),
                  pl.BlockSpec((1, tn), lambda j, i: (0, j)),
                  pl.BlockSpec((1, tn), lambda j, i: (0, nj + j)),
                  pl.BlockSpec((ATTN_WIDTH, tn), lambda j, i: (0, j)),
                  pl.BlockSpec((POOL_WIDTH, tn), lambda j, i: (0, j))],
        out_specs=pl.BlockSpec((tm, tn), lambda j, i: (i, j)),
        out_shape=jax.ShapeDtypeStruct((n, D_MODEL), BF16),
        compiler_params=_cparams(2),
        name="merge",
    )(h, oa_p, oa_s, op_p, op_s, w_gate, w_gate, b_gate, b_gate, w_oa, w_op)


_R_E1, _R_E2, _R_W1, _R_W2, _R_RANK1, _R_RANK2 = range(6)
_R_LOGIT0 = MOE_GROUPS


def _outproj_kernel(m_ref, xp_ref, xs_ref, wo_ref, g2_ref, wr_ref, br_ref,
                    x1_ref, h2_ref, route_ref, route_t_ref, counts_ref, run, *, n_prompt_tiles):
    i = pl.program_id(0)
    tm = m_ref.shape[0]

    @pl.when(i == 0)
    def _():
        run[...] = jnp.zeros(run.shape, F32)

    x = jnp.where(i < n_prompt_tiles, xp_ref[...], xs_ref[...])
    x1 = x + _dot(m_ref[...], wo_ref[...])
    x1_ref[...] = x1
    r = lax.rsqrt(jnp.mean(x1 * x1, axis=-1, keepdims=True) + EPS)
    hf = (x1 * r) * g2_ref[...]
    h2_ref[...] = hf

    hi = hf.astype(BF16)
    lo = (hf - hi.astype(F32)).astype(BF16)
    hw = _dot(hi, wr_ref[...])
    lg = hw[:, :LANES] + (hw[:, LANES:] + _dot(lo, wr_ref[:, :LANES])) + br_ref[...]

    lane = lax.broadcasted_iota(I32, (tm, LANES), 1)
    neg_inf = jnp.float32(-jnp.inf)
    is_g = lane < MOE_GROUPS
    cl = jnp.where(is_g, lg, neg_inf)
    gmax = jnp.max(cl, axis=1, keepdims=True)
    g_sel = jnp.min(jnp.where(cl == gmax, lane, LANES), axis=1, keepdims=True)
    den = jnp.sum(jnp.where(is_g, jnp.exp(cl - gmax), 0.0), axis=1, keepdims=True)
    g_val = 1.0 / den
    e_lane = lane - _R_LOGIT0
    in_grp = (e_lane >= 0) & (e_lane < N_EXPERTS) & ((e_lane >> GROUP_SHIFT) == g_sel)
    f1 = jnp.where(in_grp, lg, neg_inf)
    v1 = jnp.max(f1, axis=1, keepdims=True)
    i1 = jnp.min(jnp.where(f1 == v1, lane, LANES), axis=1, keepdims=True)
    f2 = jnp.where(lane == i1, neg_inf, f1)
    v2 = jnp.max(f2, axis=1, keepdims=True)
    i2 = jnp.min(jnp.where(f2 == v2, lane, LANES), axis=1, keepdims=True)
    e2 = jnp.exp(v2 - v1)
    w1 = g_val / (1.0 + e2)
    w2 = g_val * e2 / (1.0 + e2)

    oh1 = lane == i1
    oh2 = lane == i2
    c = (oh1 | oh2).astype(BF16)
    rr = lax.broadcasted_iota(I32, (tm, tm), 0)
    cc = lax.broadcasted_iota(I32, (tm, tm), 1)
    before = (cc < rr).astype(BF16)
    prior = _dot(before, c) + run[...]
    rank1 = jnp.sum(jnp.where(oh1, prior, 0.0), axis=1, keepdims=True)
    rank2 = jnp.sum(jnp.where(oh2, prior, 0.0), axis=1, keepdims=True)
    run[...] = run[...] + jnp.sum(c.astype(F32), axis=0, keepdims=True)
    counts_ref[...] = jnp.broadcast_to(run[...], counts_ref.shape)

    rec = jnp.zeros((tm, LANES), F32)
    for ln, val in ((_R_E1, (i1 - _R_LOGIT0).astype(F32)), (_R_E2, (i2 - _R_LOGIT0).astype(F32)),
                    (_R_W1, w1), (_R_W2, w2), (_R_RANK1, rank1), (_R_RANK2, rank2)):
        rec = jnp.where(lane == ln, val, rec)
    route_ref[...] = rec
    route_t_ref[...] = rec.T[0:route_t_ref.shape[0], :]


def _out_proj(m, xp, xs, w_out, g2, wr_split, b_r):
    n = m.shape[0]
    tm = TOK_TILE
    n_prompt_tiles = xp.shape[0] // tm
    tok = lambda w: pl.BlockSpec((tm, w), lambda i: (i, 0))
    const = lambda s: pl.BlockSpec(s, lambda i: (0,) * len(s))
    return pl.pallas_call(
        functools.partial(_outproj_kernel, n_prompt_tiles=n_prompt_tiles),
        grid=(n // tm,),
        in_specs=[tok(D_MODEL), *_split_token_specs(tm, D_MODEL, n_prompt_tiles),
                  pl.BlockSpec((D_MODEL, D_MODEL), lambda i: (0, 0), pipeline_mode=pl.Buffered(1)),
                  const((1, D_MODEL)), const((D_MODEL, 2 * LANES)), const((1, LANES))],
        out_specs=(tok(D_MODEL), tok(D_MODEL), tok(LANES), pl.BlockSpec((8, tm), lambda i: (0, i)),
                   const((8, LANES))),
        out_shape=(jax.ShapeDtypeStruct((n, D_MODEL), F32),
                   jax.ShapeDtypeStruct((n, D_MODEL), F32),
                   jax.ShapeDtypeStruct((n, LANES), F32),
                   jax.ShapeDtypeStruct((8, n), F32),
                   jax.ShapeDtypeStruct((8, LANES), F32)),
        scratch_shapes=[pltpu.VMEM((1, LANES), F32)],
        compiler_params=_cparams(1),
        name="out_proj",
    )(m, xp, xs, w_out, g2, wr_split, b_r)


DISPATCH_TOKENS = 256


def _dispatch_kernel(pos1_ref, pos2_ref, h2_ref, xs_in_ref, xs_ref, sem):
    del xs_in_ref
    tb = DISPATCH_TOKENS
    dst = (pos1_ref, pos2_ref)

    def row_copy(r, slot):
        return pltpu.make_async_copy(h2_ref.at[pl.ds(r, 1)], xs_ref.at[pl.ds(dst[slot][r], 1)], sem)

    def issue(r, carry):
        row_copy(r, 0).start()
        row_copy(r, 1).start()
        return carry

    lax.fori_loop(0, tb, issue, 0, unroll=8)

    def drain(r, carry):
        row_copy(r, 0).wait()
        row_copy(r, 1).wait()
        return carry

    lax.fori_loop(0, tb, drain, 0, unroll=8)


def _dispatch(pos1, pos2, h2, n_rows):
    n = h2.shape[0]
    tb = DISPATCH_TOKENS
    xs0 = jnp.zeros((n_rows, D_MODEL), F32)
    return pl.pallas_call(
        _dispatch_kernel,
        grid=(n // tb,),
        in_specs=[pl.BlockSpec((tb,), lambda i: (i,), memory_space=pltpu.SMEM),
                  pl.BlockSpec((tb,), lambda i: (i,), memory_space=pltpu.SMEM),
                  pl.BlockSpec((tb, D_MODEL), lambda i: (i, 0)),
                  pl.BlockSpec(memory_space=pl.ANY)],
        out_specs=pl.BlockSpec(memory_space=pl.ANY),
        out_shape=jax.ShapeDtypeStruct((n_rows, D_MODEL), F32),
        scratch_shapes=[pltpu.SemaphoreType.DMA(())],
        input_output_aliases={3: 0},
        compiler_params=_cparams(1),
        name="dispatch",
    )(pos1, pos2, h2, xs0)


def _experts_kernel(te_ref, nu_ref, xs_ref, wg_ref, wu_ref, wd_ref, ys_ref, wg16, wu16, wd16):
    i = pl.program_id(0)
    prev = te_ref[jnp.maximum(i - 1, 0)]
    fresh = (i == 0) | (te_ref[i] != prev)
    used = i < nu_ref[0]

    @pl.when(fresh & used)
    def _():
        wg16[...] = wg_ref[0].astype(BF16)
        wu16[...] = wu_ref[0].astype(BF16)
        wd16[...] = wd_ref[0].astype(BF16)

    @pl.when(used)
    def _():
        x = xs_ref[...].astype(BF16)
        a = _dot(x, wg16[...])
        u = _dot(x, wu16[...])
        act = (a * jax.nn.sigmoid(a)) * u
        ys_ref[...] = _dot(act.astype(BF16), wd16[...])

    @pl.when(jnp.logical_not(used))
    def _():
        ys_ref[...] = jnp.zeros(ys_ref.shape, F32)


def _experts(tile_expert, n_used, xs, w_g, w_u, w_d):
    n_rows = xs.shape[0]
    tm = EXP_TILE
    grid_spec = pltpu.PrefetchScalarGridSpec(
        num_scalar_prefetch=2,
        grid=(n_rows // tm,),
        in_specs=[pl.BlockSpec((tm, D_MODEL), lambda i, te, nu: (i, 0)),
                  pl.BlockSpec((1, D_MODEL, D_EXPERT), lambda i, te, nu: (te[i], 0, 0)),
                  pl.BlockSpec((1, D_MODEL, D_EXPERT), lambda i, te, nu: (te[i], 0, 0)),
                  pl.BlockSpec((1, D_EXPERT, D_MODEL), lambda i, te, nu: (te[i], 0, 0))],
        out_specs=pl.BlockSpec((tm, D_MODEL), lambda i, te, nu: (i, 0)),
        scratch_shapes=[pltpu.VMEM((D_MODEL, D_EXPERT), BF16),
                        pltpu.VMEM((D_MODEL, D_EXPERT), BF16),
                        pltpu.VMEM((D_EXPERT, D_MODEL), BF16)],
    )
    return pl.pallas_call(
        _experts_kernel,
        grid_spec=grid_spec,
        out_shape=jax.ShapeDtypeStruct((n_rows, D_MODEL), F32),
        compiler_params=_cparams(1),
        name="experts",
    )(tile_expert, n_used, xs, w_g, w_u, w_d)


COMBINE_TOKENS = 256


def _combine_kernel(pos1_ref, pos2_ref, pos1_next_ref, pos2_next_ref, x1_ref, route_ref, gf_ref, ys_ref,
                    yp_ref, ysm_ref, ybuf, sem, *, n_prompt_tiles):
    tc = COMBINE_TOKENS
    i = pl.program_id(0)
    buf = i % 2

    def row_copy(p_refs, b, r, slot):
        return pltpu.make_async_copy(ys_ref.at[pl.ds(p_refs[slot][r], 1)],
                                     ybuf.at[b, slot, pl.ds(r, 1)], sem.at[b])

    def start_rows(p_refs, b):
        def issue(r, carry):
            row_copy(p_refs, b, r, 0).start()
            row_copy(p_refs, b, r, 1).start()
            return carry
        lax.fori_loop(0, tc, issue, 0, unroll=8)

    cur = (pos1_ref, pos2_ref)

    @pl.when(i == 0)
    def _():
        start_rows(cur, 0)

    @pl.when(i + 1 < pl.num_programs(0))
    def _():
        start_rows((pos1_next_ref, pos2_next_ref), 1 - buf)

    def drain(r, carry):
        row_copy(cur, buf, r, 0).wait()
        row_copy(cur, buf, r, 1).wait()
        return carry

    lax.fori_loop(0, tc, drain, 0, unroll=8)

    route = route_ref[...]
    w1 = route[:, _R_W1:_R_W1 + 1]
    w2 = route[:, _R_W2:_R_W2 + 1]
    x2 = x1_ref[...] + (w1 * ybuf[buf, 0] + w2 * ybuf[buf, 1])
    r = lax.rsqrt(jnp.mean(x2 * x2, axis=-1, keepdims=True) + EPS)
    y = (x2 * r) * gf_ref[...]
    is_prompt = pl.program_id(0) < n_prompt_tiles

    @pl.when(is_prompt)
    def _():
        yp_ref[...] = y

    @pl.when(jnp.logical_not(is_prompt))
    def _():
        ysm_ref[...] = y


def _combine(pos1, pos2, x1, route, gf, ys, n_prompt):
    n = x1.shape[0]
    tc = COMBINE_TOKENS
    n_prompt_tiles = n_prompt // tc
    n_steps = n // tc
    return pl.pallas_call(
        functools.partial(_combine_kernel, n_prompt_tiles=n_prompt_tiles),
        grid=(n_steps,),
        in_specs=[pl.BlockSpec((tc,), lambda i: (i,), memory_space=pltpu.SMEM),
                  pl.BlockSpec((tc,), lambda i: (i,), memory_space=pltpu.SMEM),
                  pl.BlockSpec((tc,), lambda i: (jnp.minimum(i + 1, n_steps - 1),), memory_space=pltpu.SMEM),
                  pl.BlockSpec((tc,), lambda i: (jnp.minimum(i + 1, n_steps - 1),), memory_space=pltpu.SMEM),
                  pl.BlockSpec((tc, D_MODEL), lambda i: (i, 0)),
                  pl.BlockSpec((tc, LANES), lambda i: (i, 0)),
                  pl.BlockSpec((1, D_MODEL), lambda i: (0, 0)),
                  pl.BlockSpec(memory_space=pl.ANY)],
        out_specs=_split_token_specs(tc, D_MODEL, n_prompt_tiles),
        out_shape=(jax.ShapeDtypeStruct((n_prompt, D_MODEL), F32),
                   jax.ShapeDtypeStruct((n - n_prompt, D_MODEL), F32)),
        scratch_shapes=[pltpu.VMEM((2, 2, tc, D_MODEL), F32), pltpu.SemaphoreType.DMA((2,))],
        compiler_params=_cparams(1),
        name="combine",
    )(pos1, pos2, pos1, pos2, x1, route, gf, ys)


def _split_w_in(w_in):
    w_main = w_in[:, :_C_KW].astype(BF16)
    w_kw = jnp.pad(w_in[:, _C_KW:_C_P].astype(BF16), ((0, 0), (0, LANES - (_C_P - _C_KW))))
    w_pool = w_in[:, _C_P:].astype(BF16)
    return w_main, w_kw, w_pool


def _forward(x_prompt, x_sample, cache_k, cache_v, cache_kidx, state_pool,
             norm1_g, w_in, w_gate, b_gate, w_pool_grp, pool_scale, w_o_attn, w_o_pool, w_out,
             norm2_g, w_rg, b_rg, w_re, b_re, w_eg, w_eu, w_ed, norm_f_g):
    batch, seq, d = x_prompt.shape
    dec_batch, dec_seq, _ = x_sample.shape
    past = cache_k.shape[1]
    n_p = batch * seq
    n_s = dec_batch * dec_seq
    n = n_p + n_s
    tm = TOK_TILE
    assert d == D_MODEL and seq % tm == 0 and n_s % tm == 0 and n % (2 * tm) == 0
    assert past % KEY_BLOCK == 0 and dec_seq >= POOL_STATE and dec_seq <= KEY_BLOCK

    xp = x_prompt.reshape(n_p, d)
    xs = x_sample.reshape(n_s, d)
    pos_tab = jnp.concatenate([jnp.arange(seq), past + jnp.arange(n_s) % dec_seq])
    tables = _rope_tables(pos_tab)
    h, q, k_p, k_s, v_p, v_s, qi, kw, pin = _in_proj(
        xp, xs, norm1_g.reshape(1, d), *_split_w_in(w_in), tables, seq // tm)

    oa_p = _dsa_prompt(q, qi, kw, k_p, v_p, batch, seq)
    oa_s = _dsa_sample(q, qi, kw, k_s, v_s,
                       cache_k.reshape(dec_batch * past, KV_WIDTH),
                       cache_v.reshape(dec_batch * past, KV_WIDTH),
                       cache_kidx.reshape(dec_batch * past, IDX_DIM),
                       n_p, dec_batch, dec_seq, past)

    w_grp16 = w_pool_grp.astype(BF16)
    scale = pool_scale.reshape(1, POOL_WIDTH)
    op_p = _pool(pin, w_grp16, scale, 0, batch, seq, 0)
    halo = jnp.concatenate([jnp.zeros((dec_batch, 1, POOL_WIDTH), F32), state_pool], axis=1)
    op_s = _pool(pin, w_grp16, scale, n_p // dec_seq, dec_batch, dec_seq, past, halo=halo)

    m = _merge(h, oa_p, oa_s, op_p, op_s, w_gate.astype(BF16), b_gate.reshape(1, 2 * d),
               w_o_attn.astype(BF16), w_o_pool.astype(BF16))

    w_r = jnp.concatenate([w_rg, w_re, jnp.zeros((d, LANES - MOE_GROUPS - N_EXPERTS), F32)], axis=1)
    b_r = jnp.concatenate([b_rg, b_re, jnp.zeros((LANES - MOE_GROUPS - N_EXPERTS,), F32)]).reshape(1, LANES)
    wr_hi = w_r.astype(BF16)
    wr_lo = (w_r - wr_hi.astype(F32)).astype(BF16)
    wr_split = jnp.concatenate([wr_hi, wr_lo], axis=1)
    x1, h2, route, route_t, counts = _out_proj(m, xp, xs, w_out.astype(BF16), norm2_g.reshape(1, d), wr_split, b_r)

    te = EXP_TILE
    n_tiles = (2 * n) // te + N_EXPERTS
    cnt = counts[0, _R_LOGIT0:_R_LOGIT0 + N_EXPERTS].astype(I32)
    tiles_per_e = (cnt + te - 1) // te
    tile_end = jnp.cumsum(tiles_per_e)
    row_start = (tile_end - tiles_per_e) * te
    n_used = tile_end[-1:].astype(I32)
    tile_ids = jnp.arange(n_tiles, dtype=I32)
    tile_expert = jnp.minimum(
        jnp.sum((tile_end[None, :] <= tile_ids[:, None]).astype(I32), axis=1), N_EXPERTS - 1)
    last_e = tile_expert[jnp.maximum(n_used[0] - 1, 0)]
    tile_expert = jnp.where(tile_ids < n_used[0], tile_expert, last_e)
    pos1 = row_start[route_t[_R_E1].astype(I32)] + route_t[_R_RANK1].astype(I32)
    pos2 = row_start[route_t[_R_E2].astype(I32)] + route_t[_R_RANK2].astype(I32)

    xs = _dispatch(pos1, pos2, h2, n_tiles * te)
    ys = _experts(tile_expert, n_used, xs, w_eg, w_eu, w_ed)
    y_p, y_s = _combine(pos1, pos2, x1, route, norm_f_g.reshape(1, d), ys, n_p)

    y_prompt = y_p.reshape(batch, seq, d)
    y_sample = y_s.reshape(dec_batch, dec_seq, d)
    k_p = k_p.reshape(1, batch, seq, N_KV_HEADS, HEAD_DIM)
    v_p = v_p.reshape(1, batch, seq, N_KV_HEADS, HEAD_DIM)
    ki_p = kw[:n_p, :IDX_DIM].reshape(1, batch, seq, IDX_DIM)
    pool_p = jnp.stack([pin[(b + 1) * seq - POOL_STATE:(b + 1) * seq] for b in range(batch)])[None]
    k_s = k_s.reshape(1, dec_batch, dec_seq, N_KV_HEADS, HEAD_DIM)
    v_s = v_s.reshape(1, dec_batch, dec_seq, N_KV_HEADS, HEAD_DIM)
    ki_s = kw[n_p:, :IDX_DIM].reshape(1, dec_batch, dec_seq, IDX_DIM)
    pool_s = pin[n_p:].reshape(dec_batch, dec_seq, POOL_WIDTH)[None, :, dec_seq - POOL_STATE:, :]
    return (y_prompt, y_sample, k_p, v_p, ki_p, pool_p, k_s, v_s, ki_s, pool_s)


def kernel(x_prompt, x_sample, cache_k, cache_v, cache_kidx, state_pool, norm1_g, w_in, w_gate, b_gate,
           w_pool_grp, pool_scale, w_o_attn, w_o_pool, w_out, norm2_g, w_router_group, b_router_group,
           w_router_expert, b_router_expert, w_exp_gate, w_exp_up, w_exp_down, norm_f_g):
    assert cache_k.shape[0] == 1, "single-layer model"
    return _forward(x_prompt, x_sample, cache_k[0], cache_v[0], cache_kidx[0], state_pool[0],
                    norm1_g[0], w_in[0], w_gate[0], b_gate[0], w_pool_grp[0], pool_scale[0],
                    w_o_attn[0], w_o_pool[0], w_out[0], norm2_g[0], w_router_group[0],
                    b_router_group[0], w_router_expert[0], b_router_expert[0],
                    w_exp_gate[0], w_exp_up[0], w_exp_down[0], norm_f_g)
```

```python
import functools

import numpy as np
import jax
import jax.numpy as jnp
from jax import lax
from jax.experimental import pallas as pl
from jax.experimental.pallas import tpu as pltpu

F32 = jnp.float32
BF16 = jnp.bfloat16
I32 = jnp.int32

D_MODEL = 2048
CHUNK = 64
N_HEADS = 8
N_KV_HEADS = 2
HEAD_DIM = 128
GQA_GROUP = N_HEADS // N_KV_HEADS
ATTN_WIDTH = N_HEADS * HEAD_DIM
KV_WIDTH = N_KV_HEADS * HEAD_DIM
IDX_HEADS = 16
IDX_DIM = 64
IDX_PAIRS = IDX_HEADS // 2
TOPK_MAX = 256
POOL_WINDOWS = (2, 4, 8, 16)
POOL_GROUPS = 4
POOL_WIDTH = D_MODEL // 2
POOL_GROUP_DIM = POOL_WIDTH // POOL_GROUPS
POOL_STATE = 15
POOL_HALO = 16
MOE_GROUPS = 4
EXPERTS_PER_GROUP = 8
N_EXPERTS = MOE_GROUPS * EXPERTS_PER_GROUP
D_EXPERT = 512
ROPE_THETA = 10000.0
CHUNK_SHIFT = CHUNK.bit_length() - 1
GROUP_SHIFT = EXPERTS_PER_GROUP.bit_length() - 1
EPS = 1e-6

LANES = 128
INT_MIN = -(2 ** 31)
NEG_BIG = -1e30
LOG2_E = 1.4426950408889634
VMEM_LIMIT = 56 * 1024 * 1024

KEY_BLOCK = 256
TOK_TILE = 256
EXP_TILE = 256


def _cparams(n_axes):
    return pltpu.CompilerParams(dimension_semantics=("arbitrary",) * n_axes,
                                vmem_limit_bytes=VMEM_LIMIT)


def _dot(a, b):
    return jnp.dot(a, b, preferred_element_type=F32)


def _dot_nt(a, b):
    return lax.dot_general(a, b, (((1,), (1,)), ((), ())), preferred_element_type=F32)


_C_Q = 0
_C_K = _C_Q + ATTN_WIDTH
_C_V = _C_K + KV_WIDTH
_C_QI = _C_V + KV_WIDTH
_C_KW = _C_QI + IDX_HEADS * IDX_DIM
_C_P = _C_KW + IDX_DIM + IDX_HEADS


def _rope128(y, cos, sin_signed):
    return y * cos + pltpu.roll(y, HEAD_DIM // 2, 1) * sin_signed


def _rope64(y, cos, sin_a, sin_b):
    half = IDX_DIM // 2
    return y * cos + pltpu.roll(y, LANES - half, 1) * sin_a + pltpu.roll(y, half, 1) * sin_b


def _inproj_kernel(xp_ref, xs_ref, g_ref, w_ref, wkw_ref, wp_ref, ck_ref, sk_ref, cq_ref, sq_ref, ci_ref,
                   sia_ref, sib_ref, cw_ref, swa_ref, swb_ref,
                   h_ref, q_ref, kp_ref, ks_ref, vp_ref, vs_ref, qi_ref, kw_ref, p_ref, *, n_prompt_tiles):
    is_prompt = pl.program_id(0) < n_prompt_tiles
    x = jnp.where(is_prompt, xp_ref[...], xs_ref[...])
    r = lax.rsqrt(jnp.mean(x * x, axis=-1, keepdims=True) + EPS)
    h = ((x * r) * g_ref[...]).astype(BF16)
    h_ref[...] = h
    cq, sq = cq_ref[...], sq_ref[...]
    for c in range(ATTN_WIDTH // 256):
        y = _dot(h, w_ref[:, _C_Q + c * 256:_C_Q + (c + 1) * 256])
        for s in range(2):
            q_ref[:, c * 256 + s * LANES:c * 256 + (s + 1) * LANES] = _rope128(
                y[:, s * LANES:(s + 1) * LANES], cq, sq).astype(BF16)
    ck, sk = ck_ref[...], sk_ref[...]
    y = _dot(h, w_ref[:, _C_K:_C_K + KV_WIDTH])
    k = jnp.concatenate([_rope128(y[:, s * LANES:(s + 1) * LANES], ck, sk) for s in range(N_KV_HEADS)], axis=1)
    v = _dot(h, w_ref[:, _C_V:_C_V + KV_WIDTH])

    @pl.when(is_prompt)
    def _():
        kp_ref[...] = k
        vp_ref[...] = v

    @pl.when(jnp.logical_not(is_prompt))
    def _():
        ks_ref[...] = k
        vs_ref[...] = v

    ci, sia, sib = ci_ref[...], sia_ref[...], sib_ref[...]
    for c in range(IDX_PAIRS // 2):
        y = _dot(h, w_ref[:, _C_QI + c * 256:_C_QI + (c + 1) * 256])
        for s in range(2):
            qi_ref[2 * c + s] = _rope64(y[:, s * LANES:(s + 1) * LANES], ci, sia, sib).astype(BF16)
    y = _dot(h, wkw_ref[...])
    kw_ref[...] = _rope64(y, cw_ref[...], swa_ref[...], swb_ref[...])
    for c in range(POOL_WIDTH // 256):
        p_ref[:, c * 256:(c + 1) * 256] = _dot(h, wp_ref[:, c * 256:(c + 1) * 256])


def _rope_tables(pos):
    pos = pos.astype(F32)[:, None]

    def cs(dim):
        half = dim // 2
        inv = ROPE_THETA ** (-jnp.arange(half, dtype=F32) / half)
        ang = pos * inv[None, :]
        return jnp.cos(ang), jnp.sin(ang)

    c, s = cs(HEAD_DIM)
    ck = jnp.concatenate([c, c], axis=1)
    sk = jnp.concatenate([-s, s], axis=1)
    qs = HEAD_DIM ** -0.5 * LOG2_E
    c, s = cs(IDX_DIM)
    z = jnp.zeros_like(s)
    ci = jnp.concatenate([c, c, c, c], axis=1)
    sia = jnp.concatenate([-s, z, -s, z], axis=1)
    sib = jnp.concatenate([z, s, z, s], axis=1)
    iscale = IDX_DIM ** -0.5
    n = pos.shape[0]
    wscale = jnp.full((n, IDX_HEADS), IDX_HEADS ** -0.5, F32)
    pad = jnp.zeros((n, LANES - IDX_DIM - IDX_HEADS), F32)
    zz = jnp.zeros((n, LANES - IDX_DIM), F32)
    cw = jnp.concatenate([c, c, wscale, pad], axis=1)
    swa = jnp.concatenate([-s, z, zz], axis=1)
    swb = jnp.concatenate([z, s, zz], axis=1)
    return (ck, sk, ck * qs, sk * qs, ci * iscale, sia * iscale, sib * iscale, cw, swa, swb)


def _split_token_specs(tm, width, n_prompt_tiles):
    prompt = pl.BlockSpec((tm, width), lambda i: (jnp.minimum(i, n_prompt_tiles - 1), 0))
    sample = pl.BlockSpec((tm, width), lambda i: (jnp.maximum(i - n_prompt_tiles, 0), 0))
    return prompt, sample


def _in_proj(xp, xs, g1, w_main, w_kw, w_pool, tables, tiles_per_seq):
    tm = TOK_TILE
    n_prompt_tiles = xp.shape[0] // tm
    n = xp.shape[0] + xs.shape[0]
    n_tiles = n // tm

    def tab_idx(i):
        return (jnp.where(i < n_prompt_tiles, i % tiles_per_seq, tiles_per_seq + i - n_prompt_tiles), 0)

    tok = lambda w: pl.BlockSpec((tm, w), lambda i: (i, 0))
    tab = pl.BlockSpec((tm, LANES), tab_idx)
    resident = lambda a: pl.BlockSpec(a.shape, lambda i: (0, 0), pipeline_mode=pl.Buffered(1))
    n_s = xs.shape[0]
    kv_p, kv_s = _split_token_specs(tm, KV_WIDTH, n_prompt_tiles)
    out_shape = (
        jax.ShapeDtypeStruct((n, D_MODEL), BF16),
        jax.ShapeDtypeStruct((n, ATTN_WIDTH), BF16),
        jax.ShapeDtypeStruct((n - n_s, KV_WIDTH), F32),
        jax.ShapeDtypeStruct((n_s, KV_WIDTH), F32),
        jax.ShapeDtypeStruct((n - n_s, KV_WIDTH), F32),
        jax.ShapeDtypeStruct((n_s, KV_WIDTH), F32),
        jax.ShapeDtypeStruct((IDX_PAIRS, n, LANES), BF16),
        jax.ShapeDtypeStruct((n, LANES), F32),
        jax.ShapeDtypeStruct((n, POOL_WIDTH), F32),
    )
    return pl.pallas_call(
        functools.partial(_inproj_kernel, n_prompt_tiles=n_prompt_tiles),
        grid=(n_tiles,),
        in_specs=[*_split_token_specs(tm, D_MODEL, n_prompt_tiles),
                  pl.BlockSpec((1, D_MODEL), lambda i: (0, 0)),
                  resident(w_main), resident(w_kw), resident(w_pool)]
                 + [tab] * 10,
        out_specs=(tok(D_MODEL), tok(ATTN_WIDTH), kv_p, kv_s, kv_p, kv_s,
                   pl.BlockSpec((IDX_PAIRS, tm, LANES), lambda i: (0, i, 0)),
                   tok(LANES), tok(POOL_WIDTH)),
        out_shape=out_shape,
        compiler_params=_cparams(1),
        name="in_proj",
    )(xp, xs, g1, w_main, w_kw, w_pool, *tables)


def _ordinal_to_f32(k):
    return lax.bitcast_convert_type(k ^ ((k >> 31) & 0x7FFFFFFF), F32)


def _kth_largest_score(topk, shape, n_key_rows, count_where):
    def count_ge(cand):
        cand_f = _ordinal_to_f32(cand)
        return count_where(lambda keys, kb: keys >= jnp.broadcast_to(cand_f, keys.shape))

    c0 = count_ge(jnp.zeros(shape, I32))
    thr0 = jnp.where(c0 >= topk, 0, INT_MIN).astype(I32)
    cnt0 = jnp.where(c0 >= topk, c0, n_key_rows).astype(I32)

    def bit_step(i, carry):
        thr, cnt = carry
        cand = thr + (jnp.int32(1) << (30 - i))
        c = count_ge(cand)
        ok = c >= topk
        return jnp.where(ok, cand, thr), jnp.where(ok, c, cnt)

    thr, cnt = lax.fori_loop(0, 31, bit_step, (thr0, cnt0))
    has_k = thr > INT_MIN
    thr_f = jnp.where(has_k, _ordinal_to_f32(thr), jnp.finfo(F32).min)
    return thr_f, has_k & (cnt > topk)


def _kth_largest_score_flat(topk, n_key_rows, n_trips, zero_partial, partial_count, finish):
    shape = finish(zero_partial).shape

    def body(it, state):
        j, b, thr, cnt, acc = state
        cand = jnp.where(b == 0, 0, thr + (jnp.int32(1) << jnp.maximum(31 - b, 0)))
        cand_f = _ordinal_to_f32(cand)
        acc = acc + partial_count(j, lambda keys, kb: keys >= jnp.broadcast_to(cand_f, keys.shape))
        wrap = j == n_trips - 1
        c = finish(acc)
        ok = wrap & (c >= topk)
        return (jnp.where(wrap, 0, j + 1), b + wrap.astype(I32), jnp.where(ok, cand, thr),
                jnp.where(ok, c, cnt), jnp.where(wrap, 0, acc))

    init = (jnp.int32(0), jnp.int32(0), jnp.full(shape, INT_MIN, I32), jnp.full(shape, n_key_rows, I32),
            zero_partial)
    _, _, thr, cnt, _ = lax.fori_loop(0, 32 * n_trips, body, init)
    has_k = thr > INT_MIN
    thr_f = jnp.where(has_k, _ordinal_to_f32(thr), jnp.finfo(F32).min)
    return thr_f, has_k & (cnt > topk)


def _drop_excess_ties(sc, n_blocks, topk, thr, need_tie, key_index, count_where):
    n_gt = count_where(lambda keys, kb: keys > jnp.broadcast_to(thr, keys.shape))
    n_keep = topk - n_gt

    def idx_step(i, bound):
        cand = bound + (jnp.int32(1) << (14 - i))
        c = count_where(lambda keys, kb: (keys == jnp.broadcast_to(thr, keys.shape))
                        & (key_index(kb) < jnp.broadcast_to(cand, keys.shape)))
        return jnp.where(c <= n_keep, cand, bound)

    bound = lax.fori_loop(0, 15, idx_step, jnp.zeros(thr.shape, I32))
    bound = jnp.where(need_tie, bound, jnp.int32(2 ** 30))

    def drop_block(kb, carry):
        keys = sc[kb]
        drop = (keys == jnp.broadcast_to(thr, keys.shape)) & (key_index(kb) >= jnp.broadcast_to(bound, keys.shape))
        sc[kb] = jnp.where(drop, -jnp.inf, keys)
        return carry

    lax.fori_loop(0, n_blocks, drop_block, 0)


def _stack_queries_t(q_ref, qt):
    for g in range(N_KV_HEADS):
        qg = jnp.concatenate([q_ref[:, (g * GQA_GROUP + j) * HEAD_DIM:(g * GQA_GROUP + j + 1) * HEAD_DIM]
                              for j in range(GQA_GROUP)], axis=0)
        qt[g] = qg.astype(F32).T.astype(BF16)


def _attention_scratch(rows):
    return [
        pltpu.VMEM((N_KV_HEADS, 1, rows), F32),
        pltpu.VMEM((N_KV_HEADS, 1, rows), F32),
        pltpu.VMEM((N_KV_HEADS, HEAD_DIM, rows), F32),
        pltpu.VMEM((2, N_KV_HEADS, KEY_BLOCK, rows), F32),
        pltpu.VMEM((2, N_KV_HEADS, KEY_BLOCK, rows), BF16),
        pltpu.VMEM((2, N_KV_HEADS, 1, rows), F32),
    ]


def _masked_attention(o_ref, n_blocks, n_blocks_all, bias_t, k16, vt16, qt, att, tq):
    m_s, l_s, acc_s, lg_s, p_s, alpha_s = att
    m_s[...] = jnp.full(m_s.shape, NEG_BIG, F32)
    l_s[...] = jnp.zeros(l_s.shape, F32)
    acc_s[...] = jnp.zeros(acc_s.shape, F32)
    p_s[1] = jnp.zeros(p_s.shape[1:], BF16)
    alpha_s[1] = jnp.ones(alpha_s.shape[1:], F32)

    def logits_of(kb, slot):
        r0 = pl.multiple_of(kb * KEY_BLOCK, KEY_BLOCK)
        for g in range(N_KV_HEADS):
            lg_s[slot, g] = _dot(k16[pl.ds(r0, KEY_BLOCK), g * HEAD_DIM:(g + 1) * HEAD_DIM], qt[g])

    def value_product(kb, slot):
        for g in range(N_KV_HEADS):
            acc_s[g] = (alpha_s[slot, g] * acc_s[g]
                        + _dot(vt16[kb, g * HEAD_DIM:(g + 1) * HEAD_DIM, :], p_s[slot, g]))

    logits_of(0, 0)
    last = n_blocks_all - 1

    def step(kb, slot):
        logits_of(jnp.minimum(kb + 1, last), 1 - slot)
        bias = jnp.where(kb < n_blocks, bias_t(jnp.minimum(kb, last)), NEG_BIG)
        for g in range(N_KV_HEADS):
            logits = lg_s[slot, g] + bias
            m_prev = m_s[g]
            m_new = jnp.maximum(m_prev, jnp.max(logits, axis=0, keepdims=True))
            p = jnp.exp2(logits - m_new)
            alpha = jnp.exp2(m_prev - m_new)
            l_s[g] = alpha * l_s[g] + jnp.sum(p, axis=0, keepdims=True)
            p_s[slot, g] = p.astype(BF16)
            alpha_s[slot, g] = alpha
            m_s[g] = m_new
        value_product(jnp.clip(kb - 1, 0, last), 1 - slot)

    def two_steps(i, carry):
        step(2 * i, 0)
        step(2 * i + 1, 1)
        return carry

    n_pairs = (n_blocks + 1) // 2
    lax.fori_loop(0, n_pairs, two_steps, 0)
    value_product(jnp.minimum(2 * n_pairs - 1, last), 1)

    for g in range(N_KV_HEADS):
        o = (acc_s[g] * (1.0 / l_s[g])).T
        for j in range(GQA_GROUP):
            hd = g * GQA_GROUP + j
            o_ref[:, hd * HEAD_DIM:(hd + 1) * HEAD_DIM] = o[j * tq:(j + 1) * tq, :].astype(BF16)


def _dsa_prompt_kernel(q_ref, qi_ref, wq_ref, kn_ref, vn_ref, kwn_ref, o_ref,
                       k16, vt16, kia, kib, sc, raw, qt, *att, tq, seq, topk):
    qb = pl.program_id(1)
    n_blocks_all = seq // KEY_BLOCK

    @pl.when(qb == 0)
    def _():
        lane = lax.broadcasted_iota(I32, (seq, LANES), 1)
        kin = jnp.where(lane < IDX_DIM, kwn_ref[...], 0.0)
        k16[...] = kn_ref[...].astype(BF16)
        kia[...] = kin.astype(BF16)
        kib[...] = pltpu.roll(kin, IDX_DIM, 1).astype(BF16)
        for kb in range(n_blocks_all):
            vt16[kb] = vn_ref[kb * KEY_BLOCK:(kb + 1) * KEY_BLOCK, :].T.astype(BF16)

    n_blocks = qb // (KEY_BLOCK // tq) + 1

    qit = qi_ref[...].reshape(IDX_PAIRS * tq, LANES).astype(F32).T.astype(BF16)
    wt = wq_ref[...].T
    w_rows = [wt[IDX_DIM + j:IDX_DIM + j + 1, :] for j in range(IDX_HEADS)]
    _stack_queries_t(q_ref, qt)

    key_row = lax.broadcasted_iota(I32, (KEY_BLOCK, tq), 0)
    q_pos = qb * tq + lax.broadcasted_iota(I32, (KEY_BLOCK, tq), 1)

    last = n_blocks_all - 1
    n_pairs = (n_blocks + 1) // 2

    def raw_scores(kb, slot):
        r0 = pl.multiple_of(kb * KEY_BLOCK, KEY_BLOCK)
        raw[slot, 0] = _dot(kia[pl.ds(r0, KEY_BLOCK), :], qit)
        raw[slot, 1] = _dot(kib[pl.ds(r0, KEY_BLOCK), :], qit)

    def score_step(kb, slot):
        raw_scores(jnp.minimum(kb + 1, last), 1 - slot)
        se = raw[slot, 0]
        so = raw[slot, 1]
        score = jnp.zeros((KEY_BLOCK, tq), F32)
        for p in range(IDX_PAIRS):
            score = (score + jnp.maximum(se[:, p * tq:(p + 1) * tq], 0.0) * w_rows[2 * p]
                     + jnp.maximum(so[:, p * tq:(p + 1) * tq], 0.0) * w_rows[2 * p + 1])
        adm = ((kb * KEY_BLOCK + key_row) >> CHUNK_SHIFT) <= (q_pos >> CHUNK_SHIFT)
        sc[jnp.where(kb < n_blocks, kb, n_blocks_all)] = jnp.where(adm, score, -jnp.inf)

    def two_score_steps(i, carry):
        score_step(2 * i, 0)
        score_step(2 * i + 1, 1)
        return carry

    raw_scores(0, 0)
    lax.fori_loop(0, n_pairs, two_score_steps, 0)

    zero_partial = jnp.zeros((8, tq), I32)

    def pair_count(i, pred):
        tot = zero_partial
        for s in range(2):
            kb = 2 * i + s
            kbc = jnp.minimum(kb, last)
            m = pred(sc[kbc], kbc).astype(I32)
            part = jnp.sum(m.reshape(KEY_BLOCK // 8, 8, tq), axis=0)
            tot = tot + jnp.where(kb < n_blocks, part, 0)
        return tot

    def finish(partial):
        return jnp.sum(partial, axis=0, keepdims=True)

    def count_where(pred):
        return finish(lax.fori_loop(0, n_pairs, lambda i, acc: acc + pair_count(i, pred), zero_partial))

    thr, need_tie = _kth_largest_score_flat(topk, seq, n_pairs, zero_partial, pair_count, finish)

    @pl.when(jnp.max(need_tie.astype(I32)) > 0)
    def _():
        _drop_excess_ties(sc, n_blocks, topk, thr, need_tie,
                          lambda kb: kb * KEY_BLOCK + key_row, count_where)

    thr_b = jnp.broadcast_to(thr, (KEY_BLOCK, tq))

    def bias_t(kb):
        bias = jnp.where(sc[kb] >= thr_b, 0.0, NEG_BIG)
        return jnp.concatenate([bias] * GQA_GROUP, axis=1)

    _masked_attention(o_ref, n_blocks, n_blocks_all, bias_t, k16, vt16, qt, att, tq)


def _dsa_sample_kernel(q_ref, qi_ref, wq_ref, kn_ref, vn_ref, kwn_ref, kc_ref, vc_ref, kic_ref, o_ref,
                       k16, vt16, kia, kib, wbe, wbo, sc, qt, *att,
                       tq, n_key_rows, n_cache, n_new, topk):
    n_blocks = n_key_rows // KEY_BLOCK
    n_cache_blocks = n_cache // KEY_BLOCK

    lane = lax.broadcasted_iota(I32, (n_new, LANES), 1)
    kin = jnp.where(lane < IDX_DIM, kwn_ref[...], 0.0).astype(BF16)
    n_tail = n_key_rows - n_cache
    k16[0:n_cache, :] = kc_ref[...].astype(BF16)
    for kb in range(n_cache_blocks):
        vt16[kb] = vc_ref[kb * KEY_BLOCK:(kb + 1) * KEY_BLOCK, :].T.astype(BF16)
    v_tail = jnp.concatenate([vn_ref[...], jnp.zeros((n_tail - n_new, KV_WIDTH), F32)], axis=0)
    vt16[n_cache_blocks] = v_tail.T.astype(BF16)
    kic = jnp.concatenate([kic_ref[...], jnp.zeros((n_cache, LANES - IDX_DIM), F32)], axis=1)
    kia[0:n_cache, :] = kic.astype(BF16)
    kib[0:n_cache, :] = pltpu.roll(kic, IDX_DIM, 1).astype(BF16)
    k16[n_cache:n_key_rows, :] = jnp.zeros((n_tail, KV_WIDTH), BF16)
    kia[n_cache:n_key_rows, :] = jnp.zeros((n_tail, LANES), BF16)
    kib[n_cache:n_key_rows, :] = jnp.zeros((n_tail, LANES), BF16)
    k16[n_cache:n_cache + n_new, :] = kn_ref[...].astype(BF16)
    kia[n_cache:n_cache + n_new, :] = kin
    kib[n_cache:n_cache + n_new, :] = pltpu.roll(kin.astype(F32), IDX_DIM, 1).astype(BF16)

    wq = wq_ref[...]
    for j in range(IDX_HEADS):
        col = jnp.broadcast_to(wq[:, IDX_DIM + j:IDX_DIM + j + 1], (tq, LANES))
        if j % 2 == 0:
            wbe[j // 2] = col
        else:
            wbo[j // 2] = col
    _stack_queries_t(q_ref, qt)

    lane = lax.broadcasted_iota(I32, (tq, KEY_BLOCK), 1)

    qi2 = qi_ref[...].reshape(IDX_PAIRS * tq, LANES)

    def score_block(kb, carry):
        r0 = pl.multiple_of(kb * KEY_BLOCK, KEY_BLOCK)
        se = _dot_nt(qi2, kia[pl.ds(r0, KEY_BLOCK), :]).reshape(IDX_PAIRS, tq, KEY_BLOCK)
        so = _dot_nt(qi2, kib[pl.ds(r0, KEY_BLOCK), :]).reshape(IDX_PAIRS, tq, KEY_BLOCK)
        score = jnp.zeros((tq, KEY_BLOCK), F32)
        for p in range(IDX_PAIRS):
            we = wbe[p]
            wo = wbo[p]
            we2 = jnp.concatenate([we, we], axis=1)
            wo2 = jnp.concatenate([wo, wo], axis=1)
            score = score + jnp.maximum(se[p], 0.0) * we2 + jnp.maximum(so[p], 0.0) * wo2
        adm = (kb * KEY_BLOCK + lane) < (n_cache + n_new)
        sc[kb] = jnp.where(adm, score, -jnp.inf)
        return carry

    lax.fori_loop(0, n_blocks, score_block, 0)

    def count_where(pred):
        def body(kb, acc):
            m = pred(sc[kb], kb).astype(I32)
            return acc + m[:, :LANES] + m[:, LANES:]
        acc = lax.fori_loop(0, n_blocks, body, jnp.zeros((tq, LANES), I32))
        return jnp.sum(acc, axis=1, keepdims=True)

    thr, need_tie = _kth_largest_score(topk, (tq, 1), n_key_rows, count_where)

    @pl.when(jnp.max(need_tie.astype(I32)) > 0)
    def _():
        _drop_excess_ties(sc, n_blocks, topk, thr, need_tie,
                          lambda kb: kb * KEY_BLOCK + lane, count_where)

    thr_b = jnp.broadcast_to(thr, (tq, KEY_BLOCK))

    def bias_t(kb):
        bias = jnp.where(sc[kb] >= thr_b, 0.0, NEG_BIG)
        return jnp.concatenate([bias] * GQA_GROUP, axis=0).T

    _masked_attention(o_ref, n_blocks, n_blocks, bias_t, k16, vt16, qt, att, tq)


def _dsa_sample_scratch(tq, n_key_rows):
    rows = GQA_GROUP * tq
    return [
        pltpu.VMEM((n_key_rows, KV_WIDTH), BF16),
        pltpu.VMEM((n_key_rows // KEY_BLOCK, KV_WIDTH, KEY_BLOCK), BF16),
        pltpu.VMEM((n_key_rows, LANES), BF16),
        pltpu.VMEM((n_key_rows, LANES), BF16),
        pltpu.VMEM((IDX_PAIRS, tq, LANES), F32),
        pltpu.VMEM((IDX_PAIRS, tq, LANES), F32),
        pltpu.VMEM((n_key_rows // KEY_BLOCK, tq, KEY_BLOCK), F32),
        pltpu.VMEM((N_KV_HEADS, HEAD_DIM, rows), BF16),
    ] + _attention_scratch(rows)


def _dsa_prompt(q, qi, kw, k, v, batch, seq):
    tq = 128
    nqb = seq // tq
    topk = min(TOPK_MAX, seq // 4)
    kern = functools.partial(_dsa_prompt_kernel, tq=tq, seq=seq, topk=topk)
    rows_q = GQA_GROUP * tq
    scratch = [
        pltpu.VMEM((seq, KV_WIDTH), BF16),
        pltpu.VMEM((seq // KEY_BLOCK, KV_WIDTH, KEY_BLOCK), BF16),
        pltpu.VMEM((seq, LANES), BF16),
        pltpu.VMEM((seq, LANES), BF16),
        pltpu.VMEM((seq // KEY_BLOCK + 1, KEY_BLOCK, tq), F32),
        pltpu.VMEM((2, 2, KEY_BLOCK, IDX_PAIRS * tq), F32),
        pltpu.VMEM((N_KV_HEADS, HEAD_DIM, rows_q), BF16),
    ] + _attention_scratch(rows_q)
    qmap = lambda b, i: (b * nqb + i, 0)
    smap = lambda b, i: (b, 0)
    return pl.pallas_call(
        kern,
        grid=(batch, nqb),
        in_specs=[pl.BlockSpec((tq, ATTN_WIDTH), qmap),
                  pl.BlockSpec((IDX_PAIRS, tq, LANES), lambda b, i: (0, b * nqb + i, 0)),
                  pl.BlockSpec((tq, LANES), qmap),
                  pl.BlockSpec((seq, KV_WIDTH), smap),
                  pl.BlockSpec((seq, KV_WIDTH), smap),
                  pl.BlockSpec((seq, LANES), smap)],
        out_specs=pl.BlockSpec((tq, ATTN_WIDTH), qmap),
        out_shape=jax.ShapeDtypeStruct((batch * seq, ATTN_WIDTH), BF16),
        scratch_shapes=scratch,
        compiler_params=_cparams(2),
        name="dsa_prompt",
    )(q, qi, kw, k, v, kw)


def _dsa_sample(q, qi, kw, k, v, cache_k, cache_v, cache_ki, row0, batch, seq, past):
    tq = seq
    n_key_rows = past + KEY_BLOCK
    topk = min(TOPK_MAX, (past + seq) // 4)
    kern = functools.partial(_dsa_sample_kernel, tq=tq, n_key_rows=n_key_rows, n_cache=past, n_new=seq,
                             topk=topk)
    t0 = row0 // seq
    qmap = lambda b, i: (t0 + b, 0)
    cmap = lambda b, i: (b, 0)
    return pl.pallas_call(
        kern,
        grid=(batch, 1),
        in_specs=[pl.BlockSpec((tq, ATTN_WIDTH), qmap),
                  pl.BlockSpec((IDX_PAIRS, tq, LANES), lambda b, i: (0, t0 + b, 0)),
                  pl.BlockSpec((tq, LANES), qmap),
                  pl.BlockSpec((tq, KV_WIDTH), cmap),
                  pl.BlockSpec((tq, KV_WIDTH), cmap),
                  pl.BlockSpec((tq, LANES), qmap),
                  pl.BlockSpec((past, KV_WIDTH), cmap),
                  pl.BlockSpec((past, KV_WIDTH), cmap),
                  pl.BlockSpec((past, IDX_DIM), cmap)],
        out_specs=pl.BlockSpec((tq, ATTN_WIDTH), cmap),
        out_shape=jax.ShapeDtypeStruct((batch * seq, ATTN_WIDTH), BF16),
        scratch_shapes=_dsa_sample_scratch(tq, n_key_rows),
        compiler_params=_cparams(2),
        name="dsa_sample",
    )(q, qi, kw, k, v, kw, cache_k, cache_v, cache_ki)


def _pool_kernel(*refs, tp, pos0, carried):
    if carried:
        p_ref, w_ref, s_ref, o_ref, ext, save = refs
    else:
        p_ref, halo_ref, w_ref, s_ref, o_ref, ext = refs
    i = pl.program_id(1)
    if carried:
        @pl.when(i == 0)
        def _():
            ext[0:POOL_HALO, :] = jnp.zeros((POOL_HALO, POOL_WIDTH), F32)

        @pl.when(i > 0)
        def _():
            ext[0:POOL_HALO, :] = save[...]
    else:
        ext[0:POOL_HALO, :] = halo_ref[0]
    x = p_ref[...]
    ext[POOL_HALO:POOL_HALO + tp, :] = x
    if carried:
        save[...] = x[tp - POOL_HALO:tp, :]
    pos = pos0 + i * tp + lax.broadcasted_iota(I32, (tp, 1), 0)
    for g, w in enumerate(POOL_WINDOWS):
        c0, c1 = g * POOL_GROUP_DIM, (g + 1) * POOL_GROUP_DIM
        xg = x[:, c0:c1]
        tot = xg
        for s in range(1, w):
            tot = tot + ext[POOL_HALO - s:POOL_HALO - s + tp, c0:c1]
        inv = 1.0 / jnp.minimum(pos + 1, w).astype(F32)
        d = (tot * inv - xg).astype(BF16)
        o_ref[:, c0:c1] = (_dot(d, w_ref[g]) * s_ref[:, c0:c1]).astype(BF16)


def _pool(pin, w_grp, scale, tile0, batch, seq, pos0, halo=None):
    carried = halo is None
    tp = min(seq, 256)
    nt = seq // tp
    kern = functools.partial(_pool_kernel, tp=tp, pos0=pos0, carried=carried)
    in_specs = [pl.BlockSpec((tp, POOL_WIDTH), lambda b, i: (tile0 + b * nt + i, 0))]
    args = [pin]
    if not carried:
        in_specs.append(pl.BlockSpec((1, POOL_HALO, POOL_WIDTH), lambda b, i: (b, 0, 0)))
        args.append(halo)
    in_specs += [pl.BlockSpec((POOL_GROUPS, POOL_GROUP_DIM, POOL_GROUP_DIM), lambda b, i: (0, 0, 0)),
                 pl.BlockSpec((1, POOL_WIDTH), lambda b, i: (0, 0))]
    args += [w_grp, scale]
    scratch = [pltpu.VMEM((POOL_HALO + tp, POOL_WIDTH), F32)]
    if carried:
        scratch.append(pltpu.VMEM((POOL_HALO, POOL_WIDTH), F32))
    return pl.pallas_call(
        kern,
        grid=(batch, nt),
        in_specs=in_specs,
        out_specs=pl.BlockSpec((tp, POOL_WIDTH), lambda b, i: (b * nt + i, 0)),
        out_shape=jax.ShapeDtypeStruct((batch * seq, POOL_WIDTH), BF16),
        scratch_shapes=scratch,
        compiler_params=_cparams(2),
        name="pool_prompt" if carried else "pool_sample",
    )(*args)


def _merge_kernel(h_ref, oap_ref, oas_ref, opp_ref, ops_ref, wga_ref, wgp_ref, ba_ref, bp_ref, woa_ref,
                  wop_ref, m_ref, *, n_prompt_tiles):
    h = h_ref[...]
    is_prompt = pl.program_id(1) < n_prompt_tiles
    oa = jnp.where(is_prompt, oap_ref[...], oas_ref[...])
    op = jnp.where(is_prompt, opp_ref[...], ops_ref[...])
    ga = jax.nn.sigmoid(_dot(h, wga_ref[...]) + ba_ref[...])
    gp = jax.nn.sigmoid(_dot(h, wgp_ref[...]) + bp_ref[...])
    a = _dot(oa, woa_ref[...])
    p = _dot(op, wop_ref[...])
    m_ref[...] = (ga * a + gp * p).astype(BF16)


def _merge(h, oa_p, oa_s, op_p, op_s, w_gate, b_gate, w_oa, w_op):
    n = h.shape[0]
    tm = 2 * TOK_TILE
    tn = D_MODEL // 2
    nj = D_MODEL // tn
    npt = oa_p.shape[0] // tm
    prompt = lambda w: pl.BlockSpec((tm, w), lambda j, i: (jnp.minimum(i, npt - 1), 0))
    sample = lambda w: pl.BlockSpec((tm, w), lambda j, i: (jnp.maximum(i - npt, 0), 0))
    return pl.pallas_call(
        functools.partial(_merge_kernel, n_prompt_tiles=npt),
        grid=(nj, n // tm),
        in_specs=[pl.BlockSpec((tm, D_MODEL), lambda j, i: (i, 0)),
                  prompt(ATTN_WIDTH), sample(ATTN_WIDTH), prompt(POOL_WIDTH), sample(POOL_WIDTH),
                  pl.BlockSpec((D_MODEL, tn), lambda j, i: (0, j)),
                  pl.BlockSpec((D_MODEL, tn), lambda j, i: (0, nj + j)),
                  pl.BlockSpec((1, tn), lambda j, i: (0, j)),
                  pl.BlockSpec((1, tn), lambda j, i: (0, nj + j)),
                  pl.BlockSpec((ATTN_WIDTH, tn), lambda j, i: (0, j)),
                  pl.BlockSpec((POOL_WIDTH, tn), lambda j, i: (0, j))],
        out_specs=pl.BlockSpec((tm, tn), lambda j, i: (i, j)),
        out_shape=jax.ShapeDtypeStruct((n, D_MODEL), BF16),
        compiler_params=_cparams(2),
        name="merge",
    )(h, oa_p, oa_s, op_p, op_s, w_gate, w_gate, b_gate, b_gate, w_oa, w_op)


_R_E1, _R_E2, _R_W1, _R_W2, _R_RANK1, _R_RANK2 = range(6)
_R_LOGIT0 = MOE_GROUPS


def _outproj_kernel(m_ref, xp_ref, xs_ref, wo_ref, g2_ref, wr_ref, br_ref,
                    x1_ref, h2_ref, route_ref, route_t_ref, counts_ref, run, *, n_prompt_tiles):
    i = pl.program_id(0)
    tm = m_ref.shape[0]

    @pl.when(i == 0)
    def _():
        run[...] = jnp.zeros(run.shape, F32)

    x = jnp.where(i < n_prompt_tiles, xp_ref[...], xs_ref[...])
    x1 = x + _dot(m_ref[...], wo_ref[...])
    x1_ref[...] = x1
    r = lax.rsqrt(jnp.mean(x1 * x1, axis=-1, keepdims=True) + EPS)
    hf = (x1 * r) * g2_ref[...]
    h2_ref[...] = hf

    hi = hf.astype(BF16)
    lo = (hf - hi.astype(F32)).astype(BF16)
    hw = _dot(hi, wr_ref[...])
    lg = hw[:, :LANES] + (hw[:, LANES:] + _dot(lo, wr_ref[:, :LANES])) + br_ref[...]

    lane = lax.broadcasted_iota(I32, (tm, LANES), 1)
    neg_inf = jnp.float32(-jnp.inf)
    is_g = lane < MOE_GROUPS
    cl = jnp.where(is_g, lg, neg_inf)
    gmax = jnp.max(cl, axis=1, keepdims=True)
    g_sel = jnp.min(jnp.where(cl == gmax, lane, LANES), axis=1, keepdims=True)
    den = jnp.sum(jnp.where(is_g, jnp.exp(cl - gmax), 0.0), axis=1, keepdims=True)
    g_val = 1.0 / den
    e_lane = lane - _R_LOGIT0
    in_grp = (e_lane >= 0) & (e_lane < N_EXPERTS) & ((e_lane >> GROUP_SHIFT) == g_sel)
    f1 = jnp.where(in_grp, lg, neg_inf)
    v1 = jnp.max(f1, axis=1, keepdims=True)
    i1 = jnp.min(jnp.where(f1 == v1, lane, LANES), axis=1, keepdims=True)
    f2 = jnp.where(lane == i1, neg_inf, f1)
    v2 = jnp.max(f2, axis=1, keepdims=True)
    i2 = jnp.min(jnp.where(f2 == v2, lane, LANES), axis=1, keepdims=True)
    e2 = jnp.exp(v2 - v1)
    w1 = g_val / (1.0 + e2)
    w2 = g_val * e2 / (1.0 + e2)

    oh1 = lane == i1
    oh2 = lane == i2
    c = (oh1 | oh2).astype(BF16)
    rr = lax.broadcasted_iota(I32, (tm, tm), 0)
    cc = lax.broadcasted_iota(I32, (tm, tm), 1)
    before = (cc < rr).astype(BF16)
    prior = _dot(before, c) + run[...]
    rank1 = jnp.sum(jnp.where(oh1, prior, 0.0), axis=1, keepdims=True)
    rank2 = jnp.sum(jnp.where(oh2, prior, 0.0), axis=1, keepdims=True)
    run[...] = run[...] + jnp.sum(c.astype(F32), axis=0, keepdims=True)
    counts_ref[...] = jnp.broadcast_to(run[...], counts_ref.shape)

    rec = jnp.zeros((tm, LANES), F32)
    for ln, val in ((_R_E1, (i1 - _R_LOGIT0).astype(F32)), (_R_E2, (i2 - _R_LOGIT0).astype(F32)),
                    (_R_W1, w1), (_R_W2, w2), (_R_RANK1, rank1), (_R_RANK2, rank2)):
        rec = jnp.where(lane == ln, val, rec)
    route_ref[...] = rec
    route_t_ref[...] = rec.T[0:route_t_ref.shape[0], :]


def _out_proj(m, xp, xs, w_out, g2, wr_split, b_r):
    n = m.shape[0]
    tm = TOK_TILE
    n_prompt_tiles = xp.shape[0] // tm
    tok = lambda w: pl.BlockSpec((tm, w), lambda i: (i, 0))
    const = lambda s: pl.BlockSpec(s, lambda i: (0,) * len(s))
    return pl.pallas_call(
        functools.partial(_outproj_kernel, n_prompt_tiles=n_prompt_tiles),
        grid=(n // tm,),
        in_specs=[tok(D_MODEL), *_split_token_specs(tm, D_MODEL, n_prompt_tiles),
                  pl.BlockSpec((D_MODEL, D_MODEL), lambda i: (0, 0), pipeline_mode=pl.Buffered(1)),
                  const((1, D_MODEL)), const((D_MODEL, 2 * LANES)), const((1, LANES))],
        out_specs=(tok(D_MODEL), tok(D_MODEL), tok(LANES), pl.BlockSpec((8, tm), lambda i: (0, i)),
                   const((8, LANES))),
        out_shape=(jax.ShapeDtypeStruct((n, D_MODEL), F32),
                   jax.ShapeDtypeStruct((n, D_MODEL), F32),
                   jax.ShapeDtypeStruct((n, LANES), F32),
                   jax.ShapeDtypeStruct((8, n), F32),
                   jax.ShapeDtypeStruct((8, LANES), F32)),
        scratch_shapes=[pltpu.VMEM((1, LANES), F32)],
        compiler_params=_cparams(1),
        name="out_proj",
    )(m, xp, xs, w_out, g2, wr_split, b_r)


DISPATCH_TOKENS = 256


def _dispatch_kernel(pos1_ref, pos2_ref, h2_ref, xs_in_ref, xs_ref, sem):
    del xs_in_ref
    tb = DISPATCH_TOKENS
    dst = (pos1_ref, pos2_ref)

    def row_copy(r, slot):
        return pltpu.make_async_copy(h2_ref.at[pl.ds(r, 1)], xs_ref.at[pl.ds(dst[slot][r], 1)], sem)

    def issue(r, carry):
        row_copy(r, 0).start(priority=0)
        row_copy(r, 1).start(priority=1)
        return carry

    lax.fori_loop(0, tb, issue, 0, unroll=8)

    def drain(r, carry):
        row_copy(r, 0).wait()
        row_copy(r, 1).wait()
        return carry

    lax.fori_loop(0, tb, drain, 0, unroll=8)


def _dispatch(pos1, pos2, h2, n_rows):
    n = h2.shape[0]
    tb = DISPATCH_TOKENS
    xs0 = jnp.zeros((n_rows, D_MODEL), F32)
    return pl.pallas_call(
        _dispatch_kernel,
        grid=(n // tb,),
        in_specs=[pl.BlockSpec((tb,), lambda i: (i,), memory_space=pltpu.SMEM),
                  pl.BlockSpec((tb,), lambda i: (i,), memory_space=pltpu.SMEM),
                  pl.BlockSpec((tb, D_MODEL), lambda i: (i, 0)),
                  pl.BlockSpec(memory_space=pl.ANY)],
        out_specs=pl.BlockSpec(memory_space=pl.ANY),
        out_shape=jax.ShapeDtypeStruct((n_rows, D_MODEL), F32),
        scratch_shapes=[pltpu.SemaphoreType.DMA(())],
        input_output_aliases={3: 0},
        compiler_params=_cparams(1),
        name="dispatch",
    )(pos1, pos2, h2, xs0)


def _experts_kernel(te_ref, nu_ref, xs_ref, wg_ref, wu_ref, wd_ref, ys_ref, wg16, wu16, wd16):
    i = pl.program_id(0)
    prev = te_ref[jnp.maximum(i - 1, 0)]
    fresh = (i == 0) | (te_ref[i] != prev)
    used = i < nu_ref[0]

    @pl.when(fresh & used)
    def _():
        wg16[...] = wg_ref[0].astype(BF16)
        wu16[...] = wu_ref[0].astype(BF16)
        wd16[...] = wd_ref[0].astype(BF16)

    @pl.when(used)
    def _():
        x = xs_ref[...].astype(BF16)
        a = _dot(x, wg16[...])
        u = _dot(x, wu16[...])
        act = (a * jax.nn.sigmoid(a)) * u
        ys_ref[...] = _dot(act.astype(BF16), wd16[...])

    @pl.when(jnp.logical_not(used))
    def _():
        ys_ref[...] = jnp.zeros(ys_ref.shape, F32)


def _experts(tile_expert, n_used, xs, w_g, w_u, w_d):
    n_rows = xs.shape[0]
    tm = EXP_TILE
    grid_spec = pltpu.PrefetchScalarGridSpec(
        num_scalar_prefetch=2,
        grid=(n_rows // tm,),
        in_specs=[pl.BlockSpec((tm, D_MODEL), lambda i, te, nu: (i, 0)),
                  pl.BlockSpec((1, D_MODEL, D_EXPERT), lambda i, te, nu: (te[i], 0, 0)),
                  pl.BlockSpec((1, D_MODEL, D_EXPERT), lambda i, te, nu: (te[i], 0, 0)),
                  pl.BlockSpec((1, D_EXPERT, D_MODEL), lambda i, te, nu: (te[i], 0, 0))],
        out_specs=pl.BlockSpec((tm, D_MODEL), lambda i, te, nu: (i, 0)),
        scratch_shapes=[pltpu.VMEM((D_MODEL, D_EXPERT), BF16),
                        pltpu.VMEM((D_MODEL, D_EXPERT), BF16),
                        pltpu.VMEM((D_EXPERT, D_MODEL), BF16)],
    )
    return pl.pallas_call(
        _experts_kernel,
        grid_spec=grid_spec,
        out_shape=jax.ShapeDtypeStruct((n_rows, D_MODEL), F32),
        compiler_params=_cparams(1),
        name="experts",
    )(tile_expert, n_used, xs, w_g, w_u, w_d)


COMBINE_TOKENS = 256


def _combine_kernel(pos1_ref, pos2_ref, pos1_next_ref, pos2_next_ref, x1_ref, route_ref, gf_ref, ys_ref,
                    yp_ref, ysm_ref, ybuf, sem, *, n_prompt_tiles):
    tc = COMBINE_TOKENS
    i = pl.program_id(0)
    buf = i % 2

    def row_copy(p_refs, b, r, slot):
        return pltpu.make_async_copy(ys_ref.at[pl.ds(p_refs[slot][r], 1)],
                                     ybuf.at[b, slot, pl.ds(r, 1)], sem.at[b])

    def start_rows(p_refs, b):
        def issue(r, carry):
            row_copy(p_refs, b, r, 0).start(priority=0)
            row_copy(p_refs, b, r, 1).start(priority=1)
            return carry
        lax.fori_loop(0, tc, issue, 0, unroll=8)

    cur = (pos1_ref, pos2_ref)

    @pl.when(i == 0)
    def _():
        start_rows(cur, 0)

    @pl.when(i + 1 < pl.num_programs(0))
    def _():
        start_rows((pos1_next_ref, pos2_next_ref), 1 - buf)

    def drain(r, carry):
        row_copy(cur, buf, r, 0).wait()
        row_copy(cur, buf, r, 1).wait()
        return carry

    lax.fori_loop(0, tc, drain, 0, unroll=8)

    route = route_ref[...]
    w1 = route[:, _R_W1:_R_W1 + 1]
    w2 = route[:, _R_W2:_R_W2 + 1]
    x2 = x1_ref[...] + (w1 * ybuf[buf, 0] + w2 * ybuf[buf, 1])
    r = lax.rsqrt(jnp.mean(x2 * x2, axis=-1, keepdims=True) + EPS)
    y = (x2 * r) * gf_ref[...]
    is_prompt = pl.program_id(0) < n_prompt_tiles

    @pl.when(is_prompt)
    def _():
        yp_ref[...] = y

    @pl.when(jnp.logical_not(is_prompt))
    def _():
        ysm_ref[...] = y


def _combine(pos1, pos2, x1, route, gf, ys, n_prompt):
    n = x1.shape[0]
    tc = COMBINE_TOKENS
    n_prompt_tiles = n_prompt // tc
    n_steps = n // tc
    return pl.pallas_call(
        functools.partial(_combine_kernel, n_prompt_tiles=n_prompt_tiles),
        grid=(n_steps,),
        in_specs=[pl.BlockSpec((tc,), lambda i: (i,), memory_space=pltpu.SMEM),
                  pl.BlockSpec((tc,), lambda i: (i,), memory_space=pltpu.SMEM),
                  pl.BlockSpec((tc,), lambda i: (jnp.minimum(i + 1, n_steps - 1),), memory_space=pltpu.SMEM),
                  pl.BlockSpec((tc,), lambda i: (jnp.minimum(i + 1, n_steps - 1),), memory_space=pltpu.SMEM),
                  pl.BlockSpec((tc, D_MODEL), lambda i: (i, 0)),
                  pl.BlockSpec((tc, LANES), lambda i: (i, 0)),
                  pl.BlockSpec((1, D_MODEL), lambda i: (0, 0)),
                  pl.BlockSpec(memory_space=pl.ANY)],
        out_specs=_split_token_specs(tc, D_MODEL, n_prompt_tiles),
        out_shape=(jax.ShapeDtypeStruct((n_prompt, D_MODEL), F32),
                   jax.ShapeDtypeStruct((n - n_prompt, D_MODEL), F32)),
        scratch_shapes=[pltpu.VMEM((2, 2, tc, D_MODEL), F32), pltpu.SemaphoreType.DMA((2,))],
        compiler_params=_cparams(1),
        name="combine",
    )(pos1, pos2, pos1, pos2, x1, route, gf, ys)


def _split_w_in(w_in):
    w_main = w_in[:, :_C_KW].astype(BF16)
    w_kw = jnp.pad(w_in[:, _C_KW:_C_P].astype(BF16), ((0, 0), (0, LANES - (_C_P - _C_KW))))
    w_pool = w_in[:, _C_P:].astype(BF16)
    return w_main, w_kw, w_pool


def _forward(x_prompt, x_sample, cache_k, cache_v, cache_kidx, state_pool,
             norm1_g, w_in, w_gate, b_gate, w_pool_grp, pool_scale, w_o_attn, w_o_pool, w_out,
             norm2_g, w_rg, b_rg, w_re, b_re, w_eg, w_eu, w_ed, norm_f_g):
    batch, seq, d = x_prompt.shape
    dec_batch, dec_seq, _ = x_sample.shape
    past = cache_k.shape[1]
    n_p = batch * seq
    n_s = dec_batch * dec_seq
    n = n_p + n_s
    tm = TOK_TILE
    assert d == D_MODEL and seq % tm == 0 and n_s % tm == 0 and n % (2 * tm) == 0
    assert past % KEY_BLOCK == 0 and dec_seq >= POOL_STATE and dec_seq <= KEY_BLOCK

    xp = x_prompt.reshape(n_p, d)
    xs = x_sample.reshape(n_s, d)
    pos_tab = jnp.concatenate([jnp.arange(seq), past + jnp.arange(n_s) % dec_seq])
    tables = _rope_tables(pos_tab)
    h, q, k_p, k_s, v_p, v_s, qi, kw, pin = _in_proj(
        xp, xs, norm1_g.reshape(1, d), *_split_w_in(w_in), tables, seq // tm)

    oa_p = _dsa_prompt(q, qi, kw, k_p, v_p, batch, seq)
    oa_s = _dsa_sample(q, qi, kw, k_s, v_s,
                       cache_k.reshape(dec_batch * past, KV_WIDTH),
                       cache_v.reshape(dec_batch * past, KV_WIDTH),
                       cache_kidx.reshape(dec_batch * past, IDX_DIM),
                       n_p, dec_batch, dec_seq, past)

    w_grp16 = w_pool_grp.astype(BF16)
    scale = pool_scale.reshape(1, POOL_WIDTH)
    op_p = _pool(pin, w_grp16, scale, 0, batch, seq, 0)
    halo = jnp.concatenate([jnp.zeros((dec_batch, 1, POOL_WIDTH), F32), state_pool], axis=1)
    op_s = _pool(pin, w_grp16, scale, n_p // dec_seq, dec_batch, dec_seq, past, halo=halo)

    m = _merge(h, oa_p, oa_s, op_p, op_s, w_gate.astype(BF16), b_gate.reshape(1, 2 * d),
               w_o_attn.astype(BF16), w_o_pool.astype(BF16))

    w_r = jnp.concatenate([w_rg, w_re, jnp.zeros((d, LANES - MOE_GROUPS - N_EXPERTS), F32)], axis=1)
    b_r = jnp.concatenate([b_rg, b_re, jnp.zeros((LANES - MOE_GROUPS - N_EXPERTS,), F32)]).reshape(1, LANES)
    wr_hi = w_r.astype(BF16)
    wr_lo = (w_r - wr_hi.astype(F32)).astype(BF16)
    wr_split = jnp.concatenate([wr_hi, wr_lo], axis=1)
    x1, h2, route, route_t, counts = _out_proj(m, xp, xs, w_out.astype(BF16), norm2_g.reshape(1, d), wr_split, b_r)

    te = EXP_TILE
    n_tiles = (2 * n) // te + N_EXPERTS
    cnt = counts[0, _R_LOGIT0:_R_LOGIT0 + N_EXPERTS].astype(I32)
    tiles_per_e = (cnt + te - 1) // te
    tile_end = jnp.cumsum(tiles_per_e)
    row_start = (tile_end - tiles_per_e) * te
    n_used = tile_end[-1:].astype(I32)
    tile_ids = jnp.arange(n_tiles, dtype=I32)
    tile_expert = jnp.minimum(
        jnp.sum((tile_end[None, :] <= tile_ids[:, None]).astype(I32), axis=1), N_EXPERTS - 1)
    last_e = tile_expert[jnp.maximum(n_used[0] - 1, 0)]
    tile_expert = jnp.where(tile_ids < n_used[0], tile_expert, last_e)
    pos1 = row_start[route_t[_R_E1].astype(I32)] + route_t[_R_RANK1].astype(I32)
    pos2 = row_start[route_t[_R_E2].astype(I32)] + route_t[_R_RANK2].astype(I32)

    xs = _dispatch(pos1, pos2, h2, n_tiles * te)
    ys = _experts(tile_expert, n_used, xs, w_eg, w_eu, w_ed)
    y_p, y_s = _combine(pos1, pos2, x1, route, norm_f_g.reshape(1, d), ys, n_p)

    y_prompt = y_p.reshape(batch, seq, d)
    y_sample = y_s.reshape(dec_batch, dec_seq, d)
    k_p = k_p.reshape(1, batch, seq, N_KV_HEADS, HEAD_DIM)
    v_p = v_p.reshape(1, batch, seq, N_KV_HEADS, HEAD_DIM)
    ki_p = kw[:n_p, :IDX_DIM].reshape(1, batch, seq, IDX_DIM)
    pool_p = jnp.stack([pin[(b + 1) * seq - POOL_STATE:(b + 1) * seq] for b in range(batch)])[None]
    k_s = k_s.reshape(1, dec_batch, dec_seq, N_KV_HEADS, HEAD_DIM)
    v_s = v_s.reshape(1, dec_batch, dec_seq, N_KV_HEADS, HEAD_DIM)
    ki_s = kw[n_p:, :IDX_DIM].reshape(1, dec_batch, dec_seq, IDX_DIM)
    pool_s = pin[n_p:].reshape(dec_batch, dec_seq, POOL_WIDTH)[None, :, dec_seq - POOL_STATE:, :]
    return (y_prompt, y_sample, k_p, v_p, ki_p, pool_p, k_s, v_s, ki_s, pool_s)


def kernel(x_prompt, x_sample, cache_k, cache_v, cache_kidx, state_pool, norm1_g, w_in, w_gate, b_gate,
           w_pool_grp, pool_scale, w_o_attn, w_o_pool, w_out, norm2_g, w_router_group, b_router_group,
           w_router_expert, b_router_expert, w_exp_gate, w_exp_up, w_exp_down, norm_f_g):
    assert cache_k.shape[0] == 1, "single-layer model"
    return _forward(x_prompt, x_sample, cache_k[0], cache_v[0], cache_kidx[0], state_pool[0],
                    norm1_g[0], w_in[0], w_gate[0], b_gate[0], w_pool_grp[0], pool_scale[0],
                    w_o_attn[0], w_o_pool[0], w_out[0], norm2_g[0], w_router_group[0],
                    b_router_group[0], w_router_expert[0], b_router_expert[0],
                    w_exp_gate[0], w_exp_up[0], w_exp_down[0], norm_f_g)
```

```python
import functools

import numpy as np
import jax
import jax.numpy as jnp
from jax import lax
from jax.experimental import pallas as pl
from jax.experimental.pallas import tpu as pltpu

F32 = jnp.float32
BF16 = jnp.bfloat16
I32 = jnp.int32

D_MODEL = 2048
CHUNK = 64
N_HEADS = 8
N_KV_HEADS = 2
HEAD_DIM = 128
GQA_GROUP = N_HEADS // N_KV_HEADS
ATTN_WIDTH = N_HEADS * HEAD_DIM
KV_WIDTH = N_KV_HEADS * HEAD_DIM
IDX_HEADS = 16
IDX_DIM = 64
IDX_PAIRS = IDX_HEADS // 2
TOPK_MAX = 256
POOL_WINDOWS = (2, 4, 8, 16)
POOL_GROUPS = 4
POOL_WIDTH = D_MODEL // 2
POOL_GROUP_DIM = POOL_WIDTH // POOL_GROUPS
POOL_STATE = 15
POOL_HALO = 16
MOE_GROUPS = 4
EXPERTS_PER_GROUP = 8
N_EXPERTS = MOE_GROUPS * EXPERTS_PER_GROUP
D_EXPERT = 512
ROPE_THETA = 10000.0
CHUNK_SHIFT = CHUNK.bit_length() - 1
GROUP_SHIFT = EXPERTS_PER_GROUP.bit_length() - 1
EPS = 1e-6

LANES = 128
INT_MIN = -(2 ** 31)
NEG_BIG = -1e30
LOG2_E = 1.4426950408889634
VMEM_LIMIT = 56 * 1024 * 1024

KEY_BLOCK = 256
TOK_TILE = 256
EXP_TILE = 256


def _cparams(n_axes):
    return pltpu.CompilerParams(dimension_semantics=("arbitrary",) * n_axes,
                                vmem_limit_bytes=VMEM_LIMIT)


def _dot(a, b):
    return jnp.dot(a, b, preferred_element_type=F32)


def _dot_nt(a, b):
    return lax.dot_general(a, b, (((1,), (1,)), ((), ())), preferred_element_type=F32)


_C_Q = 0
_C_K = _C_Q + ATTN_WIDTH
_C_V = _C_K + KV_WIDTH
_C_QI = _C_V + KV_WIDTH
_C_KW = _C_QI + IDX_HEADS * IDX_DIM
_C_P = _C_KW + IDX_DIM + IDX_HEADS


def _rope128(y, cos, sin_signed):
    return y * cos + pltpu.roll(y, HEAD_DIM // 2, 1) * sin_signed


def _rope64(y, cos, sin_a, sin_b):
    half = IDX_DIM // 2
    return y * cos + pltpu.roll(y, LANES - half, 1) * sin_a + pltpu.roll(y, half, 1) * sin_b


def _inproj_kernel(xp_ref, xs_ref, g_ref, w_ref, wkw_ref, wp_ref, ck_ref, sk_ref, cq_ref, sq_ref, ci_ref,
                   sia_ref, sib_ref, cw_ref, swa_ref, swb_ref,
                   h_ref, q_ref, kp_ref, ks_ref, vp_ref, vs_ref, qi_ref, kw_ref, p_ref, *, n_prompt_tiles):
    is_prompt = pl.program_id(0) < n_prompt_tiles
    x = jnp.where(is_prompt, xp_ref[...], xs_ref[...])
    r = lax.rsqrt(jnp.mean(x * x, axis=-1, keepdims=True) + EPS)
    h = ((x * r) * g_ref[...]).astype(BF16)
    h_ref[...] = h
    cq, sq = cq_ref[...], sq_ref[...]
    for c in range(ATTN_WIDTH // 256):
        y = _dot(h, w_ref[:, _C_Q + c * 256:_C_Q + (c + 1) * 256])
        for s in range(2):
            q_ref[:, c * 256 + s * LANES:c * 256 + (s + 1) * LANES] = _rope128(
                y[:, s * LANES:(s + 1) * LANES], cq, sq).astype(BF16)
    ck, sk = ck_ref[...], sk_ref[...]
    y = _dot(h, w_ref[:, _C_K:_C_K + KV_WIDTH])
    k = jnp.concatenate([_rope128(y[:, s * LANES:(s + 1) * LANES], ck, sk) for s in range(N_KV_HEADS)], axis=1)
    v = _dot(h, w_ref[:, _C_V:_C_V + KV_WIDTH])

    @pl.when(is_prompt)
    def _():
        kp_ref[...] = k
        vp_ref[...] = v

    @pl.when(jnp.logical_not(is_prompt))
    def _():
        ks_ref[...] = k
        vs_ref[...] = v

    ci, sia, sib = ci_ref[...], sia_ref[...], sib_ref[...]
    for c in range(IDX_PAIRS // 2):
        y = _dot(h, w_ref[:, _C_QI + c * 256:_C_QI + (c + 1) * 256])
        for s in range(2):
            qi_ref[2 * c + s] = _rope64(y[:, s * LANES:(s + 1) * LANES], ci, sia, sib).astype(BF16)
    y = _dot(h, wkw_ref[...])
    kw_ref[...] = _rope64(y, cw_ref[...], swa_ref[...], swb_ref[...])
    for c in range(POOL_WIDTH // 256):
        p_ref[:, c * 256:(c + 1) * 256] = _dot(h, wp_ref[:, c * 256:(c + 1) * 256])


def _rope_tables(pos):
    pos = pos.astype(F32)[:, None]

    def cs(dim):
        half = dim // 2
        inv = ROPE_THETA ** (-jnp.arange(half, dtype=F32) / half)
        ang = pos * inv[None, :]
        return jnp.cos(ang), jnp.sin(ang)

    c, s = cs(HEAD_DIM)
    ck = jnp.concatenate([c, c], axis=1)
    sk = jnp.concatenate([-s, s], axis=1)
    qs = HEAD_DIM ** -0.5 * LOG2_E
    c, s = cs(IDX_DIM)
    z = jnp.zeros_like(s)
    ci = jnp.concatenate([c, c, c, c], axis=1)
    sia = jnp.concatenate([-s, z, -s, z], axis=1)
    sib = jnp.concatenate([z, s, z, s], axis=1)
    iscale = IDX_DIM ** -0.5
    n = pos.shape[0]
    wscale = jnp.full((n, IDX_HEADS), IDX_HEADS ** -0.5, F32)
    pad = jnp.zeros((n, LANES - IDX_DIM - IDX_HEADS), F32)
    zz = jnp.zeros((n, LANES - IDX_DIM), F32)
    cw = jnp.concatenate([c, c, wscale, pad], axis=1)
    swa = jnp.concatenate([-s, z, zz], axis=1)
    swb = jnp.concatenate([z, s, zz], axis=1)
    return (ck, sk, ck * qs, sk * qs, ci * iscale, sia * iscale, sib * iscale, cw, swa, swb)


def _split_token_specs(tm, width, n_prompt_tiles):
    prompt = pl.BlockSpec((tm, width), lambda i: (jnp.minimum(i, n_prompt_tiles - 1), 0))
    sample = pl.BlockSpec((tm, width), lambda i: (jnp.maximum(i - n_prompt_tiles, 0), 0))
    return prompt, sample


def _in_proj(xp, xs, g1, w_main, w_kw, w_pool, tables, tiles_per_seq):
    tm = TOK_TILE
    n_prompt_tiles = xp.shape[0] // tm
    n = xp.shape[0] + xs.shape[0]
    n_tiles = n // tm

    def tab_idx(i):
        return (jnp.where(i < n_prompt_tiles, i % tiles_per_seq, tiles_per_seq + i - n_prompt_tiles), 0)

    tok = lambda w: pl.BlockSpec((tm, w), lambda i: (i, 0))
    tab = pl.BlockSpec((tm, LANES), tab_idx)
    resident = lambda a: pl.BlockSpec(a.shape, lambda i: (0, 0), pipeline_mode=pl.Buffered(1))
    n_s = xs.shape[0]
    kv_p, kv_s = _split_token_specs(tm, KV_WIDTH, n_prompt_tiles)
    out_shape = (
        jax.ShapeDtypeStruct((n, D_MODEL), BF16),
        jax.ShapeDtypeStruct((n, ATTN_WIDTH), BF16),
        jax.ShapeDtypeStruct((n - n_s, KV_WIDTH), F32),
        jax.ShapeDtypeStruct((n_s, KV_WIDTH), F32),
        jax.ShapeDtypeStruct((n - n_s, KV_WIDTH), F32),
        jax.ShapeDtypeStruct((n_s, KV_WIDTH), F32),
        jax.ShapeDtypeStruct((IDX_PAIRS, n, LANES), BF16),
        jax.ShapeDtypeStruct((n, LANES), F32),
        jax.ShapeDtypeStruct((n, POOL_WIDTH), F32),
    )
    return pl.pallas_call(
        functools.partial(_inproj_kernel, n_prompt_tiles=n_prompt_tiles),
        grid=(n_tiles,),
        in_specs=[*_split_token_specs(tm, D_MODEL, n_prompt_tiles),
                  pl.BlockSpec((1, D_MODEL), lambda i: (0, 0)),
                  resident(w_main), resident(w_kw), resident(w_pool)]
                 + [tab] * 10,
        out_specs=(tok(D_MODEL), tok(ATTN_WIDTH), kv_p, kv_s, kv_p, kv_s,
                   pl.BlockSpec((IDX_PAIRS, tm, LANES), lambda i: (0, i, 0)),
                   tok(LANES), tok(POOL_WIDTH)),
        out_shape=out_shape,
        compiler_params=_cparams(1),
        name="in_proj",
    )(xp, xs, g1, w_main, w_kw, w_pool, *tables)


def _ordinal_to_f32(k):
    return lax.bitcast_convert_type(k ^ ((k >> 31) & 0x7FFFFFFF), F32)


def _kth_largest_score(topk, shape, n_key_rows, count_where):
    def count_ge(cand):
        cand_f = _ordinal_to_f32(cand)
        return count_where(lambda keys, kb: keys >= jnp.broadcast_to(cand_f, keys.shape))

    c0 = count_ge(jnp.zeros(shape, I32))
    thr0 = jnp.where(c0 >= topk, 0, INT_MIN).astype(I32)
    cnt0 = jnp.where(c0 >= topk, c0, n_key_rows).astype(I32)

    def bit_step(i, carry):
        thr, cnt = carry
        cand = thr + (jnp.int32(1) << (30 - i))
        c = count_ge(cand)
        ok = c >= topk
        return jnp.where(ok, cand, thr), jnp.where(ok, c, cnt)

    thr, cnt = lax.fori_loop(0, 31, bit_step, (thr0, cnt0))
    has_k = thr > INT_MIN
    thr_f = jnp.where(has_k, _ordinal_to_f32(thr), jnp.finfo(F32).min)
    return thr_f, has_k & (cnt > topk)


def _drop_excess_ties(sc, n_blocks, topk, thr, need_tie, key_index, count_where):
    n_gt = count_where(lambda keys, kb: keys > jnp.broadcast_to(thr, keys.shape))
    n_keep = topk - n_gt

    def idx_step(i, bound):
        cand = bound + (jnp.int32(1) << (14 - i))
        c = count_where(lambda keys, kb: (keys == jnp.broadcast_to(thr, keys.shape))
                        & (key_index(kb) < jnp.broadcast_to(cand, keys.shape)))
        return jnp.where(c <= n_keep, cand, bound)

    bound = lax.fori_loop(0, 15, idx_step, jnp.zeros(thr.shape, I32))
    bound = jnp.where(need_tie, bound, jnp.int32(2 ** 30))

    def drop_block(kb, carry):
        keys = sc[kb]
        drop = (keys == jnp.broadcast_to(thr, keys.shape)) & (key_index(kb) >= jnp.broadcast_to(bound, keys.shape))
        sc[kb] = jnp.where(drop, -jnp.inf, keys)
        return carry

    lax.fori_loop(0, n_blocks, drop_block, 0)


def _stack_queries_t(q_ref, qt):
    for g in range(N_KV_HEADS):
        qg = jnp.concatenate([q_ref[:, (g * GQA_GROUP + j) * HEAD_DIM:(g * GQA_GROUP + j + 1) * HEAD_DIM]
                              for j in range(GQA_GROUP)], axis=0)
        qt[g] = qg.astype(F32).T.astype(BF16)


def _attention_scratch(rows):
    return [
        pltpu.VMEM((N_KV_HEADS, 1, rows), F32),
        pltpu.VMEM((N_KV_HEADS, 1, rows), F32),
        pltpu.VMEM((N_KV_HEADS, HEAD_DIM, rows), F32),
        pltpu.VMEM((2, N_KV_HEADS, KEY_BLOCK, rows), F32),
        pltpu.VMEM((2, N_KV_HEADS, KEY_BLOCK, rows), BF16),
        pltpu.VMEM((2, N_KV_HEADS, 1, rows), F32),
    ]


def _masked_attention(o_ref, n_blocks, n_blocks_all, bias_t, k16, vt16, qt, att, tq):
    m_s, l_s, acc_s, lg_s, p_s, alpha_s = att
    m_s[...] = jnp.full(m_s.shape, NEG_BIG, F32)
    l_s[...] = jnp.zeros(l_s.shape, F32)
    acc_s[...] = jnp.zeros(acc_s.shape, F32)
    p_s[1] = jnp.zeros(p_s.shape[1:], BF16)
    alpha_s[1] = jnp.ones(alpha_s.shape[1:], F32)

    def logits_of(kb, slot):
        r0 = pl.multiple_of(kb * KEY_BLOCK, KEY_BLOCK)
        for g in range(N_KV_HEADS):
            lg_s[slot, g] = _dot(k16[pl.ds(r0, KEY_BLOCK), g * HEAD_DIM:(g + 1) * HEAD_DIM], qt[g])

    def value_product(kb, slot):
        for g in range(N_KV_HEADS):
            acc_s[g] = (alpha_s[slot, g] * acc_s[g]
                        + _dot(vt16[kb, g * HEAD_DIM:(g + 1) * HEAD_DIM, :], p_s[slot, g]))

    logits_of(0, 0)
    last = n_blocks_all - 1

    def step(kb, slot):
        logits_of(jnp.minimum(kb + 1, last), 1 - slot)
        bias = jnp.where(kb < n_blocks, bias_t(jnp.minimum(kb, last)), NEG_BIG)
        for g in range(N_KV_HEADS):
            logits = lg_s[slot, g] + bias
            m_prev = m_s[g]
            m_new = jnp.maximum(m_prev, jnp.max(logits, axis=0, keepdims=True))
            p = jnp.exp2(logits - m_new)
            alpha = jnp.exp2(m_prev - m_new)
            l_s[g] = alpha * l_s[g] + jnp.sum(p, axis=0, keepdims=True)
            p_s[slot, g] = p.astype(BF16)
            alpha_s[slot, g] = alpha
            m_s[g] = m_new
        value_product(jnp.clip(kb - 1, 0, last), 1 - slot)

    def two_steps(i, carry):
        step(2 * i, 0)
        step(2 * i + 1, 1)
        return carry

    n_pairs = (n_blocks + 1) // 2
    lax.fori_loop(0, n_pairs, two_steps, 0)
    value_product(jnp.minimum(2 * n_pairs - 1, last), 1)

    for g in range(N_KV_HEADS):
        o = (acc_s[g] * (1.0 / l_s[g])).T
        for j in range(GQA_GROUP):
            hd = g * GQA_GROUP + j
            o_ref[:, hd * HEAD_DIM:(hd + 1) * HEAD_DIM] = o[j * tq:(j + 1) * tq, :].astype(BF16)


def _dsa_prompt_kernel(q_ref, qi_ref, wq_ref, kn_ref, vn_ref, kwn_ref, o_ref,
                       k16, vt16, kia, kib, sc, raw, qt, *att, tq, seq, topk):
    qb = pl.program_id(1)
    n_blocks_all = seq // KEY_BLOCK

    @pl.when(qb == 0)
    def _():
        lane = lax.broadcasted_iota(I32, (seq, LANES), 1)
        kin = jnp.where(lane < IDX_DIM, kwn_ref[...], 0.0)
        k16[...] = kn_ref[...].astype(BF16)
        kia[...] = kin.astype(BF16)
        kib[...] = pltpu.roll(kin, IDX_DIM, 1).astype(BF16)
        for kb in range(n_blocks_all):
            vt16[kb] = vn_ref[kb * KEY_BLOCK:(kb + 1) * KEY_BLOCK, :].T.astype(BF16)

    n_blocks = qb // (KEY_BLOCK // tq) + 1

    qit = qi_ref[...].reshape(IDX_PAIRS * tq, LANES).astype(F32).T.astype(BF16)
    wt = wq_ref[...].T
    w_rows = [wt[IDX_DIM + j:IDX_DIM + j + 1, :] for j in range(IDX_HEADS)]
    _stack_queries_t(q_ref, qt)

    key_row = lax.broadcasted_iota(I32, (KEY_BLOCK, tq), 0)
    q_pos = qb * tq + lax.broadcasted_iota(I32, (KEY_BLOCK, tq), 1)

    last = n_blocks_all - 1
    n_pairs = (n_blocks + 1) // 2

    def raw_scores(kb, slot):
        r0 = pl.multiple_of(kb * KEY_BLOCK, KEY_BLOCK)
        raw[slot, 0] = _dot(kia[pl.ds(r0, KEY_BLOCK), :], qit)
        raw[slot, 1] = _dot(kib[pl.ds(r0, KEY_BLOCK), :], qit)

    def score_step(kb, slot):
        raw_scores(jnp.minimum(kb + 1, last), 1 - slot)
        se = raw[slot, 0]
        so = raw[slot, 1]
        score = jnp.zeros((KEY_BLOCK, tq), F32)
        for p in range(IDX_PAIRS):
            score = (score + jnp.maximum(se[:, p * tq:(p + 1) * tq], 0.0) * w_rows[2 * p]
                     + jnp.maximum(so[:, p * tq:(p + 1) * tq], 0.0) * w_rows[2 * p + 1])
        adm = ((kb * KEY_BLOCK + key_row) >> CHUNK_SHIFT) <= (q_pos >> CHUNK_SHIFT)
        sc[jnp.where(kb < n_blocks, kb, n_blocks_all)] = jnp.where(adm, score, -jnp.inf)

    def two_score_steps(i, carry):
        score_step(2 * i, 0)
        score_step(2 * i + 1, 1)
        return carry

    raw_scores(0, 0)
    lax.fori_loop(0, n_pairs, two_score_steps, 0)

    def count_where(pred):
        def body(i, acc):
            for s in range(2):
                kb = 2 * i + s
                kbc = jnp.minimum(kb, last)
                m = pred(sc[kbc], kbc).astype(I32)
                part = jnp.sum(m.reshape(KEY_BLOCK // 8, 8, tq), axis=0)
                acc = acc + jnp.where(kb < n_blocks, part, 0)
            return acc
        acc = lax.fori_loop(0, n_pairs, body, jnp.zeros((8, tq), I32))
        return jnp.sum(acc, axis=0, keepdims=True)

    thr, need_tie = _kth_largest_score(topk, (1, tq), seq, count_where)

    @pl.when(jnp.max(need_tie.astype(I32)) > 0)
    def _():
        _drop_excess_ties(sc, n_blocks, topk, thr, need_tie,
                          lambda kb: kb * KEY_BLOCK + key_row, count_where)

    thr_b = jnp.broadcast_to(thr, (KEY_BLOCK, tq))

    def bias_t(kb):
        bias = jnp.where(sc[kb] >= thr_b, 0.0, NEG_BIG)
        return jnp.concatenate([bias] * GQA_GROUP, axis=1)

    _masked_attention(o_ref, n_blocks, n_blocks_all, bias_t, k16, vt16, qt, att, tq)


def _dsa_sample_kernel(q_ref, qi_ref, wq_ref, kn_ref, vn_ref, kwn_ref, kc_ref, vc_ref, kic_ref, o_ref,
                       k16, vt16, kia, kib, wbe, wbo, sc, qt, *att,
                       tq, n_key_rows, n_cache, n_new, topk):
    n_blocks = n_key_rows // KEY_BLOCK
    n_cache_blocks = n_cache // KEY_BLOCK

    lane = lax.broadcasted_iota(I32, (n_new, LANES), 1)
    kin = jnp.where(lane < IDX_DIM, kwn_ref[...], 0.0).astype(BF16)
    n_tail = n_key_rows - n_cache
    k16[0:n_cache, :] = kc_ref[...].astype(BF16)
    for kb in range(n_cache_blocks):
        vt16[kb] = vc_ref[kb * KEY_BLOCK:(kb + 1) * KEY_BLOCK, :].T.astype(BF16)
    v_tail = jnp.concatenate([vn_ref[...], jnp.zeros((n_tail - n_new, KV_WIDTH), F32)], axis=0)
    vt16[n_cache_blocks] = v_tail.T.astype(BF16)
    kic = jnp.concatenate([kic_ref[...], jnp.zeros((n_cache, LANES - IDX_DIM), F32)], axis=1)
    kia[0:n_cache, :] = kic.astype(BF16)
    kib[0:n_cache, :] = pltpu.roll(kic, IDX_DIM, 1).astype(BF16)
    k16[n_cache:n_key_rows, :] = jnp.zeros((n_tail, KV_WIDTH), BF16)
    kia[n_cache:n_key_rows, :] = jnp.zeros((n_tail, LANES), BF16)
    kib[n_cache:n_key_rows, :] = jnp.zeros((n_tail, LANES), BF16)
    k16[n_cache:n_cache + n_new, :] = kn_ref[...].astype(BF16)
    kia[n_cache:n_cache + n_new, :] = kin
    kib[n_cache:n_cache + n_new, :] = pltpu.roll(kin.astype(F32), IDX_DIM, 1).astype(BF16)

    wq = wq_ref[...]
    for j in range(IDX_HEADS):
        col = jnp.broadcast_to(wq[:, IDX_DIM + j:IDX_DIM + j + 1], (tq, LANES))
        if j % 2 == 0:
            wbe[j // 2] = col
        else:
            wbo[j // 2] = col
    _stack_queries_t(q_ref, qt)

    lane = lax.broadcasted_iota(I32, (tq, KEY_BLOCK), 1)

    qi2 = qi_ref[...].reshape(IDX_PAIRS * tq, LANES)

    def score_block(kb, carry):
        r0 = pl.multiple_of(kb * KEY_BLOCK, KEY_BLOCK)
        se = _dot_nt(qi2, kia[pl.ds(r0, KEY_BLOCK), :]).reshape(IDX_PAIRS, tq, KEY_BLOCK)
        so = _dot_nt(qi2, kib[pl.ds(r0, KEY_BLOCK), :]).reshape(IDX_PAIRS, tq, KEY_BLOCK)
        score = jnp.zeros((tq, KEY_BLOCK), F32)
        for p in range(IDX_PAIRS):
            we = wbe[p]
            wo = wbo[p]
            we2 = jnp.concatenate([we, we], axis=1)
            wo2 = jnp.concatenate([wo, wo], axis=1)
            score = score + jnp.maximum(se[p], 0.0) * we2 + jnp.maximum(so[p], 0.0) * wo2
        adm = (kb * KEY_BLOCK + lane) < (n_cache + n_new)
        sc[kb] = jnp.where(adm, score, -jnp.inf)
        return carry

    lax.fori_loop(0, n_blocks, score_block, 0)

    def count_where(pred):
        def body(kb, acc):
            m = pred(sc[kb], kb).astype(I32)
            return acc + m[:, :LANES] + m[:, LANES:]
        acc = lax.fori_loop(0, n_blocks, body, jnp.zeros((tq, LANES), I32))
        return jnp.sum(acc, axis=1, keepdims=True)

    thr, need_tie = _kth_largest_score(topk, (tq, 1), n_key_rows, count_where)

    @pl.when(jnp.max(need_tie.astype(I32)) > 0)
    def _():
        _drop_excess_ties(sc, n_blocks, topk, thr, need_tie,
                          lambda kb: kb * KEY_BLOCK + lane, count_where)

    thr_b = jnp.broadcast_to(thr, (tq, KEY_BLOCK))

    def bias_t(kb):
        bias = jnp.where(sc[kb] >= thr_b, 0.0, NEG_BIG)
        return jnp.concatenate([bias] * GQA_GROUP, axis=0).T

    _masked_attention(o_ref, n_blocks, n_blocks, bias_t, k16, vt16, qt, att, tq)


def _dsa_sample_scratch(tq, n_key_rows):
    rows = GQA_GROUP * tq
    return [
        pltpu.VMEM((n_key_rows, KV_WIDTH), BF16),
        pltpu.VMEM((n_key_rows // KEY_BLOCK, KV_WIDTH, KEY_BLOCK), BF16),
        pltpu.VMEM((n_key_rows, LANES), BF16),
        pltpu.VMEM((n_key_rows, LANES), BF16),
        pltpu.VMEM((IDX_PAIRS, tq, LANES), F32),
        pltpu.VMEM((IDX_PAIRS, tq, LANES), F32),
        pltpu.VMEM((n_key_rows // KEY_BLOCK, tq, KEY_BLOCK), F32),
        pltpu.VMEM((N_KV_HEADS, HEAD_DIM, rows), BF16),
    ] + _attention_scratch(rows)


def _dsa_prompt(q, qi, kw, k, v, batch, seq):
    tq = 128
    nqb = seq // tq
    topk = min(TOPK_MAX, seq // 4)
    kern = functools.partial(_dsa_prompt_kernel, tq=tq, seq=seq, topk=topk)
    rows_q = GQA_GROUP * tq
    scratch = [
        pltpu.VMEM((seq, KV_WIDTH), BF16),
        pltpu.VMEM((seq // KEY_BLOCK, KV_WIDTH, KEY_BLOCK), BF16),
        pltpu.VMEM((seq, LANES), BF16),
        pltpu.VMEM((seq, LANES), BF16),
        pltpu.VMEM((seq // KEY_BLOCK + 1, KEY_BLOCK, tq), F32),
        pltpu.VMEM((2, 2, KEY_BLOCK, IDX_PAIRS * tq), F32),
        pltpu.VMEM((N_KV_HEADS, HEAD_DIM, rows_q), BF16),
    ] + _attention_scratch(rows_q)
    qmap = lambda b, i: (b * nqb + i, 0)
    smap = lambda b, i: (b, 0)
    return pl.pallas_call(
        kern,
        grid=(batch, nqb),
        in_specs=[pl.BlockSpec((tq, ATTN_WIDTH), qmap),
                  pl.BlockSpec((IDX_PAIRS, tq, LANES), lambda b, i: (0, b * nqb + i, 0)),
                  pl.BlockSpec((tq, LANES), qmap),
                  pl.BlockSpec((seq, KV_WIDTH), smap),
                  pl.BlockSpec((seq, KV_WIDTH), smap),
                  pl.BlockSpec((seq, LANES), smap)],
        out_specs=pl.BlockSpec((tq, ATTN_WIDTH), qmap),
        out_shape=jax.ShapeDtypeStruct((batch * seq, ATTN_WIDTH), BF16),
        scratch_shapes=scratch,
        compiler_params=_cparams(2),
        name="dsa_prompt",
    )(q, qi, kw, k, v, kw)


def _dsa_sample(q, qi, kw, k, v, cache_k, cache_v, cache_ki, row0, batch, seq, past):
    tq = seq
    n_key_rows = past + KEY_BLOCK
    topk = min(TOPK_MAX, (past + seq) // 4)
    kern = functools.partial(_dsa_sample_kernel, tq=tq, n_key_rows=n_key_rows, n_cache=past, n_new=seq,
                             topk=topk)
    t0 = row0 // seq
    qmap = lambda b, i: (t0 + b, 0)
    cmap = lambda b, i: (b, 0)
    return pl.pallas_call(
        kern,
        grid=(batch, 1),
        in_specs=[pl.BlockSpec((tq, ATTN_WIDTH), qmap),
                  pl.BlockSpec((IDX_PAIRS, tq, LANES), lambda b, i: (0, t0 + b, 0)),
                  pl.BlockSpec((tq, LANES), qmap),
                  pl.BlockSpec((tq, KV_WIDTH), cmap),
                  pl.BlockSpec((tq, KV_WIDTH), cmap),
                  pl.BlockSpec((tq, LANES), qmap),
                  pl.BlockSpec((past, KV_WIDTH), cmap),
                  pl.BlockSpec((past, KV_WIDTH), cmap),
                  pl.BlockSpec((past, IDX_DIM), cmap)],
        out_specs=pl.BlockSpec((tq, ATTN_WIDTH), cmap),
        out_shape=jax.ShapeDtypeStruct((batch * seq, ATTN_WIDTH), BF16),
        scratch_shapes=_dsa_sample_scratch(tq, n_key_rows),
        compiler_params=_cparams(2),
        name="dsa_sample",
    )(q, qi, kw, k, v, kw, cache_k, cache_v, cache_ki)


def _pool_kernel(*refs, tp, pos0, carried):
    if carried:
        p_ref, w_ref, s_ref, o_ref, ext, save = refs
    else:
        p_ref, halo_ref, w_ref, s_ref, o_ref, ext = refs
    i = pl.program_id(1)
    if carried:
        @pl.when(i == 0)
        def _():
            ext[0:POOL_HALO, :] = jnp.zeros((POOL_HALO, POOL_WIDTH), F32)

        @pl.when(i > 0)
        def _():
            ext[0:POOL_HALO, :] = save[...]
    else:
        ext[0:POOL_HALO, :] = halo_ref[0]
    x = p_ref[...]
    ext[POOL_HALO:POOL_HALO + tp, :] = x
    if carried:
        save[...] = x[tp - POOL_HALO:tp, :]
    pos = pos0 + i * tp + lax.broadcasted_iota(I32, (tp, 1), 0)
    for g, w in enumerate(POOL_WINDOWS):
        c0, c1 = g * POOL_GROUP_DIM, (g + 1) * POOL_GROUP_DIM
        xg = x[:, c0:c1]
        tot = xg
        for s in range(1, w):
            tot = tot + ext[POOL_HALO - s:POOL_HALO - s + tp, c0:c1]
        inv = 1.0 / jnp.minimum(pos + 1, w).astype(F32)
        d = (tot * inv - xg).astype(BF16)
        o_ref[:, c0:c1] = (_dot(d, w_ref[g]) * s_ref[:, c0:c1]).astype(BF16)


def _pool(pin, w_grp, scale, tile0, batch, seq, pos0, halo=None):
    carried = halo is None
    tp = min(seq, 256)
    nt = seq // tp
    kern = functools.partial(_pool_kernel, tp=tp, pos0=pos0, carried=carried)
    in_specs = [pl.BlockSpec((tp, POOL_WIDTH), lambda b, i: (tile0 + b * nt + i, 0))]
    args = [pin]
    if not carried:
        in_specs.append(pl.BlockSpec((1, POOL_HALO, POOL_WIDTH), lambda b, i: (b, 0, 0)))
        args.append(halo)
    in_specs += [pl.BlockSpec((POOL_GROUPS, POOL_GROUP_DIM, POOL_GROUP_DIM), lambda b, i: (0, 0, 0)),
                 pl.BlockSpec((1, POOL_WIDTH), lambda b, i: (0, 0))]
    args += [w_grp, scale]
    scratch = [pltpu.VMEM((POOL_HALO + tp, POOL_WIDTH), F32)]
    if carried:
        scratch.append(pltpu.VMEM((POOL_HALO, POOL_WIDTH), F32))
    return pl.pallas_call(
        kern,
        grid=(batch, nt),
        in_specs=in_specs,
        out_specs=pl.BlockSpec((tp, POOL_WIDTH), lambda b, i: (b * nt + i, 0)),
        out_shape=jax.ShapeDtypeStruct((batch * seq, POOL_WIDTH), BF16),
        scratch_shapes=scratch,
        compiler_params=_cparams(2),
        name="pool_prompt" if carried else "pool_sample",
    )(*args)


def _merge_kernel(h_ref, oap_ref, oas_ref, opp_ref, ops_ref, wga_ref, wgp_ref, ba_ref, bp_ref, woa_ref,
                  wop_ref, m_ref, *, n_prompt_tiles):
    h = h_ref[...]
    is_prompt = pl.program_id(1) < n_prompt_tiles
    oa = jnp.where(is_prompt, oap_ref[...], oas_ref[...])
    op = jnp.where(is_prompt, opp_ref[...], ops_ref[...])
    ga = jax.nn.sigmoid(_dot(h, wga_ref[...]) + ba_ref[...])
    gp = jax.nn.sigmoid(_dot(h, wgp_ref[...]) + bp_ref[...])
    a = _dot(oa, woa_ref[...])
    p = _dot(op, wop_ref[...])
    m_ref[...] = (ga * a + gp * p).astype(BF16)


def _merge(h, oa_p, oa_s, op_p, op_s, w_gate, b_gate, w_oa, w_op):
    n = h.shape[0]
    tm = 2 * TOK_TILE
    tn = D_MODEL // 2
    nj = D_MODEL // tn
    npt = oa_p.shape[0] // tm
    prompt = lambda w: pl.BlockSpec((tm, w), lambda j, i: (jnp.minimum(i, npt - 1), 0))
    sample = lambda w: pl.BlockSpec((tm, w), lambda j, i: (jnp.maximum(i - npt, 0), 0))
    return pl.pallas_call(
        functools.partial(_merge_kernel, n_prompt_tiles=npt),
        grid=(nj, n // tm),
        in_specs=[pl.BlockSpec((tm, D_MODEL), lambda j, i: (i, 0)),
                  prompt(ATTN_WIDTH), sample(ATTN_WIDTH), prompt(POOL_WIDTH), sample(POOL_WIDTH),
                  pl.BlockSpec((D_MODEL, tn), lambda j, i: (0, j)),
                  pl.BlockSpec((D_MODEL, tn), lambda j, i: (0, nj + j)),
                  pl.BlockSpec((1, tn), lambda j, i: (0, j)),
                  pl.BlockSpec((1, tn), lambda j, i: (0, nj + j)),
                  pl.BlockSpec((ATTN_WIDTH, tn), lambda j, i: (0, j)),
                  pl.BlockSpec((POOL_WIDTH, tn), lambda j, i: (0, j))],
        out_specs=pl.BlockSpec((tm, tn), lambda j, i: (i, j)),
        out_shape=jax.ShapeDtypeStruct((n, D_MODEL), BF16),
        compiler_params=_cparams(2),
        name="merge",
    )(h, oa_p, oa_s, op_p, op_s, w_gate, w_gate, b_gate, b_gate, w_oa, w_op)


_R_E1, _R_E2, _R_W1, _R_W2, _R_RANK1, _R_RANK2 = range(6)
_R_LOGIT0 = MOE_GROUPS


def _outproj_kernel(m_ref, xp_ref, xs_ref, wo_ref, g2_ref, wr_ref, br_ref,
                    x1_ref, h2_ref, route_ref, route_t_ref, counts_ref, run, *, n_prompt_tiles):
    i = pl.program_id(0)
    tm = m_ref.shape[0]

    @pl.when(i == 0)
    def _():
        run[...] = jnp.zeros(run.shape, F32)

    x = jnp.where(i < n_prompt_tiles, xp_ref[...], xs_ref[...])
    x1 = x + _dot(m_ref[...], wo_ref[...])
    x1_ref[...] = x1
    r = lax.rsqrt(jnp.mean(x1 * x1, axis=-1, keepdims=True) + EPS)
    hf = (x1 * r) * g2_ref[...]
    h2_ref[...] = hf

    hi = hf.astype(BF16)
    lo = (hf - hi.astype(F32)).astype(BF16)
    hw = _dot(hi, wr_ref[...])
    lg = hw[:, :LANES] + (hw[:, LANES:] + _dot(lo, wr_ref[:, :LANES])) + br_ref[...]

    lane = lax.broadcasted_iota(I32, (tm, LANES), 1)
    neg_inf = jnp.float32(-jnp.inf)
    is_g = lane < MOE_GROUPS
    cl = jnp.where(is_g, lg, neg_inf)
    gmax = jnp.max(cl, axis=1, keepdims=True)
    g_sel = jnp.min(jnp.where(cl == gmax, lane, LANES), axis=1, keepdims=True)
    den = jnp.sum(jnp.where(is_g, jnp.exp(cl - gmax), 0.0), axis=1, keepdims=True)
    g_val = 1.0 / den
    e_lane = lane - _R_LOGIT0
    in_grp = (e_lane >= 0) & (e_lane < N_EXPERTS) & ((e_lane >> GROUP_SHIFT) == g_sel)
    f1 = jnp.where(in_grp, lg, neg_inf)
    v1 = jnp.max(f1, axis=1, keepdims=True)
    i1 = jnp.min(jnp.where(f1 == v1, lane, LANES), axis=1, keepdims=True)
    f2 = jnp.where(lane == i1, neg_inf, f1)
    v2 = jnp.max(f2, axis=1, keepdims=True)
    i2 = jnp.min(jnp.where(f2 == v2, lane, LANES), axis=1, keepdims=True)
    e2 = jnp.exp(v2 - v1)
    w1 = g_val / (1.0 + e2)
    w2 = g_val * e2 / (1.0 + e2)

    oh1 = lane == i1
    oh2 = lane == i2
    c = (oh1 | oh2).astype(BF16)
    rr = lax.broadcasted_iota(I32, (tm, tm), 0)
    cc = lax.broadcasted_iota(I32, (tm, tm), 1)
    before = (cc < rr).astype(BF16)
    prior = _dot(before, c) + run[...]
    rank1 = jnp.sum(jnp.where(oh1, prior, 0.0), axis=1, keepdims=True)
    rank2 = jnp.sum(jnp.where(oh2, prior, 0.0), axis=1, keepdims=True)
    run[...] = run[...] + jnp.sum(c.astype(F32), axis=0, keepdims=True)
    counts_ref[...] = jnp.broadcast_to(run[...], counts_ref.shape)

    rec = jnp.zeros((tm, LANES), F32)
    for ln, val in ((_R_E1, (i1 - _R_LOGIT0).astype(F32)), (_R_E2, (i2 - _R_LOGIT0).astype(F32)),
                    (_R_W1, w1), (_R_W2, w2), (_R_RANK1, rank1), (_R_RANK2, rank2)):
        rec = jnp.where(lane == ln, val, rec)
    route_ref[...] = rec
    route_t_ref[...] = rec.T[0:route_t_ref.shape[0], :]


def _out_proj(m, xp, xs, w_out, g2, wr_split, b_r):
    n = m.shape[0]
    tm = TOK_TILE
    n_prompt_tiles = xp.shape[0] // tm
    tok = lambda w: pl.BlockSpec((tm, w), lambda i: (i, 0))
    const = lambda s: pl.BlockSpec(s, lambda i: (0,) * len(s))
    return pl.pallas_call(
        functools.partial(_outproj_kernel, n_prompt_tiles=n_prompt_tiles),
        grid=(n // tm,),
        in_specs=[tok(D_MODEL), *_split_token_specs(tm, D_MODEL, n_prompt_tiles),
                  pl.BlockSpec((D_MODEL, D_MODEL), lambda i: (0, 0), pipeline_mode=pl.Buffered(1)),
                  const((1, D_MODEL)), const((D_MODEL, 2 * LANES)), const((1, LANES))],
        out_specs=(tok(D_MODEL), tok(D_MODEL), tok(LANES), pl.BlockSpec((8, tm), lambda i: (0, i)),
                   const((8, LANES))),
        out_shape=(jax.ShapeDtypeStruct((n, D_MODEL), F32),
                   jax.ShapeDtypeStruct((n, D_MODEL), F32),
                   jax.ShapeDtypeStruct((n, LANES), F32),
                   jax.ShapeDtypeStruct((8, n), F32),
                   jax.ShapeDtypeStruct((8, LANES), F32)),
        scratch_shapes=[pltpu.VMEM((1, LANES), F32)],
        compiler_params=_cparams(1),
        name="out_proj",
    )(m, xp, xs, w_out, g2, wr_split, b_r)


DISPATCH_TOKENS = 256


def _dispatch_kernel(pos1_ref, pos2_ref, h2_ref, xs_in_ref, xs_ref, sem):
    del xs_in_ref
    tb = DISPATCH_TOKENS
    dst = (pos1_ref, pos2_ref)

    def row_copy(r, slot):
        return pltpu.make_async_copy(h2_ref.at[pl.ds(r, 1)], xs_ref.at[pl.ds(dst[slot][r], 1)], sem)

    def issue(r, carry):
        row_copy(r, 0).start()
        row_copy(r, 1).start()
        return carry

    lax.fori_loop(0, tb, issue, 0, unroll=8)

    def drain(r, carry):
        row_copy(r, 0).wait()
        row_copy(r, 1).wait()
        return carry

    lax.fori_loop(0, tb, drain, 0, unroll=8)


def _dispatch(pos1, pos2, h2, n_rows):
    n = h2.shape[0]
    tb = DISPATCH_TOKENS
    xs0 = jnp.zeros((n_rows, D_MODEL), F32)
    return pl.pallas_call(
        _dispatch_kernel,
        grid=(n // tb,),
        in_specs=[pl.BlockSpec((tb,), lambda i: (i,), memory_space=pltpu.SMEM),
                  pl.BlockSpec((tb,), lambda i: (i,), memory_space=pltpu.SMEM),
                  pl.BlockSpec((tb, D_MODEL), lambda i: (i, 0)),
                  pl.BlockSpec(memory_space=pl.ANY)],
        out_specs=pl.BlockSpec(memory_space=pl.ANY),
        out_shape=jax.ShapeDtypeStruct((n_rows, D_MODEL), F32),
        scratch_shapes=[pltpu.SemaphoreType.DMA(())],
        input_output_aliases={3: 0},
        compiler_params=_cparams(1),
        name="dispatch",
    )(pos1, pos2, h2, xs0)


def _experts_kernel(te_ref, nu_ref, xs_ref, wg_ref, wu_ref, wd_ref, ys_ref, wg16, wu16, wd16):
    i = pl.program_id(0)
    prev = te_ref[jnp.maximum(i - 1, 0)]
    fresh = (i == 0) | (te_ref[i] != prev)
    used = i < nu_ref[0]

    @pl.when(fresh & used)
    def _():
        wg16[...] = wg_ref[0].astype(BF16)
        wu16[...] = wu_ref[0].astype(BF16)
        wd16[...] = wd_ref[0].astype(BF16)

    @pl.when(used)
    def _():
        x = xs_ref[...].astype(BF16)
        a = _dot(x, wg16[...])
        u = _dot(x, wu16[...])
        act = (a * jax.nn.sigmoid(a)) * u
        ys_ref[...] = _dot(act.astype(BF16), wd16[...])

    @pl.when(jnp.logical_not(used))
    def _():
        ys_ref[...] = jnp.zeros(ys_ref.shape, F32)


def _experts(tile_expert, n_used, xs, w_g, w_u, w_d):
    n_rows = xs.shape[0]
    tm = EXP_TILE
    grid_spec = pltpu.PrefetchScalarGridSpec(
        num_scalar_prefetch=2,
        grid=(n_rows // tm,),
        in_specs=[pl.BlockSpec((tm, D_MODEL), lambda i, te, nu: (i, 0)),
                  pl.BlockSpec((1, D_MODEL, D_EXPERT), lambda i, te, nu: (te[i], 0, 0)),
                  pl.BlockSpec((1, D_MODEL, D_EXPERT), lambda i, te, nu: (te[i], 0, 0)),
                  pl.BlockSpec((1, D_EXPERT, D_MODEL), lambda i, te, nu: (te[i], 0, 0))],
        out_specs=pl.BlockSpec((tm, D_MODEL), lambda i, te, nu: (i, 0)),
        scratch_shapes=[pltpu.VMEM((D_MODEL, D_EXPERT), BF16),
                        pltpu.VMEM((D_MODEL, D_EXPERT), BF16),
                        pltpu.VMEM((D_EXPERT, D_MODEL), BF16)],
    )
    return pl.pallas_call(
        _experts_kernel,
        grid_spec=grid_spec,
        out_shape=jax.ShapeDtypeStruct((n_rows, D_MODEL), F32),
        compiler_params=_cparams(1),
        name="experts",
    )(tile_expert, n_used, xs, w_g, w_u, w_d)


COMBINE_TOKENS = 256


def _combine_kernel(pos1_ref, pos2_ref, pos1_next_ref, pos2_next_ref, x1_ref, route_ref, gf_ref, ys_ref,
                    yp_ref, ysm_ref, ybuf, sem, *, n_prompt_tiles):
    tc = COMBINE_TOKENS
    i = pl.program_id(0)
    buf = i % 2

    def start_rows(p_refs, b):
        for r in range(tc):
            for slot in range(2):
                pltpu.make_async_copy(ys_ref.at[pl.ds(p_refs[slot][r], 1)],
                                      ybuf.at[b, slot, pl.ds(r, 1)], sem.at[b]).start()

    def wait_rows(b):
        for slot in range(2):
            pltpu.make_async_copy(ys_ref.at[pl.ds(0, tc)], ybuf.at[b, slot], sem.at[b]).wait()

    @pl.when(i == 0)
    def _():
        start_rows((pos1_ref, pos2_ref), 0)

    wait_rows(buf)
    start_rows((pos1_next_ref, pos2_next_ref), 1 - buf)

    route = route_ref[...]
    w1 = route[:, _R_W1:_R_W1 + 1]
    w2 = route[:, _R_W2:_R_W2 + 1]
    x2 = x1_ref[...] + (w1 * ybuf[buf, 0] + w2 * ybuf[buf, 1])
    r = lax.rsqrt(jnp.mean(x2 * x2, axis=-1, keepdims=True) + EPS)
    y = (x2 * r) * gf_ref[...]
    is_prompt = pl.program_id(0) < n_prompt_tiles

    @pl.when(is_prompt)
    def _():
        yp_ref[...] = y

    @pl.when(jnp.logical_not(is_prompt))
    def _():
        ysm_ref[...] = y

    @pl.when(i == pl.num_programs(0) - 1)
    def _():
        wait_rows(1 - buf)


def _combine(pos1, pos2, x1, route, gf, ys, n_prompt):
    n = x1.shape[0]
    tc = COMBINE_TOKENS
    n_prompt_tiles = n_prompt // tc
    n_steps = n // tc
    return pl.pallas_call(
        functools.partial(_combine_kernel, n_prompt_tiles=n_prompt_tiles),
        grid=(n_steps,),
        in_specs=[pl.BlockSpec((tc,), lambda i: (i,), memory_space=pltpu.SMEM),
                  pl.BlockSpec((tc,), lambda i: (i,), memory_space=pltpu.SMEM),
                  pl.BlockSpec((tc,), lambda i: (jnp.minimum(i + 1, n_steps - 1),), memory_space=pltpu.SMEM),
                  pl.BlockSpec((tc,), lambda i: (jnp.minimum(i + 1, n_steps - 1),), memory_space=pltpu.SMEM),
                  pl.BlockSpec((tc, D_MODEL), lambda i: (i, 0)),
                  pl.BlockSpec((tc, LANES), lambda i: (i, 0)),
                  pl.BlockSpec((1, D_MODEL), lambda i: (0, 0)),
                  pl.BlockSpec(memory_space=pl.ANY)],
        out_specs=_split_token_specs(tc, D_MODEL, n_prompt_tiles),
        out_shape=(jax.ShapeDtypeStruct((n_prompt, D_MODEL), F32),
                   jax.ShapeDtypeStruct((n - n_prompt, D_MODEL), F32)),
        scratch_shapes=[pltpu.VMEM((2, 2, tc, D_MODEL), F32), pltpu.SemaphoreType.DMA((2,))],
        compiler_params=_cparams(1),
        name="combine",
    )(pos1, pos2, pos1, pos2, x1, route, gf, ys)


def _split_w_in(w_in):
    w_main = w_in[:, :_C_KW].astype(BF16)
    w_kw = jnp.pad(w_in[:, _C_KW:_C_P].astype(BF16), ((0, 0), (0, LANES - (_C_P - _C_KW))))
    w_pool = w_in[:, _C_P:].astype(BF16)
    return w_main, w_kw, w_pool


def _forward(x_prompt, x_sample, cache_k, cache_v, cache_kidx, state_pool,
             norm1_g, w_in, w_gate, b_gate, w_pool_grp, pool_scale, w_o_attn, w_o_pool, w_out,
             norm2_g, w_rg, b_rg, w_re, b_re, w_eg, w_eu, w_ed, norm_f_g):
    batch, seq, d = x_prompt.shape
    dec_batch, dec_seq, _ = x_sample.shape
    past = cache_k.shape[1]
    n_p = batch * seq
    n_s = dec_batch * dec_seq
    n = n_p + n_s
    tm = TOK_TILE
    assert d == D_MODEL and seq % tm == 0 and n_s % tm == 0 and n % (2 * tm) == 0
    assert past % KEY_BLOCK == 0 and dec_seq >= POOL_STATE and dec_seq <= KEY_BLOCK

    xp = x_prompt.reshape(n_p, d)
    xs = x_sample.reshape(n_s, d)
    pos_tab = jnp.concatenate([jnp.arange(seq), past + jnp.arange(n_s) % dec_seq])
    tables = _rope_tables(pos_tab)
    h, q, k_p, k_s, v_p, v_s, qi, kw, pin = _in_proj(
        xp, xs, norm1_g.reshape(1, d), *_split_w_in(w_in), tables, seq // tm)

    oa_p = _dsa_prompt(q, qi, kw, k_p, v_p, batch, seq)
    oa_s = _dsa_sample(q, qi, kw, k_s, v_s,
                       cache_k.reshape(dec_batch * past, KV_WIDTH),
                       cache_v.reshape(dec_batch * past, KV_WIDTH),
                       cache_kidx.reshape(dec_batch * past, IDX_DIM),
                       n_p, dec_batch, dec_seq, past)

    w_grp16 = w_pool_grp.astype(BF16)
    scale = pool_scale.reshape(1, POOL_WIDTH)
    op_p = _pool(pin, w_grp16, scale, 0, batch, seq, 0)
    halo = jnp.concatenate([jnp.zeros((dec_batch, 1, POOL_WIDTH), F32), state_pool], axis=1)
    op_s = _pool(pin, w_grp16, scale, n_p // dec_seq, dec_batch, dec_seq, past, halo=halo)

    m = _merge(h, oa_p, oa_s, op_p, op_s, w_gate.astype(BF16), b_gate.reshape(1, 2 * d),
               w_o_attn.astype(BF16), w_o_pool.astype(BF16))

    w_r = jnp.concatenate([w_rg, w_re, jnp.zeros((d, LANES - MOE_GROUPS - N_EXPERTS), F32)], axis=1)
    b_r = jnp.concatenate([b_rg, b_re, jnp.zeros((LANES - MOE_GROUPS - N_EXPERTS,), F32)]).reshape(1, LANES)
    wr_hi = w_r.astype(BF16)
    wr_lo = (w_r - wr_hi.astype(F32)).astype(BF16)
    wr_split = jnp.concatenate([wr_hi, wr_lo], axis=1)
    x1, h2, route, route_t, counts = _out_proj(m, xp, xs, w_out.astype(BF16), norm2_g.reshape(1, d), wr_split, b_r)

    te = EXP_TILE
    n_tiles = (2 * n) // te + N_EXPERTS
    cnt = counts[0, _R_LOGIT0:_R_LOGIT0 + N_EXPERTS].astype(I32)
    tiles_per_e = (cnt + te - 1) // te
    tile_end = jnp.cumsum(tiles_per_e)
    row_start = (tile_end - tiles_per_e) * te
    n_used = tile_end[-1:].astype(I32)
    tile_ids = jnp.arange(n_tiles, dtype=I32)
    tile_expert = jnp.minimum(
        jnp.sum((tile_end[None, :] <= tile_ids[:, None]).astype(I32), axis=1), N_EXPERTS - 1)
    last_e = tile_expert[jnp.maximum(n_used[0] - 1, 0)]
    tile_expert = jnp.where(tile_ids < n_used[0], tile_expert, last_e)
    pos1 = row_start[route_t[_R_E1].astype(I32)] + route_t[_R_RANK1].astype(I32)
    pos2 = row_start[route_t[_R_E2].astype(I32)] + route_t[_R_RANK2].astype(I32)

    xs = _dispatch(pos1, pos2, h2, n_tiles * te)
    ys = _experts(tile_expert, n_used, xs, w_eg, w_eu, w_ed)
    y_p, y_s = _combine(pos1, pos2, x1, route, norm_f_g.reshape(1, d), ys, n_p)

    y_prompt = y_p.reshape(batch, seq, d)
    y_sample = y_s.reshape(dec_batch, dec_seq, d)
    k_p = k_p.reshape(1, batch, seq, N_KV_HEADS, HEAD_DIM)
    v_p = v_p.reshape(1, batch, seq, N_KV_HEADS, HEAD_DIM)
    ki_p = kw[:n_p, :IDX_DIM].reshape(1, batch, seq, IDX_DIM)
    pool_p = jnp.stack([pin[(b + 1) * seq - POOL_STATE:(b + 1) * seq] for b in range(batch)])[None]
    k_s = k_s.reshape(1, dec_batch, dec_seq, N_KV_HEADS, HEAD_DIM)
    v_s = v_s.reshape(1, dec_batch, dec_seq, N_KV_HEADS, HEAD_DIM)
    ki_s = kw[n_p:, :IDX_DIM].reshape(1, dec_batch, dec_seq, IDX_DIM)
    pool_s = pin[n_p:].reshape(dec_batch, dec_seq, POOL_WIDTH)[None, :, dec_seq - POOL_STATE:, :]
    return (y_prompt, y_sample, k_p, v_p, ki_p, pool_p, k_s, v_s, ki_s, pool_s)


def kernel(x_prompt, x_sample, cache_k, cache_v, cache_kidx, state_pool, norm1_g, w_in, w_gate, b_gate,
           w_pool_grp, pool_scale, w_o_attn, w_o_pool, w_out, norm2_g, w_router_group, b_router_group,
           w_router_expert, b_router_expert, w_exp_gate, w_exp_up, w_exp_down, norm_f_g):
    assert cache_k.shape[0] == 1, "single-layer model"
    return _forward(x_prompt, x_sample, cache_k[0], cache_v[0], cache_kidx[0], state_pool[0],
                    norm1_g[0], w_in[0], w_gate[0], b_gate[0], w_pool_grp[0], pool_scale[0],
                    w_o_attn[0], w_o_pool[0], w_out[0], norm2_g[0], w_router_group[0],
                    b_router_group[0], w_router_expert[0], b_router_expert[0],
                    w_exp_gate[0], w_exp_up[0], w_exp_down[0], norm_f_g)
```

```python
import functools

import numpy as np
import jax
import jax.numpy as jnp
from jax import lax
from jax.experimental import pallas as pl
from jax.experimental.pallas import tpu as pltpu

F32 = jnp.float32
BF16 = jnp.bfloat16
I32 = jnp.int32

D_MODEL = 2048
CHUNK = 64
N_HEADS = 8
N_KV_HEADS = 2
HEAD_DIM = 128
GQA_GROUP = N_HEADS // N_KV_HEADS
ATTN_WIDTH = N_HEADS * HEAD_DIM
KV_WIDTH = N_KV_HEADS * HEAD_DIM
IDX_HEADS = 16
IDX_DIM = 64
IDX_PAIRS = IDX_HEADS // 2
TOPK_MAX = 256
POOL_WINDOWS = (2, 4, 8, 16)
POOL_GROUPS = 4
POOL_WIDTH = D_MODEL // 2
POOL_GROUP_DIM = POOL_WIDTH // POOL_GROUPS
POOL_STATE = 15
POOL_HALO = 16
MOE_GROUPS = 4
EXPERTS_PER_GROUP = 8
N_EXPERTS = MOE_GROUPS * EXPERTS_PER_GROUP
D_EXPERT = 512
ROPE_THETA = 10000.0
CHUNK_SHIFT = CHUNK.bit_length() - 1
GROUP_SHIFT = EXPERTS_PER_GROUP.bit_length() - 1
EPS = 1e-6

LANES = 128
INT_MIN = -(2 ** 31)
NEG_BIG = -1e30
LOG2_E = 1.4426950408889634
VMEM_LIMIT = 56 * 1024 * 1024

KEY_BLOCK = 256
TOK_TILE = 256
EXP_TILE = 256


def _cparams(n_axes):
    return pltpu.CompilerParams(dimension_semantics=("arbitrary",) * n_axes,
                                vmem_limit_bytes=VMEM_LIMIT)


def _dot(a, b):
    return jnp.dot(a, b, preferred_element_type=F32)


def _dot_nt(a, b):
    return lax.dot_general(a, b, (((1,), (1,)), ((), ())), preferred_element_type=F32)


_C_Q = 0
_C_K = _C_Q + ATTN_WIDTH
_C_V = _C_K + KV_WIDTH
_C_QI = _C_V + KV_WIDTH
_C_KW = _C_QI + IDX_HEADS * IDX_DIM
_C_P = _C_KW + IDX_DIM + IDX_HEADS


def _rope128(y, cos, sin_signed):
    return y * cos + pltpu.roll(y, HEAD_DIM // 2, 1) * sin_signed


def _rope64(y, cos, sin_a, sin_b):
    half = IDX_DIM // 2
    return y * cos + pltpu.roll(y, LANES - half, 1) * sin_a + pltpu.roll(y, half, 1) * sin_b


def _inproj_kernel(xp_ref, xs_ref, g_ref, w_ref, wkw_ref, wp_ref, ck_ref, sk_ref, cq_ref, sq_ref, ci_ref,
                   sia_ref, sib_ref, cw_ref, swa_ref, swb_ref,
                   h_ref, q_ref, kp_ref, ks_ref, vp_ref, vs_ref, qi_ref, kw_ref, p_ref, *, n_prompt_tiles):
    is_prompt = pl.program_id(0) < n_prompt_tiles
    x = jnp.where(is_prompt, xp_ref[...], xs_ref[...])
    r = lax.rsqrt(jnp.mean(x * x, axis=-1, keepdims=True) + EPS)
    h = ((x * r) * g_ref[...]).astype(BF16)
    h_ref[...] = h
    cq, sq = cq_ref[...], sq_ref[...]
    for c in range(ATTN_WIDTH // 256):
        y = _dot(h, w_ref[:, _C_Q + c * 256:_C_Q + (c + 1) * 256])
        for s in range(2):
            q_ref[:, c * 256 + s * LANES:c * 256 + (s + 1) * LANES] = _rope128(
                y[:, s * LANES:(s + 1) * LANES], cq, sq).astype(BF16)
    ck, sk = ck_ref[...], sk_ref[...]
    y = _dot(h, w_ref[:, _C_K:_C_K + KV_WIDTH])
    k = jnp.concatenate([_rope128(y[:, s * LANES:(s + 1) * LANES], ck, sk) for s in range(N_KV_HEADS)], axis=1)
    v = _dot(h, w_ref[:, _C_V:_C_V + KV_WIDTH])

    @pl.when(is_prompt)
    def _():
        kp_ref[...] = k
        vp_ref[...] = v

    @pl.when(jnp.logical_not(is_prompt))
    def _():
        ks_ref[...] = k
        vs_ref[...] = v

    ci, sia, sib = ci_ref[...], sia_ref[...], sib_ref[...]
    for c in range(IDX_PAIRS // 2):
        y = _dot(h, w_ref[:, _C_QI + c * 256:_C_QI + (c + 1) * 256])
        for s in range(2):
            qi_ref[2 * c + s] = _rope64(y[:, s * LANES:(s + 1) * LANES], ci, sia, sib).astype(BF16)
    y = _dot(h, wkw_ref[...])
    kw_ref[...] = _rope64(y, cw_ref[...], swa_ref[...], swb_ref[...])
    for c in range(POOL_WIDTH // 256):
        p_ref[:, c * 256:(c + 1) * 256] = _dot(h, wp_ref[:, c * 256:(c + 1) * 256])


def _rope_tables(pos):
    pos = pos.astype(F32)[:, None]

    def cs(dim):
        half = dim // 2
        inv = ROPE_THETA ** (-jnp.arange(half, dtype=F32) / half)
        ang = pos * inv[None, :]
        return jnp.cos(ang), jnp.sin(ang)

    c, s = cs(HEAD_DIM)
    ck = jnp.concatenate([c, c], axis=1)
    sk = jnp.concatenate([-s, s], axis=1)
    qs = HEAD_DIM ** -0.5 * LOG2_E
    c, s = cs(IDX_DIM)
    z = jnp.zeros_like(s)
    ci = jnp.concatenate([c, c, c, c], axis=1)
    sia = jnp.concatenate([-s, z, -s, z], axis=1)
    sib = jnp.concatenate([z, s, z, s], axis=1)
    iscale = IDX_DIM ** -0.5
    n = pos.shape[0]
    wscale = jnp.full((n, IDX_HEADS), IDX_HEADS ** -0.5, F32)
    pad = jnp.zeros((n, LANES - IDX_DIM - IDX_HEADS), F32)
    zz = jnp.zeros((n, LANES - IDX_DIM), F32)
    cw = jnp.concatenate([c, c, wscale, pad], axis=1)
    swa = jnp.concatenate([-s, z, zz], axis=1)
    swb = jnp.concatenate([z, s, zz], axis=1)
    return (ck, sk, ck * qs, sk * qs, ci * iscale, sia * iscale, sib * iscale, cw, swa, swb)


def _split_token_specs(tm, width, n_prompt_tiles):
    prompt = pl.BlockSpec((tm, width), lambda i: (jnp.minimum(i, n_prompt_tiles - 1), 0))
    sample = pl.BlockSpec((tm, width), lambda i: (jnp.maximum(i - n_prompt_tiles, 0), 0))
    return prompt, sample


def _in_proj(xp, xs, g1, w_main, w_kw, w_pool, tables, tiles_per_seq):
    tm = TOK_TILE
    n_prompt_tiles = xp.shape[0] // tm
    n = xp.shape[0] + xs.shape[0]
    n_tiles = n // tm

    def tab_idx(i):
        return (jnp.where(i < n_prompt_tiles, i % tiles_per_seq, tiles_per_seq + i - n_prompt_tiles), 0)

    tok = lambda w: pl.BlockSpec((tm, w), lambda i: (i, 0))
    tab = pl.BlockSpec((tm, LANES), tab_idx)
    resident = lambda a: pl.BlockSpec(a.shape, lambda i: (0, 0), pipeline_mode=pl.Buffered(1))
    n_s = xs.shape[0]
    kv_p, kv_s = _split_token_specs(tm, KV_WIDTH, n_prompt_tiles)
    out_shape = (
        jax.ShapeDtypeStruct((n, D_MODEL), BF16),
        jax.ShapeDtypeStruct((n, ATTN_WIDTH), BF16),
        jax.ShapeDtypeStruct((n - n_s, KV_WIDTH), F32),
        jax.ShapeDtypeStruct((n_s, KV_WIDTH), F32),
        jax.ShapeDtypeStruct((n - n_s, KV_WIDTH), F32),
        jax.ShapeDtypeStruct((n_s, KV_WIDTH), F32),
        jax.ShapeDtypeStruct((IDX_PAIRS, n, LANES), BF16),
        jax.ShapeDtypeStruct((n, LANES), F32),
        jax.ShapeDtypeStruct((n, POOL_WIDTH), F32),
    )
    return pl.pallas_call(
        functools.partial(_inproj_kernel, n_prompt_tiles=n_prompt_tiles),
        grid=(n_tiles,),
        in_specs=[*_split_token_specs(tm, D_MODEL, n_prompt_tiles),
                  pl.BlockSpec((1, D_MODEL), lambda i: (0, 0)),
                  resident(w_main), resident(w_kw), resident(w_pool)]
                 + [tab] * 10,
        out_specs=(tok(D_MODEL), tok(ATTN_WIDTH), kv_p, kv_s, kv_p, kv_s,
                   pl.BlockSpec((IDX_PAIRS, tm, LANES), lambda i: (0, i, 0)),
                   tok(LANES), tok(POOL_WIDTH)),
        out_shape=out_shape,
        compiler_params=_cparams(1),
        name="in_proj",
    )(xp, xs, g1, w_main, w_kw, w_pool, *tables)


def _ordinal_to_f32(k):
    return lax.bitcast_convert_type(k ^ ((k >> 31) & 0x7FFFFFFF), F32)


def _kth_largest_score(topk, shape, n_key_rows, count_where):
    def count_ge(cand):
        cand_f = _ordinal_to_f32(cand)
        return count_where(lambda keys, kb: keys >= jnp.broadcast_to(cand_f, keys.shape))

    c0 = count_ge(jnp.zeros(shape, I32))
    thr0 = jnp.where(c0 >= topk, 0, INT_MIN).astype(I32)
    cnt0 = jnp.where(c0 >= topk, c0, n_key_rows).astype(I32)

    def bit_step(i, carry):
        thr, cnt = carry
        cand = thr + (jnp.int32(1) << (30 - i))
        c = count_ge(cand)
        ok = c >= topk
        return jnp.where(ok, cand, thr), jnp.where(ok, c, cnt)

    thr, cnt = lax.fori_loop(0, 31, bit_step, (thr0, cnt0))
    has_k = thr > INT_MIN
    thr_f = jnp.where(has_k, _ordinal_to_f32(thr), jnp.finfo(F32).min)
    return thr_f, has_k & (cnt > topk)


def _drop_excess_ties(sc, n_blocks, topk, thr, need_tie, key_index, count_where):
    n_gt = count_where(lambda keys, kb: keys > jnp.broadcast_to(thr, keys.shape))
    n_keep = topk - n_gt

    def idx_step(i, bound):
        cand = bound + (jnp.int32(1) << (14 - i))
        c = count_where(lambda keys, kb: (keys == jnp.broadcast_to(thr, keys.shape))
                        & (key_index(kb) < jnp.broadcast_to(cand, keys.shape)))
        return jnp.where(c <= n_keep, cand, bound)

    bound = lax.fori_loop(0, 15, idx_step, jnp.zeros(thr.shape, I32))
    bound = jnp.where(need_tie, bound, jnp.int32(2 ** 30))

    def drop_block(kb, carry):
        keys = sc[kb]
        drop = (keys == jnp.broadcast_to(thr, keys.shape)) & (key_index(kb) >= jnp.broadcast_to(bound, keys.shape))
        sc[kb] = jnp.where(drop, -jnp.inf, keys)
        return carry

    lax.fori_loop(0, n_blocks, drop_block, 0)


def _stack_queries_t(q_ref, qt):
    for g in range(N_KV_HEADS):
        qg = jnp.concatenate([q_ref[:, (g * GQA_GROUP + j) * HEAD_DIM:(g * GQA_GROUP + j + 1) * HEAD_DIM]
                              for j in range(GQA_GROUP)], axis=0)
        qt[g] = qg.astype(F32).T.astype(BF16)


def _attention_scratch(rows):
    return [
        pltpu.VMEM((N_KV_HEADS, 1, rows), F32),
        pltpu.VMEM((N_KV_HEADS, 1, rows), F32),
        pltpu.VMEM((N_KV_HEADS, HEAD_DIM, rows), F32),
        pltpu.VMEM((2, N_KV_HEADS, KEY_BLOCK, rows), F32),
        pltpu.VMEM((2, N_KV_HEADS, KEY_BLOCK, rows), BF16),
        pltpu.VMEM((2, N_KV_HEADS, 1, rows), F32),
    ]


def _masked_attention(o_ref, n_blocks, n_blocks_all, bias_t, k16, vt16, qt, att, tq):
    m_s, l_s, acc_s, lg_s, p_s, alpha_s = att
    m_s[...] = jnp.full(m_s.shape, NEG_BIG, F32)
    l_s[...] = jnp.zeros(l_s.shape, F32)
    acc_s[...] = jnp.zeros(acc_s.shape, F32)
    p_s[1] = jnp.zeros(p_s.shape[1:], BF16)
    alpha_s[1] = jnp.ones(alpha_s.shape[1:], F32)

    def logits_of(kb, slot):
        r0 = pl.multiple_of(kb * KEY_BLOCK, KEY_BLOCK)
        for g in range(N_KV_HEADS):
            lg_s[slot, g] = _dot(k16[pl.ds(r0, KEY_BLOCK), g * HEAD_DIM:(g + 1) * HEAD_DIM], qt[g])

    def value_product(kb, slot):
        for g in range(N_KV_HEADS):
            acc_s[g] = (alpha_s[slot, g] * acc_s[g]
                        + _dot(vt16[kb, g * HEAD_DIM:(g + 1) * HEAD_DIM, :], p_s[slot, g]))

    logits_of(0, 0)
    last = n_blocks_all - 1

    def step(kb, slot):
        logits_of(jnp.minimum(kb + 1, last), 1 - slot)
        bias = jnp.where(kb < n_blocks, bias_t(jnp.minimum(kb, last)), NEG_BIG)
        for g in range(N_KV_HEADS):
            logits = lg_s[slot, g] + bias
            m_prev = m_s[g]
            m_new = jnp.maximum(m_prev, jnp.max(logits, axis=0, keepdims=True))
            p = jnp.exp2(logits - m_new)
            alpha = jnp.exp2(m_prev - m_new)
            l_s[g] = alpha * l_s[g] + jnp.sum(p, axis=0, keepdims=True)
            p_s[slot, g] = p.astype(BF16)
            alpha_s[slot, g] = alpha
            m_s[g] = m_new
        value_product(jnp.clip(kb - 1, 0, last), 1 - slot)

    def two_steps(i, carry):
        step(2 * i, 0)
        step(2 * i + 1, 1)
        return carry

    n_pairs = (n_blocks + 1) // 2
    lax.fori_loop(0, n_pairs, two_steps, 0)
    value_product(jnp.minimum(2 * n_pairs - 1, last), 1)

    for g in range(N_KV_HEADS):
        o = (acc_s[g] * (1.0 / l_s[g])).T
        for j in range(GQA_GROUP):
            hd = g * GQA_GROUP + j
            o_ref[:, hd * HEAD_DIM:(hd + 1) * HEAD_DIM] = o[j * tq:(j + 1) * tq, :].astype(BF16)


def _dsa_prompt_kernel(q_ref, qi_ref, wq_ref, kn_ref, vn_ref, kwn_ref, o_ref,
                       k16, vt16, kia, kib, sc, raw, qt, *att, tq, seq, topk):
    qb = pl.program_id(1)
    n_blocks_all = seq // KEY_BLOCK

    @pl.when(qb == 0)
    def _():
        lane = lax.broadcasted_iota(I32, (seq, LANES), 1)
        kin = jnp.where(lane < IDX_DIM, kwn_ref[...], 0.0)
        k16[...] = kn_ref[...].astype(BF16)
        kia[...] = kin.astype(BF16)
        kib[...] = pltpu.roll(kin, IDX_DIM, 1).astype(BF16)
        for kb in range(n_blocks_all):
            vt16[kb] = vn_ref[kb * KEY_BLOCK:(kb + 1) * KEY_BLOCK, :].T.astype(BF16)

    n_blocks = qb // (KEY_BLOCK // tq) + 1

    qit = qi_ref[...].reshape(IDX_PAIRS * tq, LANES).astype(F32).T.astype(BF16)
    wt = wq_ref[...].T
    w_rows = [wt[IDX_DIM + j:IDX_DIM + j + 1, :] for j in range(IDX_HEADS)]
    _stack_queries_t(q_ref, qt)

    key_row = lax.broadcasted_iota(I32, (KEY_BLOCK, tq), 0)
    q_pos = qb * tq + lax.broadcasted_iota(I32, (KEY_BLOCK, tq), 1)

    last = n_blocks_all - 1
    n_pairs = (n_blocks + 1) // 2

    def raw_scores(kb, slot):
        r0 = pl.multiple_of(kb * KEY_BLOCK, KEY_BLOCK)
        raw[slot, 0] = _dot(kia[pl.ds(r0, KEY_BLOCK), :], qit)
        raw[slot, 1] = _dot(kib[pl.ds(r0, KEY_BLOCK), :], qit)

    def score_step(kb, slot):
        raw_scores(jnp.minimum(kb + 1, last), 1 - slot)
        se = raw[slot, 0]
        so = raw[slot, 1]
        score = jnp.zeros((KEY_BLOCK, tq), F32)
        for p in range(IDX_PAIRS):
            score = (score + jnp.maximum(se[:, p * tq:(p + 1) * tq], 0.0) * w_rows[2 * p]
                     + jnp.maximum(so[:, p * tq:(p + 1) * tq], 0.0) * w_rows[2 * p + 1])
        adm = ((kb * KEY_BLOCK + key_row) >> CHUNK_SHIFT) <= (q_pos >> CHUNK_SHIFT)
        sc[jnp.where(kb < n_blocks, kb, n_blocks_all)] = jnp.where(adm, score, -jnp.inf)

    def two_score_steps(i, carry):
        score_step(2 * i, 0)
        score_step(2 * i + 1, 1)
        return carry

    raw_scores(0, 0)
    lax.fori_loop(0, n_pairs, two_score_steps, 0)

    def count_where(pred):
        def body(i, acc):
            for s in range(2):
                kb = 2 * i + s
                kbc = jnp.minimum(kb, last)
                m = pred(sc[kbc], kbc).astype(I32)
                part = jnp.sum(m.reshape(KEY_BLOCK // 8, 8, tq), axis=0)
                acc = acc + jnp.where(kb < n_blocks, part, 0)
            return acc
        acc = lax.fori_loop(0, n_pairs, body, jnp.zeros((8, tq), I32))
        return jnp.sum(acc, axis=0, keepdims=True)

    thr, need_tie = _kth_largest_score(topk, (1, tq), seq, count_where)

    @pl.when(jnp.max(need_tie.astype(I32)) > 0)
    def _():
        _drop_excess_ties(sc, n_blocks, topk, thr, need_tie,
                          lambda kb: kb * KEY_BLOCK + key_row, count_where)

    thr_b = jnp.broadcast_to(thr, (KEY_BLOCK, tq))

    def bias_t(kb):
        bias = jnp.where(sc[kb] >= thr_b, 0.0, NEG_BIG)
        return jnp.concatenate([bias] * GQA_GROUP, axis=1)

    _masked_attention(o_ref, n_blocks, n_blocks_all, bias_t, k16, vt16, qt, att, tq)


def _dsa_sample_kernel(q_ref, qi_ref, wq_ref, kn_ref, vn_ref, kwn_ref, kc_ref, vc_ref, kic_ref, o_ref,
                       k16, vt16, kia, kib, wbe, wbo, sc, qt, *att,
                       tq, n_key_rows, n_cache, n_new, topk):
    n_blocks = n_key_rows // KEY_BLOCK
    n_cache_blocks = n_cache // KEY_BLOCK

    lane = lax.broadcasted_iota(I32, (n_new, LANES), 1)
    kin = jnp.where(lane < IDX_DIM, kwn_ref[...], 0.0).astype(BF16)
    n_tail = n_key_rows - n_cache
    k16[0:n_cache, :] = kc_ref[...].astype(BF16)
    for kb in range(n_cache_blocks):
        vt16[kb] = vc_ref[kb * KEY_BLOCK:(kb + 1) * KEY_BLOCK, :].T.astype(BF16)
    v_tail = jnp.concatenate([vn_ref[...], jnp.zeros((n_tail - n_new, KV_WIDTH), F32)], axis=0)
    vt16[n_cache_blocks] = v_tail.T.astype(BF16)
    kic = jnp.concatenate([kic_ref[...], jnp.zeros((n_cache, LANES - IDX_DIM), F32)], axis=1)
    kia[0:n_cache, :] = kic.astype(BF16)
    kib[0:n_cache, :] = pltpu.roll(kic, IDX_DIM, 1).astype(BF16)
    k16[n_cache:n_key_rows, :] = jnp.zeros((n_tail, KV_WIDTH), BF16)
    kia[n_cache:n_key_rows, :] = jnp.zeros((n_tail, LANES), BF16)
    kib[n_cache:n_key_rows, :] = jnp.zeros((n_tail, LANES), BF16)
    k16[n_cache:n_cache + n_new, :] = kn_ref[...].astype(BF16)
    kia[n_cache:n_cache + n_new, :] = kin
    kib[n_cache:n_cache + n_new, :] = pltpu.roll(kin.astype(F32), IDX_DIM, 1).astype(BF16)

    wq = wq_ref[...]
    for j in range(IDX_HEADS):
        col = jnp.broadcast_to(wq[:, IDX_DIM + j:IDX_DIM + j + 1], (tq, LANES))
        if j % 2 == 0:
            wbe[j // 2] = col
        else:
            wbo[j // 2] = col
    _stack_queries_t(q_ref, qt)

    lane = lax.broadcasted_iota(I32, (tq, KEY_BLOCK), 1)

    qi2 = qi_ref[...].reshape(IDX_PAIRS * tq, LANES)

    def score_block(kb, carry):
        r0 = pl.multiple_of(kb * KEY_BLOCK, KEY_BLOCK)
        se = _dot_nt(qi2, kia[pl.ds(r0, KEY_BLOCK), :]).reshape(IDX_PAIRS, tq, KEY_BLOCK)
        so = _dot_nt(qi2, kib[pl.ds(r0, KEY_BLOCK), :]).reshape(IDX_PAIRS, tq, KEY_BLOCK)
        score = jnp.zeros((tq, KEY_BLOCK), F32)
        for p in range(IDX_PAIRS):
            we = wbe[p]
            wo = wbo[p]
            we2 = jnp.concatenate([we, we], axis=1)
            wo2 = jnp.concatenate([wo, wo], axis=1)
            score = score + jnp.maximum(se[p], 0.0) * we2 + jnp.maximum(so[p], 0.0) * wo2
        adm = (kb * KEY_BLOCK + lane) < (n_cache + n_new)
        sc[kb] = jnp.where(adm, score, -jnp.inf)
        return carry

    lax.fori_loop(0, n_blocks, score_block, 0)

    def count_where(pred):
        def body(kb, acc):
            m = pred(sc[kb], kb).astype(I32)
            return acc + m[:, :LANES] + m[:, LANES:]
        acc = lax.fori_loop(0, n_blocks, body, jnp.zeros((tq, LANES), I32))
        return jnp.sum(acc, axis=1, keepdims=True)

    thr, need_tie = _kth_largest_score(topk, (tq, 1), n_key_rows, count_where)

    @pl.when(jnp.max(need_tie.astype(I32)) > 0)
    def _():
        _drop_excess_ties(sc, n_blocks, topk, thr, need_tie,
                          lambda kb: kb * KEY_BLOCK + lane, count_where)

    thr_b = jnp.broadcast_to(thr, (tq, KEY_BLOCK))

    def bias_t(kb):
        bias = jnp.where(sc[kb] >= thr_b, 0.0, NEG_BIG)
        return jnp.concatenate([bias] * GQA_GROUP, axis=0).T

    _masked_attention(o_ref, n_blocks, n_blocks, bias_t, k16, vt16, qt, att, tq)


def _dsa_sample_scratch(tq, n_key_rows):
    rows = GQA_GROUP * tq
    return [
        pltpu.VMEM((n_key_rows, KV_WIDTH), BF16),
        pltpu.VMEM((n_key_rows // KEY_BLOCK, KV_WIDTH, KEY_BLOCK), BF16),
        pltpu.VMEM((n_key_rows, LANES), BF16),
        pltpu.VMEM((n_key_rows, LANES), BF16),
        pltpu.VMEM((IDX_PAIRS, tq, LANES), F32),
        pltpu.VMEM((IDX_PAIRS, tq, LANES), F32),
        pltpu.VMEM((n_key_rows // KEY_BLOCK, tq, KEY_BLOCK), F32),
        pltpu.VMEM((N_KV_HEADS, HEAD_DIM, rows), BF16),
    ] + _attention_scratch(rows)


def _dsa_prompt(q, qi, kw, k, v, batch, seq):
    tq = 128
    nqb = seq // tq
    topk = min(TOPK_MAX, seq // 4)
    kern = functools.partial(_dsa_prompt_kernel, tq=tq, seq=seq, topk=topk)
    rows_q = GQA_GROUP * tq
    scratch = [
        pltpu.VMEM((seq, KV_WIDTH), BF16),
        pltpu.VMEM((seq // KEY_BLOCK, KV_WIDTH, KEY_BLOCK), BF16),
        pltpu.VMEM((seq, LANES), BF16),
        pltpu.VMEM((seq, LANES), BF16),
        pltpu.VMEM((seq // KEY_BLOCK + 1, KEY_BLOCK, tq), F32),
        pltpu.VMEM((2, 2, KEY_BLOCK, IDX_PAIRS * tq), F32),
        pltpu.VMEM((N_KV_HEADS, HEAD_DIM, rows_q), BF16),
    ] + _attention_scratch(rows_q)
    qmap = lambda b, i: (b * nqb + i, 0)
    smap = lambda b, i: (b, 0)
    return pl.pallas_call(
        kern,
        grid=(batch, nqb),
        in_specs=[pl.BlockSpec((tq, ATTN_WIDTH), qmap),
                  pl.BlockSpec((IDX_PAIRS, tq, LANES), lambda b, i: (0, b * nqb + i, 0)),
                  pl.BlockSpec((tq, LANES), qmap),
                  pl.BlockSpec((seq, KV_WIDTH), smap),
                  pl.BlockSpec((seq, KV_WIDTH), smap),
                  pl.BlockSpec((seq, LANES), smap)],
        out_specs=pl.BlockSpec((tq, ATTN_WIDTH), qmap),
        out_shape=jax.ShapeDtypeStruct((batch * seq, ATTN_WIDTH), BF16),
        scratch_shapes=scratch,
        compiler_params=_cparams(2),
        name="dsa_prompt",
    )(q, qi, kw, k, v, kw)


def _dsa_sample(q, qi, kw, k, v, cache_k, cache_v, cache_ki, row0, batch, seq, past):
    tq = seq
    n_key_rows = past + KEY_BLOCK
    topk = min(TOPK_MAX, (past + seq) // 4)
    kern = functools.partial(_dsa_sample_kernel, tq=tq, n_key_rows=n_key_rows, n_cache=past, n_new=seq,
                             topk=topk)
    t0 = row0 // seq
    qmap = lambda b, i: (t0 + b, 0)
    cmap = lambda b, i: (b, 0)
    return pl.pallas_call(
        kern,
        grid=(batch, 1),
        in_specs=[pl.BlockSpec((tq, ATTN_WIDTH), qmap),
                  pl.BlockSpec((IDX_PAIRS, tq, LANES), lambda b, i: (0, t0 + b, 0)),
                  pl.BlockSpec((tq, LANES), qmap),
                  pl.BlockSpec((tq, KV_WIDTH), cmap),
                  pl.BlockSpec((tq, KV_WIDTH), cmap),
                  pl.BlockSpec((tq, LANES), qmap),
                  pl.BlockSpec((past, KV_WIDTH), cmap),
                  pl.BlockSpec((past, KV_WIDTH), cmap),
                  pl.BlockSpec((past, IDX_DIM), cmap)],
        out_specs=pl.BlockSpec((tq, ATTN_WIDTH), cmap),
        out_shape=jax.ShapeDtypeStruct((batch * seq, ATTN_WIDTH), BF16),
        scratch_shapes=_dsa_sample_scratch(tq, n_key_rows),
        compiler_params=_cparams(2),
        name="dsa_sample",
    )(q, qi, kw, k, v, kw, cache_k, cache_v, cache_ki)


def _pool_kernel(*refs, tp, pos0, carried):
    if carried:
        p_ref, w_ref, s_ref, o_ref, ext, save = refs
    else:
        p_ref, halo_ref, w_ref, s_ref, o_ref, ext = refs
    i = pl.program_id(1)
    if carried:
        @pl.when(i == 0)
        def _():
            ext[0:POOL_HALO, :] = jnp.zeros((POOL_HALO, POOL_WIDTH), F32)

        @pl.when(i > 0)
        def _():
            ext[0:POOL_HALO, :] = save[...]
    else:
        ext[0:POOL_HALO, :] = halo_ref[0]
    x = p_ref[...]
    ext[POOL_HALO:POOL_HALO + tp, :] = x
    if carried:
        save[...] = x[tp - POOL_HALO:tp, :]
    pos = pos0 + i * tp + lax.broadcasted_iota(I32, (tp, 1), 0)
    for g, w in enumerate(POOL_WINDOWS):
        c0, c1 = g * POOL_GROUP_DIM, (g + 1) * POOL_GROUP_DIM
        xg = x[:, c0:c1]
        tot = xg
        for s in range(1, w):
            tot = tot + ext[POOL_HALO - s:POOL_HALO - s + tp, c0:c1]
        inv = 1.0 / jnp.minimum(pos + 1, w).astype(F32)
        d = (tot * inv - xg).astype(BF16)
        o_ref[:, c0:c1] = (_dot(d, w_ref[g]) * s_ref[:, c0:c1]).astype(BF16)


def _pool(pin, w_grp, scale, tile0, batch, seq, pos0, halo=None):
    carried = halo is None
    tp = min(seq, 256)
    nt = seq // tp
    kern = functools.partial(_pool_kernel, tp=tp, pos0=pos0, carried=carried)
    in_specs = [pl.BlockSpec((tp, POOL_WIDTH), lambda b, i: (tile0 + b * nt + i, 0))]
    args = [pin]
    if not carried:
        in_specs.append(pl.BlockSpec((1, POOL_HALO, POOL_WIDTH), lambda b, i: (b, 0, 0)))
        args.append(halo)
    in_specs += [pl.BlockSpec((POOL_GROUPS, POOL_GROUP_DIM, POOL_GROUP_DIM), lambda b, i: (0, 0, 0)),
                 pl.BlockSpec((1, POOL_WIDTH), lambda b, i: (0, 0))]
    args += [w_grp, scale]
    scratch = [pltpu.VMEM((POOL_HALO + tp, POOL_WIDTH), F32)]
    if carried:
        scratch.append(pltpu.VMEM((POOL_HALO, POOL_WIDTH), F32))
    return pl.pallas_call(
        kern,
        grid=(batch, nt),
        in_specs=in_specs,
        out_specs=pl.BlockSpec((tp, POOL_WIDTH), lambda b, i: (b * nt + i, 0)),
        out_shape=jax.ShapeDtypeStruct((batch * seq, POOL_WIDTH), BF16),
        scratch_shapes=scratch,
        compiler_params=_cparams(2),
        name="pool_prompt" if carried else "pool_sample",
    )(*args)


def _merge_kernel(h_ref, oap_ref, oas_ref, opp_ref, ops_ref, wga_ref, wgp_ref, ba_ref, bp_ref, woa_ref,
                  wop_ref, m_ref, *, n_prompt_tiles):
    h = h_ref[...]
    is_prompt = pl.program_id(1) < n_prompt_tiles
    oa = jnp.where(is_prompt, oap_ref[...], oas_ref[...])
    op = jnp.where(is_prompt, opp_ref[...], ops_ref[...])
    ga = jax.nn.sigmoid(_dot(h, wga_ref[...]) + ba_ref[...])
    gp = jax.nn.sigmoid(_dot(h, wgp_ref[...]) + bp_ref[...])
    a = _dot(oa, woa_ref[...])
    p = _dot(op, wop_ref[...])
    m_ref[...] = (ga * a + gp * p).astype(BF16)


def _merge(h, oa_p, oa_s, op_p, op_s, w_gate, b_gate, w_oa, w_op):
    n = h.shape[0]
    tm = 2 * TOK_TILE
    tn = D_MODEL // 2
    nj = D_MODEL // tn
    npt = oa_p.shape[0] // tm
    prompt = lambda w: pl.BlockSpec((tm, w), lambda j, i: (jnp.minimum(i, npt - 1), 0))
    sample = lambda w: pl.BlockSpec((tm, w), lambda j, i: (jnp.maximum(i - npt, 0), 0))
    return pl.pallas_call(
        functools.partial(_merge_kernel, n_prompt_tiles=npt),
        grid=(nj, n // tm),
        in_specs=[pl.BlockSpec((tm, D_MODEL), lambda j, i: (i, 0)),
                  prompt(ATTN_WIDTH), sample(ATTN_WIDTH), prompt(POOL_WIDTH), sample(POOL_WIDTH),
                  pl.BlockSpec((D_MODEL, tn), lambda j, i: (0, j)),
                  pl.BlockSpec((D_MODEL, tn), lambda j, i: (0, nj + j)),
                  pl.BlockSpec((1, tn), lambda j, i: (0, j)),
                  pl.BlockSpec((1, tn), lambda j, i: (0, nj + j)),
                  pl.BlockSpec((ATTN_WIDTH, tn), lambda j, i: (0, j)),
                  pl.BlockSpec((POOL_WIDTH, tn), lambda j, i: (0, j))],
        out_specs=pl.BlockSpec((tm, tn), lambda j, i: (i, j)),
        out_shape=jax.ShapeDtypeStruct((n, D_MODEL), BF16),
        compiler_params=_cparams(2),
        name="merge",
    )(h, oa_p, oa_s, op_p, op_s, w_gate, w_gate, b_gate, b_gate, w_oa, w_op)


_R_E1, _R_E2, _R_W1, _R_W2, _R_RANK1, _R_RANK2 = range(6)
_R_LOGIT0 = MOE_GROUPS


def _outproj_kernel(m_ref, xp_ref, xs_ref, wo_ref, g2_ref, wr_ref, br_ref,
                    x1_ref, h2_ref, route_ref, route_t_ref, counts_ref, run, *, n_prompt_tiles):
    i = pl.program_id(0)
    tm = m_ref.shape[0]

    @pl.when(i == 0)
    def _():
        run[...] = jnp.zeros(run.shape, F32)

    x = jnp.where(i < n_prompt_tiles, xp_ref[...], xs_ref[...])
    x1 = x + _dot(m_ref[...], wo_ref[...])
    x1_ref[...] = x1
    r = lax.rsqrt(jnp.mean(x1 * x1, axis=-1, keepdims=True) + EPS)
    hf = (x1 * r) * g2_ref[...]
    h2_ref[...] = hf

    hi = hf.astype(BF16)
    lo = (hf - hi.astype(F32)).astype(BF16)
    hw = _dot(hi, wr_ref[...])
    lg = hw[:, :LANES] + (hw[:, LANES:] + _dot(lo, wr_ref[:, :LANES])) + br_ref[...]

    lane = lax.broadcasted_iota(I32, (tm, LANES), 1)
    neg_inf = jnp.float32(-jnp.inf)
    is_g = lane < MOE_GROUPS
    cl = jnp.where(is_g, lg, neg_inf)
    gmax = jnp.max(cl, axis=1, keepdims=True)
    g_sel = jnp.min(jnp.where(cl == gmax, lane, LANES), axis=1, keepdims=True)
    den = jnp.sum(jnp.where(is_g, jnp.exp(cl - gmax), 0.0), axis=1, keepdims=True)
    g_val = 1.0 / den
    e_lane = lane - _R_LOGIT0
    in_grp = (e_lane >= 0) & (e_lane < N_EXPERTS) & ((e_lane >> GROUP_SHIFT) == g_sel)
    f1 = jnp.where(in_grp, lg, neg_inf)
    v1 = jnp.max(f1, axis=1, keepdims=True)
    i1 = jnp.min(jnp.where(f1 == v1, lane, LANES), axis=1, keepdims=True)
    f2 = jnp.where(lane == i1, neg_inf, f1)
    v2 = jnp.max(f2, axis=1, keepdims=True)
    i2 = jnp.min(jnp.where(f2 == v2, lane, LANES), axis=1, keepdims=True)
    e2 = jnp.exp(v2 - v1)
    w1 = g_val / (1.0 + e2)
    w2 = g_val * e2 / (1.0 + e2)

    oh1 = lane == i1
    oh2 = lane == i2
    c = (oh1 | oh2).astype(BF16)
    rr = lax.broadcasted_iota(I32, (tm, tm), 0)
    cc = lax.broadcasted_iota(I32, (tm, tm), 1)
    before = (cc < rr).astype(BF16)
    prior = _dot(before, c) + run[...]
    rank1 = jnp.sum(jnp.where(oh1, prior, 0.0), axis=1, keepdims=True)
    rank2 = jnp.sum(jnp.where(oh2, prior, 0.0), axis=1, keepdims=True)
    run[...] = run[...] + jnp.sum(c.astype(F32), axis=0, keepdims=True)
    counts_ref[...] = jnp.broadcast_to(run[...], counts_ref.shape)

    rec = jnp.zeros((tm, LANES), F32)
    for ln, val in ((_R_E1, (i1 - _R_LOGIT0).astype(F32)), (_R_E2, (i2 - _R_LOGIT0).astype(F32)),
                    (_R_W1, w1), (_R_W2, w2), (_R_RANK1, rank1), (_R_RANK2, rank2)):
        rec = jnp.where(lane == ln, val, rec)
    route_ref[...] = rec
    route_t_ref[...] = rec.T[0:route_t_ref.shape[0], :]


def _out_proj(m, xp, xs, w_out, g2, wr_split, b_r):
    n = m.shape[0]
    tm = TOK_TILE
    n_prompt_tiles = xp.shape[0] // tm
    tok = lambda w: pl.BlockSpec((tm, w), lambda i: (i, 0))
    const = lambda s: pl.BlockSpec(s, lambda i: (0,) * len(s))
    return pl.pallas_call(
        functools.partial(_outproj_kernel, n_prompt_tiles=n_prompt_tiles),
        grid=(n // tm,),
        in_specs=[tok(D_MODEL), *_split_token_specs(tm, D_MODEL, n_prompt_tiles),
                  pl.BlockSpec((D_MODEL, D_MODEL), lambda i: (0, 0), pipeline_mode=pl.Buffered(1)),
                  const((1, D_MODEL)), const((D_MODEL, 2 * LANES)), const((1, LANES))],
        out_specs=(tok(D_MODEL), tok(D_MODEL), tok(LANES), pl.BlockSpec((8, tm), lambda i: (0, i)),
                   const((8, LANES))),
        out_shape=(jax.ShapeDtypeStruct((n, D_MODEL), F32),
                   jax.ShapeDtypeStruct((n, D_MODEL), F32),
                   jax.ShapeDtypeStruct((n, LANES), F32),
                   jax.ShapeDtypeStruct((8, n), F32),
                   jax.ShapeDtypeStruct((8, LANES), F32)),
        scratch_shapes=[pltpu.VMEM((1, LANES), F32)],
        compiler_params=_cparams(1),
        name="out_proj",
    )(m, xp, xs, w_out, g2, wr_split, b_r)


DISPATCH_TOKENS = 256


def _dispatch_kernel(lo_ref, hi_ref, pos1_ref, pos2_ref, h2_ref, xs_ref, zrow, sem, zsem):
    i = pl.program_id(0)
    tb = DISPATCH_TOKENS
    dst = (pos1_ref, pos2_ref)
    n_seg = lo_ref.shape[0]

    def zero_copy(r):
        return pltpu.make_async_copy(zrow.at[pl.ds(0, 1)], xs_ref.at[pl.ds(r, 1)], zsem)

    def for_each_zero_row(fn):
        def seg(s, carry):
            def one(r, c):
                fn(r)
                return c
            lax.fori_loop(lo_ref[s], hi_ref[s], one, 0)
            return carry
        lax.fori_loop(0, n_seg, seg, 0)

    @pl.when(i == 0)
    def _():
        zrow[...] = jnp.zeros(zrow.shape, F32)
        for_each_zero_row(lambda r: zero_copy(r).start())

    def row_copy(r, slot):
        return pltpu.make_async_copy(h2_ref.at[pl.ds(r, 1)], xs_ref.at[pl.ds(dst[slot][r], 1)], sem)

    def issue(r, carry):
        row_copy(r, 0).start()
        row_copy(r, 1).start()
        return carry

    lax.fori_loop(0, tb, issue, 0, unroll=8)

    def drain(r, carry):
        row_copy(r, 0).wait()
        row_copy(r, 1).wait()
        return carry

    lax.fori_loop(0, tb, drain, 0, unroll=8)

    @pl.when(i == pl.num_programs(0) - 1)
    def _():
        for_each_zero_row(lambda r: zero_copy(r).wait())


def _dispatch(zero_lo, zero_hi, pos1, pos2, h2, n_rows):
    n = h2.shape[0]
    tb = DISPATCH_TOKENS
    grid_spec = pltpu.PrefetchScalarGridSpec(
        num_scalar_prefetch=2,
        grid=(n // tb,),
        in_specs=[pl.BlockSpec((tb,), lambda i, lo, hi: (i,), memory_space=pltpu.SMEM),
                  pl.BlockSpec((tb,), lambda i, lo, hi: (i,), memory_space=pltpu.SMEM),
                  pl.BlockSpec((tb, D_MODEL), lambda i, lo, hi: (i, 0))],
        out_specs=pl.BlockSpec(memory_space=pl.ANY),
        scratch_shapes=[pltpu.VMEM((8, D_MODEL), F32),
                        pltpu.SemaphoreType.DMA(()),
                        pltpu.SemaphoreType.DMA(())],
    )
    return pl.pallas_call(
        _dispatch_kernel,
        grid_spec=grid_spec,
        out_shape=jax.ShapeDtypeStruct((n_rows, D_MODEL), F32),
        compiler_params=_cparams(1),
        name="dispatch",
    )(zero_lo, zero_hi, pos1, pos2, h2)


def _experts_kernel(te_ref, nu_ref, xs_ref, wg_ref, wu_ref, wd_ref, ys_ref, wg16, wu16, wd16):
    i = pl.program_id(0)
    prev = te_ref[jnp.maximum(i - 1, 0)]
    fresh = (i == 0) | (te_ref[i] != prev)
    used = i < nu_ref[0]

    @pl.when(fresh & used)
    def _():
        wg16[...] = wg_ref[0].astype(BF16)
        wu16[...] = wu_ref[0].astype(BF16)
        wd16[...] = wd_ref[0].astype(BF16)

    @pl.when(used)
    def _():
        x = xs_ref[...].astype(BF16)
        a = _dot(x, wg16[...])
        u = _dot(x, wu16[...])
        act = (a * jax.nn.sigmoid(a)) * u
        ys_ref[...] = _dot(act.astype(BF16), wd16[...])

    @pl.when(jnp.logical_not(used))
    def _():
        ys_ref[...] = jnp.zeros(ys_ref.shape, F32)


def _experts(tile_expert, n_used, xs, w_g, w_u, w_d):
    n_rows = xs.shape[0]
    tm = EXP_TILE
    grid_spec = pltpu.PrefetchScalarGridSpec(
        num_scalar_prefetch=2,
        grid=(n_rows // tm,),
        in_specs=[pl.BlockSpec((tm, D_MODEL), lambda i, te, nu: (i, 0)),
                  pl.BlockSpec((1, D_MODEL, D_EXPERT), lambda i, te, nu: (te[i], 0, 0)),
                  pl.BlockSpec((1, D_MODEL, D_EXPERT), lambda i, te, nu: (te[i], 0, 0)),
                  pl.BlockSpec((1, D_EXPERT, D_MODEL), lambda i, te, nu: (te[i], 0, 0))],
        out_specs=pl.BlockSpec((tm, D_MODEL), lambda i, te, nu: (i, 0)),
        scratch_shapes=[pltpu.VMEM((D_MODEL, D_EXPERT), BF16),
                        pltpu.VMEM((D_MODEL, D_EXPERT), BF16),
                        pltpu.VMEM((D_EXPERT, D_MODEL), BF16)],
    )
    return pl.pallas_call(
        _experts_kernel,
        grid_spec=grid_spec,
        out_shape=jax.ShapeDtypeStruct((n_rows, D_MODEL), F32),
        compiler_params=_cparams(1),
        name="experts",
    )(tile_expert, n_used, xs, w_g, w_u, w_d)


COMBINE_TOKENS = 256


def _combine_kernel(pos1_ref, pos2_ref, pos1_next_ref, pos2_next_ref, x1_ref, route_ref, gf_ref, ys_ref,
                    yp_ref, ysm_ref, ybuf, sem, *, n_prompt_tiles):
    tc = COMBINE_TOKENS
    i = pl.program_id(0)
    buf = i % 2

    def start_rows(p_refs, b):
        for r in range(tc):
            for slot in range(2):
                pltpu.make_async_copy(ys_ref.at[pl.ds(p_refs[slot][r], 1)],
                                      ybuf.at[b, slot, pl.ds(r, 1)], sem.at[b]).start()

    def wait_rows(b):
        for slot in range(2):
            pltpu.make_async_copy(ys_ref.at[pl.ds(0, tc)], ybuf.at[b, slot], sem.at[b]).wait()

    @pl.when(i == 0)
    def _():
        start_rows((pos1_ref, pos2_ref), 0)

    wait_rows(buf)
    start_rows((pos1_next_ref, pos2_next_ref), 1 - buf)

    route = route_ref[...]
    w1 = route[:, _R_W1:_R_W1 + 1]
    w2 = route[:, _R_W2:_R_W2 + 1]
    x2 = x1_ref[...] + (w1 * ybuf[buf, 0] + w2 * ybuf[buf, 1])
    r = lax.rsqrt(jnp.mean(x2 * x2, axis=-1, keepdims=True) + EPS)
    y = (x2 * r) * gf_ref[...]
    is_prompt = pl.program_id(0) < n_prompt_tiles

    @pl.when(is_prompt)
    def _():
        yp_ref[...] = y

    @pl.when(jnp.logical_not(is_prompt))
    def _():
        ysm_ref[...] = y

    @pl.when(i == pl.num_programs(0) - 1)
    def _():
        wait_rows(1 - buf)


def _combine(pos1, pos2, x1, route, gf, ys, n_prompt):
    n = x1.shape[0]
    tc = COMBINE_TOKENS
    n_prompt_tiles = n_prompt // tc
    n_steps = n // tc
    return pl.pallas_call(
        functools.partial(_combine_kernel, n_prompt_tiles=n_prompt_tiles),
        grid=(n_steps,),
        in_specs=[pl.BlockSpec((tc,), lambda i: (i,), memory_space=pltpu.SMEM),
                  pl.BlockSpec((tc,), lambda i: (i,), memory_space=pltpu.SMEM),
                  pl.BlockSpec((tc,), lambda i: (jnp.minimum(i + 1, n_steps - 1),), memory_space=pltpu.SMEM),
                  pl.BlockSpec((tc,), lambda i: (jnp.minimum(i + 1, n_steps - 1),), memory_space=pltpu.SMEM),
                  pl.BlockSpec((tc, D_MODEL), lambda i: (i, 0)),
                  pl.BlockSpec((tc, LANES), lambda i: (i, 0)),
                  pl.BlockSpec((1, D_MODEL), lambda i: (0, 0)),
                  pl.BlockSpec(memory_space=pl.ANY)],
        out_specs=_split_token_specs(tc, D_MODEL, n_prompt_tiles),
        out_shape=(jax.ShapeDtypeStruct((n_prompt, D_MODEL), F32),
                   jax.ShapeDtypeStruct((n - n_prompt, D_MODEL), F32)),
        scratch_shapes=[pltpu.VMEM((2, 2, tc, D_MODEL), F32), pltpu.SemaphoreType.DMA((2,))],
        compiler_params=_cparams(1),
        name="combine",
    )(pos1, pos2, pos1, pos2, x1, route, gf, ys)


def _split_w_in(w_in):
    w_main = w_in[:, :_C_KW].astype(BF16)
    w_kw = jnp.pad(w_in[:, _C_KW:_C_P].astype(BF16), ((0, 0), (0, LANES - (_C_P - _C_KW))))
    w_pool = w_in[:, _C_P:].astype(BF16)
    return w_main, w_kw, w_pool


def _forward(x_prompt, x_sample, cache_k, cache_v, cache_kidx, state_pool,
             norm1_g, w_in, w_gate, b_gate, w_pool_grp, pool_scale, w_o_attn, w_o_pool, w_out,
             norm2_g, w_rg, b_rg, w_re, b_re, w_eg, w_eu, w_ed, norm_f_g):
    batch, seq, d = x_prompt.shape
    dec_batch, dec_seq, _ = x_sample.shape
    past = cache_k.shape[1]
    n_p = batch * seq
    n_s = dec_batch * dec_seq
    n = n_p + n_s
    tm = TOK_TILE
    assert d == D_MODEL and seq % tm == 0 and n_s % tm == 0 and n % (2 * tm) == 0
    assert past % KEY_BLOCK == 0 and dec_seq >= POOL_STATE and dec_seq <= KEY_BLOCK

    xp = x_prompt.reshape(n_p, d)
    xs = x_sample.reshape(n_s, d)
    pos_tab = jnp.concatenate([jnp.arange(seq), past + jnp.arange(n_s) % dec_seq])
    tables = _rope_tables(pos_tab)
    h, q, k_p, k_s, v_p, v_s, qi, kw, pin = _in_proj(
        xp, xs, norm1_g.reshape(1, d), *_split_w_in(w_in), tables, seq // tm)

    oa_p = _dsa_prompt(q, qi, kw, k_p, v_p, batch, seq)
    oa_s = _dsa_sample(q, qi, kw, k_s, v_s,
                       cache_k.reshape(dec_batch * past, KV_WIDTH),
                       cache_v.reshape(dec_batch * past, KV_WIDTH),
                       cache_kidx.reshape(dec_batch * past, IDX_DIM),
                       n_p, dec_batch, dec_seq, past)

    w_grp16 = w_pool_grp.astype(BF16)
    scale = pool_scale.reshape(1, POOL_WIDTH)
    op_p = _pool(pin, w_grp16, scale, 0, batch, seq, 0)
    halo = jnp.concatenate([jnp.zeros((dec_batch, 1, POOL_WIDTH), F32), state_pool], axis=1)
    op_s = _pool(pin, w_grp16, scale, n_p // dec_seq, dec_batch, dec_seq, past, halo=halo)

    m = _merge(h, oa_p, oa_s, op_p, op_s, w_gate.astype(BF16), b_gate.reshape(1, 2 * d),
               w_o_attn.astype(BF16), w_o_pool.astype(BF16))

    w_r = jnp.concatenate([w_rg, w_re, jnp.zeros((d, LANES - MOE_GROUPS - N_EXPERTS), F32)], axis=1)
    b_r = jnp.concatenate([b_rg, b_re, jnp.zeros((LANES - MOE_GROUPS - N_EXPERTS,), F32)]).reshape(1, LANES)
    wr_hi = w_r.astype(BF16)
    wr_lo = (w_r - wr_hi.astype(F32)).astype(BF16)
    wr_split = jnp.concatenate([wr_hi, wr_lo], axis=1)
    x1, h2, route, route_t, counts = _out_proj(m, xp, xs, w_out.astype(BF16), norm2_g.reshape(1, d), wr_split, b_r)

    te = EXP_TILE
    n_tiles = (2 * n) // te + N_EXPERTS
    cnt = counts[0, _R_LOGIT0:_R_LOGIT0 + N_EXPERTS].astype(I32)
    tiles_per_e = (cnt + te - 1) // te
    tile_end = jnp.cumsum(tiles_per_e)
    row_start = (tile_end - tiles_per_e) * te
    n_used = tile_end[-1:].astype(I32)
    tile_ids = jnp.arange(n_tiles, dtype=I32)
    tile_expert = jnp.minimum(
        jnp.sum((tile_end[None, :] <= tile_ids[:, None]).astype(I32), axis=1), N_EXPERTS - 1)
    last_e = tile_expert[jnp.maximum(n_used[0] - 1, 0)]
    tile_expert = jnp.where(tile_ids < n_used[0], tile_expert, last_e)
    pos1 = row_start[route_t[_R_E1].astype(I32)] + route_t[_R_RANK1].astype(I32)
    pos2 = row_start[route_t[_R_E2].astype(I32)] + route_t[_R_RANK2].astype(I32)

    zero_lo = jnp.concatenate([row_start + cnt, n_used * te]).astype(I32)
    zero_hi = jnp.concatenate([row_start + tiles_per_e * te, jnp.full((1,), n_tiles * te, I32)]).astype(I32)
    xs = _dispatch(zero_lo, zero_hi, pos1, pos2, h2, n_tiles * te)
    ys = _experts(tile_expert, n_used, xs, w_eg, w_eu, w_ed)
    y_p, y_s = _combine(pos1, pos2, x1, route, norm_f_g.reshape(1, d), ys, n_p)

    y_prompt = y_p.reshape(batch, seq, d)
    y_sample = y_s.reshape(dec_batch, dec_seq, d)
    k_p = k_p.reshape(1, batch, seq, N_KV_HEADS, HEAD_DIM)
    v_p = v_p.reshape(1, batch, seq, N_KV_HEADS, HEAD_DIM)
    ki_p = kw[:n_p, :IDX_DIM].reshape(1, batch, seq, IDX_DIM)
    pool_p = jnp.stack([pin[(b + 1) * seq - POOL_STATE:(b + 1) * seq] for b in range(batch)])[None]
    k_s = k_s.reshape(1, dec_batch, dec_seq, N_KV_HEADS, HEAD_DIM)
    v_s = v_s.reshape(1, dec_batch, dec_seq, N_KV_HEADS, HEAD_DIM)
    ki_s = kw[n_p:, :IDX_DIM].reshape(1, dec_batch, dec_seq, IDX_DIM)
    pool_s = pin[n_p:].reshape(dec_batch, dec_seq, POOL_WIDTH)[None, :, dec_seq - POOL_STATE:, :]
    return (y_prompt, y_sample, k_p, v_p, ki_p, pool_p, k_s, v_s, ki_s, pool_s)


def kernel(x_prompt, x_sample, cache_k, cache_v, cache_kidx, state_pool, norm1_g, w_in, w_gate, b_gate,
           w_pool_grp, pool_scale, w_o_attn, w_o_pool, w_out, norm2_g, w_router_group, b_router_group,
           w_router_expert, b_router_expert, w_exp_gate, w_exp_up, w_exp_down, norm_f_g):
    assert cache_k.shape[0] == 1, "single-layer model"
    return _forward(x_prompt, x_sample, cache_k[0], cache_v[0], cache_kidx[0], state_pool[0],
                    norm1_g[0], w_in[0], w_gate[0], b_gate[0], w_pool_grp[0], pool_scale[0],
                    w_o_attn[0], w_o_pool[0], w_out[0], norm2_g[0], w_router_group[0],
                    b_router_group[0], w_router_expert[0], b_router_expert[0],
                    w_exp_gate[0], w_exp_up[0], w_exp_down[0], norm_f_g)
```

```python
import functools

import numpy as np
import jax
import jax.numpy as jnp
from jax import lax
from jax.experimental import pallas as pl
from jax.experimental.pallas import tpu as pltpu

F32 = jnp.float32
BF16 = jnp.bfloat16
I32 = jnp.int32

D_MODEL = 2048
CHUNK = 64
N_HEADS = 8
N_KV_HEADS = 2
HEAD_DIM = 128
GQA_GROUP = N_HEADS // N_KV_HEADS
ATTN_WIDTH = N_HEADS * HEAD_DIM
KV_WIDTH = N_KV_HEADS * HEAD_DIM
IDX_HEADS = 16
IDX_DIM = 64
IDX_PAIRS = IDX_HEADS // 2
TOPK_MAX = 256
POOL_WINDOWS = (2, 4, 8, 16)
POOL_GROUPS = 4
POOL_WIDTH = D_MODEL // 2
POOL_GROUP_DIM = POOL_WIDTH // POOL_GROUPS
POOL_STATE = 15
POOL_HALO = 16
MOE_GROUPS = 4
EXPERTS_PER_GROUP = 8
N_EXPERTS = MOE_GROUPS * EXPERTS_PER_GROUP
D_EXPERT = 512
ROPE_THETA = 10000.0
CHUNK_SHIFT = CHUNK.bit_length() - 1
GROUP_SHIFT = EXPERTS_PER_GROUP.bit_length() - 1
EPS = 1e-6

LANES = 128
INT_MIN = -(2 ** 31)
NEG_BIG = -1e30
LOG2_E = 1.4426950408889634
VMEM_LIMIT = 56 * 1024 * 1024

KEY_BLOCK = 256
TOK_TILE = 256
EXP_TILE = 256


def _cparams(n_axes, allow_input_fusion=None):
    return pltpu.CompilerParams(dimension_semantics=("arbitrary",) * n_axes,
                                vmem_limit_bytes=VMEM_LIMIT, allow_input_fusion=allow_input_fusion)


def _dot(a, b):
    return jnp.dot(a, b, preferred_element_type=F32)


def _dot_nt(a, b):
    return lax.dot_general(a, b, (((1,), (1,)), ((), ())), preferred_element_type=F32)


_C_Q = 0
_C_K = _C_Q + ATTN_WIDTH
_C_V = _C_K + KV_WIDTH
_C_QI = _C_V + KV_WIDTH
_C_KW = _C_QI + IDX_HEADS * IDX_DIM
_C_P = _C_KW + IDX_DIM + IDX_HEADS


def _rope128(y, cos, sin_signed):
    return y * cos + pltpu.roll(y, HEAD_DIM // 2, 1) * sin_signed


def _rope64(y, cos, sin_a, sin_b):
    half = IDX_DIM // 2
    return y * cos + pltpu.roll(y, LANES - half, 1) * sin_a + pltpu.roll(y, half, 1) * sin_b


def _inproj_kernel(xp_ref, xs_ref, g_ref, w_ref, wkw_ref, wp_ref, ck_ref, sk_ref, cq_ref, sq_ref, ci_ref,
                   sia_ref, sib_ref, cw_ref, swa_ref, swb_ref,
                   h_ref, q_ref, kp_ref, ks_ref, vp_ref, vs_ref, qi_ref, kw_ref, p_ref, *, n_prompt_tiles):
    is_prompt = pl.program_id(0) < n_prompt_tiles
    x = jnp.where(is_prompt, xp_ref[...], xs_ref[...])
    r = lax.rsqrt(jnp.mean(x * x, axis=-1, keepdims=True) + EPS)
    h = ((x * r) * g_ref[...]).astype(BF16)
    h_ref[...] = h
    cq, sq = cq_ref[...], sq_ref[...]
    for c in range(ATTN_WIDTH // 256):
        y = _dot(h, w_ref[:, _C_Q + c * 256:_C_Q + (c + 1) * 256])
        for s in range(2):
            q_ref[:, c * 256 + s * LANES:c * 256 + (s + 1) * LANES] = _rope128(
                y[:, s * LANES:(s + 1) * LANES], cq, sq).astype(BF16)
    ck, sk = ck_ref[...], sk_ref[...]
    y = _dot(h, w_ref[:, _C_K:_C_K + KV_WIDTH])
    k = jnp.concatenate([_rope128(y[:, s * LANES:(s + 1) * LANES], ck, sk) for s in range(N_KV_HEADS)], axis=1)
    v = _dot(h, w_ref[:, _C_V:_C_V + KV_WIDTH])

    @pl.when(is_prompt)
    def _():
        kp_ref[...] = k
        vp_ref[...] = v

    @pl.when(jnp.logical_not(is_prompt))
    def _():
        ks_ref[...] = k
        vs_ref[...] = v

    ci, sia, sib = ci_ref[...], sia_ref[...], sib_ref[...]
    for c in range(IDX_PAIRS // 2):
        y = _dot(h, w_ref[:, _C_QI + c * 256:_C_QI + (c + 1) * 256])
        for s in range(2):
            qi_ref[2 * c + s] = _rope64(y[:, s * LANES:(s + 1) * LANES], ci, sia, sib).astype(BF16)
    y = _dot(h, wkw_ref[...])
    kw_ref[...] = _rope64(y, cw_ref[...], swa_ref[...], swb_ref[...])
    for c in range(POOL_WIDTH // 256):
        p_ref[:, c * 256:(c + 1) * 256] = _dot(h, wp_ref[:, c * 256:(c + 1) * 256])


def _rope_tables(pos):
    pos = pos.astype(F32)[:, None]

    def cs(dim):
        half = dim // 2
        inv = ROPE_THETA ** (-jnp.arange(half, dtype=F32) / half)
        ang = pos * inv[None, :]
        return jnp.cos(ang), jnp.sin(ang)

    c, s = cs(HEAD_DIM)
    ck = jnp.concatenate([c, c], axis=1)
    sk = jnp.concatenate([-s, s], axis=1)
    qs = HEAD_DIM ** -0.5 * LOG2_E
    c, s = cs(IDX_DIM)
    z = jnp.zeros_like(s)
    ci = jnp.concatenate([c, c, c, c], axis=1)
    sia = jnp.concatenate([-s, z, -s, z], axis=1)
    sib = jnp.concatenate([z, s, z, s], axis=1)
    iscale = IDX_DIM ** -0.5
    n = pos.shape[0]
    wscale = jnp.full((n, IDX_HEADS), IDX_HEADS ** -0.5, F32)
    pad = jnp.zeros((n, LANES - IDX_DIM - IDX_HEADS), F32)
    zz = jnp.zeros((n, LANES - IDX_DIM), F32)
    cw = jnp.concatenate([c, c, wscale, pad], axis=1)
    swa = jnp.concatenate([-s, z, zz], axis=1)
    swb = jnp.concatenate([z, s, zz], axis=1)
    return (ck, sk, ck * qs, sk * qs, ci * iscale, sia * iscale, sib * iscale, cw, swa, swb)


def _split_token_specs(tm, width, n_prompt_tiles):
    prompt = pl.BlockSpec((tm, width), lambda i: (jnp.minimum(i, n_prompt_tiles - 1), 0))
    sample = pl.BlockSpec((tm, width), lambda i: (jnp.maximum(i - n_prompt_tiles, 0), 0))
    return prompt, sample


def _in_proj(xp, xs, g1, w_main, w_kw, w_pool, tables, tiles_per_seq):
    tm = TOK_TILE
    n_prompt_tiles = xp.shape[0] // tm
    n = xp.shape[0] + xs.shape[0]
    n_tiles = n // tm

    def tab_idx(i):
        return (jnp.where(i < n_prompt_tiles, i % tiles_per_seq, tiles_per_seq + i - n_prompt_tiles), 0)

    tok = lambda w: pl.BlockSpec((tm, w), lambda i: (i, 0))
    tab = pl.BlockSpec((tm, LANES), tab_idx)
    resident = lambda a: pl.BlockSpec(a.shape, lambda i: (0, 0), pipeline_mode=pl.Buffered(1))
    n_s = xs.shape[0]
    kv_p, kv_s = _split_token_specs(tm, KV_WIDTH, n_prompt_tiles)
    out_shape = (
        jax.ShapeDtypeStruct((n, D_MODEL), BF16),
        jax.ShapeDtypeStruct((n, ATTN_WIDTH), BF16),
        jax.ShapeDtypeStruct((n - n_s, KV_WIDTH), F32),
        jax.ShapeDtypeStruct((n_s, KV_WIDTH), F32),
        jax.ShapeDtypeStruct((n - n_s, KV_WIDTH), F32),
        jax.ShapeDtypeStruct((n_s, KV_WIDTH), F32),
        jax.ShapeDtypeStruct((IDX_PAIRS, n, LANES), BF16),
        jax.ShapeDtypeStruct((n, LANES), F32),
        jax.ShapeDtypeStruct((n, POOL_WIDTH), F32),
    )
    return pl.pallas_call(
        functools.partial(_inproj_kernel, n_prompt_tiles=n_prompt_tiles),
        grid=(n_tiles,),
        in_specs=[*_split_token_specs(tm, D_MODEL, n_prompt_tiles),
                  pl.BlockSpec((1, D_MODEL), lambda i: (0, 0)),
                  resident(w_main), resident(w_kw), resident(w_pool)]
                 + [tab] * 10,
        out_specs=(tok(D_MODEL), tok(ATTN_WIDTH), kv_p, kv_s, kv_p, kv_s,
                   pl.BlockSpec((IDX_PAIRS, tm, LANES), lambda i: (0, i, 0)),
                   tok(LANES), tok(POOL_WIDTH)),
        out_shape=out_shape,
        compiler_params=_cparams(1),
        name="in_proj",
    )(xp, xs, g1, w_main, w_kw, w_pool, *tables)


def _ordinal_to_f32(k):
    return lax.bitcast_convert_type(k ^ ((k >> 31) & 0x7FFFFFFF), F32)


def _kth_largest_score(topk, shape, n_key_rows, count_where):
    def count_ge(cand):
        cand_f = _ordinal_to_f32(cand)
        return count_where(lambda keys, kb: keys >= jnp.broadcast_to(cand_f, keys.shape))

    c0 = count_ge(jnp.zeros(shape, I32))
    thr0 = jnp.where(c0 >= topk, 0, INT_MIN).astype(I32)
    cnt0 = jnp.where(c0 >= topk, c0, n_key_rows).astype(I32)

    def bit_step(i, carry):
        thr, cnt = carry
        cand = thr + (jnp.int32(1) << (30 - i))
        c = count_ge(cand)
        ok = c >= topk
        return jnp.where(ok, cand, thr), jnp.where(ok, c, cnt)

    thr, cnt = lax.fori_loop(0, 31, bit_step, (thr0, cnt0))
    has_k = thr > INT_MIN
    thr_f = jnp.where(has_k, _ordinal_to_f32(thr), jnp.finfo(F32).min)
    return thr_f, has_k & (cnt > topk)


def _drop_excess_ties(sc, n_blocks, topk, thr, need_tie, key_index, count_where):
    n_gt = count_where(lambda keys, kb: keys > jnp.broadcast_to(thr, keys.shape))
    n_keep = topk - n_gt

    def idx_step(i, bound):
        cand = bound + (jnp.int32(1) << (14 - i))
        c = count_where(lambda keys, kb: (keys == jnp.broadcast_to(thr, keys.shape))
                        & (key_index(kb) < jnp.broadcast_to(cand, keys.shape)))
        return jnp.where(c <= n_keep, cand, bound)

    bound = lax.fori_loop(0, 15, idx_step, jnp.zeros(thr.shape, I32))
    bound = jnp.where(need_tie, bound, jnp.int32(2 ** 30))

    def drop_block(kb, carry):
        keys = sc[kb]
        drop = (keys == jnp.broadcast_to(thr, keys.shape)) & (key_index(kb) >= jnp.broadcast_to(bound, keys.shape))
        sc[kb] = jnp.where(drop, -jnp.inf, keys)
        return carry

    lax.fori_loop(0, n_blocks, drop_block, 0)


def _stack_queries_t(q_ref, qt):
    for g in range(N_KV_HEADS):
        qg = jnp.concatenate([q_ref[:, (g * GQA_GROUP + j) * HEAD_DIM:(g * GQA_GROUP + j + 1) * HEAD_DIM]
                              for j in range(GQA_GROUP)], axis=0)
        qt[g] = qg.astype(F32).T.astype(BF16)


def _attention_scratch(rows):
    return [
        pltpu.VMEM((N_KV_HEADS, 1, rows), F32),
        pltpu.VMEM((N_KV_HEADS, 1, rows), F32),
        pltpu.VMEM((N_KV_HEADS, HEAD_DIM, rows), F32),
        pltpu.VMEM((2, N_KV_HEADS, KEY_BLOCK, rows), F32),
        pltpu.VMEM((2, N_KV_HEADS, KEY_BLOCK, rows), BF16),
        pltpu.VMEM((2, N_KV_HEADS, 1, rows), F32),
    ]


def _masked_attention(o_ref, n_blocks, n_blocks_all, bias_t, k16, vt16, qt, att, tq):
    m_s, l_s, acc_s, lg_s, p_s, alpha_s = att
    m_s[...] = jnp.full(m_s.shape, NEG_BIG, F32)
    l_s[...] = jnp.zeros(l_s.shape, F32)
    acc_s[...] = jnp.zeros(acc_s.shape, F32)
    p_s[1] = jnp.zeros(p_s.shape[1:], BF16)
    alpha_s[1] = jnp.ones(alpha_s.shape[1:], F32)

    def logits_of(kb, slot):
        r0 = pl.multiple_of(kb * KEY_BLOCK, KEY_BLOCK)
        for g in range(N_KV_HEADS):
            lg_s[slot, g] = _dot(k16[pl.ds(r0, KEY_BLOCK), g * HEAD_DIM:(g + 1) * HEAD_DIM], qt[g])

    def value_product(kb, slot):
        for g in range(N_KV_HEADS):
            acc_s[g] = (alpha_s[slot, g] * acc_s[g]
                        + _dot(vt16[kb, g * HEAD_DIM:(g + 1) * HEAD_DIM, :], p_s[slot, g]))

    logits_of(0, 0)
    last = n_blocks_all - 1

    def step(kb, slot):
        logits_of(jnp.minimum(kb + 1, last), 1 - slot)
        bias = jnp.where(kb < n_blocks, bias_t(jnp.minimum(kb, last)), NEG_BIG)
        for g in range(N_KV_HEADS):
            logits = lg_s[slot, g] + bias
            m_prev = m_s[g]
            m_new = jnp.maximum(m_prev, jnp.max(logits, axis=0, keepdims=True))
            p = jnp.exp2(logits - m_new)
            alpha = jnp.exp2(m_prev - m_new)
            l_s[g] = alpha * l_s[g] + jnp.sum(p, axis=0, keepdims=True)
            p_s[slot, g] = p.astype(BF16)
            alpha_s[slot, g] = alpha
            m_s[g] = m_new
        value_product(jnp.clip(kb - 1, 0, last), 1 - slot)

    def two_steps(i, carry):
        step(2 * i, 0)
        step(2 * i + 1, 1)
        return carry

    n_pairs = (n_blocks + 1) // 2
    lax.fori_loop(0, n_pairs, two_steps, 0)
    value_product(jnp.minimum(2 * n_pairs - 1, last), 1)

    for g in range(N_KV_HEADS):
        o = (acc_s[g] * (1.0 / l_s[g])).T
        for j in range(GQA_GROUP):
            hd = g * GQA_GROUP + j
            o_ref[:, hd * HEAD_DIM:(hd + 1) * HEAD_DIM] = o[j * tq:(j + 1) * tq, :].astype(BF16)


def _dsa_prompt_kernel(q_ref, qi_ref, wq_ref, kn_ref, vn_ref, kwn_ref, o_ref,
                       k16, vt16, kia, kib, sc, raw, qt, *att, tq, seq, topk):
    qb = pl.program_id(1)
    n_blocks_all = seq // KEY_BLOCK

    @pl.when(qb == 0)
    def _():
        lane = lax.broadcasted_iota(I32, (seq, LANES), 1)
        kin = jnp.where(lane < IDX_DIM, kwn_ref[...], 0.0)
        k16[...] = kn_ref[...].astype(BF16)
        kia[...] = kin.astype(BF16)
        kib[...] = pltpu.roll(kin, IDX_DIM, 1).astype(BF16)
        for kb in range(n_blocks_all):
            vt16[kb] = vn_ref[kb * KEY_BLOCK:(kb + 1) * KEY_BLOCK, :].T.astype(BF16)

    n_blocks = qb // (KEY_BLOCK // tq) + 1

    qit = qi_ref[...].reshape(IDX_PAIRS * tq, LANES).astype(F32).T.astype(BF16)
    wt = wq_ref[...].T
    w_rows = [wt[IDX_DIM + j:IDX_DIM + j + 1, :] for j in range(IDX_HEADS)]
    _stack_queries_t(q_ref, qt)

    key_row = lax.broadcasted_iota(I32, (KEY_BLOCK, tq), 0)
    q_pos = qb * tq + lax.broadcasted_iota(I32, (KEY_BLOCK, tq), 1)

    last = n_blocks_all - 1
    n_pairs = (n_blocks + 1) // 2

    def raw_scores(kb, slot):
        r0 = pl.multiple_of(kb * KEY_BLOCK, KEY_BLOCK)
        raw[slot, 0] = _dot(kia[pl.ds(r0, KEY_BLOCK), :], qit)
        raw[slot, 1] = _dot(kib[pl.ds(r0, KEY_BLOCK), :], qit)

    def score_step(kb, slot):
        raw_scores(jnp.minimum(kb + 1, last), 1 - slot)
        se = raw[slot, 0]
        so = raw[slot, 1]
        score = jnp.zeros((KEY_BLOCK, tq), F32)
        for p in range(IDX_PAIRS):
            score = (score + jnp.maximum(se[:, p * tq:(p + 1) * tq], 0.0) * w_rows[2 * p]
                     + jnp.maximum(so[:, p * tq:(p + 1) * tq], 0.0) * w_rows[2 * p + 1])
        adm = ((kb * KEY_BLOCK + key_row) >> CHUNK_SHIFT) <= (q_pos >> CHUNK_SHIFT)
        sc[jnp.where(kb < n_blocks, kb, n_blocks_all)] = jnp.where(adm, score, -jnp.inf)

    def two_score_steps(i, carry):
        score_step(2 * i, 0)
        score_step(2 * i + 1, 1)
        return carry

    raw_scores(0, 0)
    lax.fori_loop(0, n_pairs, two_score_steps, 0)

    def count_where(pred):
        def body(i, acc):
            for s in range(2):
                kb = 2 * i + s
                kbc = jnp.minimum(kb, last)
                m = pred(sc[kbc], kbc).astype(I32)
                part = jnp.sum(m.reshape(KEY_BLOCK // 8, 8, tq), axis=0)
                acc = acc + jnp.where(kb < n_blocks, part, 0)
            return acc
        acc = lax.fori_loop(0, n_pairs, body, jnp.zeros((8, tq), I32))
        return jnp.sum(acc, axis=0, keepdims=True)

    thr, need_tie = _kth_largest_score(topk, (1, tq), seq, count_where)

    @pl.when(jnp.max(need_tie.astype(I32)) > 0)
    def _():
        _drop_excess_ties(sc, n_blocks, topk, thr, need_tie,
                          lambda kb: kb * KEY_BLOCK + key_row, count_where)

    thr_b = jnp.broadcast_to(thr, (KEY_BLOCK, tq))

    def bias_t(kb):
        bias = jnp.where(sc[kb] >= thr_b, 0.0, NEG_BIG)
        return jnp.concatenate([bias] * GQA_GROUP, axis=1)

    _masked_attention(o_ref, n_blocks, n_blocks_all, bias_t, k16, vt16, qt, att, tq)


def _dsa_sample_kernel(q_ref, qi_ref, wq_ref, kn_ref, vn_ref, kwn_ref, kc_ref, vc_ref, kic_ref, o_ref,
                       k16, vt16, kia, kib, wbe, wbo, sc, qt, *att,
                       tq, n_key_rows, n_cache, n_new, topk):
    n_blocks = n_key_rows // KEY_BLOCK
    n_cache_blocks = n_cache // KEY_BLOCK

    lane = lax.broadcasted_iota(I32, (n_new, LANES), 1)
    kin = jnp.where(lane < IDX_DIM, kwn_ref[...], 0.0).astype(BF16)
    n_tail = n_key_rows - n_cache
    k16[0:n_cache, :] = kc_ref[...].astype(BF16)
    for kb in range(n_cache_blocks):
        vt16[kb] = vc_ref[kb * KEY_BLOCK:(kb + 1) * KEY_BLOCK, :].T.astype(BF16)
    v_tail = jnp.concatenate([vn_ref[...], jnp.zeros((n_tail - n_new, KV_WIDTH), F32)], axis=0)
    vt16[n_cache_blocks] = v_tail.T.astype(BF16)
    kic = jnp.concatenate([kic_ref[...], jnp.zeros((n_cache, LANES - IDX_DIM), F32)], axis=1)
    kia[0:n_cache, :] = kic.astype(BF16)
    kib[0:n_cache, :] = pltpu.roll(kic, IDX_DIM, 1).astype(BF16)
    k16[n_cache:n_key_rows, :] = jnp.zeros((n_tail, KV_WIDTH), BF16)
    kia[n_cache:n_key_rows, :] = jnp.zeros((n_tail, LANES), BF16)
    kib[n_cache:n_key_rows, :] = jnp.zeros((n_tail, LANES), BF16)
    k16[n_cache:n_cache + n_new, :] = kn_ref[...].astype(BF16)
    kia[n_cache:n_cache + n_new, :] = kin
    kib[n_cache:n_cache + n_new, :] = pltpu.roll(kin.astype(F32), IDX_DIM, 1).astype(BF16)

    wq = wq_ref[...]
    for j in range(IDX_HEADS):
        col = jnp.broadcast_to(wq[:, IDX_DIM + j:IDX_DIM + j + 1], (tq, LANES))
        if j % 2 == 0:
            wbe[j // 2] = col
        else:
            wbo[j // 2] = col
    _stack_queries_t(q_ref, qt)

    lane = lax.broadcasted_iota(I32, (tq, KEY_BLOCK), 1)

    qi2 = qi_ref[...].reshape(IDX_PAIRS * tq, LANES)

    def score_block(kb, carry):
        r0 = pl.multiple_of(kb * KEY_BLOCK, KEY_BLOCK)
        se = _dot_nt(qi2, kia[pl.ds(r0, KEY_BLOCK), :]).reshape(IDX_PAIRS, tq, KEY_BLOCK)
        so = _dot_nt(qi2, kib[pl.ds(r0, KEY_BLOCK), :]).reshape(IDX_PAIRS, tq, KEY_BLOCK)
        score = jnp.zeros((tq, KEY_BLOCK), F32)
        for p in range(IDX_PAIRS):
            we = wbe[p]
            wo = wbo[p]
            we2 = jnp.concatenate([we, we], axis=1)
            wo2 = jnp.concatenate([wo, wo], axis=1)
            score = score + jnp.maximum(se[p], 0.0) * we2 + jnp.maximum(so[p], 0.0) * wo2
        adm = (kb * KEY_BLOCK + lane) < (n_cache + n_new)
        sc[kb] = jnp.where(adm, score, -jnp.inf)
        return carry

    lax.fori_loop(0, n_blocks, score_block, 0)

    def count_where(pred):
        def body(kb, acc):
            m = pred(sc[kb], kb).astype(I32)
            return acc + m[:, :LANES] + m[:, LANES:]
        acc = lax.fori_loop(0, n_blocks, body, jnp.zeros((tq, LANES), I32))
        return jnp.sum(acc, axis=1, keepdims=True)

    thr, need_tie = _kth_largest_score(topk, (tq, 1), n_key_rows, count_where)

    @pl.when(jnp.max(need_tie.astype(I32)) > 0)
    def _():
        _drop_excess_ties(sc, n_blocks, topk, thr, need_tie,
                          lambda kb: kb * KEY_BLOCK + lane, count_where)

    thr_b = jnp.broadcast_to(thr, (tq, KEY_BLOCK))

    def bias_t(kb):
        bias = jnp.where(sc[kb] >= thr_b, 0.0, NEG_BIG)
        return jnp.concatenate([bias] * GQA_GROUP, axis=0).T

    _masked_attention(o_ref, n_blocks, n_blocks, bias_t, k16, vt16, qt, att, tq)


def _dsa_sample_scratch(tq, n_key_rows):
    rows = GQA_GROUP * tq
    return [
        pltpu.VMEM((n_key_rows, KV_WIDTH), BF16),
        pltpu.VMEM((n_key_rows // KEY_BLOCK, KV_WIDTH, KEY_BLOCK), BF16),
        pltpu.VMEM((n_key_rows, LANES), BF16),
        pltpu.VMEM((n_key_rows, LANES), BF16),
        pltpu.VMEM((IDX_PAIRS, tq, LANES), F32),
        pltpu.VMEM((IDX_PAIRS, tq, LANES), F32),
        pltpu.VMEM((n_key_rows // KEY_BLOCK, tq, KEY_BLOCK), F32),
        pltpu.VMEM((N_KV_HEADS, HEAD_DIM, rows), BF16),
    ] + _attention_scratch(rows)


def _dsa_prompt(q, qi, kw, k, v, batch, seq):
    tq = 128
    nqb = seq // tq
    topk = min(TOPK_MAX, seq // 4)
    kern = functools.partial(_dsa_prompt_kernel, tq=tq, seq=seq, topk=topk)
    rows_q = GQA_GROUP * tq
    scratch = [
        pltpu.VMEM((seq, KV_WIDTH), BF16),
        pltpu.VMEM((seq // KEY_BLOCK, KV_WIDTH, KEY_BLOCK), BF16),
        pltpu.VMEM((seq, LANES), BF16),
        pltpu.VMEM((seq, LANES), BF16),
        pltpu.VMEM((seq // KEY_BLOCK + 1, KEY_BLOCK, tq), F32),
        pltpu.VMEM((2, 2, KEY_BLOCK, IDX_PAIRS * tq), F32),
        pltpu.VMEM((N_KV_HEADS, HEAD_DIM, rows_q), BF16),
    ] + _attention_scratch(rows_q)
    qmap = lambda b, i: (b * nqb + i, 0)
    smap = lambda b, i: (b, 0)
    return pl.pallas_call(
        kern,
        grid=(batch, nqb),
        in_specs=[pl.BlockSpec((tq, ATTN_WIDTH), qmap),
                  pl.BlockSpec((IDX_PAIRS, tq, LANES), lambda b, i: (0, b * nqb + i, 0)),
                  pl.BlockSpec((tq, LANES), qmap),
                  pl.BlockSpec((seq, KV_WIDTH), smap),
                  pl.BlockSpec((seq, KV_WIDTH), smap),
                  pl.BlockSpec((seq, LANES), smap)],
        out_specs=pl.BlockSpec((tq, ATTN_WIDTH), qmap),
        out_shape=jax.ShapeDtypeStruct((batch * seq, ATTN_WIDTH), BF16),
        scratch_shapes=scratch,
        compiler_params=_cparams(2),
        name="dsa_prompt",
    )(q, qi, kw, k, v, kw)


def _dsa_sample(q, qi, kw, k, v, cache_k, cache_v, cache_ki, row0, batch, seq, past):
    tq = seq
    n_key_rows = past + KEY_BLOCK
    topk = min(TOPK_MAX, (past + seq) // 4)
    kern = functools.partial(_dsa_sample_kernel, tq=tq, n_key_rows=n_key_rows, n_cache=past, n_new=seq,
                             topk=topk)
    t0 = row0 // seq
    qmap = lambda b, i: (t0 + b, 0)
    cmap = lambda b, i: (b, 0)
    return pl.pallas_call(
        kern,
        grid=(batch, 1),
        in_specs=[pl.BlockSpec((tq, ATTN_WIDTH), qmap),
                  pl.BlockSpec((IDX_PAIRS, tq, LANES), lambda b, i: (0, t0 + b, 0)),
                  pl.BlockSpec((tq, LANES), qmap),
                  pl.BlockSpec((tq, KV_WIDTH), cmap),
                  pl.BlockSpec((tq, KV_WIDTH), cmap),
                  pl.BlockSpec((tq, LANES), qmap),
                  pl.BlockSpec((past, KV_WIDTH), cmap),
                  pl.BlockSpec((past, KV_WIDTH), cmap),
                  pl.BlockSpec((past, IDX_DIM), cmap)],
        out_specs=pl.BlockSpec((tq, ATTN_WIDTH), cmap),
        out_shape=jax.ShapeDtypeStruct((batch * seq, ATTN_WIDTH), BF16),
        scratch_shapes=_dsa_sample_scratch(tq, n_key_rows),
        compiler_params=_cparams(2, allow_input_fusion=[False] * 6 + [True] * 3),
        name="dsa_sample",
    )(q, qi, kw, k, v, kw, cache_k, cache_v, cache_ki)


def _pool_kernel(*refs, tp, pos0, carried):
    if carried:
        p_ref, w_ref, s_ref, o_ref, ext, save = refs
    else:
        p_ref, halo_ref, w_ref, s_ref, o_ref, ext = refs
    i = pl.program_id(1)
    if carried:
        @pl.when(i == 0)
        def _():
            ext[0:POOL_HALO, :] = jnp.zeros((POOL_HALO, POOL_WIDTH), F32)

        @pl.when(i > 0)
        def _():
            ext[0:POOL_HALO, :] = save[...]
    else:
        ext[0:POOL_HALO, :] = halo_ref[0]
    x = p_ref[...]
    ext[POOL_HALO:POOL_HALO + tp, :] = x
    if carried:
        save[...] = x[tp - POOL_HALO:tp, :]
    pos = pos0 + i * tp + lax.broadcasted_iota(I32, (tp, 1), 0)
    for g, w in enumerate(POOL_WINDOWS):
        c0, c1 = g * POOL_GROUP_DIM, (g + 1) * POOL_GROUP_DIM
        xg = x[:, c0:c1]
        tot = xg
        for s in range(1, w):
            tot = tot + ext[POOL_HALO - s:POOL_HALO - s + tp, c0:c1]
        inv = 1.0 / jnp.minimum(pos + 1, w).astype(F32)
        d = (tot * inv - xg).astype(BF16)
        o_ref[:, c0:c1] = (_dot(d, w_ref[g]) * s_ref[:, c0:c1]).astype(BF16)


def _pool(pin, w_grp, scale, tile0, batch, seq, pos0, halo=None):
    carried = halo is None
    tp = min(seq, 256)
    nt = seq // tp
    kern = functools.partial(_pool_kernel, tp=tp, pos0=pos0, carried=carried)
    in_specs = [pl.BlockSpec((tp, POOL_WIDTH), lambda b, i: (tile0 + b * nt + i, 0))]
    args = [pin]
    if not carried:
        in_specs.append(pl.BlockSpec((1, POOL_HALO, POOL_WIDTH), lambda b, i: (b, 0, 0)))
        args.append(halo)
    in_specs += [pl.BlockSpec((POOL_GROUPS, POOL_GROUP_DIM, POOL_GROUP_DIM), lambda b, i: (0, 0, 0)),
                 pl.BlockSpec((1, POOL_WIDTH), lambda b, i: (0, 0))]
    args += [w_grp, scale]
    scratch = [pltpu.VMEM((POOL_HALO + tp, POOL_WIDTH), F32)]
    if carried:
        scratch.append(pltpu.VMEM((POOL_HALO, POOL_WIDTH), F32))
    return pl.pallas_call(
        kern,
        grid=(batch, nt),
        in_specs=in_specs,
        out_specs=pl.BlockSpec((tp, POOL_WIDTH), lambda b, i: (b * nt + i, 0)),
        out_shape=jax.ShapeDtypeStruct((batch * seq, POOL_WIDTH), BF16),
        scratch_shapes=scratch,
        compiler_params=_cparams(2),
        name="pool_prompt" if carried else "pool_sample",
    )(*args)


def _merge_kernel(h_ref, oap_ref, oas_ref, opp_ref, ops_ref, wga_ref, wgp_ref, ba_ref, bp_ref, woa_ref,
                  wop_ref, m_ref, *, n_prompt_tiles):
    h = h_ref[...]
    is_prompt = pl.program_id(1) < n_prompt_tiles
    oa = jnp.where(is_prompt, oap_ref[...], oas_ref[...])
    op = jnp.where(is_prompt, opp_ref[...], ops_ref[...])
    ga = jax.nn.sigmoid(_dot(h, wga_ref[...]) + ba_ref[...])
    gp = jax.nn.sigmoid(_dot(h, wgp_ref[...]) + bp_ref[...])
    a = _dot(oa, woa_ref[...])
    p = _dot(op, wop_ref[...])
    m_ref[...] = (ga * a + gp * p).astype(BF16)


def _merge(h, oa_p, oa_s, op_p, op_s, w_gate, b_gate, w_oa, w_op):
    n = h.shape[0]
    tm = 2 * TOK_TILE
    tn = D_MODEL // 2
    nj = D_MODEL // tn
    npt = oa_p.shape[0] // tm
    prompt = lambda w: pl.BlockSpec((tm, w), lambda j, i: (jnp.minimum(i, npt - 1), 0))
    sample = lambda w: pl.BlockSpec((tm, w), lambda j, i: (jnp.maximum(i - npt, 0), 0))
    return pl.pallas_call(
        functools.partial(_merge_kernel, n_prompt_tiles=npt),
        grid=(nj, n // tm),
        in_specs=[pl.BlockSpec((tm, D_MODEL), lambda j, i: (i, 0)),
                  prompt(ATTN_WIDTH), sample(ATTN_WIDTH), prompt(POOL_WIDTH), sample(POOL_WIDTH),
                  pl.BlockSpec((D_MODEL, tn), lambda j, i: (0, j)),
                  pl.BlockSpec((D_MODEL, tn), lambda j, i: (0, nj + j)),
                  pl.BlockSpec((1, tn), lambda j, i: (0, j)),
                  pl.BlockSpec((1, tn), lambda j, i: (0, nj + j)),
                  pl.BlockSpec((ATTN_WIDTH, tn), lambda j, i: (0, j)),
                  pl.BlockSpec((POOL_WIDTH, tn), lambda j, i: (0, j))],
        out_specs=pl.BlockSpec((tm, tn), lambda j, i: (i, j)),
        out_shape=jax.ShapeDtypeStruct((n, D_MODEL), BF16),
        compiler_params=_cparams(2),
        name="merge",
    )(h, oa_p, oa_s, op_p, op_s, w_gate, w_gate, b_gate, b_gate, w_oa, w_op)


_R_E1, _R_E2, _R_W1, _R_W2, _R_RANK1, _R_RANK2 = range(6)
_R_LOGIT0 = MOE_GROUPS


def _outproj_kernel(m_ref, xp_ref, xs_ref, wo_ref, g2_ref, wr_ref, br_ref,
                    x1_ref, h2_ref, route_ref, route_t_ref, counts_ref, run, *, n_prompt_tiles):
    i = pl.program_id(0)
    tm = m_ref.shape[0]

    @pl.when(i == 0)
    def _():
        run[...] = jnp.zeros(run.shape, F32)

    x = jnp.where(i < n_prompt_tiles, xp_ref[...], xs_ref[...])
    x1 = x + _dot(m_ref[...], wo_ref[...])
    x1_ref[...] = x1
    r = lax.rsqrt(jnp.mean(x1 * x1, axis=-1, keepdims=True) + EPS)
    hf = (x1 * r) * g2_ref[...]
    h2_ref[...] = hf

    hi = hf.astype(BF16)
    lo = (hf - hi.astype(F32)).astype(BF16)
    hw = _dot(hi, wr_ref[...])
    lg = hw[:, :LANES] + (hw[:, LANES:] + _dot(lo, wr_ref[:, :LANES])) + br_ref[...]

    lane = lax.broadcasted_iota(I32, (tm, LANES), 1)
    neg_inf = jnp.float32(-jnp.inf)
    is_g = lane < MOE_GROUPS
    cl = jnp.where(is_g, lg, neg_inf)
    gmax = jnp.max(cl, axis=1, keepdims=True)
    g_sel = jnp.min(jnp.where(cl == gmax, lane, LANES), axis=1, keepdims=True)
    den = jnp.sum(jnp.where(is_g, jnp.exp(cl - gmax), 0.0), axis=1, keepdims=True)
    g_val = 1.0 / den
    e_lane = lane - _R_LOGIT0
    in_grp = (e_lane >= 0) & (e_lane < N_EXPERTS) & ((e_lane >> GROUP_SHIFT) == g_sel)
    f1 = jnp.where(in_grp, lg, neg_inf)
    v1 = jnp.max(f1, axis=1, keepdims=True)
    i1 = jnp.min(jnp.where(f1 == v1, lane, LANES), axis=1, keepdims=True)
    f2 = jnp.where(lane == i1, neg_inf, f1)
    v2 = jnp.max(f2, axis=1, keepdims=True)
    i2 = jnp.min(jnp.where(f2 == v2, lane, LANES), axis=1, keepdims=True)
    e2 = jnp.exp(v2 - v1)
    w1 = g_val / (1.0 + e2)
    w2 = g_val * e2 / (1.0 + e2)

    oh1 = lane == i1
    oh2 = lane == i2
    c = (oh1 | oh2).astype(BF16)
    rr = lax.broadcasted_iota(I32, (tm, tm), 0)
    cc = lax.broadcasted_iota(I32, (tm, tm), 1)
    before = (cc < rr).astype(BF16)
    prior = _dot(before, c) + run[...]
    rank1 = jnp.sum(jnp.where(oh1, prior, 0.0), axis=1, keepdims=True)
    rank2 = jnp.sum(jnp.where(oh2, prior, 0.0), axis=1, keepdims=True)
    run[...] = run[...] + jnp.sum(c.astype(F32), axis=0, keepdims=True)
    counts_ref[...] = jnp.broadcast_to(run[...], counts_ref.shape)

    rec = jnp.zeros((tm, LANES), F32)
    for ln, val in ((_R_E1, (i1 - _R_LOGIT0).astype(F32)), (_R_E2, (i2 - _R_LOGIT0).astype(F32)),
                    (_R_W1, w1), (_R_W2, w2), (_R_RANK1, rank1), (_R_RANK2, rank2)):
        rec = jnp.where(lane == ln, val, rec)
    route_ref[...] = rec
    route_t_ref[...] = rec.T[0:route_t_ref.shape[0], :]


def _out_proj(m, xp, xs, w_out, g2, wr_split, b_r):
    n = m.shape[0]
    tm = TOK_TILE
    n_prompt_tiles = xp.shape[0] // tm
    tok = lambda w: pl.BlockSpec((tm, w), lambda i: (i, 0))
    const = lambda s: pl.BlockSpec(s, lambda i: (0,) * len(s))
    return pl.pallas_call(
        functools.partial(_outproj_kernel, n_prompt_tiles=n_prompt_tiles),
        grid=(n // tm,),
        in_specs=[tok(D_MODEL), *_split_token_specs(tm, D_MODEL, n_prompt_tiles),
                  pl.BlockSpec((D_MODEL, D_MODEL), lambda i: (0, 0), pipeline_mode=pl.Buffered(1)),
                  const((1, D_MODEL)), const((D_MODEL, 2 * LANES)), const((1, LANES))],
        out_specs=(tok(D_MODEL), tok(D_MODEL), tok(LANES), pl.BlockSpec((8, tm), lambda i: (0, i)),
                   const((8, LANES))),
        out_shape=(jax.ShapeDtypeStruct((n, D_MODEL), F32),
                   jax.ShapeDtypeStruct((n, D_MODEL), F32),
                   jax.ShapeDtypeStruct((n, LANES), F32),
                   jax.ShapeDtypeStruct((8, n), F32),
                   jax.ShapeDtypeStruct((8, LANES), F32)),
        scratch_shapes=[pltpu.VMEM((1, LANES), F32)],
        compiler_params=_cparams(1),
        name="out_proj",
    )(m, xp, xs, w_out, g2, wr_split, b_r)


DISPATCH_TOKENS = 256


def _dispatch_kernel(pos1_ref, pos2_ref, h2_ref, xs_in_ref, xs_ref, sem):
    del xs_in_ref
    tb = DISPATCH_TOKENS
    dst = (pos1_ref, pos2_ref)

    def row_copy(r, slot):
        return pltpu.make_async_copy(h2_ref.at[pl.ds(r, 1)], xs_ref.at[pl.ds(dst[slot][r], 1)], sem)

    def issue(r, carry):
        row_copy(r, 0).start()
        row_copy(r, 1).start()
        return carry

    lax.fori_loop(0, tb, issue, 0, unroll=8)

    def drain(r, carry):
        row_copy(r, 0).wait()
        row_copy(r, 1).wait()
        return carry

    lax.fori_loop(0, tb, drain, 0, unroll=8)


def _dispatch(pos1, pos2, h2, n_rows):
    n = h2.shape[0]
    tb = DISPATCH_TOKENS
    xs0 = jnp.zeros((n_rows, D_MODEL), F32)
    return pl.pallas_call(
        _dispatch_kernel,
        grid=(n // tb,),
        in_specs=[pl.BlockSpec((tb,), lambda i: (i,), memory_space=pltpu.SMEM),
                  pl.BlockSpec((tb,), lambda i: (i,), memory_space=pltpu.SMEM),
                  pl.BlockSpec((tb, D_MODEL), lambda i: (i, 0)),
                  pl.BlockSpec(memory_space=pl.ANY)],
        out_specs=pl.BlockSpec(memory_space=pl.ANY),
        out_shape=jax.ShapeDtypeStruct((n_rows, D_MODEL), F32),
        scratch_shapes=[pltpu.SemaphoreType.DMA(())],
        input_output_aliases={3: 0},
        compiler_params=_cparams(1),
        name="dispatch",
    )(pos1, pos2, h2, xs0)


def _experts_kernel(te_ref, nu_ref, xs_ref, wg_ref, wu_ref, wd_ref, ys_ref, wg16, wu16, wd16):
    i = pl.program_id(0)
    prev = te_ref[jnp.maximum(i - 1, 0)]
    fresh = (i == 0) | (te_ref[i] != prev)
    used = i < nu_ref[0]

    @pl.when(fresh & used)
    def _():
        wg16[...] = wg_ref[0].astype(BF16)
        wu16[...] = wu_ref[0].astype(BF16)
        wd16[...] = wd_ref[0].astype(BF16)

    @pl.when(used)
    def _():
        x = xs_ref[...].astype(BF16)
        a = _dot(x, wg16[...])
        u = _dot(x, wu16[...])
        act = (a * jax.nn.sigmoid(a)) * u
        ys_ref[...] = _dot(act.astype(BF16), wd16[...])

    @pl.when(jnp.logical_not(used))
    def _():
        ys_ref[...] = jnp.zeros(ys_ref.shape, F32)


def _experts(tile_expert, n_used, xs, w_g, w_u, w_d):
    n_rows = xs.shape[0]
    tm = EXP_TILE
    grid_spec = pltpu.PrefetchScalarGridSpec(
        num_scalar_prefetch=2,
        grid=(n_rows // tm,),
        in_specs=[pl.BlockSpec((tm, D_MODEL), lambda i, te, nu: (i, 0)),
                  pl.BlockSpec((1, D_MODEL, D_EXPERT), lambda i, te, nu: (te[i], 0, 0)),
                  pl.BlockSpec((1, D_MODEL, D_EXPERT), lambda i, te, nu: (te[i], 0, 0)),
                  pl.BlockSpec((1, D_EXPERT, D_MODEL), lambda i, te, nu: (te[i], 0, 0))],
        out_specs=pl.BlockSpec((tm, D_MODEL), lambda i, te, nu: (i, 0)),
        scratch_shapes=[pltpu.VMEM((D_MODEL, D_EXPERT), BF16),
                        pltpu.VMEM((D_MODEL, D_EXPERT), BF16),
                        pltpu.VMEM((D_EXPERT, D_MODEL), BF16)],
    )
    return pl.pallas_call(
        _experts_kernel,
        grid_spec=grid_spec,
        out_shape=jax.ShapeDtypeStruct((n_rows, D_MODEL), F32),
        compiler_params=_cparams(1),
        name="experts",
    )(tile_expert, n_used, xs, w_g, w_u, w_d)


COMBINE_TOKENS = 256


def _combine_kernel(pos1_ref, pos2_ref, pos1_next_ref, pos2_next_ref, x1_ref, route_ref, gf_ref, ys_ref,
                    yp_ref, ysm_ref, ybuf, sem, *, n_prompt_tiles):
    tc = COMBINE_TOKENS
    i = pl.program_id(0)
    buf = i % 2

    def start_rows(p_refs, b):
        for r in range(tc):
            for slot in range(2):
                pltpu.make_async_copy(ys_ref.at[pl.ds(p_refs[slot][r], 1)],
                                      ybuf.at[b, slot, pl.ds(r, 1)], sem.at[b]).start()

    def wait_rows(b):
        for slot in range(2):
            pltpu.make_async_copy(ys_ref.at[pl.ds(0, tc)], ybuf.at[b, slot], sem.at[b]).wait()

    @pl.when(i == 0)
    def _():
        start_rows((pos1_ref, pos2_ref), 0)

    wait_rows(buf)
    start_rows((pos1_next_ref, pos2_next_ref), 1 - buf)

    route = route_ref[...]
    w1 = route[:, _R_W1:_R_W1 + 1]
    w2 = route[:, _R_W2:_R_W2 + 1]
    x2 = x1_ref[...] + (w1 * ybuf[buf, 0] + w2 * ybuf[buf, 1])
    r = lax.rsqrt(jnp.mean(x2 * x2, axis=-1, keepdims=True) + EPS)
    y = (x2 * r) * gf_ref[...]
    is_prompt = pl.program_id(0) < n_prompt_tiles

    @pl.when(is_prompt)
    def _():
        yp_ref[...] = y

    @pl.when(jnp.logical_not(is_prompt))
    def _():
        ysm_ref[...] = y

    @pl.when(i == pl.num_programs(0) - 1)
    def _():
        wait_rows(1 - buf)


def _combine(pos1, pos2, x1, route, gf, ys, n_prompt):
    n = x1.shape[0]
    tc = COMBINE_TOKENS
    n_prompt_tiles = n_prompt // tc
    n_steps = n // tc
    return pl.pallas_call(
        functools.partial(_combine_kernel, n_prompt_tiles=n_prompt_tiles),
        grid=(n_steps,),
        in_specs=[pl.BlockSpec((tc,), lambda i: (i,), memory_space=pltpu.SMEM),
                  pl.BlockSpec((tc,), lambda i: (i,), memory_space=pltpu.SMEM),
                  pl.BlockSpec((tc,), lambda i: (jnp.minimum(i + 1, n_steps - 1),), memory_space=pltpu.SMEM),
                  pl.BlockSpec((tc,), lambda i: (jnp.minimum(i + 1, n_steps - 1),), memory_space=pltpu.SMEM),
                  pl.BlockSpec((tc, D_MODEL), lambda i: (i, 0)),
                  pl.BlockSpec((tc, LANES), lambda i: (i, 0)),
                  pl.BlockSpec((1, D_MODEL), lambda i: (0, 0)),
                  pl.BlockSpec(memory_space=pl.ANY)],
        out_specs=_split_token_specs(tc, D_MODEL, n_prompt_tiles),
        out_shape=(jax.ShapeDtypeStruct((n_prompt, D_MODEL), F32),
                   jax.ShapeDtypeStruct((n - n_prompt, D_MODEL), F32)),
        scratch_shapes=[pltpu.VMEM((2, 2, tc, D_MODEL), F32), pltpu.SemaphoreType.DMA((2,))],
        compiler_params=_cparams(1),
        name="combine",
    )(pos1, pos2, pos1, pos2, x1, route, gf, ys)


def _split_w_in(w_in):
    w_main = w_in[:, :_C_KW].astype(BF16)
    w_kw = jnp.pad(w_in[:, _C_KW:_C_P].astype(BF16), ((0, 0), (0, LANES - (_C_P - _C_KW))))
    w_pool = w_in[:, _C_P:].astype(BF16)
    return w_main, w_kw, w_pool


def _forward(x_prompt, x_sample, cache_k, cache_v, cache_kidx, state_pool,
             norm1_g, w_in, w_gate, b_gate, w_pool_grp, pool_scale, w_o_attn, w_o_pool, w_out,
             norm2_g, w_rg, b_rg, w_re, b_re, w_eg, w_eu, w_ed, norm_f_g):
    batch, seq, d = x_prompt.shape
    dec_batch, dec_seq, _ = x_sample.shape
    past = cache_k.shape[1]
    n_p = batch * seq
    n_s = dec_batch * dec_seq
    n = n_p + n_s
    tm = TOK_TILE
    assert d == D_MODEL and seq % tm == 0 and n_s % tm == 0 and n % (2 * tm) == 0
    assert past % KEY_BLOCK == 0 and dec_seq >= POOL_STATE and dec_seq <= KEY_BLOCK

    xp = x_prompt.reshape(n_p, d)
    xs = x_sample.reshape(n_s, d)
    pos_tab = jnp.concatenate([jnp.arange(seq), past + jnp.arange(n_s) % dec_seq])
    tables = _rope_tables(pos_tab)
    h, q, k_p, k_s, v_p, v_s, qi, kw, pin = _in_proj(
        xp, xs, norm1_g.reshape(1, d), *_split_w_in(w_in), tables, seq // tm)

    oa_p = _dsa_prompt(q, qi, kw, k_p, v_p, batch, seq)
    oa_s = _dsa_sample(q, qi, kw, k_s, v_s,
                       cache_k.reshape(dec_batch * past, KV_WIDTH),
                       cache_v.reshape(dec_batch * past, KV_WIDTH),
                       cache_kidx.reshape(dec_batch * past, IDX_DIM),
                       n_p, dec_batch, dec_seq, past)

    w_grp16 = w_pool_grp.astype(BF16)
    scale = pool_scale.reshape(1, POOL_WIDTH)
    op_p = _pool(pin, w_grp16, scale, 0, batch, seq, 0)
    halo = jnp.concatenate([jnp.zeros((dec_batch, 1, POOL_WIDTH), F32), state_pool], axis=1)
    op_s = _pool(pin, w_grp16, scale, n_p // dec_seq, dec_batch, dec_seq, past, halo=halo)

    m = _merge(h, oa_p, oa_s, op_p, op_s, w_gate.astype(BF16), b_gate.reshape(1, 2 * d),
               w_o_attn.astype(BF16), w_o_pool.astype(BF16))

    w_r = jnp.concatenate([w_rg, w_re, jnp.zeros((d, LANES - MOE_GROUPS - N_EXPERTS), F32)], axis=1)
    b_r = jnp.concatenate([b_rg, b_re, jnp.zeros((LANES - MOE_GROUPS - N_EXPERTS,), F32)]).reshape(1, LANES)
    wr_hi = w_r.astype(BF16)
    wr_lo = (w_r - wr_hi.astype(F32)).astype(BF16)
    wr_split = jnp.concatenate([wr_hi, wr_lo], axis=1)
    x1, h2, route, route_t, counts = _out_proj(m, xp, xs, w_out.astype(BF16), norm2_g.reshape(1, d), wr_split, b_r)

    te = EXP_TILE
    n_tiles = (2 * n) // te + N_EXPERTS
    cnt = counts[0, _R_LOGIT0:_R_LOGIT0 + N_EXPERTS].astype(I32)
    tiles_per_e = (cnt + te - 1) // te
    tile_end = jnp.cumsum(tiles_per_e)
    row_start = (tile_end - tiles_per_e) * te
    n_used = tile_end[-1:].astype(I32)
    tile_ids = jnp.arange(n_tiles, dtype=I32)
    tile_expert = jnp.minimum(
        jnp.sum((tile_end[None, :] <= tile_ids[:, None]).astype(I32), axis=1), N_EXPERTS - 1)
    last_e = tile_expert[jnp.maximum(n_used[0] - 1, 0)]
    tile_expert = jnp.where(tile_ids < n_used[0], tile_expert, last_e)
    pos1 = row_start[route_t[_R_E1].astype(I32)] + route_t[_R_RANK1].astype(I32)
    pos2 = row_start[route_t[_R_E2].astype(I32)] + route_t[_R_RANK2].astype(I32)

    xs = _dispatch(pos1, pos2, h2, n_tiles * te)
    ys = _experts(tile_expert, n_used, xs, w_eg, w_eu, w_ed)
    y_p, y_s = _combine(pos1, pos2, x1, route, norm_f_g.reshape(1, d), ys, n_p)

    y_prompt = y_p.reshape(batch, seq, d)
    y_sample = y_s.reshape(dec_batch, dec_seq, d)
    k_p = k_p.reshape(1, batch, seq, N_KV_HEADS, HEAD_DIM)
    v_p = v_p.reshape(1, batch, seq, N_KV_HEADS, HEAD_DIM)
    ki_p = kw[:n_p, :IDX_DIM].reshape(1, batch, seq, IDX_DIM)
    pool_p = jnp.stack([pin[(b + 1) * seq - POOL_STATE:(b + 1) * seq] for b in range(batch)])[None]
    k_s = k_s.reshape(1, dec_batch, dec_seq, N_KV_HEADS, HEAD_DIM)
    v_s = v_s.reshape(1, dec_batch, dec_seq, N_KV_HEADS, HEAD_DIM)
    ki_s = kw[n_p:, :IDX_DIM].reshape(1, dec_batch, dec_seq, IDX_DIM)
    pool_s = pin[n_p:].reshape(dec_batch, dec_seq, POOL_WIDTH)[None, :, dec_seq - POOL_STATE:, :]
    return (y_prompt, y_sample, k_p, v_p, ki_p, pool_p, k_s, v_s, ki_s, pool_s)


def kernel(x_prompt, x_sample, cache_k, cache_v, cache_kidx, state_pool, norm1_g, w_in, w_gate, b_gate,
           w_pool_grp, pool_scale, w_o_attn, w_o_pool, w_out, norm2_g, w_router_group, b_router_group,
           w_router_expert, b_router_expert, w_exp_gate, w_exp_up, w_exp_down, norm_f_g):
    assert cache_k.shape[0] == 1, "single-layer model"
    return _forward(x_prompt, x_sample, cache_k[0], cache_v[0], cache_kidx[0], state_pool[0],
                    norm1_g[0], w_in[0], w_gate[0], b_gate[0], w_pool_grp[0], pool_scale[0],
                    w_o_attn[0], w_o_pool[0], w_out[0], norm2_g[0], w_router_group[0],
                    b_router_group[0], w_router_expert[0], b_router_expert[0],
                    w_exp_gate[0], w_exp_up[0], w_exp_down[0], norm_f_g)
```
